```python
import math
import jax
import jax.numpy as jnp
from jax import lax
import numpy as np

D_MODEL = 1024
BATCH = 4
SEQ = 4096
DEPTH = 2

GRID_W = 64
CTX_LEN = 256
HEAD_DIM = 64
EPS = 1e-6
S5_WIDTH = 512
S5_GROUP = 16
S5_GROUPS = S5_WIDTH // S5_GROUP
S5_STATE = 64
NA_HEADS = 8
NA_WIDTH = NA_HEADS * HEAD_DIM
NA_ROWS = 8
NA_COLS = 16
LRU_WIDTH = 512
LRU_BLOCKS = 8
LRU_BLOCK = LRU_WIDTH // LRU_BLOCKS
LRU_CONV = 4
LRU_CONV_PAD = (1, 2)
LRU_C = 8.0
GQA_HEADS = 8
GQA_KV_HEADS = 2
GQA_WIDTH = GQA_HEADS * HEAD_DIM
GQA_KV_WIDTH = GQA_KV_HEADS * HEAD_DIM
Q_BLOCK = 128
ROPE_THETA = 10000.0
N_EVEN = (DEPTH + 1) // 2
N_ODD = DEPTH // 2
EVEN_SPLITS = (S5_WIDTH, 2 * S5_WIDTH, 2 * S5_WIDTH + NA_WIDTH, 2 * S5_WIDTH + 2 * NA_WIDTH, 2 * S5_WIDTH + 3 * NA_WIDTH)
EVEN_IN = 2 * S5_WIDTH + 4 * NA_WIDTH
EVEN_MIX = S5_WIDTH + NA_WIDTH
ODD_SPLITS = (LRU_WIDTH, 2 * LRU_WIDTH, 2 * LRU_WIDTH + GQA_WIDTH, 2 * LRU_WIDTH + GQA_WIDTH + GQA_KV_WIDTH, 2 * LRU_WIDTH + GQA_WIDTH + 2 * GQA_KV_WIDTH)
ODD_IN = 2 * LRU_WIDTH + 2 * GQA_WIDTH + 2 * GQA_KV_WIDTH
ODD_MIX = LRU_WIDTH + GQA_WIDTH

kernel_name = "hybrid_s5_natten_rglru_gqa_prefix_dit"

F32 = jnp.float32


def rms_norm(x, g):
    xf = x.astype(F32)
    y = xf * lax.rsqrt(jnp.mean(xf * xf, axis=-1, keepdims=True) + EPS)
    return (y * g.astype(F32)).astype(x.dtype)


def heads(t, n):
    return t.reshape(*t.shape[:-1], n, HEAD_DIM)


def linear_scan(a, b, h0, reverse):
    first = -1 if reverse else 0
    b = b.at[:, first].add(a[:, first] * h0)

    def combine(left, right):
        a_l, b_l = left
        a_r, b_r = right
        return a_r * a_l, a_r * b_l + b_r

    _, h = lax.associative_scan(combine, (a, b), axis=1, reverse=reverse)
    return h


def s5_states(u, h0s, lam_re, lam_im, log_dt, b_re, b_im):
    bsz, length, _ = u.shape
    ug = u.astype(F32).reshape(bsz, length, S5_GROUPS, S5_GROUP).astype(jnp.complex64)
    hs, finals = [], []
    for dirn in range(2):
        lam = lax.complex(lam_re[dirn].astype(F32), lam_im[dirn].astype(F32))
        dt = jnp.exp(log_dt[dirn].astype(F32))[:, None]
        lam_bar = jnp.exp(lam * dt)
        b = lax.complex(b_re[dirn].astype(F32), b_im[dirn].astype(F32))
        b_bar = ((lam_bar - 1.0) / lam)[..., None] * b
        bu = jnp.einsum('blgh,gph->blgp', ug, b_bar)
        a = jnp.broadcast_to(lam_bar, bu.shape)
        h = linear_scan(a, bu, h0s[dirn], reverse=(dirn == 1))
        hs.append(h)
        finals.append(h[:, 0] if dirn == 1 else h[:, -1])
    return hs, finals


def s5_readout(u, hs, c_re, c_im, d_skip, w_glu, b_glu):
    bsz, length, width = u.shape
    y = d_skip.astype(F32) * u.astype(F32)
    for dirn in range(2):
        cm = lax.complex(c_re[dirn].astype(F32), c_im[dirn].astype(F32))
        y = y + jnp.einsum('blgp,ghp->blgh', hs[dirn], cm).real.reshape(bsz, length, width)
    y = jax.nn.gelu(y)
    y = y * jax.nn.sigmoid(y @ w_glu.astype(F32) + b_glu.astype(F32))
    return y.astype(u.dtype)


def neighbourhood_attention(q, k, v, k_ctx, v_ctx, rel_bias):
    bsz, length, n_h, dh = q.shape
    rows = length // GRID_W
    kr = min(NA_ROWS, rows)
    kc = NA_COLS
    scale = dh ** -0.5
    qg = q.reshape(bsz, rows, GRID_W, n_h, dh)
    kg = k.reshape(bsz, rows, GRID_W, n_h, dh)
    vg = v.reshape(bsz, rows, GRID_W, n_h, dh)
    col = jnp.arange(GRID_W)
    col_idx = jnp.clip(col - kc // 2, 0, GRID_W - kc)[:, None] + jnp.arange(kc)[None, :]
    dc = col_idx - col[:, None] + (NA_COLS - 1)

    def one_row(r):
        row_start = jnp.clip(r - kr // 2, 0, rows - kr)
        k_win = lax.dynamic_slice_in_dim(kg, row_start, kr, axis=1)[:, :, col_idx]
        v_win = lax.dynamic_slice_in_dim(vg, row_start, kr, axis=1)[:, :, col_idx]
        q_r = lax.dynamic_index_in_dim(qg, r, axis=1, keepdims=False)
        dr = row_start + jnp.arange(kr) - r + (NA_ROWS - 1)
        bias = rel_bias[:, dr][:, :, dc].transpose(0, 2, 1, 3).astype(F32)
        s_loc = jnp.einsum('bwhd,brwjhd->bhwrj', q_r, k_win).astype(F32) * scale + bias[None]
        s_ctx = jnp.einsum('bwhd,bchd->bhwc', q_r, k_ctx).astype(F32) * scale
        s = jnp.concatenate([s_loc.reshape(bsz, n_h, GRID_W, kr * kc), s_ctx], axis=-1)
        p = jax.nn.softmax(s, axis=-1).astype(v.dtype)
        p_loc = p[..., :kr * kc].reshape(bsz, n_h, GRID_W, kr, kc)
        o = jnp.einsum('bhwrj,brwjhd->bwhd', p_loc, v_win) + jnp.einsum('bhwc,bchd->bwhd', p[..., kr * kc:], v_ctx)
        return o

    out = lax.map(one_row, jnp.arange(rows))
    return out.transpose(1, 0, 2, 3, 4).reshape(bsz, length, n_h * dh)


def context_attention(q, k, v):
    bsz, lc, hq, dh = q.shape
    hkv = k.shape[2]
    qg = q.reshape(bsz, lc, hkv, hq // hkv, dh)
    s = jnp.einsum('bqkgd,bskd->bkgqs', qg, k).astype(F32) * dh ** -0.5
    p = jax.nn.softmax(s, axis=-1).astype(v.dtype)
    return jnp.einsum('bkgqs,bskd->bqkgd', p, v).reshape(bsz, lc, hq * dh)


def centred_depthwise_conv(x, w, b):
    y = lax.conv_general_dilated(x, w[:, None, :], window_strides=(1,), padding=[LRU_CONV_PAD],
                                 dimension_numbers=('NWC', 'WIO', 'NWC'), feature_group_count=x.shape[-1])
    return y + b


def rglru_states(x, h0s, lam, w_a, b_a, w_x, b_x):
    bsz, length, width = x.shape
    xf = x.astype(F32)
    xb = xf.reshape(bsz, length, LRU_BLOCKS, LRU_BLOCK)
    hs, finals = [], []
    for dirn in range(2):
        gate_r = jax.nn.sigmoid(jnp.einsum('blni,nij->blnj', xb, w_a[dirn].astype(F32)).reshape(bsz, length, width) + b_a[dirn].astype(F32))
        gate_i = jax.nn.sigmoid(jnp.einsum('blni,nij->blnj', xb, w_x[dirn].astype(F32)).reshape(bsz, length, width) + b_x[dirn].astype(F32))
        log_a = -LRU_C * gate_r * jax.nn.softplus(-lam[dirn].astype(F32))
        a = jnp.exp(log_a)
        mult = jnp.sqrt(jnp.maximum(-jnp.expm1(2.0 * log_a), 0.0))
        h = linear_scan(a, mult * gate_i * xf, h0s[dirn], reverse=(dirn == 1))
        hs.append(h)
        finals.append(h[:, 0] if dirn == 1 else h[:, -1])
    return hs, finals


def axial_rope_tables(n_tokens):
    t = jnp.arange(n_tokens)
    row = (t // GRID_W).astype(F32)
    col = (t % GRID_W).astype(F32)
    half = HEAD_DIM // 2
    inv = ROPE_THETA ** (-jnp.arange(0, half, 2, dtype=F32) / half)
    ang = jnp.concatenate([row[:, None] * inv, col[:, None] * inv], axis=-1)
    return jnp.cos(ang), jnp.sin(ang)


def apply_rope(x, cos, sin):
    xf = x.astype(F32)
    x1, x2 = xf[..., 0::2], xf[..., 1::2]
    cs, sn = cos[None, :, None, :], sin[None, :, None, :]
    return jnp.stack([x1 * cs - x2 * sn, x1 * sn + x2 * cs], axis=-1).reshape(x.shape).astype(x.dtype)


def gqa_latent(q, k, v, k_ctx, v_ctx):
    bsz, length, hq, dh = q.shape
    g = hq // GQA_KV_HEADS
    k_all = jnp.concatenate([k_ctx, k], axis=1)
    v_all = jnp.concatenate([v_ctx, v], axis=1)
    nb = length // Q_BLOCK
    qb = jnp.moveaxis(q.reshape(bsz, nb, Q_BLOCK, GQA_KV_HEADS, g, dh), 1, 0)
    scale = dh ** -0.5

    def block(q_blk):
        s = jnp.einsum('bqkgd,bskd->bkgqs', q_blk, k_all).astype(F32) * scale
        p = jax.nn.softmax(s, axis=-1).astype(v.dtype)
        return jnp.einsum('bkgqs,bskd->bqkgd', p, v_all)

    o = lax.map(block, qb)
    return jnp.moveaxis(o, 0, 1).reshape(bsz, length, hq * dh)


def even_layer(xl, xc, w_in, w_out, lam_re, lam_im, log_dt, b_re, b_im, c_re, c_im, d_skip, w_glu, b_glu, rel_bias, ctx_out):
    bsz = xl.shape[0]
    u_l, ga_l, q_l, k_l, v_l, gb_l = jnp.split(xl @ w_in, list(EVEN_SPLITS), axis=-1)
    u_c, ga_c, q_c, k_c, v_c, gb_c = jnp.split(xc @ w_in, list(EVEN_SPLITS), axis=-1)
    zero = jnp.zeros((bsz, S5_GROUPS, S5_STATE), jnp.complex64)
    hs_c, fin_c = s5_states(u_c, (zero, zero), lam_re, lam_im, log_dt, b_re, b_im)
    hs_l, _ = s5_states(u_l, fin_c, lam_re, lam_im, log_dt, b_re, b_im)
    y_a = s5_readout(u_l, hs_l, c_re, c_im, d_skip, w_glu, b_glu) * jax.nn.silu(ga_l)
    kc, vc = heads(k_c, NA_HEADS), heads(v_c, NA_HEADS)
    y_b = neighbourhood_attention(heads(q_l, NA_HEADS), heads(k_l, NA_HEADS), heads(v_l, NA_HEADS), kc, vc, rel_bias) * jax.nn.silu(gb_l)
    out_l = jnp.concatenate([y_a, y_b], axis=-1) @ w_out
    if not ctx_out:
        return out_l, None
    y_ac = s5_readout(u_c, hs_c, c_re, c_im, d_skip, w_glu, b_glu) * jax.nn.silu(ga_c)
    y_bc = context_attention(heads(q_c, NA_HEADS), kc, vc) * jax.nn.silu(gb_c)
    out_c = jnp.concatenate([y_ac, y_bc], axis=-1) @ w_out
    return out_l, out_c


def odd_layer(xl, xc, w_in, w_out, conv_w, conv_b, lam, w_a, b_a, w_x, b_x, q_norm, k_norm, ctx_out):
    bsz, length, _ = xl.shape
    x_l, gc_l, q_l, k_l, v_l, gd_l = jnp.split(xl @ w_in, list(ODD_SPLITS), axis=-1)
    x_c, gc_c, q_c, k_c, v_c, gd_c = jnp.split(xc @ w_in, list(ODD_SPLITS), axis=-1)
    zero = jnp.zeros((bsz, LRU_WIDTH), F32)
    hs_c, fin_c = rglru_states(centred_depthwise_conv(x_c, conv_w, conv_b), (zero, zero), lam, w_a, b_a, w_x, b_x)
    hs_l, _ = rglru_states(centred_depthwise_conv(x_l, conv_w, conv_b), fin_c, lam, w_a, b_a, w_x, b_x)
    y_c = (hs_l[0] + hs_l[1]).astype(xl.dtype) * jax.nn.silu(gc_l)
    cos, sin = axial_rope_tables(length)
    q = apply_rope(rms_norm(heads(q_l, GQA_HEADS), q_norm), cos, sin)
    k = apply_rope(rms_norm(heads(k_l, GQA_KV_HEADS), k_norm), cos, sin)
    kc = rms_norm(heads(k_c, GQA_KV_HEADS), k_norm)
    vc = heads(v_c, GQA_KV_HEADS)
    y_d = gqa_latent(q, k, heads(v_l, GQA_KV_HEADS), kc, vc) * jax.nn.silu(gd_l)
    out_l = jnp.concatenate([y_c, y_d], axis=-1) @ w_out
    if not ctx_out:
        return out_l, None
    y_cc = (hs_c[0] + hs_c[1]).astype(xc.dtype) * jax.nn.silu(gc_c)
    y_dc = context_attention(rms_norm(heads(q_c, GQA_HEADS), q_norm), kc, vc) * jax.nn.silu(gd_c)
    out_c = jnp.concatenate([y_cc, y_dc], axis=-1) @ w_out
    return out_l, out_c


def setup_inputs(seed: int = 0) -> dict:
    key = jax.random.key(seed)
    ks = jax.random.split(key, 32)
    d = D_MODEL
    g, p, hgrp = S5_GROUPS, S5_STATE, S5_GROUP

    def nrm(k, shape, s):
        return jax.random.normal(k, shape, F32) * s

    lru_u = jax.random.uniform(ks[25], (N_ODD, 2, LRU_WIDTH), F32, 0.9, 0.999)
    lru_p = lru_u ** (1.0 / LRU_C)
    return {
        "x": nrm(ks[0], (BATCH, SEQ, d), 1.0),
        "c": nrm(ks[1], (BATCH, d), 1.0),
        "ctx": nrm(ks[2], (BATCH, CTX_LEN, d), 1.0),
        "c_ctx": nrm(ks[3], (d,), 1.0),
        "ada_w": nrm(ks[4], (DEPTH, d, 3 * d), 0.5 * d ** -0.5),
        "ada_b": nrm(ks[5], (DEPTH, 3 * d), 0.02),
        "pre_g": 1.0 + nrm(ks[6], (DEPTH, d), 0.05),
        "post_g": 1.0 + nrm(ks[7], (DEPTH, d), 0.05),
        "ev_w_in": nrm(ks[8], (N_EVEN, d, EVEN_IN), d ** -0.5),
        "ev_w_out": nrm(ks[9], (N_EVEN, EVEN_MIX, d), EVEN_MIX ** -0.5),
        "s5_lam_re": -0.5 * jnp.exp(nrm(ks[10], (N_EVEN, 2, g, p), 0.1)),
        "s5_lam_im": math.pi * jnp.arange(p, dtype=F32) + nrm(ks[11], (N_EVEN, 2, g, p), 0.05),
        "s5_log_dt": jax.random.uniform(ks[12], (N_EVEN, 2, g), F32, math.log(1e-3), math.log(1e-1)),
        "s5_b_re": nrm(ks[13], (N_EVEN, 2, g, p, hgrp), (2 * hgrp) ** -0.5),
        "s5_b_im": nrm(ks[14], (N_EVEN, 2, g, p, hgrp), (2 * hgrp) ** -0.5),
        "s5_c_re": nrm(ks[15], (N_EVEN, 2, g, hgrp, p), p ** -0.5),
        "s5_c_im": nrm(ks[16], (N_EVEN, 2, g, hgrp, p), p ** -0.5),
        "s5_d": nrm(ks[17], (N_EVEN, S5_WIDTH), 1.0),
        "s5_w_glu": nrm(ks[18], (N_EVEN, S5_WIDTH, S5_WIDTH), S5_WIDTH ** -0.5),
        "s5_b_glu": nrm(ks[19], (N_EVEN, S5_WIDTH), 0.02),
        "na_rel_bias": nrm(ks[20], (N_EVEN, NA_HEADS, 2 * NA_ROWS - 1, 2 * NA_COLS - 1), 0.1),
        "od_w_in": nrm(ks[21], (N_ODD, d, ODD_IN), d ** -0.5),
        "od_w_out": nrm(ks[22], (N_ODD, ODD_MIX, d), ODD_MIX ** -0.5),
        "lru_conv_w": nrm(ks[23], (N_ODD, LRU_CONV, LRU_WIDTH), LRU_CONV ** -0.5),
        "lru_conv_b": nrm(ks[24], (N_ODD, LRU_WIDTH), 0.02),
        "lru_lam": jnp.log(lru_p) - jnp.log1p(-lru_p),
        "lru_w_a": nrm(ks[26], (N_ODD, 2, LRU_BLOCKS, LRU_BLOCK, LRU_BLOCK), LRU_BLOCK ** -0.5),
        "lru_b_a": nrm(ks[27], (N_ODD, 2, LRU_WIDTH), 0.02),
        "lru_w_x": nrm(ks[28], (N_ODD, 2, LRU_BLOCKS, LRU_BLOCK, LRU_BLOCK), LRU_BLOCK ** -0.5),
        "lru_b_x": nrm(ks[29], (N_ODD, 2, LRU_WIDTH), 0.02),
        "gqa_q_norm": 1.0 + nrm(ks[30], (N_ODD, HEAD_DIM), 0.05),
        "gqa_k_norm": 1.0 + nrm(ks[31], (N_ODD, HEAD_DIM), 0.05),
    }


def reference(x, c, ctx, c_ctx, ada_w, ada_b, pre_g, post_g,
              ev_w_in, ev_w_out, s5_lam_re, s5_lam_im, s5_log_dt, s5_b_re, s5_b_im, s5_c_re, s5_c_im,
              s5_d, s5_w_glu, s5_b_glu, na_rel_bias,
              od_w_in, od_w_out, lru_conv_w, lru_conv_b, lru_lam, lru_w_a, lru_b_a, lru_w_x, lru_b_x,
              gqa_q_norm, gqa_k_norm):
    h_lat, h_ctx = x, ctx
    for i in range(DEPTH):
        j = i // 2
        last = i == DEPTH - 1
        sh_l, sc_l, gt_l = jnp.split(jax.nn.silu(c) @ ada_w[i] + ada_b[i], 3, axis=-1)
        sh_c, sc_c, gt_c = jnp.split(jax.nn.silu(c_ctx) @ ada_w[i] + ada_b[i], 3, axis=-1)
        xl = rms_norm(h_lat, pre_g[i]) * (1.0 + sc_l[:, None]) + sh_l[:, None]
        xc = rms_norm(h_ctx, pre_g[i]) * (1.0 + sc_c) + sh_c
        if i % 2 == 0:
            out_l, out_c = even_layer(xl, xc, ev_w_in[j], ev_w_out[j], s5_lam_re[j], s5_lam_im[j], s5_log_dt[j],
                                      s5_b_re[j], s5_b_im[j], s5_c_re[j], s5_c_im[j], s5_d[j], s5_w_glu[j],
                                      s5_b_glu[j], na_rel_bias[j], not last)
        else:
            out_l, out_c = odd_layer(xl, xc, od_w_in[j], od_w_out[j], lru_conv_w[j], lru_conv_b[j], lru_lam[j],
                                     lru_w_a[j], lru_b_a[j], lru_w_x[j], lru_b_x[j], gqa_q_norm[j], gqa_k_norm[j],
                                     not last)
        h_lat = h_lat + gt_l[:, None] * rms_norm(out_l, post_g[i])
        if not last:
            h_ctx = h_ctx + gt_c * rms_norm(out_c, post_g[i])
    return h_lat
```

```python
import functools
import math

import numpy as np
import jax
import jax.numpy as jnp
from jax import lax
from jax.experimental import pallas as pl
from jax.experimental.pallas import tpu as pltpu

F32 = jnp.float32
BF16 = jnp.bfloat16
HIGHEST = lax.Precision.HIGHEST

D_MODEL = 1024
BATCH = 4
SEQ = 4096
GRID_W = 64
CTX_LEN = 256
LT = CTX_LEN + SEQ
HEAD_DIM = 64
EPS = 1e-6
S5_WIDTH = 512
S5_GROUP = 16
S5_GROUPS = 32
S5_STATE = 64
NA_HEADS = 8
NA_ROWS = 8
NA_COLS = 16
LRU_WIDTH = 512
LRU_BLOCKS = 8
LRU_BLOCK = 64
LRU_C = 8.0
GQA_HEADS = 8
ROPE_THETA = 10000.0
EVEN_IN = 3072
ODD_IN = 2944

TOK_BLK = 256
SCAN_BLK = 64
N_SCAN = LT // SCAN_BLK
CTX_SCAN = CTX_LEN // SCAN_BLK
Q_BLK = 128
NEG = -1e30
VMEM_LIMIT = 56 * 1024 * 1024


def _cparams(sem):
    return pltpu.CompilerParams(dimension_semantics=sem, vmem_limit_bytes=VMEM_LIMIT)


def _dot(a, b):
    return jnp.dot(a, b, preferred_element_type=F32)


def _dot_nt(a, b):
    return lax.dot_general(a, b, (((1,), (1,)), ((), ())), preferred_element_type=F32)


def _split_dot(p, y):
    hi = y.astype(BF16)
    lo = (y - hi.astype(F32)).astype(BF16)
    return _dot(p, hi) + _dot(p, lo)


def _adaln_kernel(c_ref, w_ref, b_ref, o_ref):
    c = c_ref[...]
    s = c * jax.nn.sigmoid(c)
    o_ref[0] = jnp.dot(s, w_ref[0], preferred_element_type=F32, precision=HIGHEST) + b_ref[0]


def _adaln(c8, ada_w, ada_b):
    depth = ada_w.shape[0]
    nb = 3 * D_MODEL // 1024
    return pl.pallas_call(
        _adaln_kernel,
        grid=(depth, nb),
        in_specs=[
            pl.BlockSpec((8, D_MODEL), lambda i, n: (0, 0)),
            pl.BlockSpec((1, D_MODEL, 1024), lambda i, n: (i, 0, n)),
            pl.BlockSpec((1, 1, 1024), lambda i, n: (i, 0, n)),
        ],
        out_specs=pl.BlockSpec((1, 8, 1024), lambda i, n: (i, 0, n)),
        out_shape=jax.ShapeDtypeStruct((depth, 8, 3 * D_MODEL), F32),
        compiler_params=_cparams(("arbitrary", "arbitrary")),
        name="adaln",
    )(c8, ada_w, ada_b.reshape(depth, 1, 3 * D_MODEL))


def _inproj_kernel(x_ref, mod_ref, g_ref, w_ref, o_ref):
    b = pl.program_id(0)
    tb = pl.program_id(1)
    x = x_ref[0]
    r = lax.rsqrt(jnp.mean(x * x, axis=-1, keepdims=True) + EPS)
    row = jnp.where(tb == 0, BATCH, b)
    m = mod_ref[pl.ds(row, 1), :]
    sh = m[:, :D_MODEL]
    sc = m[:, D_MODEL:2 * D_MODEL]
    y = (x * r) * g_ref[...]
    y = y * (1.0 + sc) + sh
    o_ref[0] = _dot(y.astype(BF16), w_ref[...]).astype(BF16)


def _inproj(h_cat, mod, g, w_bf):
    n = w_bf.shape[1]
    return pl.pallas_call(
        _inproj_kernel,
        grid=(BATCH, LT // TOK_BLK),
        in_specs=[
            pl.BlockSpec((1, TOK_BLK, D_MODEL), lambda b, t: (b, t, 0)),
            pl.BlockSpec((8, 3 * D_MODEL), lambda b, t: (0, 0)),
            pl.BlockSpec((1, D_MODEL), lambda b, t: (0, 0)),
            pl.BlockSpec((D_MODEL, n), lambda b, t: (0, 0)),
        ],
        out_specs=pl.BlockSpec((1, TOK_BLK, n), lambda b, t: (b, t, 0)),
        out_shape=jax.ShapeDtypeStruct((BATCH, LT, n), BF16),
        compiler_params=_cparams(("arbitrary", "arbitrary")),
        name="inproj",
    )(h_cat, mod, g.reshape(1, D_MODEL), w_bf)


def _fwd_blk(i):
    return i


def _rev_blk(i):
    return jnp.where(i < CTX_SCAN, CTX_SCAN - 1 - i, N_SCAN + CTX_SCAN - 1 - i)


def _scan_perms():
    t = SCAN_BLK
    pf = np.zeros((8 * t, BATCH * t), np.float32)
    pr = np.zeros((8 * t, BATCH * t), np.float32)
    for tt in range(t):
        for b in range(BATCH):
            pf[tt * 8 + b, b * t + tt] = 1.0
            pr[tt * 8 + 4 + b, b * t + (t - 1 - tt)] = 1.0
    return pf, pr


def _conv_perms():
    t = SCAN_BLK
    n_src = BATCH * (16 + t + 16)
    sf = np.zeros((4, 8 * t, n_src), np.float32)
    sr = np.zeros((4, 8 * t, n_src), np.float32)

    def src_row(b, tau):
        if tau < 0:
            return b * 16 + 16 + tau
        if tau < t:
            return BATCH * 16 + b * t + tau
        return BATCH * 16 + BATCH * t + b * 16 + (tau - t)

    for k in range(4):
        for tt in range(t):
            for b in range(BATCH):
                sf[k, tt * 8 + b, src_row(b, tt + k - 1)] = 1.0
                sr[k, tt * 8 + 4 + b, src_row(b, (t - 1 - tt) + k - 1)] = 1.0
    return sf, sr


def _s5_kernel(uf_ref, ur_ref, pf_ref, pr_ref, pft_ref, prt_ref, bcat_ref, ccat_ref, lre_ref, lim_ref,
               yf_ref, yr_ref, bu_ref, st_ref):
    i = pl.program_id(0)

    @pl.when(i == 0)
    def _():
        st_ref[...] = jnp.zeros_like(st_ref)

    nrow = 8 * SCAN_BLK
    uf = uf_ref[...].reshape(BATCH * SCAN_BLK, S5_WIDTH)
    ur = ur_ref[...].reshape(BATCH * SCAN_BLK, S5_WIDTH)
    u_f = _dot(pf_ref[...], uf).astype(BF16)
    u_r = _dot(pr_ref[...], ur).astype(BF16)
    for j in range(4):
        lhs = jnp.concatenate([u_f[:, j * 128:(j + 1) * 128], u_r[:, j * 128:(j + 1) * 128]], axis=1)
        bu_ref[:, j * 1024:(j + 1) * 1024] = _dot(lhs, bcat_ref[j])

    for j in range(4):
        c_re = slice(j * 1024, j * 1024 + 512)
        c_im = slice(j * 1024 + 512, (j + 1) * 1024)
        lre = lre_ref[j]
        lim = lim_ref[j]

        def body(t, carry, c_re=c_re, c_im=c_im, lre=lre, lim=lim):
            hre, him = carry
            r0 = pl.multiple_of(t * 8, 8)
            bre = bu_ref[pl.ds(r0, 8), c_re]
            bim = bu_ref[pl.ds(r0, 8), c_im]
            nre = lre * hre - lim * him + bre
            nim = lre * him + lim * hre + bim
            bu_ref[pl.ds(r0, 8), c_re] = nre
            bu_ref[pl.ds(r0, 8), c_im] = nim
            return nre, nim

        hre, him = lax.fori_loop(0, SCAN_BLK, body, (st_ref[:, c_re], st_ref[:, c_im]), unroll=4)
        st_ref[:, c_re] = hre
        st_ref[:, c_im] = him

    is_f = (lax.broadcasted_iota(jnp.int32, (nrow, 128), 0) & 7) < 4
    ys = []
    for j in range(4):
        hj = bu_ref[:, j * 1024:(j + 1) * 1024].astype(BF16)
        yj = _dot(hj, ccat_ref[j])
        ys.append(jnp.where(is_f, yj[:, :128], yj[:, 128:]))
    yp = jnp.concatenate(ys, axis=1)
    yf_ref[...] = _split_dot(pft_ref[...], yp).reshape(BATCH, SCAN_BLK, S5_WIDTH)
    yr_ref[...] = _split_dot(prt_ref[...], yp).reshape(BATCH, SCAN_BLK, S5_WIDTH)


def _s5_weights(lam_re, lam_im, log_dt, b_re, b_im, c_re, c_im):
    eye8 = jnp.eye(8, dtype=F32)
    bcat, ccat, lre8, lim8 = [], [], [], []
    for d in range(2):
        lam = lax.complex(lam_re[d].astype(F32), lam_im[d].astype(F32))
        dt = jnp.exp(log_dt[d].astype(F32))[:, None]
        lam_bar = jnp.exp(lam * dt)
        b = lax.complex(b_re[d].astype(F32), b_im[d].astype(F32))
        b_bar = ((lam_bar - 1.0) / lam)[..., None] * b
        parts = []
        for comp in (jnp.real(b_bar), jnp.imag(b_bar)):
            c4 = comp.reshape(4, 8, S5_STATE, S5_GROUP)
            parts.append(jnp.einsum('ag,jgph->jahgp', eye8, c4).reshape(4, 128, 512))
        bcat.append(jnp.concatenate(parts, axis=2))
        cparts = []
        for comp in (c_re[d].astype(F32), -c_im[d].astype(F32)):
            c4 = comp.reshape(4, 8, S5_GROUP, S5_STATE)
            cparts.append(jnp.einsum('ag,jghp->jgpah', eye8, c4).reshape(4, 512, 128))
        ccat.append(jnp.concatenate(cparts, axis=1))
        lre8.append(jnp.broadcast_to(jnp.real(lam_bar).reshape(4, 1, 512), (4, 4, 512)))
        lim8.append(jnp.broadcast_to(jnp.imag(lam_bar).reshape(4, 1, 512), (4, 4, 512)))
    return (jnp.concatenate(bcat, axis=1).astype(BF16),
            jnp.concatenate(ccat, axis=2).astype(BF16),
            jnp.concatenate(lre8, axis=1), jnp.concatenate(lim8, axis=1))


def _s5(p, pf, pr, pft, prt, bcat, ccat, lre8, lim8):
    nrow = 8 * SCAN_BLK
    const2 = lambda i: (0, 0)
    const3 = lambda i: (0, 0, 0)
    return pl.pallas_call(
        _s5_kernel,
        grid=(N_SCAN,),
        in_specs=[
            pl.BlockSpec((BATCH, SCAN_BLK, S5_WIDTH), lambda i: (0, _fwd_blk(i), 0)),
            pl.BlockSpec((BATCH, SCAN_BLK, S5_WIDTH), lambda i: (0, _rev_blk(i), 0)),
            pl.BlockSpec(pf.shape, const2), pl.BlockSpec(pr.shape, const2),
            pl.BlockSpec(pft.shape, const2), pl.BlockSpec(prt.shape, const2),
            pl.BlockSpec(bcat.shape, const3), pl.BlockSpec(ccat.shape, const3),
            pl.BlockSpec(lre8.shape, const3), pl.BlockSpec(lim8.shape, const3),
        ],
        out_specs=[
            pl.BlockSpec((BATCH, SCAN_BLK, S5_WIDTH), lambda i: (0, _fwd_blk(i), 0)),
            pl.BlockSpec((BATCH, SCAN_BLK, S5_WIDTH), lambda i: (0, _rev_blk(i), 0)),
        ],
        out_shape=[jax.ShapeDtypeStruct((BATCH, LT, S5_WIDTH), F32)] * 2,
        scratch_shapes=[pltpu.VMEM((nrow, 4096), F32), pltpu.VMEM((8, 4096), F32)],
        compiler_params=_cparams(("arbitrary",)),
        name="s5_scan",
    )(p, p, pf, pr, pft, prt, bcat, ccat, lre8, lim8)


def _na_bias_table(rel_bias):
    w = np.arange(GRID_W)
    cs = np.clip(w - NA_COLS // 2, 0, GRID_W - NA_COLS)
    cp = np.arange(GRID_W)
    valid = (cp[None, :] >= cs[:, None]) & (cp[None, :] < cs[:, None] + NA_COLS)
    dc = np.clip(cp[None, :] - w[:, None] + (NA_COLS - 1), 0, 2 * NA_COLS - 2)
    dr = np.arange(8)[:, None] + np.arange(NA_ROWS)[None, :]
    t = rel_bias.astype(F32)[:, dr][:, :, :, dc]
    t = jnp.where(jnp.asarray(valid)[None, None, None], t, NEG)
    t = t.transpose(1, 0, 3, 2, 4).reshape(8, NA_HEADS // 2, 128, NA_ROWS * GRID_W)
    return jnp.concatenate([t, jnp.full((1,) + t.shape[1:], NEG, F32)], axis=0)


def _na_kernel(q_ref, k_ref, v_ref, bias_ref, o_ref):
    s_idx = pl.program_id(1)
    r = s_idx - CTX_SCAN
    start = jnp.clip(r - NA_ROWS // 2, 0, SEQ // GRID_W - NA_ROWS)
    koff = pl.multiple_of(CTX_LEN + start * GRID_W, GRID_W)
    lo = lax.broadcasted_iota(jnp.int32, (GRID_W, 128), 1) < HEAD_DIM
    q = q_ref[0]
    nwin = NA_ROWS * GRID_W
    for hp in range(NA_HEADS // 2):
        cols = slice(hp * 128, (hp + 1) * 128)
        qp = q[:, cols] * jnp.asarray(HEAD_DIM ** -0.5, BF16)
        zero = jnp.zeros_like(qp)
        q2 = jnp.concatenate([jnp.where(lo, qp, zero), jnp.where(lo, zero, qp)], axis=0)
        kw = k_ref[0, pl.ds(koff, nwin), cols]
        kc = k_ref[0, 0:CTX_LEN, cols]
        s_loc = _dot_nt(q2, kw) + bias_ref[0, hp]
        s_ctx = _dot_nt(q2, kc)
        m = jnp.maximum(jnp.max(s_loc, axis=-1, keepdims=True), jnp.max(s_ctx, axis=-1, keepdims=True))
        p_loc = jnp.exp(s_loc - m)
        p_ctx = jnp.exp(s_ctx - m)
        l = jnp.sum(p_loc, axis=-1, keepdims=True) + jnp.sum(p_ctx, axis=-1, keepdims=True)
        vw = v_ref[0, pl.ds(koff, nwin), cols]
        vc = v_ref[0, 0:CTX_LEN, cols]
        o = (_dot(p_loc.astype(BF16), vw) + _dot(p_ctx.astype(BF16), vc)) / l
        o_ref[0, :, cols] = jnp.where(lo, o[:GRID_W], o[GRID_W:]).astype(BF16)


def _na(p, bias_tb):
    def bias_idx(b, s):
        r = s - CTX_SCAN
        start = jnp.clip(r - NA_ROWS // 2, 0, SEQ // GRID_W - NA_ROWS)
        return (jnp.where(s < CTX_SCAN, 8, start - r + NA_ROWS - 1), 0, 0, 0)

    return pl.pallas_call(
        _na_kernel,
        grid=(BATCH, N_SCAN),
        in_specs=[
            pl.BlockSpec((1, GRID_W, 512), lambda b, s: (b, s, 2)),
            pl.BlockSpec((1, LT, 512), lambda b, s: (b, 0, 3)),
            pl.BlockSpec((1, LT, 512), lambda b, s: (b, 0, 4)),
            pl.BlockSpec((1, NA_HEADS // 2, 128, NA_ROWS * GRID_W), bias_idx),
        ],
        out_specs=pl.BlockSpec((1, GRID_W, 512), lambda b, s: (b, s, 0)),
        out_shape=jax.ShapeDtypeStruct((BATCH, LT, 512), BF16),
        compiler_params=_cparams(("arbitrary", "arbitrary")),
        name="na_attn",
    )(p, p, p, bias_tb)


def _gelu_tanh(x):
    return 0.5 * x * (1.0 + jnp.tanh(math.sqrt(2.0 / math.pi) * (x + 0.044715 * (x * x * x))))


def _silu(x):
    return x * jax.nn.sigmoid(x)


def _finish(h, out, gt, pg):
    r = lax.rsqrt(jnp.mean(out * out, axis=-1, keepdims=True) + EPS)
    return h + gt * ((out * r) * pg)


def _even_out_kernel(yf_ref, yr_ref, u_ref, ga_ref, gb_ref, at_ref, h_ref, mod_ref, d_ref, wg_ref, bg_ref,
                     wo_ref, pg_ref, o_ref):
    b = pl.program_id(0)
    tb = pl.program_id(1)
    row = jnp.where(tb == 0, BATCH, b)
    gt = mod_ref[pl.ds(row, 1), 2 * D_MODEL:]
    u = u_ref[0].astype(F32)
    y = d_ref[...] * u + yf_ref[0] + yr_ref[0]
    y = _gelu_tanh(y)
    y = y * jax.nn.sigmoid(_dot(y.astype(BF16), wg_ref[...]) + bg_ref[...])
    y_a = y * _silu(ga_ref[0].astype(F32))
    y_b = at_ref[0].astype(F32) * _silu(gb_ref[0].astype(F32))
    out = _dot(y_a.astype(BF16), wo_ref[0:512, :]) + _dot(y_b.astype(BF16), wo_ref[512:1024, :])
    o_ref[0] = _finish(h_ref[0], out, gt, pg_ref[...])


def _even_out(yf, yr, p, attn, h_cat, mod, d_skip, w_glu, b_glu, w_out, post_g):
    tok = lambda c: pl.BlockSpec((1, TOK_BLK, 512), lambda b, t: (b, t, c))
    full = lambda a: pl.BlockSpec(a.shape, lambda b, t: (0,) * a.ndim)
    d2, bg2, pg2 = d_skip.reshape(1, 512), b_glu.reshape(1, 512), post_g.reshape(1, D_MODEL)
    return pl.pallas_call(
        _even_out_kernel,
        grid=(BATCH, LT // TOK_BLK),
        in_specs=[tok(0), tok(0), tok(0), tok(1), tok(5), tok(0),
                  pl.BlockSpec((1, TOK_BLK, D_MODEL), lambda b, t: (b, t, 0)),
                  full(mod), full(d2), full(w_glu), full(bg2), full(w_out), full(pg2)],
        out_specs=pl.BlockSpec((1, TOK_BLK, D_MODEL), lambda b, t: (b, t, 0)),
        out_shape=jax.ShapeDtypeStruct((BATCH, LT, D_MODEL), F32),
        compiler_params=_cparams(("arbitrary", "arbitrary")),
        name="even_out",
    )(yf, yr, p, p, p, attn, h_cat, mod, d2, w_glu, bg2, w_out, pg2)


def _softplus(z):
    return jnp.maximum(z, 0.0) + jnp.log1p(jnp.exp(-jnp.abs(z)))


def _lru_kernel(xf_ref, xfp_ref, xfn_ref, xr_ref, xrp_ref, xrn_ref, sf_ref, sr_ref, pft_ref, prt_ref,
                cw_ref, cb_ref, wg_ref, bg_ref, lam_ref, hf_ref, hr_ref, a_ref, b_ref, st_ref):
    i = pl.program_id(0)

    @pl.when(i == 0)
    def _():
        st_ref[...] = jnp.zeros_like(st_ref)

    nrow = 8 * SCAN_BLK

    def ext(cur_ref, prev_ref, next_ref, blk):
        prev_ok = jnp.logical_and(blk != 0, blk != CTX_SCAN)
        next_ok = jnp.logical_and(blk != CTX_SCAN - 1, blk != N_SCAN - 1)
        pv = prev_ref[...].reshape(BATCH * 16, LRU_WIDTH)
        nx = next_ref[...].reshape(BATCH * 16, LRU_WIDTH)
        pv = jnp.where(prev_ok, pv, jnp.zeros_like(pv))
        nx = jnp.where(next_ok, nx, jnp.zeros_like(nx))
        return jnp.concatenate([pv, cur_ref[...].reshape(BATCH * SCAN_BLK, LRU_WIDTH), nx], axis=0)

    xef = ext(xf_ref, xfp_ref, xfn_ref, _fwd_blk(i))
    xer = ext(xr_ref, xrp_ref, xrn_ref, _rev_blk(i))
    xc = jnp.zeros((nrow, LRU_WIDTH), F32) + cb_ref[...]
    for k in range(4):
        xc = xc + cw_ref[k] * (_dot(sf_ref[k], xef) + _dot(sr_ref[k], xer))

    is_f = (lax.broadcasted_iota(jnp.int32, (nrow, 1), 0) & 7) < 4
    xcb = xc.astype(BF16)
    zero = jnp.zeros_like(xcb)
    lhs = jnp.concatenate([jnp.where(is_f, xcb, zero), jnp.where(is_f, zero, xcb)], axis=1)
    pre = _dot(lhs, wg_ref[...]) + jnp.where(is_f, bg_ref[0:1, :], bg_ref[1:2, :])
    gate_r = jax.nn.sigmoid(pre[:, :LRU_WIDTH])
    gate_i = jax.nn.sigmoid(pre[:, LRU_WIDTH:])
    sp = _softplus(-lam_ref[...])
    log_a = (-LRU_C) * gate_r * jnp.where(is_f, sp[0:1, :], sp[1:2, :])
    a = jnp.exp(log_a)
    a_ref[...] = a
    one_m_a2 = -jnp.tanh(log_a) * (a * a + 1.0)
    b_ref[...] = jnp.sqrt(jnp.maximum(one_m_a2, 0.0)) * gate_i * xc

    def body(t, h):
        r0 = pl.multiple_of(t * 8, 8)
        h = a_ref[pl.ds(r0, 8), :] * h + b_ref[pl.ds(r0, 8), :]
        b_ref[pl.ds(r0, 8), :] = h
        return h

    st_ref[...] = lax.fori_loop(0, SCAN_BLK, body, st_ref[...], unroll=4)
    hs = b_ref[...]
    hf_ref[...] = _split_dot(pft_ref[...], hs).reshape(BATCH, SCAN_BLK, LRU_WIDTH)
    hr_ref[...] = _split_dot(prt_ref[...], hs).reshape(BATCH, SCAN_BLK, LRU_WIDTH)


def _lru(p, sf, sr, pft, prt, conv_w, conv_b, wg, bg, lam):
    nrow = 8 * SCAN_BLK
    full = lambda a: pl.BlockSpec(a.shape, lambda i: (0,) * a.ndim)
    cur = lambda f: pl.BlockSpec((BATCH, SCAN_BLK, LRU_WIDTH), lambda i: (0, f(i), 0))
    prev = lambda f: pl.BlockSpec((BATCH, 16, LRU_WIDTH), lambda i: (0, jnp.maximum(f(i) * 4 - 1, 0), 0))
    nxt = lambda f: pl.BlockSpec((BATCH, 16, LRU_WIDTH),
                                 lambda i: (0, jnp.minimum(f(i) * 4 + 4, LT // 16 - 1), 0))
    cw3 = conv_w.reshape(4, 1, LRU_WIDTH)
    cb2 = conv_b.reshape(1, LRU_WIDTH)
    return pl.pallas_call(
        _lru_kernel,
        grid=(N_SCAN,),
        in_specs=[cur(_fwd_blk), prev(_fwd_blk), nxt(_fwd_blk), cur(_rev_blk), prev(_rev_blk), nxt(_rev_blk),
                  full(sf), full(sr), full(pft), full(prt), full(cw3), full(cb2), full(wg), full(bg), full(lam)],
        out_specs=[
            pl.BlockSpec((BATCH, SCAN_BLK, LRU_WIDTH), lambda i: (0, _fwd_blk(i), 0)),
            pl.BlockSpec((BATCH, SCAN_BLK, LRU_WIDTH), lambda i: (0, _rev_blk(i), 0)),
        ],
        out_shape=[jax.ShapeDtypeStruct((BATCH, LT, LRU_WIDTH), F32)] * 2,
        scratch_shapes=[pltpu.VMEM((nrow, LRU_WIDTH), F32), pltpu.VMEM((nrow, LRU_WIDTH), F32),
                        pltpu.VMEM((8, LRU_WIDTH), F32)],
        compiler_params=_cparams(("arbitrary",)),
        name="lru_scan",
    )(p, p, p, p, p, p, sf, sr, pft, prt, cw3, cb2, wg, bg, lam)


def _norm_rope(x, xrot, g, grot, cos, sin, ones_blk):
    x = x.astype(F32)
    xrot = xrot.astype(F32)
    ms = _dot((x * x).astype(BF16), ones_blk) * (1.0 / HEAD_DIM)
    rs = lax.rsqrt(ms + EPS)
    return rs * ((x * g) * cos + (xrot * grot) * sin)


def _gqa_kernel(q_ref, qr_ref, k_ref, kr_ref, v_ref, cq_ref, sq_ref, ck_ref, sk_ref, g_ref, ones_ref,
                o_ref, kn_ref):
    qb = pl.program_id(1)
    ones_blk = ones_ref[...]

    @pl.when(qb == 0)
    def _():
        kn = _norm_rope(k_ref[0], kr_ref[0], g_ref[2:3, :], g_ref[3:4, :], ck_ref[...], sk_ref[...], ones_blk)
        kn_ref[...] = kn.astype(BF16)

    lo = lax.broadcasted_iota(jnp.int32, (Q_BLK, 128), 1) < HEAD_DIM
    kn = kn_ref[...]
    v = v_ref[0]
    cos = cq_ref[...]
    sin = sq_ref[...]
    for pb in range(GQA_HEADS // 2):
        cols = slice(pb * 128, (pb + 1) * 128)
        qn = _norm_rope(q_ref[0, :, cols], qr_ref[0, :, cols], g_ref[0:1, :], g_ref[1:2, :], cos, sin, ones_blk)
        qn = (qn * HEAD_DIM ** -0.5).astype(BF16)
        zero = jnp.zeros_like(qn)
        qs = jnp.concatenate([jnp.where(lo, qn, zero), jnp.where(lo, zero, qn)], axis=0)
        s = _dot_nt(qs, kn)
        m = jnp.max(s, axis=-1, keepdims=True)
        p = jnp.exp(s - m)
        l = jnp.sum(p, axis=-1, keepdims=True)
        o = _dot(p.astype(BF16), v) / l
        o_ref[0, :, cols] = jnp.where(lo, o[:Q_BLK], o[Q_BLK:]).astype(BF16)


def _gqa(p, cos128, sin128, gvec, ones_blk):
    nqb = SEQ // Q_BLK
    cb = CTX_LEN // Q_BLK
    return pl.pallas_call(
        _gqa_kernel,
        grid=(BATCH, nqb),
        in_specs=[
            pl.BlockSpec((1, Q_BLK, 512), lambda b, q: (b, q + cb, 2)),
            pl.BlockSpec((1, Q_BLK, 512), lambda b, q: (b, q + cb, 3)),
            pl.BlockSpec((1, LT, 128), lambda b, q: (b, 0, 20)),
            pl.BlockSpec((1, LT, 128), lambda b, q: (b, 0, 21)),
            pl.BlockSpec((1, LT, 128), lambda b, q: (b, 0, 22)),
            pl.BlockSpec((Q_BLK, 128), lambda b, q: (q + cb, 0)),
            pl.BlockSpec((Q_BLK, 128), lambda b, q: (q + cb, 0)),
            pl.BlockSpec((LT, 128), lambda b, q: (0, 0)),
            pl.BlockSpec((LT, 128), lambda b, q: (0, 0)),
            pl.BlockSpec((8, 128), lambda b, q: (0, 0)),
            pl.BlockSpec((128, 128), lambda b, q: (0, 0)),
        ],
        out_specs=pl.BlockSpec((1, Q_BLK, 512), lambda b, q: (b, q, 0)),
        out_shape=jax.ShapeDtypeStruct((BATCH, SEQ, 512), BF16),
        scratch_shapes=[pltpu.VMEM((LT, 128), BF16)],
        compiler_params=_cparams(("arbitrary", "arbitrary")),
        name="gqa_attn",
    )(p, p, p, p, p, cos128, sin128, cos128, sin128, gvec, ones_blk)


def _odd_out_kernel(hf_ref, hr_ref, gc_ref, gd_ref, at_ref, h_ref, mod_ref, wo_ref, pg_ref, o_ref):
    b = pl.program_id(0)
    gt = mod_ref[pl.ds(b, 1), 2 * D_MODEL:]
    y_c = (hf_ref[0] + hr_ref[0]) * _silu(gc_ref[0].astype(F32))
    y_d = at_ref[0].astype(F32) * _silu(gd_ref[0].astype(F32))
    out = _dot(y_c.astype(BF16), wo_ref[0:512, :]) + _dot(y_d.astype(BF16), wo_ref[512:1024, :])
    o_ref[0] = _finish(h_ref[0], out, gt, pg_ref[...])


def _odd_out(hf, hr, p, attn, h_cat, mod, w_out, post_g):
    cb = CTX_LEN // TOK_BLK
    cat = lambda c: pl.BlockSpec((1, TOK_BLK, 512), lambda b, t: (b, t + cb, c))
    full = lambda a: pl.BlockSpec(a.shape, lambda b, t: (0,) * a.ndim)
    pg2 = post_g.reshape(1, D_MODEL)
    return pl.pallas_call(
        _odd_out_kernel,
        grid=(BATCH, SEQ // TOK_BLK),
        in_specs=[cat(0), cat(0), cat(1), cat(4),
                  pl.BlockSpec((1, TOK_BLK, 512), lambda b, t: (b, t, 0)),
                  pl.BlockSpec((1, TOK_BLK, D_MODEL), lambda b, t: (b, t + cb, 0)),
                  full(mod), full(w_out), full(pg2)],
        out_specs=pl.BlockSpec((1, TOK_BLK, D_MODEL), lambda b, t: (b, t, 0)),
        out_shape=jax.ShapeDtypeStruct((BATCH, SEQ, D_MODEL), F32),
        compiler_params=_cparams(("arbitrary", "arbitrary")),
        name="odd_out",
    )(hf, hr, p, p, attn, h_cat, mod, w_out, pg2)


_HEAD_ORDER = (0, 4, 1, 5, 2, 6, 3, 7)


def _pair_swap(w):
    w2 = w.reshape(w.shape[0], -1, 2)
    return jnp.stack([-w2[..., 1], w2[..., 0]], axis=-1).reshape(w.shape)


def _odd_w_in(w):
    perm = np.concatenate([np.arange(HEAD_DIM) + h * HEAD_DIM for h in _HEAD_ORDER])
    x, gc = w[:, 0:512], w[:, 512:1024]
    q = w[:, 1024:1536][:, perm]
    k, v = w[:, 1536:1664], w[:, 1664:1792]
    gd = w[:, 1792:2304][:, perm]
    return jnp.concatenate([x, gc, q, _pair_swap(q), gd, k, _pair_swap(k), v], axis=1), perm


def _rope_tables():
    t = np.arange(SEQ)
    row = (t // GRID_W).astype(np.float32)
    col = (t % GRID_W).astype(np.float32)
    half = HEAD_DIM // 2
    inv = (ROPE_THETA ** (-np.arange(0, half, 2, dtype=np.float32) / half)).astype(np.float32)
    ang = np.concatenate([row[:, None] * inv, col[:, None] * inv], axis=-1)
    cos = np.repeat(np.cos(ang), 2, axis=-1)
    sin = np.repeat(np.sin(ang), 2, axis=-1)
    cos = np.concatenate([np.ones((CTX_LEN, HEAD_DIM), np.float32), cos], axis=0)
    sin = np.concatenate([np.zeros((CTX_LEN, HEAD_DIM), np.float32), sin], axis=0)
    return np.tile(cos, (1, 2)).astype(np.float32), np.tile(sin, (1, 2)).astype(np.float32)


def _swap_pairs_vec(g):
    g2 = g.reshape(-1, 2)
    return jnp.stack([g2[:, 1], g2[:, 0]], axis=-1).reshape(g.shape)


def _block_diag(w):
    n, a, b = w.shape
    return jnp.einsum('mn,nij->minj', jnp.eye(n, dtype=w.dtype), w).reshape(n * a, n * b)


def kernel(x, c, ctx, c_ctx, ada_w, ada_b, pre_g, post_g, ev_w_in, ev_w_out, s5_lam_re, s5_lam_im, s5_log_dt,
           s5_b_re, s5_b_im, s5_c_re, s5_c_im, s5_d, s5_w_glu, s5_b_glu, na_rel_bias, od_w_in, od_w_out,
           lru_conv_w, lru_conv_b, lru_lam, lru_w_a, lru_b_a, lru_w_x, lru_b_x, gqa_q_norm, gqa_k_norm):
    pf_np, pr_np = _scan_perms()
    sf_np, sr_np = _conv_perms()
    pf, pr = jnp.asarray(pf_np, BF16), jnp.asarray(pr_np, BF16)
    pft, prt = jnp.asarray(pf_np.T, BF16), jnp.asarray(pr_np.T, BF16)
    sf, sr = jnp.asarray(sf_np, BF16), jnp.asarray(sr_np, BF16)

    c8 = jnp.concatenate([c, c_ctx[None], jnp.zeros((3, D_MODEL), F32)], axis=0)
    mod = _adaln(c8, ada_w, ada_b)
    h0 = jnp.concatenate([ctx, x], axis=1)

    p0 = _inproj(h0, mod[0], pre_g[0], ev_w_in[0].astype(BF16))
    bcat, ccat, lre8, lim8 = _s5_weights(s5_lam_re[0], s5_lam_im[0], s5_log_dt[0], s5_b_re[0], s5_b_im[0],
                                         s5_c_re[0], s5_c_im[0])
    yf, yr = _s5(p0, pf, pr, pft, prt, bcat, ccat, lre8, lim8)
    attn0 = _na(p0, _na_bias_table(na_rel_bias[0]))
    h1 = _even_out(yf, yr, p0, attn0, h0, mod[0], s5_d[0], s5_w_glu[0].astype(BF16), s5_b_glu[0],
                   ev_w_out[0].astype(BF16), post_g[0])

    w_in1, perm = _odd_w_in(od_w_in[0])
    p1 = _inproj(h1, mod[1], pre_g[1], w_in1.astype(BF16))
    wg = jnp.concatenate([
        jnp.concatenate([_block_diag(lru_w_a[0, 0]), _block_diag(lru_w_x[0, 0])], axis=1),
        jnp.concatenate([_block_diag(lru_w_a[0, 1]), _block_diag(lru_w_x[0, 1])], axis=1)], axis=0).astype(BF16)
    bg = jnp.concatenate([lru_b_a[0], lru_b_x[0]], axis=1)
    hf, hr = _lru(p1, sf, sr, pft, prt, lru_conv_w[0], lru_conv_b[0], wg, bg, lru_lam[0])
    cos_np, sin_np = _rope_tables()
    gq, gk = gqa_q_norm[0], gqa_k_norm[0]
    gvec = jnp.stack([jnp.tile(gq, 2), jnp.tile(_swap_pairs_vec(gq), 2),
                      jnp.tile(gk, 2), jnp.tile(_swap_pairs_vec(gk), 2)] + [jnp.zeros((128,), F32)] * 4)
    ones_np = np.kron(np.eye(2, dtype=np.float32), np.ones((HEAD_DIM, HEAD_DIM), np.float32))
    attn1 = _gqa(p1, jnp.asarray(cos_np), jnp.asarray(sin_np), gvec, jnp.asarray(ones_np, BF16))
    w_out1 = jnp.concatenate([od_w_out[0][:512], od_w_out[0][512:][perm]], axis=0).astype(BF16)
    return _odd_out(hf, hr, p1, attn1, h1, mod[1], w_out1, post_g[1])
```

```python
import functools
import math

import numpy as np
import jax
import jax.numpy as jnp
from jax import lax
from jax.experimental import pallas as pl
from jax.experimental.pallas import tpu as pltpu

F32 = jnp.float32
BF16 = jnp.bfloat16
HIGHEST = lax.Precision.HIGHEST

D_MODEL = 1024
BATCH = 4
SEQ = 4096
GRID_W = 64
CTX_LEN = 256
LT = CTX_LEN + SEQ
HEAD_DIM = 64
EPS = 1e-6
S5_WIDTH = 512
S5_GROUP = 16
S5_GROUPS = 32
S5_STATE = 64
NA_HEADS = 8
NA_ROWS = 8
NA_COLS = 16
LRU_WIDTH = 512
LRU_BLOCKS = 8
LRU_BLOCK = 64
LRU_C = 8.0
GQA_HEADS = 8
ROPE_THETA = 10000.0
EVEN_IN = 3072
ODD_IN = 2944

TOK_BLK = 256
SCAN_BLK = 64
N_SCAN = LT // SCAN_BLK
CTX_SCAN = CTX_LEN // SCAN_BLK
Q_BLK = 128
NEG = -1e30
LOG2E = math.log2(math.e)
VMEM_LIMIT = 56 * 1024 * 1024


def _cparams(sem):
    return pltpu.CompilerParams(dimension_semantics=sem, vmem_limit_bytes=VMEM_LIMIT)


def _dot(a, b):
    return jnp.dot(a, b, preferred_element_type=F32)


def _dot_nt(a, b):
    return lax.dot_general(a, b, (((1,), (1,)), ((), ())), preferred_element_type=F32)


def _adaln_kernel(c_ref, w_ref, b_ref, o_ref):
    c = c_ref[...]
    s = c * jax.nn.sigmoid(c)
    o_ref[0] = jnp.dot(s, w_ref[0], preferred_element_type=F32, precision=HIGHEST) + b_ref[0]


def _adaln(c8, ada_w, ada_b):
    depth = ada_w.shape[0]
    nb = 3 * D_MODEL // 1024
    return pl.pallas_call(
        _adaln_kernel,
        grid=(depth, nb),
        in_specs=[
            pl.BlockSpec((8, D_MODEL), lambda i, n: (0, 0)),
            pl.BlockSpec((1, D_MODEL, 1024), lambda i, n: (i, 0, n)),
            pl.BlockSpec((1, 1, 1024), lambda i, n: (i, 0, n)),
        ],
        out_specs=pl.BlockSpec((1, 8, 1024), lambda i, n: (i, 0, n)),
        out_shape=jax.ShapeDtypeStruct((depth, 8, 3 * D_MODEL), F32),
        compiler_params=_cparams(("arbitrary", "arbitrary")),
        name="adaln",
    )(c8, ada_w, ada_b.reshape(depth, 1, 3 * D_MODEL))


def _cat_specs(width):
    return [pl.BlockSpec((1, TOK_BLK, width), lambda b, t: (b, 0, 0)),
            pl.BlockSpec((1, TOK_BLK, width), lambda b, t: (b, jnp.maximum(t - 1, 0), 0))]


def _inproj_kernel(*refs, two_src):
    b = pl.program_id(0)
    tb = pl.program_id(1)
    if two_src:
        c_ref, x_ref, mod_ref, g_ref, w_ref, o_ref = refs
        x = jnp.where(tb == 0, c_ref[0], x_ref[0])
    else:
        x_ref, mod_ref, g_ref, w_ref, o_ref = refs
        x = x_ref[0]
    r = lax.rsqrt(jnp.mean(x * x, axis=-1, keepdims=True) + EPS)
    row = jnp.where(tb == 0, BATCH, b)
    m = mod_ref[pl.ds(row, 1), :]
    sh = m[:, :D_MODEL]
    sc = m[:, D_MODEL:2 * D_MODEL]
    y = (x * r) * g_ref[...]
    y = y * (1.0 + sc) + sh
    o_ref[0] = _dot(y.astype(BF16), w_ref[...]).astype(BF16)


def _inproj(src, mod, g, w_bf):
    n = w_bf.shape[1]
    two_src = isinstance(src, tuple)
    if two_src:
        src_specs = _cat_specs(D_MODEL)
    else:
        src_specs = [pl.BlockSpec((1, TOK_BLK, D_MODEL), lambda b, t: (b, t, 0))]
        src = (src,)
    return pl.pallas_call(
        functools.partial(_inproj_kernel, two_src=two_src),
        grid=(BATCH, LT // TOK_BLK),
        in_specs=src_specs + [
            pl.BlockSpec((8, 3 * D_MODEL), lambda b, t: (0, 0)),
            pl.BlockSpec((1, D_MODEL), lambda b, t: (0, 0)),
            pl.BlockSpec((D_MODEL, n), lambda b, t: (0, 0)),
        ],
        out_specs=pl.BlockSpec((1, TOK_BLK, n), lambda b, t: (b, t, 0)),
        out_shape=jax.ShapeDtypeStruct((BATCH, LT, n), BF16),
        compiler_params=_cparams(("arbitrary", "arbitrary")),
        name="inproj",
    )(*src, mod, g.reshape(1, D_MODEL), w_bf)


def _fwd_blk(i):
    return i


def _rev_blk(i):
    return jnp.where(i < CTX_SCAN, CTX_SCAN - 1 - i, N_SCAN + CTX_SCAN - 1 - i)


def _scan_perms():
    t = SCAN_BLK
    pf = np.zeros((8 * t, BATCH * t), np.float32)
    pr = np.zeros((8 * t, BATCH * t), np.float32)
    for tt in range(t):
        for b in range(BATCH):
            pf[tt * 8 + b, b * t + tt] = 1.0
            pr[tt * 8 + 4 + b, b * t + (t - 1 - tt)] = 1.0
    return pf, pr


def _conv_perms():
    t = SCAN_BLK
    n_src = BATCH * (16 + t + 16)
    sf = np.zeros((4, 8 * t, n_src), np.float32)
    sr = np.zeros((4, 8 * t, n_src), np.float32)

    def src_row(b, tau):
        if tau < 0:
            return b * 16 + 16 + tau
        if tau < t:
            return BATCH * 16 + b * t + tau
        return BATCH * 16 + BATCH * t + b * 16 + (tau - t)

    for k in range(4):
        for tt in range(t):
            for b in range(BATCH):
                sf[k, tt * 8 + b, src_row(b, tt + k - 1)] = 1.0
                sr[k, tt * 8 + 4 + b, src_row(b, (t - 1 - tt) + k - 1)] = 1.0
    return sf, sr


def _s5_kernel(uf_ref, ur_ref, pf_ref, pr_ref, pft_ref, prt_ref, bcat_ref, ccat_ref, lre_ref, lim_ref,
               yf_ref, yr_ref, bu_ref, st_ref):
    i = pl.program_id(0)

    @pl.when(i == 0)
    def _():
        st_ref[...] = jnp.zeros_like(st_ref)

    nrow = 8 * SCAN_BLK
    uf = uf_ref[...].reshape(BATCH * SCAN_BLK, S5_WIDTH)
    ur = ur_ref[...].reshape(BATCH * SCAN_BLK, S5_WIDTH)
    u_f = _dot(pf_ref[...], uf).astype(BF16)
    u_r = _dot(pr_ref[...], ur).astype(BF16)
    for j in range(4):
        lhs = jnp.concatenate([u_f[:, j * 128:(j + 1) * 128], u_r[:, j * 128:(j + 1) * 128]], axis=1)
        bu_ref[:, j * 1024:(j + 1) * 1024] = _dot(lhs, bcat_ref[j])

    for j in range(4):
        c_re = slice(j * 1024, j * 1024 + 512)
        c_im = slice(j * 1024 + 512, (j + 1) * 1024)
        lre = lre_ref[j]
        lim = lim_ref[j]

        def body(t, carry, c_re=c_re, c_im=c_im, lre=lre, lim=lim):
            hre, him = carry
            r0 = pl.multiple_of(t * 8, 8)
            bre = bu_ref[pl.ds(r0, 8), c_re]
            bim = bu_ref[pl.ds(r0, 8), c_im]
            nre = lre * hre - lim * him + bre
            nim = lre * him + lim * hre + bim
            bu_ref[pl.ds(r0, 8), c_re] = nre
            bu_ref[pl.ds(r0, 8), c_im] = nim
            return nre, nim

        hre, him = lax.fori_loop(0, SCAN_BLK, body, (st_ref[:, c_re], st_ref[:, c_im]), unroll=4)
        st_ref[:, c_re] = hre
        st_ref[:, c_im] = him

    is_f = (lax.broadcasted_iota(jnp.int32, (nrow, 128), 0) & 7) < 4
    ys = []
    for j in range(4):
        hj = bu_ref[:, j * 1024:(j + 1) * 1024].astype(BF16)
        yj = _dot(hj, ccat_ref[j])
        ys.append(jnp.where(is_f, yj[:, :128], yj[:, 128:]))
    yp = jnp.concatenate(ys, axis=1).astype(BF16)
    yf_ref[...] = _dot(pft_ref[...], yp).astype(BF16).reshape(BATCH, SCAN_BLK, S5_WIDTH)
    yr_ref[...] = _dot(prt_ref[...], yp).astype(BF16).reshape(BATCH, SCAN_BLK, S5_WIDTH)


def _s5_weights(lam_re, lam_im, log_dt, b_re, b_im, c_re, c_im):
    eye8 = jnp.eye(8, dtype=F32)
    bcat, ccat, lre8, lim8 = [], [], [], []
    for d in range(2):
        lam = lax.complex(lam_re[d].astype(F32), lam_im[d].astype(F32))
        dt = jnp.exp(log_dt[d].astype(F32))[:, None]
        lam_bar = jnp.exp(lam * dt)
        b = lax.complex(b_re[d].astype(F32), b_im[d].astype(F32))
        b_bar = ((lam_bar - 1.0) / lam)[..., None] * b
        parts = []
        for comp in (jnp.real(b_bar), jnp.imag(b_bar)):
            c4 = comp.reshape(4, 8, S5_STATE, S5_GROUP)
            parts.append(jnp.einsum('ag,jgph->jahgp', eye8, c4).reshape(4, 128, 512))
        bcat.append(jnp.concatenate(parts, axis=2))
        cparts = []
        for comp in (c_re[d].astype(F32), -c_im[d].astype(F32)):
            c4 = comp.reshape(4, 8, S5_GROUP, S5_STATE)
            cparts.append(jnp.einsum('ag,jghp->jgpah', eye8, c4).reshape(4, 512, 128))
        ccat.append(jnp.concatenate(cparts, axis=1))
        lre8.append(jnp.broadcast_to(jnp.real(lam_bar).reshape(4, 1, 512), (4, 4, 512)))
        lim8.append(jnp.broadcast_to(jnp.imag(lam_bar).reshape(4, 1, 512), (4, 4, 512)))
    return (jnp.concatenate(bcat, axis=1).astype(BF16),
            jnp.concatenate(ccat, axis=2).astype(BF16),
            jnp.concatenate(lre8, axis=1), jnp.concatenate(lim8, axis=1))


def _s5(p, pf, pr, pft, prt, bcat, ccat, lre8, lim8):
    nrow = 8 * SCAN_BLK
    const2 = lambda i: (0, 0)
    const3 = lambda i: (0, 0, 0)
    return pl.pallas_call(
        _s5_kernel,
        grid=(N_SCAN,),
        in_specs=[
            pl.BlockSpec((BATCH, SCAN_BLK, S5_WIDTH), lambda i: (0, _fwd_blk(i), 0)),
            pl.BlockSpec((BATCH, SCAN_BLK, S5_WIDTH), lambda i: (0, _rev_blk(i), 0)),
            pl.BlockSpec(pf.shape, const2), pl.BlockSpec(pr.shape, const2),
            pl.BlockSpec(pft.shape, const2), pl.BlockSpec(prt.shape, const2),
            pl.BlockSpec(bcat.shape, const3), pl.BlockSpec(ccat.shape, const3),
            pl.BlockSpec(lre8.shape, const3), pl.BlockSpec(lim8.shape, const3),
        ],
        out_specs=[
            pl.BlockSpec((BATCH, SCAN_BLK, S5_WIDTH), lambda i: (0, _fwd_blk(i), 0)),
            pl.BlockSpec((BATCH, SCAN_BLK, S5_WIDTH), lambda i: (0, _rev_blk(i), 0)),
        ],
        out_shape=[jax.ShapeDtypeStruct((BATCH, LT, S5_WIDTH), BF16)] * 2,
        scratch_shapes=[pltpu.VMEM((nrow, 4096), F32), pltpu.VMEM((8, 4096), F32)],
        compiler_params=_cparams(("arbitrary",)),
        name="s5_scan",
    )(p, p, pf, pr, pft, prt, bcat, ccat, lre8, lim8)


def _na_bias_table(rel_bias):
    w = np.arange(GRID_W)
    cs = np.clip(w - NA_COLS // 2, 0, GRID_W - NA_COLS)
    cp = np.arange(GRID_W)
    valid = (cp[None, :] >= cs[:, None]) & (cp[None, :] < cs[:, None] + NA_COLS)
    dc = cp[None, :] - w[:, None] + (NA_COLS - 1)
    n_dc = 2 * NA_COLS - 1
    onehot = ((dc[None] == np.arange(n_dc)[:, None, None]) & valid[None]).astype(np.float32)
    band = jnp.einsum('hrd,dwc->hrwc', rel_bias.astype(F32), jnp.asarray(onehot), precision=HIGHEST)
    band = band + jnp.asarray(np.where(valid, 0.0, NEG).astype(np.float32))
    tabs = [band[:, off:off + NA_ROWS].transpose(0, 2, 1, 3).reshape(NA_HEADS // 2, 128, NA_ROWS * GRID_W)
            for off in range(8)]
    tabs.append(jnp.full_like(tabs[0], NEG))
    return jnp.stack(tabs)


NA_STEP_ROWS = TOK_BLK // GRID_W


def _na_kernel(q_ref, k_ref, v_ref, bias_ref, o_ref):
    s_idx = pl.program_id(1)
    is_ctx = s_idx == 0
    lo = lax.broadcasted_iota(jnp.int32, (GRID_W, 128), 1) < HEAD_DIM
    nwin = NA_ROWS * GRID_W
    for rr in range(NA_STEP_ROWS):
        r = jnp.maximum(s_idx - 1, 0) * NA_STEP_ROWS + rr
        start = jnp.clip(r - NA_ROWS // 2, 0, SEQ // GRID_W - NA_ROWS)
        koff = pl.multiple_of(CTX_LEN + start * GRID_W, GRID_W)
        off = jnp.where(is_ctx, 8, start - r + NA_ROWS - 1)
        rows = slice(rr * GRID_W, (rr + 1) * GRID_W)
        for hp in range(NA_HEADS // 2):
            cols = slice(hp * 128, (hp + 1) * 128)
            qp = q_ref[0, rows, cols] * jnp.asarray(HEAD_DIM ** -0.5, BF16)
            zero = jnp.zeros_like(qp)
            q2 = jnp.concatenate([jnp.where(lo, qp, zero), jnp.where(lo, zero, qp)], axis=0)
            kw = k_ref[0, pl.ds(koff, nwin), cols]
            kc = k_ref[0, 0:CTX_LEN, cols]
            s_loc = _dot_nt(q2, kw) + bias_ref[off, hp]
            s_ctx = _dot_nt(q2, kc)
            m = jnp.maximum(jnp.max(s_loc, axis=-1, keepdims=True), jnp.max(s_ctx, axis=-1, keepdims=True))
            p_loc = jnp.exp(s_loc - m)
            p_ctx = jnp.exp(s_ctx - m)
            l = jnp.sum(p_loc, axis=-1, keepdims=True) + jnp.sum(p_ctx, axis=-1, keepdims=True)
            vw = v_ref[0, pl.ds(koff, nwin), cols]
            vc = v_ref[0, 0:CTX_LEN, cols]
            o = (_dot(p_loc.astype(BF16), vw) + _dot(p_ctx.astype(BF16), vc)) / l
            o_ref[0, rows, cols] = jnp.where(lo, o[:GRID_W], o[GRID_W:]).astype(BF16)


def _na(p, bias_tb):
    return pl.pallas_call(
        _na_kernel,
        grid=(BATCH, LT // TOK_BLK),
        in_specs=[
            pl.BlockSpec((1, TOK_BLK, 512), lambda b, s: (b, s, 2)),
            pl.BlockSpec((1, LT, 512), lambda b, s: (b, 0, 3)),
            pl.BlockSpec((1, LT, 512), lambda b, s: (b, 0, 4)),
            pl.BlockSpec(bias_tb.shape, lambda b, s: (0, 0, 0, 0)),
        ],
        out_specs=pl.BlockSpec((1, TOK_BLK, 512), lambda b, s: (b, s, 0)),
        out_shape=jax.ShapeDtypeStruct((BATCH, LT, 512), BF16),
        compiler_params=_cparams(("arbitrary", "arbitrary")),
        name="na_attn",
    )(p, p, p, bias_tb)


def _gelu_tanh(x):
    return 0.5 * x * (1.0 + jnp.tanh(math.sqrt(2.0 / math.pi) * (x + 0.044715 * (x * x * x))))


def _silu(x):
    return x * jax.nn.sigmoid(x)


def _finish(h, out, gt, pg):
    r = lax.rsqrt(jnp.mean(out * out, axis=-1, keepdims=True) + EPS)
    return h + gt * ((out * r) * pg)


def _even_out_kernel(yf_ref, yr_ref, u_ref, ga_ref, gb_ref, at_ref, hc_ref, hl_ref, mod_ref, d_ref, wg_ref,
                     bg_ref, wo_ref, pg_ref, o_ref):
    b = pl.program_id(0)
    tb = pl.program_id(1)
    row = jnp.where(tb == 0, BATCH, b)
    gt = mod_ref[pl.ds(row, 1), 2 * D_MODEL:]
    u = u_ref[0].astype(F32)
    y = d_ref[...] * u + yf_ref[0].astype(F32) + yr_ref[0].astype(F32)
    y = _gelu_tanh(y)
    y = y * jax.nn.sigmoid(_dot(y.astype(BF16), wg_ref[...]) + bg_ref[...])
    y_a = y * _silu(ga_ref[0].astype(F32))
    y_b = at_ref[0].astype(F32) * _silu(gb_ref[0].astype(F32))
    out = _dot(y_a.astype(BF16), wo_ref[0:512, :]) + _dot(y_b.astype(BF16), wo_ref[512:1024, :])
    h = jnp.where(tb == 0, hc_ref[0], hl_ref[0])
    o_ref[0] = _finish(h, out, gt, pg_ref[...])


def _even_out(yf, yr, p, attn, ctx, x, mod, d_skip, w_glu, b_glu, w_out, post_g):
    tok = lambda c: pl.BlockSpec((1, TOK_BLK, 512), lambda b, t: (b, t, c))
    full = lambda a: pl.BlockSpec(a.shape, lambda b, t: (0,) * a.ndim)
    d2, bg2, pg2 = d_skip.reshape(1, 512), b_glu.reshape(1, 512), post_g.reshape(1, D_MODEL)
    return pl.pallas_call(
        _even_out_kernel,
        grid=(BATCH, LT // TOK_BLK),
        in_specs=[tok(0), tok(0), tok(0), tok(1), tok(5), tok(0)] + _cat_specs(D_MODEL) +
                 [full(mod), full(d2), full(w_glu), full(bg2), full(w_out), full(pg2)],
        out_specs=pl.BlockSpec((1, TOK_BLK, D_MODEL), lambda b, t: (b, t, 0)),
        out_shape=jax.ShapeDtypeStruct((BATCH, LT, D_MODEL), F32),
        compiler_params=_cparams(("arbitrary", "arbitrary")),
        name="even_out",
    )(yf, yr, p, p, p, attn, ctx, x, mod, d2, w_glu, bg2, w_out, pg2)


def _softplus(z):
    return jnp.maximum(z, 0.0) + jnp.log1p(jnp.exp(-jnp.abs(z)))


def _lru_kernel(xf_ref, xfp_ref, xfn_ref, xr_ref, xrp_ref, xrn_ref, sf_ref, sr_ref, pft_ref, prt_ref,
                cw_ref, cb_ref, wg_ref, bg_ref, lam_ref, hf_ref, hr_ref, a_ref, b_ref, st_ref):
    i = pl.program_id(0)

    @pl.when(i == 0)
    def _():
        st_ref[...] = jnp.zeros_like(st_ref)

    nrow = 8 * SCAN_BLK

    def ext(cur_ref, prev_ref, next_ref, blk):
        prev_ok = jnp.logical_and(blk != 0, blk != CTX_SCAN)
        next_ok = jnp.logical_and(blk != CTX_SCAN - 1, blk != N_SCAN - 1)
        pv = prev_ref[...].reshape(BATCH * 16, LRU_WIDTH)
        nx = next_ref[...].reshape(BATCH * 16, LRU_WIDTH)
        pv = jnp.where(prev_ok, pv, jnp.zeros_like(pv))
        nx = jnp.where(next_ok, nx, jnp.zeros_like(nx))
        return jnp.concatenate([pv, cur_ref[...].reshape(BATCH * SCAN_BLK, LRU_WIDTH), nx], axis=0)

    xef = ext(xf_ref, xfp_ref, xfn_ref, _fwd_blk(i))
    xer = ext(xr_ref, xrp_ref, xrn_ref, _rev_blk(i))
    xc = jnp.zeros((nrow, LRU_WIDTH), F32) + cb_ref[...]
    for k in range(4):
        xc = xc + cw_ref[k] * (_dot(sf_ref[k], xef) + _dot(sr_ref[k], xer))

    is_f = (lax.broadcasted_iota(jnp.int32, (nrow, 1), 0) & 7) < 4
    xcb = xc.astype(BF16)
    zero = jnp.zeros_like(xcb)
    lhs = jnp.concatenate([jnp.where(is_f, xcb, zero), jnp.where(is_f, zero, xcb)], axis=1)
    pre = _dot(lhs, wg_ref[...]) + jnp.where(is_f, bg_ref[0:1, :], bg_ref[1:2, :])
    gate_r = jax.nn.sigmoid(pre[:, :LRU_WIDTH])
    gate_i = jax.nn.sigmoid(pre[:, LRU_WIDTH:])
    sp = _softplus(-lam_ref[...])
    log_a = (-LRU_C) * gate_r * jnp.where(is_f, sp[0:1, :], sp[1:2, :])
    a = jnp.exp(log_a)
    a_ref[...] = a
    one_m_a2 = -jnp.tanh(log_a) * (a * a + 1.0)
    b_ref[...] = jnp.sqrt(jnp.maximum(one_m_a2, 0.0)) * gate_i * xc

    def body(t, h):
        r0 = pl.multiple_of(t * 8, 8)
        h = a_ref[pl.ds(r0, 8), :] * h + b_ref[pl.ds(r0, 8), :]
        b_ref[pl.ds(r0, 8), :] = h
        return h

    st_ref[...] = lax.fori_loop(0, SCAN_BLK, body, st_ref[...], unroll=4)
    hs = b_ref[...].astype(BF16)
    hf_ref[...] = _dot(pft_ref[...], hs).astype(BF16).reshape(BATCH, SCAN_BLK, LRU_WIDTH)
    hr_ref[...] = _dot(prt_ref[...], hs).astype(BF16).reshape(BATCH, SCAN_BLK, LRU_WIDTH)


def _lru(p, sf, sr, pft, prt, conv_w, conv_b, wg, bg, lam):
    nrow = 8 * SCAN_BLK
    full = lambda a: pl.BlockSpec(a.shape, lambda i: (0,) * a.ndim)
    cur = lambda f: pl.BlockSpec((BATCH, SCAN_BLK, LRU_WIDTH), lambda i: (0, f(i), 0))
    prev = lambda f: pl.BlockSpec((BATCH, 16, LRU_WIDTH), lambda i: (0, jnp.maximum(f(i) * 4 - 1, 0), 0))
    nxt = lambda f: pl.BlockSpec((BATCH, 16, LRU_WIDTH),
                                 lambda i: (0, jnp.minimum(f(i) * 4 + 4, LT // 16 - 1), 0))
    cw3 = conv_w.reshape(4, 1, LRU_WIDTH)
    cb2 = conv_b.reshape(1, LRU_WIDTH)
    return pl.pallas_call(
        _lru_kernel,
        grid=(N_SCAN,),
        in_specs=[cur(_fwd_blk), prev(_fwd_blk), nxt(_fwd_blk), cur(_rev_blk), prev(_rev_blk), nxt(_rev_blk),
                  full(sf), full(sr), full(pft), full(prt), full(cw3), full(cb2), full(wg), full(bg), full(lam)],
        out_specs=[
            pl.BlockSpec((BATCH, SCAN_BLK, LRU_WIDTH), lambda i: (0, _fwd_blk(i), 0)),
            pl.BlockSpec((BATCH, SCAN_BLK, LRU_WIDTH), lambda i: (0, _rev_blk(i), 0)),
        ],
        out_shape=[jax.ShapeDtypeStruct((BATCH, LT, LRU_WIDTH), BF16)] * 2,
        scratch_shapes=[pltpu.VMEM((nrow, LRU_WIDTH), F32), pltpu.VMEM((nrow, LRU_WIDTH), F32),
                        pltpu.VMEM((8, LRU_WIDTH), F32)],
        compiler_params=_cparams(("arbitrary",)),
        name="lru_scan",
    )(p, p, p, p, p, p, sf, sr, pft, prt, cw3, cb2, wg, bg, lam)


def _norm_rope(x, xrot, g, grot, cos, sin, ones_blk):
    x = x.astype(F32)
    xrot = xrot.astype(F32)
    ms = _dot((x * x).astype(BF16), ones_blk) * (1.0 / HEAD_DIM)
    rs = lax.rsqrt(ms + EPS)
    return rs * ((x * g) * cos + (xrot * grot) * sin)


def _gqa_kernel(q_ref, qr_ref, k_ref, kr_ref, v_ref, cq_ref, sq_ref, ck_ref, sk_ref, g_ref, ones_ref,
                o_ref, kn_ref, vlo_ref, vhi_ref):
    qb = pl.program_id(1)
    ones_blk = ones_ref[...]

    @pl.when(qb == 0)
    def _():
        kn = _norm_rope(k_ref[0], kr_ref[0], g_ref[2:3, :], g_ref[3:4, :], ck_ref[...], sk_ref[...], ones_blk)
        kn_ref[...] = kn.astype(BF16)
        v = v_ref[0]
        lo_k = lax.broadcasted_iota(jnp.int32, (LT, 128), 1) < HEAD_DIM
        one = jnp.ones_like(v)
        vlo_ref[...] = jnp.where(lo_k, v, one)
        vhi_ref[...] = jnp.where(lo_k, one, v)

    lo = lax.broadcasted_iota(jnp.int32, (Q_BLK, 128), 1) < HEAD_DIM
    kn = kn_ref[...]
    cos = cq_ref[...]
    sin = sq_ref[...]
    for pb in range(GQA_HEADS // 2):
        cols = slice(pb * 128, (pb + 1) * 128)
        qn = _norm_rope(q_ref[0, :, cols], qr_ref[0, :, cols], g_ref[0:1, :], g_ref[1:2, :], cos, sin, ones_blk)
        qn = (qn * (HEAD_DIM ** -0.5 * LOG2E)).astype(BF16)
        zero = jnp.zeros_like(qn)
        qs = jnp.concatenate([jnp.where(lo, qn, zero), jnp.where(lo, zero, qn)], axis=0)
        s = _dot_nt(qs, kn)
        m = jnp.max(s, axis=-1, keepdims=True)
        p = jnp.exp2(s - m).astype(BF16)
        o_lo = _dot(p[:Q_BLK], vlo_ref[...])
        o_hi = _dot(p[Q_BLK:], vhi_ref[...])
        num = jnp.where(lo, o_lo, o_hi)
        den = pltpu.roll(jnp.where(lo, o_hi, o_lo), HEAD_DIM, axis=1)
        o_ref[0, :, cols] = (num / den).astype(BF16)


def _gqa(p, cos128, sin128, gvec, ones_blk):
    nqb = SEQ // Q_BLK
    cb = CTX_LEN // Q_BLK
    return pl.pallas_call(
        _gqa_kernel,
        grid=(BATCH, nqb),
        in_specs=[
            pl.BlockSpec((1, Q_BLK, 512), lambda b, q: (b, q + cb, 2)),
            pl.BlockSpec((1, Q_BLK, 512), lambda b, q: (b, q + cb, 3)),
            pl.BlockSpec((1, LT, 128), lambda b, q: (b, 0, 20)),
            pl.BlockSpec((1, LT, 128), lambda b, q: (b, 0, 21)),
            pl.BlockSpec((1, LT, 128), lambda b, q: (b, 0, 22)),
            pl.BlockSpec((Q_BLK, 128), lambda b, q: (q + cb, 0)),
            pl.BlockSpec((Q_BLK, 128), lambda b, q: (q + cb, 0)),
            pl.BlockSpec((LT, 128), lambda b, q: (0, 0)),
            pl.BlockSpec((LT, 128), lambda b, q: (0, 0)),
            pl.BlockSpec((8, 128), lambda b, q: (0, 0)),
            pl.BlockSpec((128, 128), lambda b, q: (0, 0)),
        ],
        out_specs=pl.BlockSpec((1, Q_BLK, 512), lambda b, q: (b, q, 0)),
        out_shape=jax.ShapeDtypeStruct((BATCH, SEQ, 512), BF16),
        scratch_shapes=[pltpu.VMEM((LT, 128), BF16)] * 3,
        compiler_params=_cparams(("arbitrary", "arbitrary")),
        name="gqa_attn",
    )(p, p, p, p, p, cos128, sin128, cos128, sin128, gvec, ones_blk)


def _odd_out_kernel(hf_ref, hr_ref, gc_ref, gd_ref, at_ref, h_ref, mod_ref, wo_ref, pg_ref, o_ref):
    b = pl.program_id(0)
    gt = mod_ref[pl.ds(b, 1), 2 * D_MODEL:]
    y_c = (hf_ref[0].astype(F32) + hr_ref[0].astype(F32)) * _silu(gc_ref[0].astype(F32))
    y_d = at_ref[0].astype(F32) * _silu(gd_ref[0].astype(F32))
    out = _dot(y_c.astype(BF16), wo_ref[0:512, :]) + _dot(y_d.astype(BF16), wo_ref[512:1024, :])
    o_ref[0] = _finish(h_ref[0], out, gt, pg_ref[...])


def _odd_out(hf, hr, p, attn, h_cat, mod, w_out, post_g):
    cb = CTX_LEN // TOK_BLK
    cat = lambda c: pl.BlockSpec((1, TOK_BLK, 512), lambda b, t: (b, t + cb, c))
    full = lambda a: pl.BlockSpec(a.shape, lambda b, t: (0,) * a.ndim)
    pg2 = post_g.reshape(1, D_MODEL)
    return pl.pallas_call(
        _odd_out_kernel,
        grid=(BATCH, SEQ // TOK_BLK),
        in_specs=[cat(0), cat(0), cat(1), cat(4),
                  pl.BlockSpec((1, TOK_BLK, 512), lambda b, t: (b, t, 0)),
                  pl.BlockSpec((1, TOK_BLK, D_MODEL), lambda b, t: (b, t + cb, 0)),
                  full(mod), full(w_out), full(pg2)],
        out_specs=pl.BlockSpec((1, TOK_BLK, D_MODEL), lambda b, t: (b, t, 0)),
        out_shape=jax.ShapeDtypeStruct((BATCH, SEQ, D_MODEL), F32),
        compiler_params=_cparams(("arbitrary", "arbitrary")),
        name="odd_out",
    )(hf, hr, p, p, attn, h_cat, mod, w_out, pg2)


_HEAD_ORDER = (0, 4, 1, 5, 2, 6, 3, 7)


def _pair_swap(w):
    n = w.shape[1]
    r = np.zeros((n, n), np.float32)
    r[np.arange(1, n, 2), np.arange(0, n, 2)] = -1.0
    r[np.arange(0, n, 2), np.arange(1, n, 2)] = 1.0
    return jnp.dot(w, jnp.asarray(r, BF16), preferred_element_type=F32).astype(BF16)


def _interleave_kv_groups(w, axis):
    if axis == 0:
        return w.reshape(2, 4, HEAD_DIM, w.shape[1]).transpose(1, 0, 2, 3).reshape(w.shape)
    return w.reshape(w.shape[0], 2, 4, HEAD_DIM).transpose(0, 2, 1, 3).reshape(w.shape)


def _odd_w_in(w):
    w = w.astype(BF16)
    x, gc = w[:, 0:512], w[:, 512:1024]
    q = _interleave_kv_groups(w[:, 1024:1536], 1)
    k, v = w[:, 1536:1664], w[:, 1664:1792]
    gd = _interleave_kv_groups(w[:, 1792:2304], 1)
    return jnp.concatenate([x, gc, q, _pair_swap(q), gd, k, _pair_swap(k), v], axis=1)


def _rope_tables():
    t = np.arange(SEQ)
    row = (t // GRID_W).astype(np.float32)
    col = (t % GRID_W).astype(np.float32)
    half = HEAD_DIM // 2
    inv = (ROPE_THETA ** (-np.arange(0, half, 2, dtype=np.float32) / half)).astype(np.float32)
    ang = np.concatenate([row[:, None] * inv, col[:, None] * inv], axis=-1)
    cos = np.repeat(np.cos(ang), 2, axis=-1)
    sin = np.repeat(np.sin(ang), 2, axis=-1)
    cos = np.concatenate([np.ones((CTX_LEN, HEAD_DIM), np.float32), cos], axis=0)
    sin = np.concatenate([np.zeros((CTX_LEN, HEAD_DIM), np.float32), sin], axis=0)
    return np.tile(cos, (1, 2)).astype(np.float32), np.tile(sin, (1, 2)).astype(np.float32)


def _swap_pairs_vec(g):
    g2 = g.reshape(-1, 2)
    return jnp.stack([g2[:, 1], g2[:, 0]], axis=-1).reshape(g.shape)


def _block_diag(w):
    n, a, b = w.shape
    return jnp.einsum('mn,nij->minj', jnp.eye(n, dtype=w.dtype), w).reshape(n * a, n * b)


def kernel(x, c, ctx, c_ctx, ada_w, ada_b, pre_g, post_g, ev_w_in, ev_w_out, s5_lam_re, s5_lam_im, s5_log_dt,
           s5_b_re, s5_b_im, s5_c_re, s5_c_im, s5_d, s5_w_glu, s5_b_glu, na_rel_bias, od_w_in, od_w_out,
           lru_conv_w, lru_conv_b, lru_lam, lru_w_a, lru_b_a, lru_w_x, lru_b_x, gqa_q_norm, gqa_k_norm):
    pf_np, pr_np = _scan_perms()
    sf_np, sr_np = _conv_perms()
    pf, pr = jnp.asarray(pf_np, BF16), jnp.asarray(pr_np, BF16)
    pft, prt = jnp.asarray(pf_np.T, BF16), jnp.asarray(pr_np.T, BF16)
    sf, sr = jnp.asarray(sf_np, BF16), jnp.asarray(sr_np, BF16)

    c8 = jnp.concatenate([c, c_ctx[None], jnp.zeros((3, D_MODEL), F32)], axis=0)
    mod = _adaln(c8, ada_w, ada_b)

    p0 = _inproj((ctx, x), mod[0], pre_g[0], ev_w_in[0].astype(BF16))
    bcat, ccat, lre8, lim8 = _s5_weights(s5_lam_re[0], s5_lam_im[0], s5_log_dt[0], s5_b_re[0], s5_b_im[0],
                                         s5_c_re[0], s5_c_im[0])
    yf, yr = _s5(p0, pf, pr, pft, prt, bcat, ccat, lre8, lim8)
    attn0 = _na(p0, _na_bias_table(na_rel_bias[0]))
    h1 = _even_out(yf, yr, p0, attn0, ctx, x, mod[0], s5_d[0], s5_w_glu[0].astype(BF16), s5_b_glu[0],
                   ev_w_out[0].astype(BF16), post_g[0])

    p1 = _inproj(h1, mod[1], pre_g[1], _odd_w_in(od_w_in[0]))
    wg = jnp.concatenate([
        jnp.concatenate([_block_diag(lru_w_a[0, 0]), _block_diag(lru_w_x[0, 0])], axis=1),
        jnp.concatenate([_block_diag(lru_w_a[0, 1]), _block_diag(lru_w_x[0, 1])], axis=1)], axis=0).astype(BF16)
    bg = jnp.concatenate([lru_b_a[0], lru_b_x[0]], axis=1)
    hf, hr = _lru(p1, sf, sr, pft, prt, lru_conv_w[0], lru_conv_b[0], wg, bg, lru_lam[0])
    cos_np, sin_np = _rope_tables()
    gq, gk = gqa_q_norm[0], gqa_k_norm[0]
    gvec = jnp.stack([jnp.tile(gq, 2), jnp.tile(_swap_pairs_vec(gq), 2),
                      jnp.tile(gk, 2), jnp.tile(_swap_pairs_vec(gk), 2)] + [jnp.zeros((128,), F32)] * 4)
    ones_np = np.kron(np.eye(2, dtype=np.float32), np.ones((HEAD_DIM, HEAD_DIM), np.float32))
    attn1 = _gqa(p1, jnp.asarray(cos_np), jnp.asarray(sin_np), gvec, jnp.asarray(ones_np, BF16))
    w_out1 = od_w_out[0].astype(BF16)
    w_out1 = jnp.concatenate([w_out1[:512], _interleave_kv_groups(w_out1[512:], 0)], axis=0)
    return _odd_out(hf, hr, p1, attn1, h1, mod[1], w_out1, post_g[1])
```

```python
import functools
import math

import numpy as np
import jax
import jax.numpy as jnp
from jax import lax
from jax.experimental import pallas as pl
from jax.experimental.pallas import tpu as pltpu

F32 = jnp.float32
BF16 = jnp.bfloat16
HIGHEST = lax.Precision.HIGHEST

D_MODEL = 1024
BATCH = 4
SEQ = 4096
GRID_W = 64
CTX_LEN = 256
LT = CTX_LEN + SEQ
HEAD_DIM = 64
EPS = 1e-6
S5_WIDTH = 512
S5_GROUP = 16
S5_GROUPS = 32
S5_STATE = 64
NA_HEADS = 8
NA_ROWS = 8
NA_COLS = 16
LRU_WIDTH = 512
LRU_BLOCKS = 8
LRU_BLOCK = 64
LRU_C = 8.0
GQA_HEADS = 8
ROPE_THETA = 10000.0
EVEN_IN = 3072
ODD_IN = 2944

TOK_BLK = 256
SCAN_BLK = 64
N_SCAN = LT // SCAN_BLK
CTX_SCAN = CTX_LEN // SCAN_BLK
Q_BLK = 128
NEG = -1e30
LOG2E = math.log2(math.e)
VMEM_LIMIT = 56 * 1024 * 1024


def _cparams(sem):
    return pltpu.CompilerParams(dimension_semantics=sem, vmem_limit_bytes=VMEM_LIMIT)


def _dot(a, b):
    return jnp.dot(a, b, preferred_element_type=F32)


def _dot_nt(a, b):
    return lax.dot_general(a, b, (((1,), (1,)), ((), ())), preferred_element_type=F32)


def _adaln_kernel(c_ref, w_ref, b_ref, o_ref):
    c = c_ref[...]
    s = c * jax.nn.sigmoid(c)
    o_ref[0] = jnp.dot(s, w_ref[0], preferred_element_type=F32, precision=HIGHEST) + b_ref[0]


def _adaln(c8, ada_w, ada_b):
    depth = ada_w.shape[0]
    nb = 3 * D_MODEL // 1024
    return pl.pallas_call(
        _adaln_kernel,
        grid=(depth, nb),
        in_specs=[
            pl.BlockSpec((8, D_MODEL), lambda i, n: (0, 0)),
            pl.BlockSpec((1, D_MODEL, 1024), lambda i, n: (i, 0, n)),
            pl.BlockSpec((1, 1, 1024), lambda i, n: (i, 0, n)),
        ],
        out_specs=pl.BlockSpec((1, 8, 1024), lambda i, n: (i, 0, n)),
        out_shape=jax.ShapeDtypeStruct((depth, 8, 3 * D_MODEL), F32),
        compiler_params=_cparams(("arbitrary", "arbitrary")),
        name="adaln",
    )(c8, ada_w, ada_b.reshape(depth, 1, 3 * D_MODEL))


def _cat_specs(width):
    return [pl.BlockSpec((1, TOK_BLK, width), lambda b, t: (b, 0, 0)),
            pl.BlockSpec((1, TOK_BLK, width), lambda b, t: (b, jnp.maximum(t - 1, 0), 0))]


def _inproj_kernel(*refs, two_src):
    b = pl.program_id(0)
    tb = pl.program_id(1)
    if two_src:
        c_ref, x_ref, mod_ref, g_ref, w_ref, o_ref = refs
        x = jnp.where(tb == 0, c_ref[0], x_ref[0])
    else:
        x_ref, mod_ref, g_ref, w_ref, o_ref = refs
        x = x_ref[0]
    r = lax.rsqrt(jnp.mean(x * x, axis=-1, keepdims=True) + EPS)
    row = jnp.where(tb == 0, BATCH, b)
    m = mod_ref[pl.ds(row, 1), :]
    sh = m[:, :D_MODEL]
    sc = m[:, D_MODEL:2 * D_MODEL]
    y = (x * r) * g_ref[...]
    y = y * (1.0 + sc) + sh
    o_ref[0] = _dot(y.astype(BF16), w_ref[...]).astype(BF16)


def _inproj(src, mod, g, w_bf):
    n = w_bf.shape[1]
    two_src = isinstance(src, tuple)
    if two_src:
        src_specs = _cat_specs(D_MODEL)
    else:
        src_specs = [pl.BlockSpec((1, TOK_BLK, D_MODEL), lambda b, t: (b, t, 0))]
        src = (src,)
    return pl.pallas_call(
        functools.partial(_inproj_kernel, two_src=two_src),
        grid=(BATCH, LT // TOK_BLK),
        in_specs=src_specs + [
            pl.BlockSpec((8, 3 * D_MODEL), lambda b, t: (0, 0)),
            pl.BlockSpec((1, D_MODEL), lambda b, t: (0, 0)),
            pl.BlockSpec((D_MODEL, n), lambda b, t: (0, 0)),
        ],
        out_specs=pl.BlockSpec((1, TOK_BLK, n), lambda b, t: (b, t, 0)),
        out_shape=jax.ShapeDtypeStruct((BATCH, LT, n), BF16),
        compiler_params=_cparams(("arbitrary", "arbitrary")),
        name="inproj",
    )(*src, mod, g.reshape(1, D_MODEL), w_bf)


def _fwd_blk(i):
    return i


def _rev_blk(i):
    return jnp.where(i < CTX_SCAN, CTX_SCAN - 1 - i, N_SCAN + CTX_SCAN - 1 - i)


def _scan_perms():
    t = SCAN_BLK
    pf = np.zeros((8 * t, BATCH * t), np.float32)
    pr = np.zeros((8 * t, BATCH * t), np.float32)
    for tt in range(t):
        for b in range(BATCH):
            pf[tt * 8 + b, b * t + tt] = 1.0
            pr[tt * 8 + 4 + b, b * t + (t - 1 - tt)] = 1.0
    return pf, pr


def _halo_perm():
    ph = np.zeros((32, 16 * 4 * BATCH), np.float32)
    for b in range(BATCH):
        ph[1 * 8 + b, 0 * 64 + b * 16 + 15] = 1.0
        ph[2 * 8 + b, 1 * 64 + b * 16 + 0] = 1.0
        ph[3 * 8 + b, 1 * 64 + b * 16 + 1] = 1.0
        ph[0 * 8 + 4 + b, 3 * 64 + b * 16 + 1] = 1.0
        ph[1 * 8 + 4 + b, 3 * 64 + b * 16 + 0] = 1.0
        ph[2 * 8 + 4 + b, 2 * 64 + b * 16 + 15] = 1.0
    return ph


def _conv_coef(conv_w):
    zero = jnp.zeros_like(conv_w[0])
    rows = []
    for s in range(-2, 3):
        wf = conv_w[s + 1] if -1 <= s <= 2 else zero
        wr = conv_w[1 - s] if -2 <= s <= 1 else zero
        rows.append(jnp.concatenate([jnp.broadcast_to(wf, (4, LRU_WIDTH)), jnp.broadcast_to(wr, (4, LRU_WIDTH))]))
    return jnp.stack(rows)


def _s5_kernel(uf_ref, ur_ref, pf_ref, pr_ref, pft_ref, prt_ref, bcat_ref, ccat_ref, lre_ref, lim_ref,
               yf_ref, yr_ref, buf0_ref, buf1_ref, st_ref):
    i = pl.program_id(0)

    @pl.when(i == 0)
    def _():
        st_ref[...] = jnp.zeros_like(st_ref)
        buf0_ref[...] = jnp.zeros_like(buf0_ref)
        buf1_ref[...] = jnp.zeros_like(buf1_ref)

    nrow = 8 * SCAN_BLK

    def step(buf_a, buf_b):
        vals = {}

        def perm():
            uf = uf_ref[...].reshape(BATCH * SCAN_BLK, S5_WIDTH)
            ur = ur_ref[...].reshape(BATCH * SCAN_BLK, S5_WIDTH)
            vals['u_f'] = _dot(pf_ref[...], uf).astype(BF16)
            vals['u_r'] = _dot(pr_ref[...], ur).astype(BF16)

        def readout(j):
            is_f = (lax.broadcasted_iota(jnp.int32, (nrow, 128), 0) & 7) < 4
            yj = _dot(buf_a[:, j * 1024:(j + 1) * 1024].astype(BF16), ccat_ref[j])
            vals['y%d' % j] = jnp.where(is_f, yj[:, :128], yj[:, 128:]).astype(BF16)

        def project(j):
            lhs = jnp.concatenate([vals['u_f'][:, j * 128:(j + 1) * 128], vals['u_r'][:, j * 128:(j + 1) * 128]],
                                  axis=1)
            buf_a[:, j * 1024:(j + 1) * 1024] = _dot(lhs, bcat_ref[j])

        def unperm(p_ref, o_ref):
            yp = jnp.concatenate([vals['y%d' % j] for j in range(4)], axis=1)
            o_ref[...] = _dot(p_ref[...], yp).astype(BF16).reshape(BATCH, SCAN_BLK, S5_WIDTH)

        mxu = [perm]
        for j in range(4):
            mxu += [functools.partial(readout, j), functools.partial(project, j)]
        mxu += [functools.partial(unperm, pft_ref, yf_ref), functools.partial(unperm, prt_ref, yr_ref)]

        def scan(j, t0):
            c_re = slice(j * 1024, j * 1024 + 512)
            c_im = slice(j * 1024 + 512, (j + 1) * 1024)
            lre = lre_ref[j]
            lim = lim_ref[j]
            hre = st_ref[:, c_re]
            him = st_ref[:, c_im]
            for t in range(t0, t0 + 16):
                rows = slice(t * 8, (t + 1) * 8)
                nre = lre * hre - lim * him + buf_b[rows, c_re]
                nim = lre * him + lim * hre + buf_b[rows, c_im]
                buf_b[rows, c_re] = nre
                buf_b[rows, c_im] = nim
                hre, him = nre, nim
            st_ref[:, c_re] = hre
            st_ref[:, c_im] = him

        vpu = [functools.partial(scan, j, t0) for j in range(4) for t0 in range(0, SCAN_BLK, 16)]

        per_piece = [1, 2, 1, 2, 1, 2, 1, 2, 1, 2, 1]
        for piece, n_scan in zip(mxu, per_piece):
            piece()
            for _ in range(n_scan):
                vpu.pop(0)()

    @pl.when(i % 2 == 0)
    def _():
        step(buf0_ref, buf1_ref)

    @pl.when(i % 2 == 1)
    def _():
        step(buf1_ref, buf0_ref)


def _s5_weights(lam_re, lam_im, log_dt, b_re, b_im, c_re, c_im):
    eye8 = jnp.eye(8, dtype=F32)
    bcat, ccat, lre8, lim8 = [], [], [], []
    for d in range(2):
        lam = lax.complex(lam_re[d].astype(F32), lam_im[d].astype(F32))
        dt = jnp.exp(log_dt[d].astype(F32))[:, None]
        lam_bar = jnp.exp(lam * dt)
        b = lax.complex(b_re[d].astype(F32), b_im[d].astype(F32))
        b_bar = ((lam_bar - 1.0) / lam)[..., None] * b
        parts = []
        for comp in (jnp.real(b_bar), jnp.imag(b_bar)):
            c4 = comp.reshape(4, 8, S5_STATE, S5_GROUP)
            parts.append(jnp.einsum('ag,jgph->jahgp', eye8, c4).reshape(4, 128, 512))
        bcat.append(jnp.concatenate(parts, axis=2))
        cparts = []
        for comp in (c_re[d].astype(F32), -c_im[d].astype(F32)):
            c4 = comp.reshape(4, 8, S5_GROUP, S5_STATE)
            cparts.append(jnp.einsum('ag,jghp->jgpah', eye8, c4).reshape(4, 512, 128))
        ccat.append(jnp.concatenate(cparts, axis=1))
        lre8.append(jnp.broadcast_to(jnp.real(lam_bar).reshape(4, 1, 512), (4, 4, 512)))
        lim8.append(jnp.broadcast_to(jnp.imag(lam_bar).reshape(4, 1, 512), (4, 4, 512)))
    return (jnp.concatenate(bcat, axis=1).astype(BF16),
            jnp.concatenate(ccat, axis=2).astype(BF16),
            jnp.concatenate(lre8, axis=1), jnp.concatenate(lim8, axis=1))


def _s5(p, pf, pr, pft, prt, bcat, ccat, lre8, lim8):
    nrow = 8 * SCAN_BLK
    const2 = lambda i: (0, 0)
    const3 = lambda i: (0, 0, 0)
    proj = lambda i: jnp.minimum(i, N_SCAN - 1)
    read = lambda i: jnp.clip(i - 2, 0, N_SCAN - 1)
    return pl.pallas_call(
        _s5_kernel,
        grid=(N_SCAN + 2,),
        in_specs=[
            pl.BlockSpec((BATCH, SCAN_BLK, S5_WIDTH), lambda i: (0, _fwd_blk(proj(i)), 0)),
            pl.BlockSpec((BATCH, SCAN_BLK, S5_WIDTH), lambda i: (0, _rev_blk(proj(i)), 0)),
            pl.BlockSpec(pf.shape, const2), pl.BlockSpec(pr.shape, const2),
            pl.BlockSpec(pft.shape, const2), pl.BlockSpec(prt.shape, const2),
            pl.BlockSpec(bcat.shape, const3), pl.BlockSpec(ccat.shape, const3),
            pl.BlockSpec(lre8.shape, const3), pl.BlockSpec(lim8.shape, const3),
        ],
        out_specs=[
            pl.BlockSpec((BATCH, SCAN_BLK, S5_WIDTH), lambda i: (0, _fwd_blk(read(i)), 0)),
            pl.BlockSpec((BATCH, SCAN_BLK, S5_WIDTH), lambda i: (0, _rev_blk(read(i)), 0)),
        ],
        out_shape=[jax.ShapeDtypeStruct((BATCH, LT, S5_WIDTH), BF16)] * 2,
        scratch_shapes=[pltpu.VMEM((nrow, 4096), F32), pltpu.VMEM((nrow, 4096), F32),
                        pltpu.VMEM((8, 4096), F32)],
        compiler_params=_cparams(("arbitrary",)),
        name="s5_scan",
    )(p, p, pf, pr, pft, prt, bcat, ccat, lre8, lim8)


def _na_bias_table(rel_bias):
    w = np.arange(GRID_W)
    cs = np.clip(w - NA_COLS // 2, 0, GRID_W - NA_COLS)
    cp = np.arange(GRID_W)
    valid = (cp[None, :] >= cs[:, None]) & (cp[None, :] < cs[:, None] + NA_COLS)
    dc = cp[None, :] - w[:, None] + (NA_COLS - 1)
    n_dc = 2 * NA_COLS - 1
    onehot = ((dc[None] == np.arange(n_dc)[:, None, None]) & valid[None]).astype(np.float32)
    band = jnp.einsum('hrd,dwc->hrwc', rel_bias.astype(F32), jnp.asarray(onehot), precision=HIGHEST)
    band = band + jnp.asarray(np.where(valid, 0.0, NEG).astype(np.float32))
    tabs = [band[:, off:off + NA_ROWS].transpose(0, 2, 1, 3).reshape(NA_HEADS // 2, 128, NA_ROWS * GRID_W)
            for off in range(8)]
    tabs.append(jnp.full_like(tabs[0], NEG))
    return jnp.stack(tabs)


NA_STEP_ROWS = TOK_BLK // GRID_W


def _na_kernel(q_ref, k_ref, v_ref, bias_ref, o_ref):
    s_idx = pl.program_id(1)
    is_ctx = s_idx == 0
    lo = lax.broadcasted_iota(jnp.int32, (GRID_W, 128), 1) < HEAD_DIM
    nwin = NA_ROWS * GRID_W
    for rr in range(NA_STEP_ROWS):
        r = jnp.maximum(s_idx - 1, 0) * NA_STEP_ROWS + rr
        start = jnp.clip(r - NA_ROWS // 2, 0, SEQ // GRID_W - NA_ROWS)
        koff = pl.multiple_of(CTX_LEN + start * GRID_W, GRID_W)
        off = jnp.where(is_ctx, 8, start - r + NA_ROWS - 1)
        rows = slice(rr * GRID_W, (rr + 1) * GRID_W)
        for hp in range(NA_HEADS // 2):
            cols = slice(hp * 128, (hp + 1) * 128)
            qp = q_ref[0, rows, cols] * jnp.asarray(HEAD_DIM ** -0.5, BF16)
            zero = jnp.zeros_like(qp)
            q2 = jnp.concatenate([jnp.where(lo, qp, zero), jnp.where(lo, zero, qp)], axis=0)
            kw = k_ref[0, pl.ds(koff, nwin), cols]
            kc = k_ref[0, 0:CTX_LEN, cols]
            s_loc = _dot_nt(q2, kw) + bias_ref[off, hp]
            s_ctx = _dot_nt(q2, kc)
            m = jnp.maximum(jnp.max(s_loc, axis=-1, keepdims=True), jnp.max(s_ctx, axis=-1, keepdims=True))
            p_loc = jnp.exp(s_loc - m)
            p_ctx = jnp.exp(s_ctx - m)
            l = jnp.sum(p_loc, axis=-1, keepdims=True) + jnp.sum(p_ctx, axis=-1, keepdims=True)
            vw = v_ref[0, pl.ds(koff, nwin), cols]
            vc = v_ref[0, 0:CTX_LEN, cols]
            o = (_dot(p_loc.astype(BF16), vw) + _dot(p_ctx.astype(BF16), vc)) / l
            o_ref[0, rows, cols] = jnp.where(lo, o[:GRID_W], o[GRID_W:]).astype(BF16)


def _na(p, bias_tb):
    return pl.pallas_call(
        _na_kernel,
        grid=(BATCH, LT // TOK_BLK),
        in_specs=[
            pl.BlockSpec((1, TOK_BLK, 512), lambda b, s: (b, s, 2)),
            pl.BlockSpec((1, LT, 512), lambda b, s: (b, 0, 3)),
            pl.BlockSpec((1, LT, 512), lambda b, s: (b, 0, 4)),
            pl.BlockSpec(bias_tb.shape, lambda b, s: (0, 0, 0, 0)),
        ],
        out_specs=pl.BlockSpec((1, TOK_BLK, 512), lambda b, s: (b, s, 0)),
        out_shape=jax.ShapeDtypeStruct((BATCH, LT, 512), BF16),
        compiler_params=_cparams(("arbitrary", "arbitrary")),
        name="na_attn",
    )(p, p, p, bias_tb)


def _gelu_tanh(x):
    return 0.5 * x * (1.0 + jnp.tanh(math.sqrt(2.0 / math.pi) * (x + 0.044715 * (x * x * x))))


def _silu(x):
    return x * jax.nn.sigmoid(x)


def _finish(h, out, gt, pg):
    r = lax.rsqrt(jnp.mean(out * out, axis=-1, keepdims=True) + EPS)
    return h + gt * ((out * r) * pg)


def _even_out_kernel(yf_ref, yr_ref, u_ref, ga_ref, gb_ref, at_ref, hc_ref, hl_ref, mod_ref, d_ref, wg_ref,
                     bg_ref, wo_ref, pg_ref, o_ref):
    b = pl.program_id(0)
    tb = pl.program_id(1)
    row = jnp.where(tb == 0, BATCH, b)
    gt = mod_ref[pl.ds(row, 1), 2 * D_MODEL:]
    u = u_ref[0].astype(F32)
    y = d_ref[...] * u + yf_ref[0].astype(F32) + yr_ref[0].astype(F32)
    y = _gelu_tanh(y)
    y = y * jax.nn.sigmoid(_dot(y.astype(BF16), wg_ref[...]) + bg_ref[...])
    y_a = y * _silu(ga_ref[0].astype(F32))
    y_b = at_ref[0].astype(F32) * _silu(gb_ref[0].astype(F32))
    out = _dot(y_a.astype(BF16), wo_ref[0:512, :]) + _dot(y_b.astype(BF16), wo_ref[512:1024, :])
    h = jnp.where(tb == 0, hc_ref[0], hl_ref[0])
    o_ref[0] = _finish(h, out, gt, pg_ref[...])


def _even_out(yf, yr, p, attn, ctx, x, mod, d_skip, w_glu, b_glu, w_out, post_g):
    tok = lambda c: pl.BlockSpec((1, TOK_BLK, 512), lambda b, t: (b, t, c))
    full = lambda a: pl.BlockSpec(a.shape, lambda b, t: (0,) * a.ndim)
    d2, bg2, pg2 = d_skip.reshape(1, 512), b_glu.reshape(1, 512), post_g.reshape(1, D_MODEL)
    return pl.pallas_call(
        _even_out_kernel,
        grid=(BATCH, LT // TOK_BLK),
        in_specs=[tok(0), tok(0), tok(0), tok(1), tok(5), tok(0)] + _cat_specs(D_MODEL) +
                 [full(mod), full(d2), full(w_glu), full(bg2), full(w_out), full(pg2)],
        out_specs=pl.BlockSpec((1, TOK_BLK, D_MODEL), lambda b, t: (b, t, 0)),
        out_shape=jax.ShapeDtypeStruct((BATCH, LT, D_MODEL), F32),
        compiler_params=_cparams(("arbitrary", "arbitrary")),
        name="even_out",
    )(yf, yr, p, p, p, attn, ctx, x, mod, d2, w_glu, bg2, w_out, pg2)


def _softplus(z):
    return jnp.maximum(z, 0.0) + jnp.log1p(jnp.exp(-jnp.abs(z)))


def _lru_kernel(xf_ref, xfp_ref, xfn_ref, xr_ref, xrp_ref, xrn_ref, pf_ref, pr_ref, ph_ref, pft_ref, prt_ref,
                coef_ref, cb_ref, wg_ref, bg_ref, lam_ref, hf_ref, hr_ref, a_ref, b_ref, st_ref):
    i = pl.program_id(0)

    @pl.when(i == 0)
    def _():
        st_ref[...] = jnp.zeros_like(st_ref)

    nrow = 8 * SCAN_BLK

    def halo(prev_ref, next_ref, blk):
        prev_ok = jnp.logical_and(blk != 0, blk != CTX_SCAN)
        next_ok = jnp.logical_and(blk != CTX_SCAN - 1, blk != N_SCAN - 1)
        pv = prev_ref[...].reshape(BATCH * 16, LRU_WIDTH)
        nx = next_ref[...].reshape(BATCH * 16, LRU_WIDTH)
        return [jnp.where(prev_ok, pv, jnp.zeros_like(pv)), jnp.where(next_ok, nx, jnp.zeros_like(nx))]

    hal = jnp.concatenate(halo(xfp_ref, xfn_ref, _fwd_blk(i)) + halo(xrp_ref, xrn_ref, _rev_blk(i)), axis=0)
    xh = _dot(ph_ref[...], hal)
    xp = (_dot(pf_ref[...], xf_ref[...].reshape(BATCH * SCAN_BLK, LRU_WIDTH)) +
          _dot(pr_ref[...], xr_ref[...].reshape(BATCH * SCAN_BLK, LRU_WIDTH)))
    x_ext = jnp.concatenate([xh[:16], xp, xh[16:]], axis=0)
    xc = jnp.zeros((SCAN_BLK, 8, LRU_WIDTH), F32) + cb_ref[...]
    for s in range(5):
        xc = xc + x_ext[s * 8:s * 8 + nrow].reshape(SCAN_BLK, 8, LRU_WIDTH) * coef_ref[s]
    xc = xc.reshape(nrow, LRU_WIDTH)

    is_f = (lax.broadcasted_iota(jnp.int32, (nrow, 1), 0) & 7) < 4
    xcb = xc.astype(BF16)
    zero = jnp.zeros_like(xcb[:, :256])
    pre_r, pre_i = [], []
    for c in range(2):
        xt = xcb[:, c * 256:(c + 1) * 256]
        lhs = jnp.concatenate([jnp.where(is_f, xt, zero), jnp.where(is_f, zero, xt)], axis=1)
        pre = _dot(lhs, wg_ref[c])
        pre_r.append(pre[:, :256])
        pre_i.append(pre[:, 256:])
    bias = jnp.where(is_f, bg_ref[0:1, :], bg_ref[1:2, :])
    gate_r = jax.nn.sigmoid(jnp.concatenate(pre_r, axis=1) + bias[:, :LRU_WIDTH])
    gate_i = jax.nn.sigmoid(jnp.concatenate(pre_i, axis=1) + bias[:, LRU_WIDTH:])
    sp = _softplus(-lam_ref[...])
    log_a = (-LRU_C) * gate_r * jnp.where(is_f, sp[0:1, :], sp[1:2, :])
    a = jnp.exp(log_a)
    a_ref[...] = a
    one_m_a2 = -jnp.tanh(log_a) * (a * a + 1.0)
    b_ref[...] = jnp.sqrt(jnp.maximum(one_m_a2, 0.0)) * gate_i * xc

    def body(t, h):
        r0 = pl.multiple_of(t * 8, 8)
        h = a_ref[pl.ds(r0, 8), :] * h + b_ref[pl.ds(r0, 8), :]
        b_ref[pl.ds(r0, 8), :] = h
        return h

    st_ref[...] = lax.fori_loop(0, SCAN_BLK, body, st_ref[...], unroll=4)
    hs = b_ref[...].astype(BF16)
    hf_ref[...] = _dot(pft_ref[...], hs).astype(BF16).reshape(BATCH, SCAN_BLK, LRU_WIDTH)
    hr_ref[...] = _dot(prt_ref[...], hs).astype(BF16).reshape(BATCH, SCAN_BLK, LRU_WIDTH)


def _lru_gate_weights(w_a, w_x):
    tiles = []
    for c in range(2):
        blk = slice(4 * c, 4 * c + 4)
        tiles.append(jnp.concatenate([
            jnp.concatenate([_block_diag(w_a[0][blk]), _block_diag(w_x[0][blk])], axis=1),
            jnp.concatenate([_block_diag(w_a[1][blk]), _block_diag(w_x[1][blk])], axis=1)], axis=0))
    return jnp.stack(tiles).astype(BF16)


def _lru(p, pf, pr, ph, pft, prt, coef, conv_b, wg, bg, lam):
    nrow = 8 * SCAN_BLK
    full = lambda a: pl.BlockSpec(a.shape, lambda i: (0,) * a.ndim)
    cur = lambda f: pl.BlockSpec((BATCH, SCAN_BLK, LRU_WIDTH), lambda i: (0, f(i), 0))
    prev = lambda f: pl.BlockSpec((BATCH, 16, LRU_WIDTH), lambda i: (0, jnp.maximum(f(i) * 4 - 1, 0), 0))
    nxt = lambda f: pl.BlockSpec((BATCH, 16, LRU_WIDTH),
                                 lambda i: (0, jnp.minimum(f(i) * 4 + 4, LT // 16 - 1), 0))
    cb2 = conv_b.reshape(1, LRU_WIDTH)
    return pl.pallas_call(
        _lru_kernel,
        grid=(N_SCAN,),
        in_specs=[cur(_fwd_blk), prev(_fwd_blk), nxt(_fwd_blk), cur(_rev_blk), prev(_rev_blk), nxt(_rev_blk),
                  full(pf), full(pr), full(ph), full(pft), full(prt), full(coef), full(cb2), full(wg), full(bg),
                  full(lam)],
        out_specs=[
            pl.BlockSpec((BATCH, SCAN_BLK, LRU_WIDTH), lambda i: (0, _fwd_blk(i), 0)),
            pl.BlockSpec((BATCH, SCAN_BLK, LRU_WIDTH), lambda i: (0, _rev_blk(i), 0)),
        ],
        out_shape=[jax.ShapeDtypeStruct((BATCH, LT, LRU_WIDTH), BF16)] * 2,
        scratch_shapes=[pltpu.VMEM((nrow, LRU_WIDTH), F32), pltpu.VMEM((nrow, LRU_WIDTH), F32),
                        pltpu.VMEM((8, LRU_WIDTH), F32)],
        compiler_params=_cparams(("arbitrary",)),
        name="lru_scan",
    )(p, p, p, p, p, p, pf, pr, ph, pft, prt, coef, cb2, wg, bg, lam)


def _norm_rope(x, xrot, g, grot, cos, sin, ones_blk):
    x = x.astype(F32)
    xrot = xrot.astype(F32)
    ms = _dot((x * x).astype(BF16), ones_blk) * (1.0 / HEAD_DIM)
    rs = lax.rsqrt(ms + EPS)
    return rs * ((x * g) * cos + (xrot * grot) * sin)


def _gqa_kernel(q_ref, qr_ref, k_ref, kr_ref, v_ref, cq_ref, sq_ref, ck_ref, sk_ref, g_ref, ones_ref,
                o_ref, kn_ref, vlo_ref, vhi_ref):
    qb = pl.program_id(1)
    ones_blk = ones_ref[...]

    @pl.when(qb == 0)
    def _():
        kn = _norm_rope(k_ref[0], kr_ref[0], g_ref[2:3, :], g_ref[3:4, :], ck_ref[...], sk_ref[...], ones_blk)
        kn_ref[...] = kn.astype(BF16)
        v = v_ref[0]
        lo_k = lax.broadcasted_iota(jnp.int32, (LT, 128), 1) < HEAD_DIM
        one = jnp.ones_like(v)
        vlo_ref[...] = jnp.where(lo_k, v, one)
        vhi_ref[...] = jnp.where(lo_k, one, v)

    lo = lax.broadcasted_iota(jnp.int32, (Q_BLK, 128), 1) < HEAD_DIM
    kn = kn_ref[...]
    cos = cq_ref[...]
    sin = sq_ref[...]
    for pb in range(GQA_HEADS // 2):
        cols = slice(pb * 128, (pb + 1) * 128)
        qn = _norm_rope(q_ref[0, :, cols], qr_ref[0, :, cols], g_ref[0:1, :], g_ref[1:2, :], cos, sin, ones_blk)
        qn = (qn * (HEAD_DIM ** -0.5 * LOG2E)).astype(BF16)
        zero = jnp.zeros_like(qn)
        qs = jnp.concatenate([jnp.where(lo, qn, zero), jnp.where(lo, zero, qn)], axis=0)
        s = _dot_nt(qs, kn)
        m = jnp.max(s, axis=-1, keepdims=True)
        p = jnp.exp2(s - m).astype(BF16)
        o_lo = _dot(p[:Q_BLK], vlo_ref[...])
        o_hi = _dot(p[Q_BLK:], vhi_ref[...])
        num = jnp.where(lo, o_lo, o_hi)
        den = pltpu.roll(jnp.where(lo, o_hi, o_lo), HEAD_DIM, axis=1)
        o_ref[0, :, cols] = (num / den).astype(BF16)


def _gqa(p, cos128, sin128, gvec, ones_blk):
    nqb = SEQ // Q_BLK
    cb = CTX_LEN // Q_BLK
    return pl.pallas_call(
        _gqa_kernel,
        grid=(BATCH, nqb),
        in_specs=[
            pl.BlockSpec((1, Q_BLK, 512), lambda b, q: (b, q + cb, 2)),
            pl.BlockSpec((1, Q_BLK, 512), lambda b, q: (b, q + cb, 3)),
            pl.BlockSpec((1, LT, 128), lambda b, q: (b, 0, 20)),
            pl.BlockSpec((1, LT, 128), lambda b, q: (b, 0, 21)),
            pl.BlockSpec((1, LT, 128), lambda b, q: (b, 0, 22)),
            pl.BlockSpec((Q_BLK, 128), lambda b, q: (q + cb, 0)),
            pl.BlockSpec((Q_BLK, 128), lambda b, q: (q + cb, 0)),
            pl.BlockSpec((LT, 128), lambda b, q: (0, 0)),
            pl.BlockSpec((LT, 128), lambda b, q: (0, 0)),
            pl.BlockSpec((8, 128), lambda b, q: (0, 0)),
            pl.BlockSpec((128, 128), lambda b, q: (0, 0)),
        ],
        out_specs=pl.BlockSpec((1, Q_BLK, 512), lambda b, q: (b, q, 0)),
        out_shape=jax.ShapeDtypeStruct((BATCH, SEQ, 512), BF16),
        scratch_shapes=[pltpu.VMEM((LT, 128), BF16)] * 3,
        compiler_params=_cparams(("arbitrary", "arbitrary")),
        name="gqa_attn",
    )(p, p, p, p, p, cos128, sin128, cos128, sin128, gvec, ones_blk)


def _odd_out_kernel(hf_ref, hr_ref, gc_ref, gd_ref, at_ref, h_ref, mod_ref, wo_ref, pg_ref, o_ref):
    b = pl.program_id(0)
    gt = mod_ref[pl.ds(b, 1), 2 * D_MODEL:]
    y_c = (hf_ref[0].astype(F32) + hr_ref[0].astype(F32)) * _silu(gc_ref[0].astype(F32))
    y_d = at_ref[0].astype(F32) * _silu(gd_ref[0].astype(F32))
    out = _dot(y_c.astype(BF16), wo_ref[0:512, :]) + _dot(y_d.astype(BF16), wo_ref[512:1024, :])
    o_ref[0] = _finish(h_ref[0], out, gt, pg_ref[...])


def _odd_out(hf, hr, p, attn, h_cat, mod, w_out, post_g):
    cb = CTX_LEN // TOK_BLK
    cat = lambda c: pl.BlockSpec((1, TOK_BLK, 512), lambda b, t: (b, t + cb, c))
    full = lambda a: pl.BlockSpec(a.shape, lambda b, t: (0,) * a.ndim)
    pg2 = post_g.reshape(1, D_MODEL)
    return pl.pallas_call(
        _odd_out_kernel,
        grid=(BATCH, SEQ // TOK_BLK),
        in_specs=[cat(0), cat(0), cat(1), cat(4),
                  pl.BlockSpec((1, TOK_BLK, 512), lambda b, t: (b, t, 0)),
                  pl.BlockSpec((1, TOK_BLK, D_MODEL), lambda b, t: (b, t + cb, 0)),
                  full(mod), full(w_out), full(pg2)],
        out_specs=pl.BlockSpec((1, TOK_BLK, D_MODEL), lambda b, t: (b, t, 0)),
        out_shape=jax.ShapeDtypeStruct((BATCH, SEQ, D_MODEL), F32),
        compiler_params=_cparams(("arbitrary", "arbitrary")),
        name="odd_out",
    )(hf, hr, p, p, attn, h_cat, mod, w_out, pg2)


def _pair_swap(w):
    n = w.shape[1]
    r = np.zeros((n, n), np.float32)
    r[np.arange(1, n, 2), np.arange(0, n, 2)] = -1.0
    r[np.arange(0, n, 2), np.arange(1, n, 2)] = 1.0
    return jnp.dot(w, jnp.asarray(r, BF16), preferred_element_type=F32).astype(BF16)


def _interleave_kv_groups(w, axis):
    if axis == 0:
        return w.reshape(2, 4, HEAD_DIM, w.shape[1]).transpose(1, 0, 2, 3).reshape(w.shape)
    return w.reshape(w.shape[0], 2, 4, HEAD_DIM).transpose(0, 2, 1, 3).reshape(w.shape)


def _odd_w_in(w):
    w = w.astype(BF16)
    x, gc = w[:, 0:512], w[:, 512:1024]
    q = _interleave_kv_groups(w[:, 1024:1536], 1)
    k, v = w[:, 1536:1664], w[:, 1664:1792]
    gd = _interleave_kv_groups(w[:, 1792:2304], 1)
    return jnp.concatenate([x, gc, q, _pair_swap(q), gd, k, _pair_swap(k), v], axis=1)


def _rope_tables():
    t = np.arange(SEQ)
    row = (t // GRID_W).astype(np.float32)
    col = (t % GRID_W).astype(np.float32)
    half = HEAD_DIM // 2
    inv = (ROPE_THETA ** (-np.arange(0, half, 2, dtype=np.float32) / half)).astype(np.float32)
    ang = np.concatenate([row[:, None] * inv, col[:, None] * inv], axis=-1)
    cos = np.repeat(np.cos(ang), 2, axis=-1)
    sin = np.repeat(np.sin(ang), 2, axis=-1)
    cos = np.concatenate([np.ones((CTX_LEN, HEAD_DIM), np.float32), cos], axis=0)
    sin = np.concatenate([np.zeros((CTX_LEN, HEAD_DIM), np.float32), sin], axis=0)
    return np.tile(cos, (1, 2)).astype(np.float32), np.tile(sin, (1, 2)).astype(np.float32)


def _swap_pairs_vec(g):
    g2 = g.reshape(-1, 2)
    return jnp.stack([g2[:, 1], g2[:, 0]], axis=-1).reshape(g.shape)


def _block_diag(w):
    n, a, b = w.shape
    return jnp.einsum('mn,nij->minj', jnp.eye(n, dtype=w.dtype), w).reshape(n * a, n * b)


def kernel(x, c, ctx, c_ctx, ada_w, ada_b, pre_g, post_g, ev_w_in, ev_w_out, s5_lam_re, s5_lam_im, s5_log_dt,
           s5_b_re, s5_b_im, s5_c_re, s5_c_im, s5_d, s5_w_glu, s5_b_glu, na_rel_bias, od_w_in, od_w_out,
           lru_conv_w, lru_conv_b, lru_lam, lru_w_a, lru_b_a, lru_w_x, lru_b_x, gqa_q_norm, gqa_k_norm):
    pf_np, pr_np = _scan_perms()
    pf, pr = jnp.asarray(pf_np, BF16), jnp.asarray(pr_np, BF16)
    pft, prt = jnp.asarray(pf_np.T, BF16), jnp.asarray(pr_np.T, BF16)
    ph = jnp.asarray(_halo_perm(), BF16)

    c8 = jnp.concatenate([c, c_ctx[None], jnp.zeros((3, D_MODEL), F32)], axis=0)
    mod = _adaln(c8, ada_w, ada_b)

    p0 = _inproj((ctx, x), mod[0], pre_g[0], ev_w_in[0].astype(BF16))
    bcat, ccat, lre8, lim8 = _s5_weights(s5_lam_re[0], s5_lam_im[0], s5_log_dt[0], s5_b_re[0], s5_b_im[0],
                                         s5_c_re[0], s5_c_im[0])
    yf, yr = _s5(p0, pf, pr, pft, prt, bcat, ccat, lre8, lim8)
    attn0 = _na(p0, _na_bias_table(na_rel_bias[0]))
    h1 = _even_out(yf, yr, p0, attn0, ctx, x, mod[0], s5_d[0], s5_w_glu[0].astype(BF16), s5_b_glu[0],
                   ev_w_out[0].astype(BF16), post_g[0])

    p1 = _inproj(h1, mod[1], pre_g[1], _odd_w_in(od_w_in[0]))
    wg = _lru_gate_weights(lru_w_a[0], lru_w_x[0])
    bg = jnp.concatenate([lru_b_a[0], lru_b_x[0]], axis=1)
    hf, hr = _lru(p1, pf, pr, ph, pft, prt, _conv_coef(lru_conv_w[0]), lru_conv_b[0], wg, bg, lru_lam[0])
    cos_np, sin_np = _rope_tables()
    gq, gk = gqa_q_norm[0], gqa_k_norm[0]
    gvec = jnp.stack([jnp.tile(gq, 2), jnp.tile(_swap_pairs_vec(gq), 2),
                      jnp.tile(gk, 2), jnp.tile(_swap_pairs_vec(gk), 2)] + [jnp.zeros((128,), F32)] * 4)
    ones_np = np.kron(np.eye(2, dtype=np.float32), np.ones((HEAD_DIM, HEAD_DIM), np.float32))
    attn1 = _gqa(p1, jnp.asarray(cos_np), jnp.asarray(sin_np), gvec, jnp.asarray(ones_np, BF16))
    w_out1 = od_w_out[0].astype(BF16)
    w_out1 = jnp.concatenate([w_out1[:512], _interleave_kv_groups(w_out1[512:], 0)], axis=0)
    return _odd_out(hf, hr, p1, attn1, h1, mod[1], w_out1, post_g[1])
```

```python
import functools
import math

import numpy as np
import jax
import jax.numpy as jnp
from jax import lax
from jax.experimental import pallas as pl
from jax.experimental.pallas import tpu as pltpu

F32 = jnp.float32
BF16 = jnp.bfloat16
HIGHEST = lax.Precision.HIGHEST

D_MODEL = 1024
BATCH = 4
SEQ = 4096
GRID_W = 64
CTX_LEN = 256
LT = CTX_LEN + SEQ
HEAD_DIM = 64
EPS = 1e-6
S5_WIDTH = 512
S5_GROUP = 16
S5_GROUPS = 32
S5_STATE = 64
NA_HEADS = 8
NA_ROWS = 8
NA_COLS = 16
LRU_WIDTH = 512
LRU_BLOCKS = 8
LRU_BLOCK = 64
LRU_C = 8.0
GQA_HEADS = 8
ROPE_THETA = 10000.0
EVEN_IN = 3072
ODD_IN = 2944

TOK_BLK = 256
SCAN_BLK = 64
N_SCAN = LT // SCAN_BLK
CTX_SCAN = CTX_LEN // SCAN_BLK
Q_BLK = 256
NEG = -1e30
LOG2E = math.log2(math.e)
VMEM_LIMIT = 56 * 1024 * 1024


def _cparams(sem):
    return pltpu.CompilerParams(dimension_semantics=sem, vmem_limit_bytes=VMEM_LIMIT)


def _dot(a, b):
    return jnp.dot(a, b, preferred_element_type=F32)


def _dot_nt(a, b):
    return lax.dot_general(a, b, (((1,), (1,)), ((), ())), preferred_element_type=F32)


def _adaln_kernel(c_ref, w_ref, b_ref, o_ref):
    c = c_ref[...]
    s = c * jax.nn.sigmoid(c)
    o_ref[0] = jnp.dot(s, w_ref[0], preferred_element_type=F32, precision=HIGHEST) + b_ref[0]


def _adaln(c8, ada_w, ada_b):
    depth = ada_w.shape[0]
    nb = 3 * D_MODEL // 1024
    return pl.pallas_call(
        _adaln_kernel,
        grid=(depth, nb),
        in_specs=[
            pl.BlockSpec((8, D_MODEL), lambda i, n: (0, 0)),
            pl.BlockSpec((1, D_MODEL, 1024), lambda i, n: (i, 0, n)),
            pl.BlockSpec((1, 1, 1024), lambda i, n: (i, 0, n)),
        ],
        out_specs=pl.BlockSpec((1, 8, 1024), lambda i, n: (i, 0, n)),
        out_shape=jax.ShapeDtypeStruct((depth, 8, 3 * D_MODEL), F32),
        compiler_params=_cparams(("arbitrary", "arbitrary")),
        name="adaln",
    )(c8, ada_w, ada_b.reshape(depth, 1, 3 * D_MODEL))


def _cat_specs(width):
    return [pl.BlockSpec((1, TOK_BLK, width), lambda b, t: (b, 0, 0)),
            pl.BlockSpec((1, TOK_BLK, width), lambda b, t: (b, jnp.maximum(t - 1, 0), 0))]


def _inproj_kernel(*refs, two_src):
    b = pl.program_id(0)
    tb = pl.program_id(1)
    if two_src:
        c_ref, x_ref, mod_ref, g_ref, w_ref, o_ref = refs
        x = jnp.where(tb == 0, c_ref[0], x_ref[0])
    else:
        x_ref, mod_ref, g_ref, w_ref, o_ref = refs
        x = x_ref[0]
    r = lax.rsqrt(jnp.mean(x * x, axis=-1, keepdims=True) + EPS)
    row = jnp.where(tb == 0, BATCH, b)
    m = mod_ref[pl.ds(row, 1), :]
    sh = m[:, :D_MODEL]
    sc = m[:, D_MODEL:2 * D_MODEL]
    y = (x * r) * g_ref[...]
    y = y * (1.0 + sc) + sh
    o_ref[0] = _dot(y.astype(BF16), w_ref[...]).astype(BF16)


def _inproj(src, mod, g, w_bf):
    n = w_bf.shape[1]
    two_src = isinstance(src, tuple)
    if two_src:
        src_specs = _cat_specs(D_MODEL)
    else:
        src_specs = [pl.BlockSpec((1, TOK_BLK, D_MODEL), lambda b, t: (b, t, 0))]
        src = (src,)
    return pl.pallas_call(
        functools.partial(_inproj_kernel, two_src=two_src),
        grid=(BATCH, LT // TOK_BLK),
        in_specs=src_specs + [
            pl.BlockSpec((8, 3 * D_MODEL), lambda b, t: (0, 0)),
            pl.BlockSpec((1, D_MODEL), lambda b, t: (0, 0)),
            pl.BlockSpec((D_MODEL, n), lambda b, t: (0, 0)),
        ],
        out_specs=pl.BlockSpec((1, TOK_BLK, n), lambda b, t: (b, t, 0)),
        out_shape=jax.ShapeDtypeStruct((BATCH, LT, n), BF16),
        compiler_params=_cparams(("arbitrary", "arbitrary")),
        name="inproj",
    )(*src, mod, g.reshape(1, D_MODEL), w_bf)


def _fwd_blk(i):
    return i


def _rev_blk(i):
    return jnp.where(i < CTX_SCAN, CTX_SCAN - 1 - i, N_SCAN + CTX_SCAN - 1 - i)


def _scan_perms():
    t = SCAN_BLK
    pf = np.zeros((8 * t, BATCH * t), np.float32)
    pr = np.zeros((8 * t, BATCH * t), np.float32)
    for tt in range(t):
        for b in range(BATCH):
            pf[tt * 8 + b, b * t + tt] = 1.0
            pr[tt * 8 + 4 + b, b * t + (t - 1 - tt)] = 1.0
    return pf, pr


def _halo_perm():
    ph = np.zeros((32, 16 * 4 * BATCH), np.float32)
    for b in range(BATCH):
        ph[1 * 8 + b, 0 * 64 + b * 16 + 15] = 1.0
        ph[2 * 8 + b, 1 * 64 + b * 16 + 0] = 1.0
        ph[3 * 8 + b, 1 * 64 + b * 16 + 1] = 1.0
        ph[0 * 8 + 4 + b, 3 * 64 + b * 16 + 1] = 1.0
        ph[1 * 8 + 4 + b, 3 * 64 + b * 16 + 0] = 1.0
        ph[2 * 8 + 4 + b, 2 * 64 + b * 16 + 15] = 1.0
    return ph


def _conv_coef(conv_w):
    zero = jnp.zeros_like(conv_w[0])
    rows = []
    for s in range(-2, 3):
        wf = conv_w[s + 1] if -1 <= s <= 2 else zero
        wr = conv_w[1 - s] if -2 <= s <= 1 else zero
        rows.append(jnp.concatenate([jnp.broadcast_to(wf, (4, LRU_WIDTH)), jnp.broadcast_to(wr, (4, LRU_WIDTH))]))
    return jnp.stack(rows)


def _s5_kernel(uf_ref, ur_ref, pf_ref, pr_ref, pft_ref, prt_ref, bcat_ref, ccat_ref, lre_ref, lim_ref,
               yf_ref, yr_ref, buf0_ref, buf1_ref, st_ref):
    i = pl.program_id(0)

    @pl.when(i == 0)
    def _():
        st_ref[...] = jnp.zeros_like(st_ref)
        buf0_ref[...] = jnp.zeros_like(buf0_ref)
        buf1_ref[...] = jnp.zeros_like(buf1_ref)

    nrow = 8 * SCAN_BLK

    def step(buf_a, buf_b):
        vals = {}

        def perm():
            uf = uf_ref[...].reshape(BATCH * SCAN_BLK, S5_WIDTH)
            ur = ur_ref[...].reshape(BATCH * SCAN_BLK, S5_WIDTH)
            vals['u_f'] = _dot(pf_ref[...], uf).astype(BF16)
            vals['u_r'] = _dot(pr_ref[...], ur).astype(BF16)

        def readout(j):
            is_f = (lax.broadcasted_iota(jnp.int32, (nrow, 128), 0) & 7) < 4
            yj = _dot(buf_a[:, j * 1024:(j + 1) * 1024].astype(BF16), ccat_ref[j])
            vals['y%d' % j] = jnp.where(is_f, yj[:, :128], yj[:, 128:]).astype(BF16)

        def project(j):
            lhs = jnp.concatenate([vals['u_f'][:, j * 128:(j + 1) * 128], vals['u_r'][:, j * 128:(j + 1) * 128]],
                                  axis=1)
            buf_a[:, j * 1024:(j + 1) * 1024] = _dot(lhs, bcat_ref[j])

        def unperm(p_ref, o_ref):
            yp = jnp.concatenate([vals['y%d' % j] for j in range(4)], axis=1)
            o_ref[...] = _dot(p_ref[...], yp).astype(BF16).reshape(BATCH, SCAN_BLK, S5_WIDTH)

        mxu = [perm]
        for j in range(4):
            mxu += [functools.partial(readout, j), functools.partial(project, j)]
        mxu += [functools.partial(unperm, pft_ref, yf_ref), functools.partial(unperm, prt_ref, yr_ref)]

        def scan(j, t0):
            c_re = slice(j * 1024, j * 1024 + 512)
            c_im = slice(j * 1024 + 512, (j + 1) * 1024)
            lre = lre_ref[j]
            lim = lim_ref[j]
            hre = st_ref[:, c_re]
            him = st_ref[:, c_im]
            for t in range(t0, t0 + 16):
                rows = slice(t * 8, (t + 1) * 8)
                nre = lre * hre - lim * him + buf_b[rows, c_re]
                nim = lre * him + lim * hre + buf_b[rows, c_im]
                buf_b[rows, c_re] = nre
                buf_b[rows, c_im] = nim
                hre, him = nre, nim
            st_ref[:, c_re] = hre
            st_ref[:, c_im] = him

        vpu = [functools.partial(scan, j, t0) for j in range(4) for t0 in range(0, SCAN_BLK, 16)]

        per_piece = [1, 2, 1, 2, 1, 2, 1, 2, 1, 2, 1]
        for piece, n_scan in zip(mxu, per_piece):
            piece()
            for _ in range(n_scan):
                vpu.pop(0)()

    @pl.when(i % 2 == 0)
    def _():
        step(buf0_ref, buf1_ref)

    @pl.when(i % 2 == 1)
    def _():
        step(buf1_ref, buf0_ref)


def _s5_weights(lam_re, lam_im, log_dt, b_re, b_im, c_re, c_im):
    eye8 = jnp.eye(8, dtype=F32)
    bcat, ccat, lre8, lim8 = [], [], [], []
    for d in range(2):
        lam = lax.complex(lam_re[d].astype(F32), lam_im[d].astype(F32))
        dt = jnp.exp(log_dt[d].astype(F32))[:, None]
        lam_bar = jnp.exp(lam * dt)
        b = lax.complex(b_re[d].astype(F32), b_im[d].astype(F32))
        b_bar = ((lam_bar - 1.0) / lam)[..., None] * b
        parts = []
        for comp in (jnp.real(b_bar), jnp.imag(b_bar)):
            c4 = comp.reshape(4, 8, S5_STATE, S5_GROUP)
            parts.append(jnp.einsum('ag,jgph->jahgp', eye8, c4).reshape(4, 128, 512))
        bcat.append(jnp.concatenate(parts, axis=2))
        cparts = []
        for comp in (c_re[d].astype(F32), -c_im[d].astype(F32)):
            c4 = comp.reshape(4, 8, S5_GROUP, S5_STATE)
            cparts.append(jnp.einsum('ag,jghp->jgpah', eye8, c4).reshape(4, 512, 128))
        ccat.append(jnp.concatenate(cparts, axis=1))
        lre8.append(jnp.broadcast_to(jnp.real(lam_bar).reshape(4, 1, 512), (4, 4, 512)))
        lim8.append(jnp.broadcast_to(jnp.imag(lam_bar).reshape(4, 1, 512), (4, 4, 512)))
    return (jnp.concatenate(bcat, axis=1).astype(BF16),
            jnp.concatenate(ccat, axis=2).astype(BF16),
            jnp.concatenate(lre8, axis=1), jnp.concatenate(lim8, axis=1))


def _s5(p, pf, pr, pft, prt, bcat, ccat, lre8, lim8):
    nrow = 8 * SCAN_BLK
    const2 = lambda i: (0, 0)
    const3 = lambda i: (0, 0, 0)
    proj = lambda i: jnp.minimum(i, N_SCAN - 1)
    read = lambda i: jnp.clip(i - 2, 0, N_SCAN - 1)
    return pl.pallas_call(
        _s5_kernel,
        grid=(N_SCAN + 2,),
        in_specs=[
            pl.BlockSpec((BATCH, SCAN_BLK, S5_WIDTH), lambda i: (0, _fwd_blk(proj(i)), 0)),
            pl.BlockSpec((BATCH, SCAN_BLK, S5_WIDTH), lambda i: (0, _rev_blk(proj(i)), 0)),
            pl.BlockSpec(pf.shape, const2), pl.BlockSpec(pr.shape, const2),
            pl.BlockSpec(pft.shape, const2), pl.BlockSpec(prt.shape, const2),
            pl.BlockSpec(bcat.shape, const3), pl.BlockSpec(ccat.shape, const3),
            pl.BlockSpec(lre8.shape, const3), pl.BlockSpec(lim8.shape, const3),
        ],
        out_specs=[
            pl.BlockSpec((BATCH, SCAN_BLK, S5_WIDTH), lambda i: (0, _fwd_blk(read(i)), 0)),
            pl.BlockSpec((BATCH, SCAN_BLK, S5_WIDTH), lambda i: (0, _rev_blk(read(i)), 0)),
        ],
        out_shape=[jax.ShapeDtypeStruct((BATCH, LT, S5_WIDTH), BF16)] * 2,
        scratch_shapes=[pltpu.VMEM((nrow, 4096), F32), pltpu.VMEM((nrow, 4096), F32),
                        pltpu.VMEM((8, 4096), F32)],
        compiler_params=_cparams(("arbitrary",)),
        name="s5_scan",
    )(p, p, pf, pr, pft, prt, bcat, ccat, lre8, lim8)


def _na_bias_table(rel_bias):
    w = np.arange(GRID_W)
    cs = np.clip(w - NA_COLS // 2, 0, GRID_W - NA_COLS)
    cp = np.arange(GRID_W)
    valid = (cp[None, :] >= cs[:, None]) & (cp[None, :] < cs[:, None] + NA_COLS)
    dc = cp[None, :] - w[:, None] + (NA_COLS - 1)
    n_dc = 2 * NA_COLS - 1
    onehot = ((dc[None] == np.arange(n_dc)[:, None, None]) & valid[None]).astype(np.float32)
    band = jnp.einsum('hrd,dwc->hrwc', rel_bias.astype(F32), jnp.asarray(onehot), precision=HIGHEST)
    band = band + jnp.asarray(np.where(valid, 0.0, NEG).astype(np.float32))
    tabs = [band[:, off:off + NA_ROWS].transpose(0, 2, 1, 3).reshape(NA_HEADS // 2, 128, NA_ROWS * GRID_W)
            for off in range(8)]
    tabs.append(jnp.full_like(tabs[0], NEG))
    return jnp.stack(tabs)


NA_STEP_ROWS = TOK_BLK // GRID_W


def _na_kernel(q_ref, k_ref, v_ref, bias_ref, o_ref):
    s_idx = pl.program_id(1)
    is_ctx = s_idx == 0
    lo = lax.broadcasted_iota(jnp.int32, (GRID_W, 128), 1) < HEAD_DIM
    nwin = NA_ROWS * GRID_W
    for rr in range(NA_STEP_ROWS):
        r = jnp.maximum(s_idx - 1, 0) * NA_STEP_ROWS + rr
        start = jnp.clip(r - NA_ROWS // 2, 0, SEQ // GRID_W - NA_ROWS)
        koff = pl.multiple_of(CTX_LEN + start * GRID_W, GRID_W)
        off = jnp.where(is_ctx, 8, start - r + NA_ROWS - 1)
        rows = slice(rr * GRID_W, (rr + 1) * GRID_W)
        pairs = range(NA_HEADS // 2)
        cols = [slice(hp * 128, (hp + 1) * 128) for hp in pairs]
        s_loc, s_ctx, m, p_loc, p_ctx, l, o = [], [], [], [], [], [], []
        for hp in pairs:
            qp = q_ref[0, rows, cols[hp]] * jnp.asarray(HEAD_DIM ** -0.5 * LOG2E, BF16)
            zero = jnp.zeros_like(qp)
            q2 = jnp.concatenate([jnp.where(lo, qp, zero), jnp.where(lo, zero, qp)], axis=0)
            s_loc.append(_dot_nt(q2, k_ref[0, pl.ds(koff, nwin), cols[hp]]) + bias_ref[off, hp])
            s_ctx.append(_dot_nt(q2, k_ref[0, 0:CTX_LEN, cols[hp]]))
        for hp in pairs:
            m.append(jnp.maximum(jnp.max(s_loc[hp], axis=-1, keepdims=True),
                                 jnp.max(s_ctx[hp], axis=-1, keepdims=True)))
        for hp in pairs:
            pl_, pc_ = jnp.exp2(s_loc[hp] - m[hp]), jnp.exp2(s_ctx[hp] - m[hp])
            l.append(jnp.sum(pl_, axis=-1, keepdims=True) + jnp.sum(pc_, axis=-1, keepdims=True))
            p_loc.append(pl_.astype(BF16))
            p_ctx.append(pc_.astype(BF16))
        for hp in pairs:
            o.append(_dot(p_loc[hp], v_ref[0, pl.ds(koff, nwin), cols[hp]]) +
                     _dot(p_ctx[hp], v_ref[0, 0:CTX_LEN, cols[hp]]))
        for hp in pairs:
            oh = o[hp] / l[hp]
            o_ref[0, rows, cols[hp]] = jnp.where(lo, oh[:GRID_W], oh[GRID_W:]).astype(BF16)


def _na(p, bias_tb):
    return pl.pallas_call(
        _na_kernel,
        grid=(BATCH, LT // TOK_BLK),
        in_specs=[
            pl.BlockSpec((1, TOK_BLK, 512), lambda b, s: (b, s, 2)),
            pl.BlockSpec((1, LT, 512), lambda b, s: (b, 0, 3)),
            pl.BlockSpec((1, LT, 512), lambda b, s: (b, 0, 4)),
            pl.BlockSpec(bias_tb.shape, lambda b, s: (0, 0, 0, 0)),
        ],
        out_specs=pl.BlockSpec((1, TOK_BLK, 512), lambda b, s: (b, s, 0)),
        out_shape=jax.ShapeDtypeStruct((BATCH, LT, 512), BF16),
        compiler_params=_cparams(("arbitrary", "arbitrary")),
        name="na_attn",
    )(p, p, p, bias_tb)


def _gelu_tanh(x):
    return 0.5 * x * (1.0 + jnp.tanh(math.sqrt(2.0 / math.pi) * (x + 0.044715 * (x * x * x))))


def _silu(x):
    return x * jax.nn.sigmoid(x)


def _finish(h, out, gt, pg):
    r = lax.rsqrt(jnp.mean(out * out, axis=-1, keepdims=True) + EPS)
    return h + gt * ((out * r) * pg)


def _even_out_kernel(yf_ref, yr_ref, u_ref, ga_ref, gb_ref, at_ref, hc_ref, hl_ref, mod_ref, d_ref, wg_ref,
                     bg_ref, wo_ref, pg_ref, o_ref):
    b = pl.program_id(0)
    tb = pl.program_id(1)
    row = jnp.where(tb == 0, BATCH, b)
    gt = mod_ref[pl.ds(row, 1), 2 * D_MODEL:]
    u = u_ref[0].astype(F32)
    y = d_ref[...] * u + yf_ref[0].astype(F32) + yr_ref[0].astype(F32)
    y = _gelu_tanh(y)
    y = y * jax.nn.sigmoid(_dot(y.astype(BF16), wg_ref[...]) + bg_ref[...])
    y_a = y * _silu(ga_ref[0].astype(F32))
    y_b = at_ref[0].astype(F32) * _silu(gb_ref[0].astype(F32))
    out = _dot(y_a.astype(BF16), wo_ref[0:512, :]) + _dot(y_b.astype(BF16), wo_ref[512:1024, :])
    h = jnp.where(tb == 0, hc_ref[0], hl_ref[0])
    o_ref[0] = _finish(h, out, gt, pg_ref[...])


def _even_out(yf, yr, p, attn, ctx, x, mod, d_skip, w_glu, b_glu, w_out, post_g):
    tok = lambda c: pl.BlockSpec((1, TOK_BLK, 512), lambda b, t: (b, t, c))
    full = lambda a: pl.BlockSpec(a.shape, lambda b, t: (0,) * a.ndim)
    d2, bg2, pg2 = d_skip.reshape(1, 512), b_glu.reshape(1, 512), post_g.reshape(1, D_MODEL)
    return pl.pallas_call(
        _even_out_kernel,
        grid=(BATCH, LT // TOK_BLK),
        in_specs=[tok(0), tok(0), tok(0), tok(1), tok(5), tok(0)] + _cat_specs(D_MODEL) +
                 [full(mod), full(d2), full(w_glu), full(bg2), full(w_out), full(pg2)],
        out_specs=pl.BlockSpec((1, TOK_BLK, D_MODEL), lambda b, t: (b, t, 0)),
        out_shape=jax.ShapeDtypeStruct((BATCH, LT, D_MODEL), F32),
        compiler_params=_cparams(("arbitrary", "arbitrary")),
        name="even_out",
    )(yf, yr, p, p, p, attn, ctx, x, mod, d2, w_glu, bg2, w_out, pg2)


def _softplus(z):
    return jnp.maximum(z, 0.0) + jnp.log1p(jnp.exp(-jnp.abs(z)))


def _lru_kernel(xf_ref, xfp_ref, xfn_ref, xr_ref, xrp_ref, xrn_ref, pf_ref, pr_ref, ph_ref, pft_ref, prt_ref,
                coef_ref, cb_ref, wg_ref, bg_ref, lam_ref, hf_ref, hr_ref, a_ref, b_ref, st_ref):
    i = pl.program_id(0)

    @pl.when(i == 0)
    def _():
        st_ref[...] = jnp.zeros_like(st_ref)

    nrow = 8 * SCAN_BLK

    def halo(prev_ref, next_ref, blk):
        prev_ok = jnp.logical_and(blk != 0, blk != CTX_SCAN)
        next_ok = jnp.logical_and(blk != CTX_SCAN - 1, blk != N_SCAN - 1)
        pv = prev_ref[...].reshape(BATCH * 16, LRU_WIDTH)
        nx = next_ref[...].reshape(BATCH * 16, LRU_WIDTH)
        return [jnp.where(prev_ok, pv, jnp.zeros_like(pv)), jnp.where(next_ok, nx, jnp.zeros_like(nx))]

    hal = jnp.concatenate(halo(xfp_ref, xfn_ref, _fwd_blk(i)) + halo(xrp_ref, xrn_ref, _rev_blk(i)), axis=0)
    xh = _dot(ph_ref[...], hal)
    xp = (_dot(pf_ref[...], xf_ref[...].reshape(BATCH * SCAN_BLK, LRU_WIDTH)) +
          _dot(pr_ref[...], xr_ref[...].reshape(BATCH * SCAN_BLK, LRU_WIDTH)))
    x_ext = jnp.concatenate([xh[:16], xp, xh[16:]], axis=0)
    xc = jnp.zeros((SCAN_BLK, 8, LRU_WIDTH), F32) + cb_ref[...]
    for s in range(5):
        xc = xc + x_ext[s * 8:s * 8 + nrow].reshape(SCAN_BLK, 8, LRU_WIDTH) * coef_ref[s]
    xc = xc.reshape(nrow, LRU_WIDTH)

    is_f = (lax.broadcasted_iota(jnp.int32, (nrow, 1), 0) & 7) < 4
    xcb = xc.astype(BF16)
    zero = jnp.zeros_like(xcb[:, :256])
    pre_r, pre_i = [], []
    for c in range(2):
        xt = xcb[:, c * 256:(c + 1) * 256]
        lhs = jnp.concatenate([jnp.where(is_f, xt, zero), jnp.where(is_f, zero, xt)], axis=1)
        pre = _dot(lhs, wg_ref[c])
        pre_r.append(pre[:, :256])
        pre_i.append(pre[:, 256:])
    bias = jnp.where(is_f, bg_ref[0:1, :], bg_ref[1:2, :])
    gate_r = jax.nn.sigmoid(jnp.concatenate(pre_r, axis=1) + bias[:, :LRU_WIDTH])
    gate_i = jax.nn.sigmoid(jnp.concatenate(pre_i, axis=1) + bias[:, LRU_WIDTH:])
    sp = _softplus(-lam_ref[...])
    log_a = (-LRU_C) * gate_r * jnp.where(is_f, sp[0:1, :], sp[1:2, :])
    a = jnp.exp(log_a)
    a_ref[...] = a
    one_m_a2 = -jnp.tanh(log_a) * (a * a + 1.0)
    b_ref[...] = jnp.sqrt(jnp.maximum(one_m_a2, 0.0)) * gate_i * xc

    def body(t, h):
        r0 = pl.multiple_of(t * 8, 8)
        h = a_ref[pl.ds(r0, 8), :] * h + b_ref[pl.ds(r0, 8), :]
        b_ref[pl.ds(r0, 8), :] = h
        return h

    st_ref[...] = lax.fori_loop(0, SCAN_BLK, body, st_ref[...], unroll=4)
    hs = b_ref[...].astype(BF16)
    hf_ref[...] = _dot(pft_ref[...], hs).astype(BF16).reshape(BATCH, SCAN_BLK, LRU_WIDTH)
    hr_ref[...] = _dot(prt_ref[...], hs).astype(BF16).reshape(BATCH, SCAN_BLK, LRU_WIDTH)


def _lru_gate_weights(w_a, w_x):
    tiles = []
    for c in range(2):
        blk = slice(4 * c, 4 * c + 4)
        tiles.append(jnp.concatenate([
            jnp.concatenate([_block_diag(w_a[0][blk]), _block_diag(w_x[0][blk])], axis=1),
            jnp.concatenate([_block_diag(w_a[1][blk]), _block_diag(w_x[1][blk])], axis=1)], axis=0))
    return jnp.stack(tiles).astype(BF16)


def _lru(p, pf, pr, ph, pft, prt, coef, conv_b, wg, bg, lam):
    nrow = 8 * SCAN_BLK
    full = lambda a: pl.BlockSpec(a.shape, lambda i: (0,) * a.ndim)
    cur = lambda f: pl.BlockSpec((BATCH, SCAN_BLK, LRU_WIDTH), lambda i: (0, f(i), 0))
    prev = lambda f: pl.BlockSpec((BATCH, 16, LRU_WIDTH), lambda i: (0, jnp.maximum(f(i) * 4 - 1, 0), 0))
    nxt = lambda f: pl.BlockSpec((BATCH, 16, LRU_WIDTH),
                                 lambda i: (0, jnp.minimum(f(i) * 4 + 4, LT // 16 - 1), 0))
    cb2 = conv_b.reshape(1, LRU_WIDTH)
    return pl.pallas_call(
        _lru_kernel,
        grid=(N_SCAN,),
        in_specs=[cur(_fwd_blk), prev(_fwd_blk), nxt(_fwd_blk), cur(_rev_blk), prev(_rev_blk), nxt(_rev_blk),
                  full(pf), full(pr), full(ph), full(pft), full(prt), full(coef), full(cb2), full(wg), full(bg),
                  full(lam)],
        out_specs=[
            pl.BlockSpec((BATCH, SCAN_BLK, LRU_WIDTH), lambda i: (0, _fwd_blk(i), 0)),
            pl.BlockSpec((BATCH, SCAN_BLK, LRU_WIDTH), lambda i: (0, _rev_blk(i), 0)),
        ],
        out_shape=[jax.ShapeDtypeStruct((BATCH, LT, LRU_WIDTH), BF16)] * 2,
        scratch_shapes=[pltpu.VMEM((nrow, LRU_WIDTH), F32), pltpu.VMEM((nrow, LRU_WIDTH), F32),
                        pltpu.VMEM((8, LRU_WIDTH), F32)],
        compiler_params=_cparams(("arbitrary",)),
        name="lru_scan",
    )(p, p, p, p, p, p, pf, pr, ph, pft, prt, coef, cb2, wg, bg, lam)


def _norm_rope(x, xrot, g, grot, cos, sin, ones_blk):
    x = x.astype(F32)
    xrot = xrot.astype(F32)
    ms = _dot((x * x).astype(BF16), ones_blk) * (1.0 / HEAD_DIM)
    rs = lax.rsqrt(ms + EPS)
    return rs * ((x * g) * cos + (xrot * grot) * sin)


def _gqa_kernel(q_ref, qr_ref, k_ref, kr_ref, v_ref, cq_ref, sq_ref, ck_ref, sk_ref, g_ref, ones_ref,
                o_ref, kn_ref, vlo_ref, vhi_ref):
    qb = pl.program_id(1)
    ones_blk = ones_ref[...]

    @pl.when(qb == 0)
    def _():
        kn = _norm_rope(k_ref[0], kr_ref[0], g_ref[2:3, :], g_ref[3:4, :], ck_ref[...], sk_ref[...], ones_blk)
        kn_ref[...] = kn.astype(BF16)
        v = v_ref[0]
        lo_k = lax.broadcasted_iota(jnp.int32, (LT, 128), 1) < HEAD_DIM
        one = jnp.ones_like(v)
        vlo_ref[...] = jnp.where(lo_k, v, one)
        vhi_ref[...] = jnp.where(lo_k, one, v)

    lo = lax.broadcasted_iota(jnp.int32, (Q_BLK, 128), 1) < HEAD_DIM
    kn = kn_ref[...]
    cos = cq_ref[...]
    sin = sq_ref[...]
    n_pb = GQA_HEADS // 2

    def scores(pb):
        cols = slice(pb * 128, (pb + 1) * 128)
        qn = _norm_rope(q_ref[0, :, cols], qr_ref[0, :, cols], g_ref[0:1, :], g_ref[1:2, :], cos, sin, ones_blk)
        qn = (qn * (HEAD_DIM ** -0.5 * LOG2E)).astype(BF16)
        zero = jnp.zeros_like(qn)
        qs = jnp.concatenate([jnp.where(lo, qn, zero), jnp.where(lo, zero, qn)], axis=0)
        return _dot_nt(qs, kn)

    s_next = scores(0)
    for pb in range(n_pb):
        cols = slice(pb * 128, (pb + 1) * 128)
        s = s_next
        if pb + 1 < n_pb:
            s_next = scores(pb + 1)
        m = jnp.max(s, axis=-1, keepdims=True)
        p = jnp.exp2(s - m).astype(BF16)
        o_lo = _dot(p[:Q_BLK], vlo_ref[...])
        o_hi = _dot(p[Q_BLK:], vhi_ref[...])
        num = jnp.where(lo, o_lo, o_hi)
        den = pltpu.roll(jnp.where(lo, o_hi, o_lo), HEAD_DIM, axis=1)
        o_ref[0, :, cols] = (num / den).astype(BF16)


def _gqa(p, cos128, sin128, gvec, ones_blk):
    nqb = SEQ // Q_BLK
    cb = CTX_LEN // Q_BLK
    return pl.pallas_call(
        _gqa_kernel,
        grid=(BATCH, nqb),
        in_specs=[
            pl.BlockSpec((1, Q_BLK, 512), lambda b, q: (b, q + cb, 2)),
            pl.BlockSpec((1, Q_BLK, 512), lambda b, q: (b, q + cb, 3)),
            pl.BlockSpec((1, LT, 128), lambda b, q: (b, 0, 20)),
            pl.BlockSpec((1, LT, 128), lambda b, q: (b, 0, 21)),
            pl.BlockSpec((1, LT, 128), lambda b, q: (b, 0, 22)),
            pl.BlockSpec((Q_BLK, 128), lambda b, q: (q + cb, 0)),
            pl.BlockSpec((Q_BLK, 128), lambda b, q: (q + cb, 0)),
            pl.BlockSpec((LT, 128), lambda b, q: (0, 0)),
            pl.BlockSpec((LT, 128), lambda b, q: (0, 0)),
            pl.BlockSpec((8, 128), lambda b, q: (0, 0)),
            pl.BlockSpec((128, 128), lambda b, q: (0, 0)),
        ],
        out_specs=pl.BlockSpec((1, Q_BLK, 512), lambda b, q: (b, q, 0)),
        out_shape=jax.ShapeDtypeStruct((BATCH, SEQ, 512), BF16),
        scratch_shapes=[pltpu.VMEM((LT, 128), BF16)] * 3,
        compiler_params=_cparams(("arbitrary", "arbitrary")),
        name="gqa_attn",
    )(p, p, p, p, p, cos128, sin128, cos128, sin128, gvec, ones_blk)


def _odd_out_kernel(hf_ref, hr_ref, gc_ref, gd_ref, at_ref, h_ref, mod_ref, wo_ref, pg_ref, o_ref):
    b = pl.program_id(0)
    gt = mod_ref[pl.ds(b, 1), 2 * D_MODEL:]
    y_c = (hf_ref[0].astype(F32) + hr_ref[0].astype(F32)) * _silu(gc_ref[0].astype(F32))
    y_d = at_ref[0].astype(F32) * _silu(gd_ref[0].astype(F32))
    out = _dot(y_c.astype(BF16), wo_ref[0:512, :]) + _dot(y_d.astype(BF16), wo_ref[512:1024, :])
    o_ref[0] = _finish(h_ref[0], out, gt, pg_ref[...])


def _odd_out(hf, hr, p, attn, h_cat, mod, w_out, post_g):
    cb = CTX_LEN // TOK_BLK
    cat = lambda c: pl.BlockSpec((1, TOK_BLK, 512), lambda b, t: (b, t + cb, c))
    full = lambda a: pl.BlockSpec(a.shape, lambda b, t: (0,) * a.ndim)
    pg2 = post_g.reshape(1, D_MODEL)
    return pl.pallas_call(
        _odd_out_kernel,
        grid=(BATCH, SEQ // TOK_BLK),
        in_specs=[cat(0), cat(0), cat(1), cat(4),
                  pl.BlockSpec((1, TOK_BLK, 512), lambda b, t: (b, t, 0)),
                  pl.BlockSpec((1, TOK_BLK, D_MODEL), lambda b, t: (b, t + cb, 0)),
                  full(mod), full(w_out), full(pg2)],
        out_specs=pl.BlockSpec((1, TOK_BLK, D_MODEL), lambda b, t: (b, t, 0)),
        out_shape=jax.ShapeDtypeStruct((BATCH, SEQ, D_MODEL), F32),
        compiler_params=_cparams(("arbitrary", "arbitrary")),
        name="odd_out",
    )(hf, hr, p, p, attn, h_cat, mod, w_out, pg2)


def _pair_swap(w):
    n = w.shape[1]
    r = np.zeros((n, n), np.float32)
    r[np.arange(1, n, 2), np.arange(0, n, 2)] = -1.0
    r[np.arange(0, n, 2), np.arange(1, n, 2)] = 1.0
    return jnp.dot(w, jnp.asarray(r, BF16), preferred_element_type=F32).astype(BF16)


def _interleave_kv_groups(w, axis):
    if axis == 0:
        return w.reshape(2, 4, HEAD_DIM, w.shape[1]).transpose(1, 0, 2, 3).reshape(w.shape)
    return w.reshape(w.shape[0], 2, 4, HEAD_DIM).transpose(0, 2, 1, 3).reshape(w.shape)


def _odd_w_in(w):
    w = w.astype(BF16)
    x, gc = w[:, 0:512], w[:, 512:1024]
    q = _interleave_kv_groups(w[:, 1024:1536], 1)
    k, v = w[:, 1536:1664], w[:, 1664:1792]
    gd = _interleave_kv_groups(w[:, 1792:2304], 1)
    return jnp.concatenate([x, gc, q, _pair_swap(q), gd, k, _pair_swap(k), v], axis=1)


def _rope_tables():
    t = np.arange(SEQ)
    row = (t // GRID_W).astype(np.float32)
    col = (t % GRID_W).astype(np.float32)
    half = HEAD_DIM // 2
    inv = (ROPE_THETA ** (-np.arange(0, half, 2, dtype=np.float32) / half)).astype(np.float32)
    ang = np.concatenate([row[:, None] * inv, col[:, None] * inv], axis=-1)
    cos = np.repeat(np.cos(ang), 2, axis=-1)
    sin = np.repeat(np.sin(ang), 2, axis=-1)
    cos = np.concatenate([np.ones((CTX_LEN, HEAD_DIM), np.float32), cos], axis=0)
    sin = np.concatenate([np.zeros((CTX_LEN, HEAD_DIM), np.float32), sin], axis=0)
    return np.tile(cos, (1, 2)).astype(np.float32), np.tile(sin, (1, 2)).astype(np.float32)


def _swap_pairs_vec(g):
    g2 = g.reshape(-1, 2)
    return jnp.stack([g2[:, 1], g2[:, 0]], axis=-1).reshape(g.shape)


def _block_diag(w):
    n, a, b = w.shape
    return jnp.einsum('mn,nij->minj', jnp.eye(n, dtype=w.dtype), w).reshape(n * a, n * b)


def kernel(x, c, ctx, c_ctx, ada_w, ada_b, pre_g, post_g, ev_w_in, ev_w_out, s5_lam_re, s5_lam_im, s5_log_dt,
           s5_b_re, s5_b_im, s5_c_re, s5_c_im, s5_d, s5_w_glu, s5_b_glu, na_rel_bias, od_w_in, od_w_out,
           lru_conv_w, lru_conv_b, lru_lam, lru_w_a, lru_b_a, lru_w_x, lru_b_x, gqa_q_norm, gqa_k_norm):
    pf_np, pr_np = _scan_perms()
    pf, pr = jnp.asarray(pf_np, BF16), jnp.asarray(pr_np, BF16)
    pft, prt = jnp.asarray(pf_np.T, BF16), jnp.asarray(pr_np.T, BF16)
    ph = jnp.asarray(_halo_perm(), BF16)

    c8 = jnp.concatenate([c, c_ctx[None], jnp.zeros((3, D_MODEL), F32)], axis=0)
    mod = _adaln(c8, ada_w, ada_b)

    p0 = _inproj((ctx, x), mod[0], pre_g[0], ev_w_in[0].astype(BF16))
    bcat, ccat, lre8, lim8 = _s5_weights(s5_lam_re[0], s5_lam_im[0], s5_log_dt[0], s5_b_re[0], s5_b_im[0],
                                         s5_c_re[0], s5_c_im[0])
    yf, yr = _s5(p0, pf, pr, pft, prt, bcat, ccat, lre8, lim8)
    attn0 = _na(p0, _na_bias_table(na_rel_bias[0]))
    h1 = _even_out(yf, yr, p0, attn0, ctx, x, mod[0], s5_d[0], s5_w_glu[0].astype(BF16), s5_b_glu[0],
                   ev_w_out[0].astype(BF16), post_g[0])

    p1 = _inproj(h1, mod[1], pre_g[1], _odd_w_in(od_w_in[0]))
    wg = _lru_gate_weights(lru_w_a[0], lru_w_x[0])
    bg = jnp.concatenate([lru_b_a[0], lru_b_x[0]], axis=1)
    hf, hr = _lru(p1, pf, pr, ph, pft, prt, _conv_coef(lru_conv_w[0]), lru_conv_b[0], wg, bg, lru_lam[0])
    cos_np, sin_np = _rope_tables()
    gq, gk = gqa_q_norm[0], gqa_k_norm[0]
    gvec = jnp.stack([jnp.tile(gq, 2), jnp.tile(_swap_pairs_vec(gq), 2),
                      jnp.tile(gk, 2), jnp.tile(_swap_pairs_vec(gk), 2)] + [jnp.zeros((128,), F32)] * 4)
    ones_np = np.kron(np.eye(2, dtype=np.float32), np.ones((HEAD_DIM, HEAD_DIM), np.float32))
    attn1 = _gqa(p1, jnp.asarray(cos_np), jnp.asarray(sin_np), gvec, jnp.asarray(ones_np, BF16))
    w_out1 = od_w_out[0].astype(BF16)
    w_out1 = jnp.concatenate([w_out1[:512], _interleave_kv_groups(w_out1[512:], 0)], axis=0)
    return _odd_out(hf, hr, p1, attn1, h1, mod[1], w_out1, post_g[1])
```

```python
import functools
import math

import numpy as np
import jax
import jax.numpy as jnp
from jax import lax
from jax.experimental import pallas as pl
from jax.experimental.pallas import tpu as pltpu

F32 = jnp.float32
BF16 = jnp.bfloat16
HIGHEST = lax.Precision.HIGHEST

D_MODEL = 1024
BATCH = 4
SEQ = 4096
GRID_W = 64
CTX_LEN = 256
LT = CTX_LEN + SEQ
HEAD_DIM = 64
EPS = 1e-6
S5_WIDTH = 512
S5_GROUP = 16
S5_GROUPS = 32
S5_STATE = 64
NA_HEADS = 8
NA_ROWS = 8
NA_COLS = 16
LRU_WIDTH = 512
LRU_BLOCKS = 8
LRU_BLOCK = 64
LRU_C = 8.0
GQA_HEADS = 8
ROPE_THETA = 10000.0
EVEN_IN = 3072
ODD_IN = 2944

TOK_BLK = 256
SCAN_BLK = 64
N_SCAN = LT // SCAN_BLK
CTX_SCAN = CTX_LEN // SCAN_BLK
Q_BLK = 256
NEG = -1e30
LOG2E = math.log2(math.e)
VMEM_LIMIT = 56 * 1024 * 1024


def _cparams(sem):
    return pltpu.CompilerParams(dimension_semantics=sem, vmem_limit_bytes=VMEM_LIMIT)


def _dot(a, b):
    return jnp.dot(a, b, preferred_element_type=F32)


def _dot_nt(a, b):
    return lax.dot_general(a, b, (((1,), (1,)), ((), ())), preferred_element_type=F32)


def _adaln_kernel(c_ref, w_ref, b_ref, o_ref):
    c = c_ref[...]
    s = c * jax.nn.sigmoid(c)
    o_ref[0] = jnp.dot(s, w_ref[0], preferred_element_type=F32, precision=HIGHEST) + b_ref[0]


def _adaln(c8, ada_w, ada_b):
    depth = ada_w.shape[0]
    nb = 3 * D_MODEL // 1024
    return pl.pallas_call(
        _adaln_kernel,
        grid=(depth, nb),
        in_specs=[
            pl.BlockSpec((8, D_MODEL), lambda i, n: (0, 0)),
            pl.BlockSpec((1, D_MODEL, 1024), lambda i, n: (i, 0, n)),
            pl.BlockSpec((1, 1, 1024), lambda i, n: (i, 0, n)),
        ],
        out_specs=pl.BlockSpec((1, 8, 1024), lambda i, n: (i, 0, n)),
        out_shape=jax.ShapeDtypeStruct((depth, 8, 3 * D_MODEL), F32),
        compiler_params=_cparams(("arbitrary", "arbitrary")),
        name="adaln",
    )(c8, ada_w, ada_b.reshape(depth, 1, 3 * D_MODEL))


def _cat_specs(width):
    return [pl.BlockSpec((1, TOK_BLK, width), lambda b, t: (b, 0, 0)),
            pl.BlockSpec((1, TOK_BLK, width), lambda b, t: (b, jnp.maximum(t - 1, 0), 0))]


def _inproj_kernel(*refs, two_src):
    b = pl.program_id(0)
    tb = pl.program_id(1)
    if two_src:
        c_ref, x_ref, mod_ref, g_ref, w_ref, o_ref = refs
        x = jnp.where(tb == 0, c_ref[0], x_ref[0])
    else:
        x_ref, mod_ref, g_ref, w_ref, o_ref = refs
        x = x_ref[0]
    r = lax.rsqrt(jnp.mean(x * x, axis=-1, keepdims=True) + EPS)
    row = jnp.where(tb == 0, BATCH, b)
    m = mod_ref[pl.ds(row, 1), :]
    sh = m[:, :D_MODEL]
    sc = m[:, D_MODEL:2 * D_MODEL]
    y = (x * r) * g_ref[...]
    y = y * (1.0 + sc) + sh
    o_ref[0] = _dot(y.astype(BF16), w_ref[...]).astype(BF16)


def _inproj(src, mod, g, w_bf):
    n = w_bf.shape[1]
    two_src = isinstance(src, tuple)
    if two_src:
        src_specs = _cat_specs(D_MODEL)
    else:
        src_specs = [pl.BlockSpec((1, TOK_BLK, D_MODEL), lambda b, t: (b, t, 0))]
        src = (src,)
    return pl.pallas_call(
        functools.partial(_inproj_kernel, two_src=two_src),
        grid=(BATCH, LT // TOK_BLK),
        in_specs=src_specs + [
            pl.BlockSpec((8, 3 * D_MODEL), lambda b, t: (0, 0)),
            pl.BlockSpec((1, D_MODEL), lambda b, t: (0, 0)),
            pl.BlockSpec((D_MODEL, n), lambda b, t: (0, 0)),
        ],
        out_specs=pl.BlockSpec((1, TOK_BLK, n), lambda b, t: (b, t, 0)),
        out_shape=jax.ShapeDtypeStruct((BATCH, LT, n), BF16),
        compiler_params=_cparams(("arbitrary", "arbitrary")),
        name="inproj",
    )(*src, mod, g.reshape(1, D_MODEL), w_bf)


def _fwd_blk(i):
    return i


def _rev_blk(i):
    return jnp.where(i < CTX_SCAN, CTX_SCAN - 1 - i, N_SCAN + CTX_SCAN - 1 - i)


def _scan_perms():
    t = SCAN_BLK
    pf = np.zeros((8 * t, BATCH * t), np.float32)
    pr = np.zeros((8 * t, BATCH * t), np.float32)
    for tt in range(t):
        for b in range(BATCH):
            pf[tt * 8 + b, b * t + tt] = 1.0
            pr[tt * 8 + 4 + b, b * t + (t - 1 - tt)] = 1.0
    return pf, pr


def _halo_perm():
    ph = np.zeros((32, 16 * 4 * BATCH), np.float32)
    for b in range(BATCH):
        ph[1 * 8 + b, 0 * 64 + b * 16 + 15] = 1.0
        ph[2 * 8 + b, 1 * 64 + b * 16 + 0] = 1.0
        ph[3 * 8 + b, 1 * 64 + b * 16 + 1] = 1.0
        ph[0 * 8 + 4 + b, 3 * 64 + b * 16 + 1] = 1.0
        ph[1 * 8 + 4 + b, 3 * 64 + b * 16 + 0] = 1.0
        ph[2 * 8 + 4 + b, 2 * 64 + b * 16 + 15] = 1.0
    return ph


def _conv_coef(conv_w):
    zero = jnp.zeros((1, LRU_WIDTH), conv_w.dtype)
    fwd = jnp.concatenate([zero, conv_w], axis=0)
    rev = jnp.concatenate([conv_w[::-1], zero], axis=0)
    return jnp.concatenate([jnp.broadcast_to(fwd[:, None], (5, 4, LRU_WIDTH)),
                            jnp.broadcast_to(rev[:, None], (5, 4, LRU_WIDTH))], axis=1)


def _s5_kernel(uf_ref, ur_ref, pf_ref, pr_ref, pft_ref, prt_ref, bcat_ref, ccat_ref, lre_ref, lim_ref,
               yf_ref, yr_ref, buf0_ref, buf1_ref, st_ref):
    i = pl.program_id(0)

    @pl.when(i == 0)
    def _():
        st_ref[...] = jnp.zeros_like(st_ref)
        buf0_ref[...] = jnp.zeros_like(buf0_ref)
        buf1_ref[...] = jnp.zeros_like(buf1_ref)

    nrow = 8 * SCAN_BLK

    def step(buf_a, buf_b):
        vals = {}

        def perm():
            uf = uf_ref[...].reshape(BATCH * SCAN_BLK, S5_WIDTH)
            ur = ur_ref[...].reshape(BATCH * SCAN_BLK, S5_WIDTH)
            vals['u_f'] = _dot(pf_ref[...], uf).astype(BF16)
            vals['u_r'] = _dot(pr_ref[...], ur).astype(BF16)

        def readout(j):
            is_f = (lax.broadcasted_iota(jnp.int32, (nrow, 128), 0) & 7) < 4
            yj = _dot(buf_a[:, j * 1024:(j + 1) * 1024].astype(BF16), ccat_ref[j])
            vals['y%d' % j] = jnp.where(is_f, yj[:, :128], yj[:, 128:]).astype(BF16)

        def project(j):
            lhs = jnp.concatenate([vals['u_f'][:, j * 128:(j + 1) * 128], vals['u_r'][:, j * 128:(j + 1) * 128]],
                                  axis=1)
            buf_a[:, j * 1024:(j + 1) * 1024] = _dot(lhs, bcat_ref[j])

        def unperm(p_ref, o_ref):
            yp = jnp.concatenate([vals['y%d' % j] for j in range(4)], axis=1)
            o_ref[...] = _dot(p_ref[...], yp).astype(BF16).reshape(BATCH, SCAN_BLK, S5_WIDTH)

        mxu = [perm]
        for j in range(4):
            mxu += [functools.partial(readout, j), functools.partial(project, j)]
        mxu += [functools.partial(unperm, pft_ref, yf_ref), functools.partial(unperm, prt_ref, yr_ref)]

        def scan(j, t0):
            c_re = slice(j * 1024, j * 1024 + 512)
            c_im = slice(j * 1024 + 512, (j + 1) * 1024)
            lre = lre_ref[j]
            lim = lim_ref[j]
            hre = st_ref[:, c_re]
            him = st_ref[:, c_im]
            for t in range(t0, t0 + 16):
                rows = slice(t * 8, (t + 1) * 8)
                nre = lre * hre - lim * him + buf_b[rows, c_re]
                nim = lre * him + lim * hre + buf_b[rows, c_im]
                buf_b[rows, c_re] = nre
                buf_b[rows, c_im] = nim
                hre, him = nre, nim
            st_ref[:, c_re] = hre
            st_ref[:, c_im] = him

        vpu = [functools.partial(scan, j, t0) for j in range(4) for t0 in range(0, SCAN_BLK, 16)]

        per_piece = [1, 2, 1, 2, 1, 2, 1, 2, 1, 2, 1]
        for piece, n_scan in zip(mxu, per_piece):
            piece()
            for _ in range(n_scan):
                vpu.pop(0)()

    @pl.when(i % 2 == 0)
    def _():
        step(buf0_ref, buf1_ref)

    @pl.when(i % 2 == 1)
    def _():
        step(buf1_ref, buf0_ref)


def _s5_weights(lam_re, lam_im, log_dt, b_re, b_im, c_re, c_im):
    a = lam_re.astype(F32) * jnp.exp(log_dt.astype(F32))[..., None]
    b = lam_im.astype(F32) * jnp.exp(log_dt.astype(F32))[..., None]
    lbr = jnp.exp(a) * jnp.cos(b)
    lbi = jnp.exp(a) * jnp.sin(b)
    nr = jnp.expm1(a) * jnp.cos(b) - 2.0 * jnp.sin(0.5 * b) ** 2
    d2 = lam_re * lam_re + lam_im * lam_im
    qr = (nr * lam_re + lbi * lam_im) / d2
    qi = (lbi * lam_re - nr * lam_im) / d2
    bbr = qr[..., None] * b_re - qi[..., None] * b_im
    bbi = qr[..., None] * b_im + qi[..., None] * b_re
    same = jnp.asarray(np.eye(8, dtype=np.float32))
    bb = jnp.stack([bbr, bbi], axis=1).reshape(2, 2, 4, 8, S5_STATE, S5_GROUP)
    bb = bb.transpose(2, 0, 5, 1, 3, 4)[:, :, None] * same[:, None, None, :, None]
    bcat = bb.reshape(4, 2 * 128, 2 * 512).astype(BF16)
    cc = jnp.stack([c_re.astype(F32), -c_im.astype(F32)], axis=1).reshape(2, 2, 4, 8, S5_GROUP, S5_STATE)
    cc = cc.transpose(2, 1, 3, 5, 0, 4)[:, :, :, :, :, None] * same[:, None, None, :, None]
    ccat = cc.reshape(4, 2 * 512, 2 * 128).astype(BF16)
    lre8 = jnp.repeat(lbr.reshape(2, 4, 512).transpose(1, 0, 2), 4, axis=1)
    lim8 = jnp.repeat(lbi.reshape(2, 4, 512).transpose(1, 0, 2), 4, axis=1)
    return bcat, ccat, lre8, lim8


def _s5(p, pf, pr, pft, prt, bcat, ccat, lre8, lim8):
    nrow = 8 * SCAN_BLK
    const2 = lambda i: (0, 0)
    const3 = lambda i: (0, 0, 0)
    proj = lambda i: jnp.minimum(i, N_SCAN - 1)
    read = lambda i: jnp.clip(i - 2, 0, N_SCAN - 1)
    return pl.pallas_call(
        _s5_kernel,
        grid=(N_SCAN + 2,),
        in_specs=[
            pl.BlockSpec((BATCH, SCAN_BLK, S5_WIDTH), lambda i: (0, _fwd_blk(proj(i)), 0)),
            pl.BlockSpec((BATCH, SCAN_BLK, S5_WIDTH), lambda i: (0, _rev_blk(proj(i)), 0)),
            pl.BlockSpec(pf.shape, const2), pl.BlockSpec(pr.shape, const2),
            pl.BlockSpec(pft.shape, const2), pl.BlockSpec(prt.shape, const2),
            pl.BlockSpec(bcat.shape, const3), pl.BlockSpec(ccat.shape, const3),
            pl.BlockSpec(lre8.shape, const3), pl.BlockSpec(lim8.shape, const3),
        ],
        out_specs=[
            pl.BlockSpec((BATCH, SCAN_BLK, S5_WIDTH), lambda i: (0, _fwd_blk(read(i)), 0)),
            pl.BlockSpec((BATCH, SCAN_BLK, S5_WIDTH), lambda i: (0, _rev_blk(read(i)), 0)),
        ],
        out_shape=[jax.ShapeDtypeStruct((BATCH, LT, S5_WIDTH), BF16)] * 2,
        scratch_shapes=[pltpu.VMEM((nrow, 4096), F32), pltpu.VMEM((nrow, 4096), F32),
                        pltpu.VMEM((8, 4096), F32)],
        compiler_params=_cparams(("arbitrary",)),
        name="s5_scan",
    )(p, p, pf, pr, pft, prt, bcat, ccat, lre8, lim8)


def _na_bias_table(rel_bias):
    w = np.arange(GRID_W)
    cs = np.clip(w - NA_COLS // 2, 0, GRID_W - NA_COLS)
    cp = np.arange(GRID_W)
    valid = (cp[None, :] >= cs[:, None]) & (cp[None, :] < cs[:, None] + NA_COLS)
    dc = cp[None, :] - w[:, None] + (NA_COLS - 1)
    n_dc = 2 * NA_COLS - 1
    onehot = ((dc[None] == np.arange(n_dc)[:, None, None]) & valid[None]).astype(np.float32)
    n_dr = 2 * NA_ROWS - 1
    sel = (np.arange(8)[:, None, None] + np.arange(NA_ROWS)[None, :, None] ==
           np.arange(n_dr)[None, None, :]).astype(np.float32)
    rows = jnp.einsum('oir,hrd->ohid', jnp.asarray(sel), rel_bias.astype(F32), precision=HIGHEST)
    tabs = jnp.einsum('ohid,dwc->ohwic', rows, jnp.asarray(onehot), precision=HIGHEST)
    tabs = tabs + jnp.asarray(np.where(valid, 0.0, NEG).astype(np.float32))[:, None, :]
    tabs = tabs.reshape(8, NA_HEADS // 2, 128, NA_ROWS * GRID_W)
    return jnp.concatenate([tabs, jnp.full((1,) + tabs.shape[1:], NEG, F32)], axis=0)


NA_STEP_ROWS = TOK_BLK // GRID_W


def _na_kernel(q_ref, k_ref, v_ref, bias_ref, o_ref):
    s_idx = pl.program_id(1)
    is_ctx = s_idx == 0
    lo = lax.broadcasted_iota(jnp.int32, (GRID_W, 128), 1) < HEAD_DIM
    nwin = NA_ROWS * GRID_W
    for rr in range(NA_STEP_ROWS):
        r = jnp.maximum(s_idx - 1, 0) * NA_STEP_ROWS + rr
        start = jnp.clip(r - NA_ROWS // 2, 0, SEQ // GRID_W - NA_ROWS)
        koff = pl.multiple_of(CTX_LEN + start * GRID_W, GRID_W)
        off = jnp.where(is_ctx, 8, start - r + NA_ROWS - 1)
        rows = slice(rr * GRID_W, (rr + 1) * GRID_W)
        pairs = range(NA_HEADS // 2)
        cols = [slice(hp * 128, (hp + 1) * 128) for hp in pairs]
        s_loc, s_ctx, m, p_loc, p_ctx, l, o = [], [], [], [], [], [], []
        for hp in pairs:
            qp = q_ref[0, rows, cols[hp]] * jnp.asarray(HEAD_DIM ** -0.5, BF16)
            zero = jnp.zeros_like(qp)
            q2 = jnp.concatenate([jnp.where(lo, qp, zero), jnp.where(lo, zero, qp)], axis=0)
            s_loc.append(_dot_nt(q2, k_ref[0, pl.ds(koff, nwin), cols[hp]]) + bias_ref[off, hp])
            s_ctx.append(_dot_nt(q2, k_ref[0, 0:CTX_LEN, cols[hp]]))
        for hp in pairs:
            m.append(jnp.maximum(jnp.max(s_loc[hp], axis=-1, keepdims=True),
                                 jnp.max(s_ctx[hp], axis=-1, keepdims=True)))
        for hp in pairs:
            pl_, pc_ = jnp.exp(s_loc[hp] - m[hp]), jnp.exp(s_ctx[hp] - m[hp])
            l.append(jnp.sum(pl_, axis=-1, keepdims=True) + jnp.sum(pc_, axis=-1, keepdims=True))
            p_loc.append(pl_.astype(BF16))
            p_ctx.append(pc_.astype(BF16))
        for hp in pairs:
            o.append(_dot(p_loc[hp], v_ref[0, pl.ds(koff, nwin), cols[hp]]) +
                     _dot(p_ctx[hp], v_ref[0, 0:CTX_LEN, cols[hp]]))
        for hp in pairs:
            oh = o[hp] / l[hp]
            o_ref[0, rows, cols[hp]] = jnp.where(lo, oh[:GRID_W], oh[GRID_W:]).astype(BF16)


def _na(p, bias_tb):
    return pl.pallas_call(
        _na_kernel,
        grid=(BATCH, LT // TOK_BLK),
        in_specs=[
            pl.BlockSpec((1, TOK_BLK, 512), lambda b, s: (b, s, 2)),
            pl.BlockSpec((1, LT, 512), lambda b, s: (b, 0, 3)),
            pl.BlockSpec((1, LT, 512), lambda b, s: (b, 0, 4)),
            pl.BlockSpec(bias_tb.shape, lambda b, s: (0, 0, 0, 0)),
        ],
        out_specs=pl.BlockSpec((1, TOK_BLK, 512), lambda b, s: (b, s, 0)),
        out_shape=jax.ShapeDtypeStruct((BATCH, LT, 512), BF16),
        compiler_params=_cparams(("arbitrary", "arbitrary")),
        name="na_attn",
    )(p, p, p, bias_tb)


def _gelu_tanh(x):
    return 0.5 * x * (1.0 + jnp.tanh(math.sqrt(2.0 / math.pi) * (x + 0.044715 * (x * x * x))))


def _sigmoid(x):
    return 0.5 + 0.5 * jnp.tanh(0.5 * x)


def _silu(x):
    return x * _sigmoid(x)


def _finish(h, out, gt, pg):
    r = lax.rsqrt(jnp.mean(out * out, axis=-1, keepdims=True) + EPS)
    return h + gt * ((out * r) * pg)


def _even_out_kernel(yf_ref, yr_ref, u_ref, ga_ref, gb_ref, at_ref, hc_ref, hl_ref, mod_ref, d_ref, wg_ref,
                     bg_ref, wo_ref, pg_ref, o_ref):
    b = pl.program_id(0)
    tb = pl.program_id(1)
    row = jnp.where(tb == 0, BATCH, b)
    gt = mod_ref[pl.ds(row, 1), 2 * D_MODEL:]
    u = u_ref[0].astype(F32)
    y = d_ref[...] * u + yf_ref[0].astype(F32) + yr_ref[0].astype(F32)
    y = _gelu_tanh(y)
    y = y * _sigmoid(_dot(y.astype(BF16), wg_ref[...]) + bg_ref[...])
    y_a = y * _silu(ga_ref[0].astype(F32))
    y_b = at_ref[0].astype(F32) * _silu(gb_ref[0].astype(F32))
    out = _dot(y_a.astype(BF16), wo_ref[0:512, :]) + _dot(y_b.astype(BF16), wo_ref[512:1024, :])
    h = jnp.where(tb == 0, hc_ref[0], hl_ref[0])
    o_ref[0] = _finish(h, out, gt, pg_ref[...])


def _even_out(yf, yr, p, attn, ctx, x, mod, d_skip, w_glu, b_glu, w_out, post_g):
    tok = lambda c: pl.BlockSpec((1, TOK_BLK, 512), lambda b, t: (b, t, c))
    full = lambda a: pl.BlockSpec(a.shape, lambda b, t: (0,) * a.ndim)
    d2, bg2, pg2 = d_skip.reshape(1, 512), b_glu.reshape(1, 512), post_g.reshape(1, D_MODEL)
    return pl.pallas_call(
        _even_out_kernel,
        grid=(BATCH, LT // TOK_BLK),
        in_specs=[tok(0), tok(0), tok(0), tok(1), tok(5), tok(0)] + _cat_specs(D_MODEL) +
                 [full(mod), full(d2), full(w_glu), full(bg2), full(w_out), full(pg2)],
        out_specs=pl.BlockSpec((1, TOK_BLK, D_MODEL), lambda b, t: (b, t, 0)),
        out_shape=jax.ShapeDtypeStruct((BATCH, LT, D_MODEL), F32),
        compiler_params=_cparams(("arbitrary", "arbitrary")),
        name="even_out",
    )(yf, yr, p, p, p, attn, ctx, x, mod, d2, w_glu, bg2, w_out, pg2)


def _softplus(z):
    return jnp.maximum(z, 0.0) + jnp.log1p(jnp.exp(-jnp.abs(z)))


def _lru_kernel(xf_ref, xfp_ref, xfn_ref, xr_ref, xrp_ref, xrn_ref, pf_ref, pr_ref, ph_ref, pft_ref, prt_ref,
                coef_ref, cb_ref, wg_ref, bg_ref, lam_ref, hf_ref, hr_ref, a_ref, b_ref, st_ref):
    i = pl.program_id(0)

    @pl.when(i == 0)
    def _():
        st_ref[...] = jnp.zeros_like(st_ref)

    nrow = 8 * SCAN_BLK

    def halo(prev_ref, next_ref, blk):
        prev_ok = jnp.logical_and(blk != 0, blk != CTX_SCAN)
        next_ok = jnp.logical_and(blk != CTX_SCAN - 1, blk != N_SCAN - 1)
        pv = prev_ref[...].reshape(BATCH * 16, LRU_WIDTH)
        nx = next_ref[...].reshape(BATCH * 16, LRU_WIDTH)
        return [jnp.where(prev_ok, pv, jnp.zeros_like(pv)), jnp.where(next_ok, nx, jnp.zeros_like(nx))]

    hal = jnp.concatenate(halo(xfp_ref, xfn_ref, _fwd_blk(i)) + halo(xrp_ref, xrn_ref, _rev_blk(i)), axis=0)
    xh = _dot(ph_ref[...], hal)
    xp = (_dot(pf_ref[...], xf_ref[...].reshape(BATCH * SCAN_BLK, LRU_WIDTH)) +
          _dot(pr_ref[...], xr_ref[...].reshape(BATCH * SCAN_BLK, LRU_WIDTH)))
    x_ext = jnp.concatenate([xh[:16], xp, xh[16:]], axis=0)
    xc = jnp.zeros((SCAN_BLK, 8, LRU_WIDTH), F32) + cb_ref[...]
    for s in range(5):
        xc = xc + x_ext[s * 8:s * 8 + nrow].reshape(SCAN_BLK, 8, LRU_WIDTH) * coef_ref[s]
    xc = xc.reshape(nrow, LRU_WIDTH)

    is_f = (lax.broadcasted_iota(jnp.int32, (nrow, 1), 0) & 7) < 4
    xcb = xc.astype(BF16)
    zero = jnp.zeros_like(xcb[:, :256])
    pre_r, pre_i = [], []
    for c in range(2):
        xt = xcb[:, c * 256:(c + 1) * 256]
        lhs = jnp.concatenate([jnp.where(is_f, xt, zero), jnp.where(is_f, zero, xt)], axis=1)
        pre = _dot(lhs, wg_ref[c])
        pre_r.append(pre[:, :256])
        pre_i.append(pre[:, 256:])
    bias = jnp.where(is_f, bg_ref[0:1, :], bg_ref[1:2, :])
    gate_r = _sigmoid(jnp.concatenate(pre_r, axis=1) + bias[:, :LRU_WIDTH])
    gate_i = _sigmoid(jnp.concatenate(pre_i, axis=1) + bias[:, LRU_WIDTH:])
    sp = _softplus(-lam_ref[...])
    log_a = (-LRU_C) * gate_r * jnp.where(is_f, sp[0:1, :], sp[1:2, :])
    a = jnp.exp(log_a)
    a_ref[...] = a
    one_m_a2 = -jnp.tanh(log_a) * (a * a + 1.0)
    b_ref[...] = jnp.sqrt(jnp.maximum(one_m_a2, 0.0)) * gate_i * xc

    def body(t, h):
        r0 = pl.multiple_of(t * 8, 8)
        h = a_ref[pl.ds(r0, 8), :] * h + b_ref[pl.ds(r0, 8), :]
        b_ref[pl.ds(r0, 8), :] = h
        return h

    st_ref[...] = lax.fori_loop(0, SCAN_BLK, body, st_ref[...], unroll=4)
    hs = b_ref[...].astype(BF16)
    hf_ref[...] = _dot(pft_ref[...], hs).astype(BF16).reshape(BATCH, SCAN_BLK, LRU_WIDTH)
    hr_ref[...] = _dot(prt_ref[...], hs).astype(BF16).reshape(BATCH, SCAN_BLK, LRU_WIDTH)


def _lru_gate_weights(w_a, w_x):
    w = jnp.stack([w_a, w_x], axis=1).reshape(2, 2, 2, 4, LRU_BLOCK, LRU_BLOCK)
    same = jnp.asarray(np.eye(4, dtype=np.float32))
    w = w.transpose(2, 0, 3, 4, 1, 5)[:, :, :, :, :, None] * same[:, None, None, :, None]
    return w.reshape(2, 512, 512).astype(BF16)


def _lru(p, pf, pr, ph, pft, prt, coef, conv_b, wg, bg, lam):
    nrow = 8 * SCAN_BLK
    full = lambda a: pl.BlockSpec(a.shape, lambda i: (0,) * a.ndim)
    cur = lambda f: pl.BlockSpec((BATCH, SCAN_BLK, LRU_WIDTH), lambda i: (0, f(i), 0))
    prev = lambda f: pl.BlockSpec((BATCH, 16, LRU_WIDTH), lambda i: (0, jnp.maximum(f(i) * 4 - 1, 0), 0))
    nxt = lambda f: pl.BlockSpec((BATCH, 16, LRU_WIDTH),
                                 lambda i: (0, jnp.minimum(f(i) * 4 + 4, LT // 16 - 1), 0))
    cb2 = conv_b.reshape(1, LRU_WIDTH)
    return pl.pallas_call(
        _lru_kernel,
        grid=(N_SCAN,),
        in_specs=[cur(_fwd_blk), prev(_fwd_blk), nxt(_fwd_blk), cur(_rev_blk), prev(_rev_blk), nxt(_rev_blk),
                  full(pf), full(pr), full(ph), full(pft), full(prt), full(coef), full(cb2), full(wg), full(bg),
                  full(lam)],
        out_specs=[
            pl.BlockSpec((BATCH, SCAN_BLK, LRU_WIDTH), lambda i: (0, _fwd_blk(i), 0)),
            pl.BlockSpec((BATCH, SCAN_BLK, LRU_WIDTH), lambda i: (0, _rev_blk(i), 0)),
        ],
        out_shape=[jax.ShapeDtypeStruct((BATCH, LT, LRU_WIDTH), BF16)] * 2,
        scratch_shapes=[pltpu.VMEM((nrow, LRU_WIDTH), F32), pltpu.VMEM((nrow, LRU_WIDTH), F32),
                        pltpu.VMEM((8, LRU_WIDTH), F32)],
        compiler_params=_cparams(("arbitrary",)),
        name="lru_scan",
    )(p, p, p, p, p, p, pf, pr, ph, pft, prt, coef, cb2, wg, bg, lam)


def _norm_rope(x, xrot, g, grot, cos, sin, ones_blk):
    x = x.astype(F32)
    xrot = xrot.astype(F32)
    ms = _dot((x * x).astype(BF16), ones_blk) * (1.0 / HEAD_DIM)
    rs = lax.rsqrt(ms + EPS)
    return rs * ((x * g) * cos + (xrot * grot) * sin)


def _gqa_kernel(q_ref, qr_ref, k_ref, kr_ref, v_ref, cq_ref, sq_ref, ck_ref, sk_ref, g_ref, ones_ref,
                o_ref, kn_ref, vlo_ref, vhi_ref):
    qb = pl.program_id(1)
    ones_blk = ones_ref[...]

    @pl.when(qb == 0)
    def _():
        kn = _norm_rope(k_ref[0], kr_ref[0], g_ref[2:3, :], g_ref[3:4, :], ck_ref[...], sk_ref[...], ones_blk)
        kn_ref[...] = kn.astype(BF16)
        v = v_ref[0]
        lo_k = lax.broadcasted_iota(jnp.int32, (LT, 128), 1) < HEAD_DIM
        one = jnp.ones_like(v)
        vlo_ref[...] = jnp.where(lo_k, v, one)
        vhi_ref[...] = jnp.where(lo_k, one, v)

    lo = lax.broadcasted_iota(jnp.int32, (Q_BLK, 128), 1) < HEAD_DIM
    kn = kn_ref[...]
    cos = cq_ref[...]
    sin = sq_ref[...]
    n_pb = GQA_HEADS // 2

    def scores(pb):
        cols = slice(pb * 128, (pb + 1) * 128)
        qn = _norm_rope(q_ref[0, :, cols], qr_ref[0, :, cols], g_ref[0:1, :], g_ref[1:2, :], cos, sin, ones_blk)
        qn = (qn * (HEAD_DIM ** -0.5 * LOG2E)).astype(BF16)
        zero = jnp.zeros_like(qn)
        qs = jnp.concatenate([jnp.where(lo, qn, zero), jnp.where(lo, zero, qn)], axis=0)
        return _dot_nt(qs, kn)

    s_next = scores(0)
    for pb in range(n_pb):
        cols = slice(pb * 128, (pb + 1) * 128)
        s = s_next
        if pb + 1 < n_pb:
            s_next = scores(pb + 1)
        m = jnp.max(s, axis=-1, keepdims=True)
        p = jnp.exp2(s - m).astype(BF16)
        o_lo = _dot(p[:Q_BLK], vlo_ref[...])
        o_hi = _dot(p[Q_BLK:], vhi_ref[...])
        num = jnp.where(lo, o_lo, o_hi)
        den = pltpu.roll(jnp.where(lo, o_hi, o_lo), HEAD_DIM, axis=1)
        o_ref[0, :, cols] = (num / den).astype(BF16)


def _gqa(p, cos128, sin128, gvec, ones_blk):
    nqb = SEQ // Q_BLK
    cb = CTX_LEN // Q_BLK
    return pl.pallas_call(
        _gqa_kernel,
        grid=(BATCH, nqb),
        in_specs=[
            pl.BlockSpec((1, Q_BLK, 512), lambda b, q: (b, q + cb, 2)),
            pl.BlockSpec((1, Q_BLK, 512), lambda b, q: (b, q + cb, 3)),
            pl.BlockSpec((1, LT, 128), lambda b, q: (b, 0, 20)),
            pl.BlockSpec((1, LT, 128), lambda b, q: (b, 0, 21)),
            pl.BlockSpec((1, LT, 128), lambda b, q: (b, 0, 22)),
            pl.BlockSpec((Q_BLK, 128), lambda b, q: (q + cb, 0)),
            pl.BlockSpec((Q_BLK, 128), lambda b, q: (q + cb, 0)),
            pl.BlockSpec((LT, 128), lambda b, q: (0, 0)),
            pl.BlockSpec((LT, 128), lambda b, q: (0, 0)),
            pl.BlockSpec((8, 128), lambda b, q: (0, 0)),
            pl.BlockSpec((128, 128), lambda b, q: (0, 0)),
        ],
        out_specs=pl.BlockSpec((1, Q_BLK, 512), lambda b, q: (b, q, 0)),
        out_shape=jax.ShapeDtypeStruct((BATCH, SEQ, 512), BF16),
        scratch_shapes=[pltpu.VMEM((LT, 128), BF16)] * 3,
        compiler_params=_cparams(("arbitrary", "arbitrary")),
        name="gqa_attn",
    )(p, p, p, p, p, cos128, sin128, cos128, sin128, gvec, ones_blk)


def _odd_out_kernel(hf_ref, hr_ref, gc_ref, gd_ref, at_ref, h_ref, mod_ref, wo_ref, pg_ref, o_ref):
    b = pl.program_id(0)
    gt = mod_ref[pl.ds(b, 1), 2 * D_MODEL:]
    y_c = (hf_ref[0].astype(F32) + hr_ref[0].astype(F32)) * _silu(gc_ref[0].astype(F32))
    y_d = at_ref[0].astype(F32) * _silu(gd_ref[0].astype(F32))
    out = _dot(y_c.astype(BF16), wo_ref[0:512, :]) + _dot(y_d.astype(BF16), wo_ref[512:1024, :])
    o_ref[0] = _finish(h_ref[0], out, gt, pg_ref[...])


def _odd_out(hf, hr, p, attn, h_cat, mod, w_out, post_g):
    cb = CTX_LEN // TOK_BLK
    cat = lambda c: pl.BlockSpec((1, TOK_BLK, 512), lambda b, t: (b, t + cb, c))
    full = lambda a: pl.BlockSpec(a.shape, lambda b, t: (0,) * a.ndim)
    pg2 = post_g.reshape(1, D_MODEL)
    return pl.pallas_call(
        _odd_out_kernel,
        grid=(BATCH, SEQ // TOK_BLK),
        in_specs=[cat(0), cat(0), cat(1), cat(4),
                  pl.BlockSpec((1, TOK_BLK, 512), lambda b, t: (b, t, 0)),
                  pl.BlockSpec((1, TOK_BLK, D_MODEL), lambda b, t: (b, t + cb, 0)),
                  full(mod), full(w_out), full(pg2)],
        out_specs=pl.BlockSpec((1, TOK_BLK, D_MODEL), lambda b, t: (b, t, 0)),
        out_shape=jax.ShapeDtypeStruct((BATCH, SEQ, D_MODEL), F32),
        compiler_params=_cparams(("arbitrary", "arbitrary")),
        name="odd_out",
    )(hf, hr, p, p, attn, h_cat, mod, w_out, pg2)


def _pair_swap(w):
    n = w.shape[1]
    r = np.zeros((n, n), np.float32)
    r[np.arange(1, n, 2), np.arange(0, n, 2)] = -1.0
    r[np.arange(0, n, 2), np.arange(1, n, 2)] = 1.0
    return jnp.dot(w, jnp.asarray(r, BF16), preferred_element_type=F32).astype(BF16)


def _interleave_kv_groups(w, axis):
    if axis == 0:
        return w.reshape(2, 4, HEAD_DIM, w.shape[1]).transpose(1, 0, 2, 3).reshape(w.shape)
    return w.reshape(w.shape[0], 2, 4, HEAD_DIM).transpose(0, 2, 1, 3).reshape(w.shape)


def _odd_w_in(w):
    w = w.astype(BF16)
    x, gc = w[:, 0:512], w[:, 512:1024]
    q = _interleave_kv_groups(w[:, 1024:1536], 1)
    k, v = w[:, 1536:1664], w[:, 1664:1792]
    gd = _interleave_kv_groups(w[:, 1792:2304], 1)
    return jnp.concatenate([x, gc, q, _pair_swap(q), gd, k, _pair_swap(k), v], axis=1)


def _rope_tables():
    t = np.arange(SEQ)
    row = (t // GRID_W).astype(np.float32)
    col = (t % GRID_W).astype(np.float32)
    half = HEAD_DIM // 2
    inv = (ROPE_THETA ** (-np.arange(0, half, 2, dtype=np.float32) / half)).astype(np.float32)
    ang = np.concatenate([row[:, None] * inv, col[:, None] * inv], axis=-1)
    cos = np.repeat(np.cos(ang), 2, axis=-1)
    sin = np.repeat(np.sin(ang), 2, axis=-1)
    cos = np.concatenate([np.ones((CTX_LEN, HEAD_DIM), np.float32), cos], axis=0)
    sin = np.concatenate([np.zeros((CTX_LEN, HEAD_DIM), np.float32), sin], axis=0)
    return np.tile(cos, (1, 2)).astype(np.float32), np.tile(sin, (1, 2)).astype(np.float32)


def _swap_pairs_vec(g):
    g2 = g.reshape(-1, 2)
    return jnp.stack([g2[:, 1], g2[:, 0]], axis=-1).reshape(g.shape)


def _block_diag(w):
    n, a, b = w.shape
    return jnp.einsum('mn,nij->minj', jnp.eye(n, dtype=w.dtype), w).reshape(n * a, n * b)


def kernel(x, c, ctx, c_ctx, ada_w, ada_b, pre_g, post_g, ev_w_in, ev_w_out, s5_lam_re, s5_lam_im, s5_log_dt,
           s5_b_re, s5_b_im, s5_c_re, s5_c_im, s5_d, s5_w_glu, s5_b_glu, na_rel_bias, od_w_in, od_w_out,
           lru_conv_w, lru_conv_b, lru_lam, lru_w_a, lru_b_a, lru_w_x, lru_b_x, gqa_q_norm, gqa_k_norm):
    pf_np, pr_np = _scan_perms()
    pf, pr = jnp.asarray(pf_np, BF16), jnp.asarray(pr_np, BF16)
    pft, prt = jnp.asarray(pf_np.T, BF16), jnp.asarray(pr_np.T, BF16)
    ph = jnp.asarray(_halo_perm(), BF16)

    c8 = jnp.concatenate([c, c_ctx[None], jnp.zeros((3, D_MODEL), F32)], axis=0)
    mod = _adaln(c8, ada_w, ada_b)

    p0 = _inproj((ctx, x), mod[0], pre_g[0], ev_w_in[0].astype(BF16))
    bcat, ccat, lre8, lim8 = _s5_weights(s5_lam_re[0], s5_lam_im[0], s5_log_dt[0], s5_b_re[0], s5_b_im[0],
                                         s5_c_re[0], s5_c_im[0])
    yf, yr = _s5(p0, pf, pr, pft, prt, bcat, ccat, lre8, lim8)
    attn0 = _na(p0, _na_bias_table(na_rel_bias[0]))
    h1 = _even_out(yf, yr, p0, attn0, ctx, x, mod[0], s5_d[0], s5_w_glu[0].astype(BF16), s5_b_glu[0],
                   ev_w_out[0].astype(BF16), post_g[0])

    p1 = _inproj(h1, mod[1], pre_g[1], _odd_w_in(od_w_in[0]))
    wg = _lru_gate_weights(lru_w_a[0], lru_w_x[0])
    bg = jnp.concatenate([lru_b_a[0], lru_b_x[0]], axis=1)
    hf, hr = _lru(p1, pf, pr, ph, pft, prt, _conv_coef(lru_conv_w[0]), lru_conv_b[0], wg, bg, lru_lam[0])
    cos_np, sin_np = _rope_tables()
    gq, gk = gqa_q_norm[0], gqa_k_norm[0]
    gvec = jnp.stack([jnp.tile(gq, 2), jnp.tile(_swap_pairs_vec(gq), 2),
                      jnp.tile(gk, 2), jnp.tile(_swap_pairs_vec(gk), 2)] + [jnp.zeros((128,), F32)] * 4)
    ones_np = np.kron(np.eye(2, dtype=np.float32), np.ones((HEAD_DIM, HEAD_DIM), np.float32))
    attn1 = _gqa(p1, jnp.asarray(cos_np), jnp.asarray(sin_np), gvec, jnp.asarray(ones_np, BF16))
    w_out1 = od_w_out[0].astype(BF16)
    w_out1 = jnp.concatenate([w_out1[:512], _interleave_kv_groups(w_out1[512:], 0)], axis=0)
    return _odd_out(hf, hr, p1, attn1, h1, mod[1], w_out1, post_g[1])
```

```python
import functools
import math

import numpy as np
import jax
import jax.numpy as jnp
from jax import lax
from jax.experimental import pallas as pl
from jax.experimental.pallas import tpu as pltpu

F32 = jnp.float32
BF16 = jnp.bfloat16
HIGHEST = lax.Precision.HIGHEST

D_MODEL = 1024
BATCH = 4
SEQ = 4096
GRID_W = 64
CTX_LEN = 256
LT = CTX_LEN + SEQ
HEAD_DIM = 64
EPS = 1e-6
S5_WIDTH = 512
S5_GROUP = 16
S5_GROUPS = 32
S5_STATE = 64
NA_HEADS = 8
NA_ROWS = 8
NA_COLS = 16
LRU_WIDTH = 512
LRU_BLOCKS = 8
LRU_BLOCK = 64
LRU_C = 8.0
GQA_HEADS = 8
ROPE_THETA = 10000.0
EVEN_IN = 3072
ODD_IN = 2944

TOK_BLK = 256
SCAN_BLK = 64
N_SCAN = LT // SCAN_BLK
CTX_SCAN = CTX_LEN // SCAN_BLK
Q_BLK = 256
NEG = -1e30
LOG2E = math.log2(math.e)
VMEM_LIMIT = 56 * 1024 * 1024


def _cparams(sem):
    return pltpu.CompilerParams(dimension_semantics=sem, vmem_limit_bytes=VMEM_LIMIT)


def _dot(a, b):
    return jnp.dot(a, b, preferred_element_type=F32)


def _dot_nt(a, b):
    return lax.dot_general(a, b, (((1,), (1,)), ((), ())), preferred_element_type=F32)


def _adaln_kernel(c_ref, w_ref, b_ref, o_ref):
    c = c_ref[...]
    s = c * jax.nn.sigmoid(c)
    o_ref[0] = jnp.dot(s, w_ref[0], preferred_element_type=F32, precision=HIGHEST) + b_ref[0]


def _adaln(c8, ada_w, ada_b):
    depth = ada_w.shape[0]
    nb = 3 * D_MODEL // 1024
    return pl.pallas_call(
        _adaln_kernel,
        grid=(depth, nb),
        in_specs=[
            pl.BlockSpec((8, D_MODEL), lambda i, n: (0, 0)),
            pl.BlockSpec((1, D_MODEL, 1024), lambda i, n: (i, 0, n)),
            pl.BlockSpec((1, 1, 1024), lambda i, n: (i, 0, n)),
        ],
        out_specs=pl.BlockSpec((1, 8, 1024), lambda i, n: (i, 0, n)),
        out_shape=jax.ShapeDtypeStruct((depth, 8, 3 * D_MODEL), F32),
        compiler_params=_cparams(("arbitrary", "arbitrary")),
        name="adaln",
    )(c8, ada_w, ada_b.reshape(depth, 1, 3 * D_MODEL))


PAIR = 2


def _cat_specs(width):
    return [pl.BlockSpec((PAIR, TOK_BLK, width), lambda b, t: (b, 0, 0)),
            pl.BlockSpec((PAIR, TOK_BLK, width), lambda b, t: (b, jnp.maximum(t - 1, 0), 0))]


def _mod_row(mod_ref, h, tb):
    row = jnp.where(tb == 0, BATCH, pl.program_id(0) * PAIR + h)
    return mod_ref[pl.ds(row, 1), :]


def _inproj_kernel(*refs, two_src):
    tb = pl.program_id(1)
    if two_src:
        c_ref, x_ref, mod_ref, g_ref, w_ref, o_ref = refs
    else:
        x_ref, mod_ref, g_ref, w_ref, o_ref = refs
    ys = []
    for h in range(PAIR):
        x = jnp.where(tb == 0, c_ref[h], x_ref[h]) if two_src else x_ref[h]
        r = lax.rsqrt(jnp.mean(x * x, axis=-1, keepdims=True) + EPS)
        m = _mod_row(mod_ref, h, tb)
        y = (x * r) * g_ref[...]
        ys.append((y * (1.0 + m[:, D_MODEL:2 * D_MODEL]) + m[:, :D_MODEL]).astype(BF16))
    for h in range(PAIR):
        o_ref[h] = _dot(ys[h], w_ref[...]).astype(BF16)


def _inproj(src, mod, g, w_bf):
    n = w_bf.shape[1]
    two_src = isinstance(src, tuple)
    if two_src:
        src_specs = _cat_specs(D_MODEL)
    else:
        src_specs = [pl.BlockSpec((PAIR, TOK_BLK, D_MODEL), lambda b, t: (b, t, 0))]
        src = (src,)
    return pl.pallas_call(
        functools.partial(_inproj_kernel, two_src=two_src),
        grid=(BATCH // PAIR, LT // TOK_BLK),
        in_specs=src_specs + [
            pl.BlockSpec((8, 3 * D_MODEL), lambda b, t: (0, 0)),
            pl.BlockSpec((1, D_MODEL), lambda b, t: (0, 0)),
            pl.BlockSpec((D_MODEL, n), lambda b, t: (0, 0)),
        ],
        out_specs=pl.BlockSpec((PAIR, TOK_BLK, n), lambda b, t: (b, t, 0)),
        out_shape=jax.ShapeDtypeStruct((BATCH, LT, n), BF16),
        compiler_params=_cparams(("arbitrary", "arbitrary")),
        name="inproj",
    )(*src, mod, g.reshape(1, D_MODEL), w_bf)


def _fwd_blk(i):
    return i


def _rev_blk(i):
    return jnp.where(i < CTX_SCAN, CTX_SCAN - 1 - i, N_SCAN + CTX_SCAN - 1 - i)


def _scan_perms():
    t = SCAN_BLK
    pf = np.zeros((8 * t, BATCH * t), np.float32)
    pr = np.zeros((8 * t, BATCH * t), np.float32)
    for tt in range(t):
        for b in range(BATCH):
            pf[tt * 8 + b, b * t + tt] = 1.0
            pr[tt * 8 + 4 + b, b * t + (t - 1 - tt)] = 1.0
    return pf, pr


def _halo_perm():
    ph = np.zeros((32, 16 * 4 * BATCH), np.float32)
    for b in range(BATCH):
        ph[1 * 8 + b, 0 * 64 + b * 16 + 15] = 1.0
        ph[2 * 8 + b, 1 * 64 + b * 16 + 0] = 1.0
        ph[3 * 8 + b, 1 * 64 + b * 16 + 1] = 1.0
        ph[0 * 8 + 4 + b, 3 * 64 + b * 16 + 1] = 1.0
        ph[1 * 8 + 4 + b, 3 * 64 + b * 16 + 0] = 1.0
        ph[2 * 8 + 4 + b, 2 * 64 + b * 16 + 15] = 1.0
    return ph


def _conv_coef(conv_w):
    zero = jnp.zeros((1, LRU_WIDTH), conv_w.dtype)
    fwd = jnp.concatenate([zero, conv_w], axis=0)
    rev = jnp.concatenate([conv_w[::-1], zero], axis=0)
    return jnp.concatenate([jnp.broadcast_to(fwd[:, None], (5, 4, LRU_WIDTH)),
                            jnp.broadcast_to(rev[:, None], (5, 4, LRU_WIDTH))], axis=1)


def _s5_kernel(uf_ref, ur_ref, pf_ref, pr_ref, pft_ref, prt_ref, bcat_ref, ccat_ref, lre_ref, lim_ref,
               yf_ref, yr_ref, buf0_ref, buf1_ref, st_ref):
    i = pl.program_id(0)

    @pl.when(i == 0)
    def _():
        st_ref[...] = jnp.zeros_like(st_ref)
        buf0_ref[...] = jnp.zeros_like(buf0_ref)
        buf1_ref[...] = jnp.zeros_like(buf1_ref)

    nrow = 8 * SCAN_BLK

    def step(buf_a, buf_b):
        vals = {}

        def perm():
            uf = uf_ref[...].reshape(BATCH * SCAN_BLK, S5_WIDTH)
            ur = ur_ref[...].reshape(BATCH * SCAN_BLK, S5_WIDTH)
            vals['u_f'] = _dot(pf_ref[...], uf).astype(BF16)
            vals['u_r'] = _dot(pr_ref[...], ur).astype(BF16)

        def readout(j):
            is_f = (lax.broadcasted_iota(jnp.int32, (nrow, 128), 0) & 7) < 4
            yj = _dot(buf_a[:, j * 1024:(j + 1) * 1024].astype(BF16), ccat_ref[j])
            vals['y%d' % j] = jnp.where(is_f, yj[:, :128], yj[:, 128:]).astype(BF16)

        def project(j):
            lhs = jnp.concatenate([vals['u_f'][:, j * 128:(j + 1) * 128], vals['u_r'][:, j * 128:(j + 1) * 128]],
                                  axis=1)
            buf_a[:, j * 1024:(j + 1) * 1024] = _dot(lhs, bcat_ref[j])

        def unperm(p_ref, o_ref):
            yp = jnp.concatenate([vals['y%d' % j] for j in range(4)], axis=1)
            o_ref[...] = _dot(p_ref[...], yp).astype(BF16).reshape(BATCH, SCAN_BLK, S5_WIDTH)

        mxu = [perm]
        for j in range(4):
            mxu += [functools.partial(readout, j), functools.partial(project, j)]
        mxu += [functools.partial(unperm, pft_ref, yf_ref), functools.partial(unperm, prt_ref, yr_ref)]

        def scan(j, t0):
            c_re = slice(j * 1024, j * 1024 + 512)
            c_im = slice(j * 1024 + 512, (j + 1) * 1024)
            lre = lre_ref[j]
            lim = lim_ref[j]
            hre = st_ref[:, c_re]
            him = st_ref[:, c_im]
            for t in range(t0, t0 + 16):
                rows = slice(t * 8, (t + 1) * 8)
                nre = lre * hre - lim * him + buf_b[rows, c_re]
                nim = lre * him + lim * hre + buf_b[rows, c_im]
                buf_b[rows, c_re] = nre
                buf_b[rows, c_im] = nim
                hre, him = nre, nim
            st_ref[:, c_re] = hre
            st_ref[:, c_im] = him

        vpu = [functools.partial(scan, j, t0) for j in range(4) for t0 in range(0, SCAN_BLK, 16)]

        per_piece = [1, 2, 1, 2, 1, 2, 1, 2, 1, 2, 1]
        for piece, n_scan in zip(mxu, per_piece):
            piece()
            for _ in range(n_scan):
                vpu.pop(0)()

    @pl.when(i % 2 == 0)
    def _():
        step(buf0_ref, buf1_ref)

    @pl.when(i % 2 == 1)
    def _():
        step(buf1_ref, buf0_ref)


def _s5_weights(lam_re, lam_im, log_dt, b_re, b_im, c_re, c_im):
    a = lam_re.astype(F32) * jnp.exp(log_dt.astype(F32))[..., None]
    b = lam_im.astype(F32) * jnp.exp(log_dt.astype(F32))[..., None]
    lbr = jnp.exp(a) * jnp.cos(b)
    lbi = jnp.exp(a) * jnp.sin(b)
    nr = jnp.expm1(a) * jnp.cos(b) - 2.0 * jnp.sin(0.5 * b) ** 2
    d2 = lam_re * lam_re + lam_im * lam_im
    qr = (nr * lam_re + lbi * lam_im) / d2
    qi = (lbi * lam_re - nr * lam_im) / d2
    bbr = qr[..., None] * b_re - qi[..., None] * b_im
    bbi = qr[..., None] * b_im + qi[..., None] * b_re
    same = jnp.asarray(np.eye(8, dtype=np.float32))
    bb = jnp.stack([bbr, bbi], axis=1).reshape(2, 2, 4, 8, S5_STATE, S5_GROUP)
    bb = bb.transpose(2, 0, 5, 1, 3, 4)[:, :, None] * same[:, None, None, :, None]
    bcat = bb.reshape(4, 2 * 128, 2 * 512).astype(BF16)
    cc = jnp.stack([c_re.astype(F32), -c_im.astype(F32)], axis=1).reshape(2, 2, 4, 8, S5_GROUP, S5_STATE)
    cc = cc.transpose(2, 1, 3, 5, 0, 4)[:, :, :, :, :, None] * same[:, None, None, :, None]
    ccat = cc.reshape(4, 2 * 512, 2 * 128).astype(BF16)
    lre8 = jnp.repeat(lbr.reshape(2, 4, 512).transpose(1, 0, 2), 4, axis=1)
    lim8 = jnp.repeat(lbi.reshape(2, 4, 512).transpose(1, 0, 2), 4, axis=1)
    return bcat, ccat, lre8, lim8


def _s5(p, pf, pr, pft, prt, bcat, ccat, lre8, lim8):
    nrow = 8 * SCAN_BLK
    const2 = lambda i: (0, 0)
    const3 = lambda i: (0, 0, 0)
    proj = lambda i: jnp.minimum(i, N_SCAN - 1)
    read = lambda i: jnp.clip(i - 2, 0, N_SCAN - 1)
    return pl.pallas_call(
        _s5_kernel,
        grid=(N_SCAN + 2,),
        in_specs=[
            pl.BlockSpec((BATCH, SCAN_BLK, S5_WIDTH), lambda i: (0, _fwd_blk(proj(i)), 0)),
            pl.BlockSpec((BATCH, SCAN_BLK, S5_WIDTH), lambda i: (0, _rev_blk(proj(i)), 0)),
            pl.BlockSpec(pf.shape, const2), pl.BlockSpec(pr.shape, const2),
            pl.BlockSpec(pft.shape, const2), pl.BlockSpec(prt.shape, const2),
            pl.BlockSpec(bcat.shape, const3), pl.BlockSpec(ccat.shape, const3),
            pl.BlockSpec(lre8.shape, const3), pl.BlockSpec(lim8.shape, const3),
        ],
        out_specs=[
            pl.BlockSpec((BATCH, SCAN_BLK, S5_WIDTH), lambda i: (0, _fwd_blk(read(i)), 0)),
            pl.BlockSpec((BATCH, SCAN_BLK, S5_WIDTH), lambda i: (0, _rev_blk(read(i)), 0)),
        ],
        out_shape=[jax.ShapeDtypeStruct((BATCH, LT, S5_WIDTH), BF16)] * 2,
        scratch_shapes=[pltpu.VMEM((nrow, 4096), F32), pltpu.VMEM((nrow, 4096), F32),
                        pltpu.VMEM((8, 4096), F32)],
        compiler_params=_cparams(("arbitrary",)),
        name="s5_scan",
    )(p, p, pf, pr, pft, prt, bcat, ccat, lre8, lim8)


def _na_bias_table(rel_bias):
    w = np.arange(GRID_W)
    cs = np.clip(w - NA_COLS // 2, 0, GRID_W - NA_COLS)
    cp = np.arange(GRID_W)
    valid = (cp[None, :] >= cs[:, None]) & (cp[None, :] < cs[:, None] + NA_COLS)
    dc = cp[None, :] - w[:, None] + (NA_COLS - 1)
    n_dc = 2 * NA_COLS - 1
    onehot = ((dc[None] == np.arange(n_dc)[:, None, None]) & valid[None]).astype(np.float32)
    n_dr = 2 * NA_ROWS - 1
    oh2 = np.zeros((2, n_dc, GRID_W, 2, GRID_W), np.float32)
    oh2[0, :, :, 0, :] = onehot
    oh2[1, :, :, 1, :] = onehot
    rb = rel_bias.astype(F32)
    rb2 = jnp.concatenate([rb[:, :n_dr - 1], rb[:, 1:]], axis=2).reshape(NA_HEADS * (n_dr - 1), 2 * n_dc)
    band2 = jnp.dot(rb2, jnp.asarray(oh2.reshape(2 * n_dc, 2 * GRID_W * GRID_W)), precision=HIGHEST)
    neg2 = np.tile(np.where(valid, 0.0, NEG).astype(np.float32), (1, 2))
    band2 = band2.reshape(NA_HEADS, n_dr - 1, GRID_W, 2 * GRID_W) * LOG2E + jnp.asarray(neg2)
    tabs = [jnp.concatenate([band2[:, off + 2 * i2] for i2 in range(NA_ROWS // 2)], axis=-1)
            for off in range(8)]
    tabs.append(jnp.full_like(tabs[0], NEG))
    return jnp.stack(tabs).reshape(9, NA_HEADS // 2, 128, NA_ROWS * GRID_W)


NA_STEP_ROWS = TOK_BLK // GRID_W


def _na_kernel(q_ref, k_ref, v_ref, bias_ref, o_ref):
    s_idx = pl.program_id(1)
    is_ctx = s_idx == 0
    lo = lax.broadcasted_iota(jnp.int32, (GRID_W, 128), 1) < HEAD_DIM
    nwin = NA_ROWS * GRID_W
    for rr in range(NA_STEP_ROWS):
        r = jnp.maximum(s_idx - 1, 0) * NA_STEP_ROWS + rr
        start = jnp.clip(r - NA_ROWS // 2, 0, SEQ // GRID_W - NA_ROWS)
        koff = pl.multiple_of(CTX_LEN + start * GRID_W, GRID_W)
        off = jnp.where(is_ctx, 8, start - r + NA_ROWS - 1)
        rows = slice(rr * GRID_W, (rr + 1) * GRID_W)
        pairs = range(NA_HEADS // 2)
        cols = [slice(hp * 128, (hp + 1) * 128) for hp in pairs]
        s_loc, s_ctx, m, p_loc, p_ctx, l, o = [], [], [], [], [], [], []
        for hp in pairs:
            qp = q_ref[0, rows, cols[hp]]
            zero = jnp.zeros_like(qp)
            q2 = jnp.concatenate([jnp.where(lo, qp, zero), jnp.where(lo, zero, qp)], axis=0)
            s_loc.append(_dot_nt(q2, k_ref[0, pl.ds(koff, nwin), cols[hp]]) + bias_ref[off, hp])
            s_ctx.append(_dot_nt(q2, k_ref[0, 0:CTX_LEN, cols[hp]]))
        for hp in pairs:
            m.append(jnp.maximum(jnp.max(s_loc[hp], axis=-1, keepdims=True),
                                 jnp.max(s_ctx[hp], axis=-1, keepdims=True)))
        for hp in pairs:
            pl_, pc_ = jnp.exp2(s_loc[hp] - m[hp]), jnp.exp2(s_ctx[hp] - m[hp])
            l.append(jnp.sum(pl_, axis=-1, keepdims=True) + jnp.sum(pc_, axis=-1, keepdims=True))
            p_loc.append(pl_.astype(BF16))
            p_ctx.append(pc_.astype(BF16))
        for hp in pairs:
            o.append(_dot(p_loc[hp], v_ref[0, pl.ds(koff, nwin), cols[hp]]) +
                     _dot(p_ctx[hp], v_ref[0, 0:CTX_LEN, cols[hp]]))
        for hp in pairs:
            oh = o[hp] / l[hp]
            o_ref[0, rows, cols[hp]] = jnp.where(lo, oh[:GRID_W], oh[GRID_W:]).astype(BF16)


def _na(p, bias_tb):
    return pl.pallas_call(
        _na_kernel,
        grid=(BATCH, LT // TOK_BLK),
        in_specs=[
            pl.BlockSpec((1, TOK_BLK, 512), lambda b, s: (b, s, 2)),
            pl.BlockSpec((1, LT, 512), lambda b, s: (b, 0, 3)),
            pl.BlockSpec((1, LT, 512), lambda b, s: (b, 0, 4)),
            pl.BlockSpec(bias_tb.shape, lambda b, s: (0, 0, 0, 0)),
        ],
        out_specs=pl.BlockSpec((1, TOK_BLK, 512), lambda b, s: (b, s, 0)),
        out_shape=jax.ShapeDtypeStruct((BATCH, LT, 512), BF16),
        compiler_params=_cparams(("arbitrary", "arbitrary")),
        name="na_attn",
    )(p, p, p, bias_tb)


def _gelu_tanh(x):
    return 0.5 * x * (1.0 + jnp.tanh(math.sqrt(2.0 / math.pi) * (x + 0.044715 * (x * x * x))))


def _sigmoid(x):
    return 0.5 + 0.5 * jnp.tanh(0.5 * x)


def _silu(x):
    return x * _sigmoid(x)


def _finish(h, out, gt, pg):
    r = lax.rsqrt(jnp.mean(out * out, axis=-1, keepdims=True) + EPS)
    return h + gt * ((out * r) * pg)


def _even_out_kernel(yf_ref, yr_ref, u_ref, ga_ref, gb_ref, at_ref, hc_ref, hl_ref, mod_ref, d_ref, wg_ref,
                     bg_ref, wo_ref, pg_ref, o_ref):
    tb = pl.program_id(1)
    chains = range(PAIR)
    y = [_gelu_tanh(d_ref[...] * u_ref[h].astype(F32) + yf_ref[h].astype(F32) + yr_ref[h].astype(F32))
         for h in chains]
    glu = [_dot(y[h].astype(BF16), wg_ref[...]) for h in chains]
    y_a = [(y[h] * _sigmoid(glu[h] + bg_ref[...]) * _silu(ga_ref[h].astype(F32))).astype(BF16) for h in chains]
    y_b = [(at_ref[h].astype(F32) * _silu(gb_ref[h].astype(F32))).astype(BF16) for h in chains]
    out = [_dot(y_a[h], wo_ref[0:512, :]) + _dot(y_b[h], wo_ref[512:1024, :]) for h in chains]
    for h in chains:
        res = jnp.where(tb == 0, hc_ref[h], hl_ref[h])
        o_ref[h] = _finish(res, out[h], _mod_row(mod_ref, h, tb)[:, 2 * D_MODEL:], pg_ref[...])


def _even_out(yf, yr, p, attn, ctx, x, mod, d_skip, w_glu, b_glu, w_out, post_g):
    tok = lambda c: pl.BlockSpec((PAIR, TOK_BLK, 512), lambda b, t: (b, t, c))
    full = lambda a: pl.BlockSpec(a.shape, lambda b, t: (0,) * a.ndim)
    d2, bg2, pg2 = d_skip.reshape(1, 512), b_glu.reshape(1, 512), post_g.reshape(1, D_MODEL)
    return pl.pallas_call(
        _even_out_kernel,
        grid=(BATCH // PAIR, LT // TOK_BLK),
        in_specs=[tok(0), tok(0), tok(0), tok(1), tok(5), tok(0)] + _cat_specs(D_MODEL) +
                 [full(mod), full(d2), full(w_glu), full(bg2), full(w_out), full(pg2)],
        out_specs=pl.BlockSpec((PAIR, TOK_BLK, D_MODEL), lambda b, t: (b, t, 0)),
        out_shape=jax.ShapeDtypeStruct((BATCH, LT, D_MODEL), F32),
        compiler_params=_cparams(("arbitrary", "arbitrary")),
        name="even_out",
    )(yf, yr, p, p, p, attn, ctx, x, mod, d2, w_glu, bg2, w_out, pg2)


def _softplus(z):
    return jnp.maximum(z, 0.0) + jnp.log1p(jnp.exp(-jnp.abs(z)))


def _lru_kernel(xf_ref, xfp_ref, xfn_ref, xr_ref, xrp_ref, xrn_ref, pf_ref, pr_ref, ph_ref, pft_ref, prt_ref,
                coef_ref, cb_ref, wg_ref, bg_ref, lam_ref, hf_ref, hr_ref, a_ref, b_ref, st_ref):
    i = pl.program_id(0)

    @pl.when(i == 0)
    def _():
        st_ref[...] = jnp.zeros_like(st_ref)

    nrow = 8 * SCAN_BLK

    def halo(prev_ref, next_ref, blk):
        prev_ok = jnp.logical_and(blk != 0, blk != CTX_SCAN)
        next_ok = jnp.logical_and(blk != CTX_SCAN - 1, blk != N_SCAN - 1)
        pv = prev_ref[...].reshape(BATCH * 16, LRU_WIDTH)
        nx = next_ref[...].reshape(BATCH * 16, LRU_WIDTH)
        return [jnp.where(prev_ok, pv, jnp.zeros_like(pv)), jnp.where(next_ok, nx, jnp.zeros_like(nx))]

    hal = jnp.concatenate(halo(xfp_ref, xfn_ref, _fwd_blk(i)) + halo(xrp_ref, xrn_ref, _rev_blk(i)), axis=0)
    xh = _dot(ph_ref[...], hal)
    xp = (_dot(pf_ref[...], xf_ref[...].reshape(BATCH * SCAN_BLK, LRU_WIDTH)) +
          _dot(pr_ref[...], xr_ref[...].reshape(BATCH * SCAN_BLK, LRU_WIDTH)))
    x_ext = jnp.concatenate([xh[:16], xp, xh[16:]], axis=0)
    xc = jnp.zeros((SCAN_BLK, 8, LRU_WIDTH), F32) + cb_ref[...]
    for s in range(5):
        xc = xc + x_ext[s * 8:s * 8 + nrow].reshape(SCAN_BLK, 8, LRU_WIDTH) * coef_ref[s]
    xc = xc.reshape(nrow, LRU_WIDTH)

    is_f = (lax.broadcasted_iota(jnp.int32, (nrow, 1), 0) & 7) < 4
    xcb = xc.astype(BF16)
    zero = jnp.zeros_like(xcb[:, :256])
    pre_r, pre_i = [], []
    for c in range(2):
        xt = xcb[:, c * 256:(c + 1) * 256]
        lhs = jnp.concatenate([jnp.where(is_f, xt, zero), jnp.where(is_f, zero, xt)], axis=1)
        pre = _dot(lhs, wg_ref[c])
        pre_r.append(pre[:, :256])
        pre_i.append(pre[:, 256:])
    bias = jnp.where(is_f, bg_ref[0:1, :], bg_ref[1:2, :])
    gate_r = _sigmoid(jnp.concatenate(pre_r, axis=1) + bias[:, :LRU_WIDTH])
    gate_i = _sigmoid(jnp.concatenate(pre_i, axis=1) + bias[:, LRU_WIDTH:])
    sp = _softplus(-lam_ref[...])
    log_a = (-LRU_C) * gate_r * jnp.where(is_f, sp[0:1, :], sp[1:2, :])
    a = jnp.exp(log_a)
    a_ref[...] = a
    one_m_a2 = -jnp.tanh(log_a) * (a * a + 1.0)
    b_ref[...] = jnp.sqrt(jnp.maximum(one_m_a2, 0.0)) * gate_i * xc

    def body(t, h):
        r0 = pl.multiple_of(t * 8, 8)
        h = a_ref[pl.ds(r0, 8), :] * h + b_ref[pl.ds(r0, 8), :]
        b_ref[pl.ds(r0, 8), :] = h
        return h

    st_ref[...] = lax.fori_loop(0, SCAN_BLK, body, st_ref[...], unroll=4)
    hs = b_ref[...].astype(BF16)
    hf_ref[...] = _dot(pft_ref[...], hs).astype(BF16).reshape(BATCH, SCAN_BLK, LRU_WIDTH)
    hr_ref[...] = _dot(prt_ref[...], hs).astype(BF16).reshape(BATCH, SCAN_BLK, LRU_WIDTH)


def _lru_gate_weights(w_a, w_x):
    w = jnp.stack([w_a, w_x], axis=1).reshape(2, 2, 2, 4, LRU_BLOCK, LRU_BLOCK)
    same = jnp.asarray(np.eye(4, dtype=np.float32))
    w = w.transpose(2, 0, 3, 4, 1, 5)[:, :, :, :, :, None] * same[:, None, None, :, None]
    return w.reshape(2, 512, 512).astype(BF16)


def _lru(p, pf, pr, ph, pft, prt, coef, conv_b, wg, bg, lam):
    nrow = 8 * SCAN_BLK
    full = lambda a: pl.BlockSpec(a.shape, lambda i: (0,) * a.ndim)
    cur = lambda f: pl.BlockSpec((BATCH, SCAN_BLK, LRU_WIDTH), lambda i: (0, f(i), 0))
    prev = lambda f: pl.BlockSpec((BATCH, 16, LRU_WIDTH), lambda i: (0, jnp.maximum(f(i) * 4 - 1, 0), 0))
    nxt = lambda f: pl.BlockSpec((BATCH, 16, LRU_WIDTH),
                                 lambda i: (0, jnp.minimum(f(i) * 4 + 4, LT // 16 - 1), 0))
    cb2 = conv_b.reshape(1, LRU_WIDTH)
    return pl.pallas_call(
        _lru_kernel,
        grid=(N_SCAN,),
        in_specs=[cur(_fwd_blk), prev(_fwd_blk), nxt(_fwd_blk), cur(_rev_blk), prev(_rev_blk), nxt(_rev_blk),
                  full(pf), full(pr), full(ph), full(pft), full(prt), full(coef), full(cb2), full(wg), full(bg),
                  full(lam)],
        out_specs=[
            pl.BlockSpec((BATCH, SCAN_BLK, LRU_WIDTH), lambda i: (0, _fwd_blk(i), 0)),
            pl.BlockSpec((BATCH, SCAN_BLK, LRU_WIDTH), lambda i: (0, _rev_blk(i), 0)),
        ],
        out_shape=[jax.ShapeDtypeStruct((BATCH, LT, LRU_WIDTH), BF16)] * 2,
        scratch_shapes=[pltpu.VMEM((nrow, LRU_WIDTH), F32), pltpu.VMEM((nrow, LRU_WIDTH), F32),
                        pltpu.VMEM((8, LRU_WIDTH), F32)],
        compiler_params=_cparams(("arbitrary",)),
        name="lru_scan",
    )(p, p, p, p, p, p, pf, pr, ph, pft, prt, coef, cb2, wg, bg, lam)


def _norm_rope(x, xrot, g, grot, cos, sin, ones_blk):
    x = x.astype(F32)
    xrot = xrot.astype(F32)
    ms = _dot((x * x).astype(BF16), ones_blk) * (1.0 / HEAD_DIM)
    rs = lax.rsqrt(ms + EPS)
    return rs * ((x * g) * cos + (xrot * grot) * sin)


def _gqa_kernel(q_ref, qr_ref, k_ref, kr_ref, v_ref, cq_ref, sq_ref, ck_ref, sk_ref, g_ref, ones_ref,
                o_ref, kn_ref, vlo_ref, vhi_ref):
    qb = pl.program_id(1)
    ones_blk = ones_ref[...]

    @pl.when(qb == 0)
    def _():
        kn = _norm_rope(k_ref[0], kr_ref[0], g_ref[2:3, :], g_ref[3:4, :], ck_ref[...], sk_ref[...], ones_blk)
        kn_ref[...] = kn.astype(BF16)
        v = v_ref[0]
        lo_k = lax.broadcasted_iota(jnp.int32, (LT, 128), 1) < HEAD_DIM
        one = jnp.ones_like(v)
        vlo_ref[...] = jnp.where(lo_k, v, one)
        vhi_ref[...] = jnp.where(lo_k, one, v)

    lo = lax.broadcasted_iota(jnp.int32, (Q_BLK, 128), 1) < HEAD_DIM
    kn = kn_ref[...]
    cos = cq_ref[...]
    sin = sq_ref[...]
    n_pb = GQA_HEADS // 2

    def scores(pb):
        cols = slice(pb * 128, (pb + 1) * 128)
        qn = _norm_rope(q_ref[0, :, cols], qr_ref[0, :, cols], g_ref[0:1, :], g_ref[1:2, :], cos, sin, ones_blk)
        qn = (qn * (HEAD_DIM ** -0.5 * LOG2E)).astype(BF16)
        zero = jnp.zeros_like(qn)
        qs = jnp.concatenate([jnp.where(lo, qn, zero), jnp.where(lo, zero, qn)], axis=0)
        return _dot_nt(qs, kn)

    s_next = scores(0)
    for pb in range(n_pb):
        cols = slice(pb * 128, (pb + 1) * 128)
        s = s_next
        if pb + 1 < n_pb:
            s_next = scores(pb + 1)
        m = jnp.max(s, axis=-1, keepdims=True)
        p = jnp.exp2(s - m).astype(BF16)
        o_lo = _dot(p[:Q_BLK], vlo_ref[...])
        o_hi = _dot(p[Q_BLK:], vhi_ref[...])
        num = jnp.where(lo, o_lo, o_hi)
        den = pltpu.roll(jnp.where(lo, o_hi, o_lo), HEAD_DIM, axis=1)
        o_ref[0, :, cols] = (num / den).astype(BF16)


def _gqa(p, cos128, sin128, gvec, ones_blk):
    nqb = SEQ // Q_BLK
    cb = CTX_LEN // Q_BLK
    return pl.pallas_call(
        _gqa_kernel,
        grid=(BATCH, nqb),
        in_specs=[
            pl.BlockSpec((1, Q_BLK, 512), lambda b, q: (b, q + cb, 2)),
            pl.BlockSpec((1, Q_BLK, 512), lambda b, q: (b, q + cb, 3)),
            pl.BlockSpec((1, LT, 128), lambda b, q: (b, 0, 20)),
            pl.BlockSpec((1, LT, 128), lambda b, q: (b, 0, 21)),
            pl.BlockSpec((1, LT, 128), lambda b, q: (b, 0, 22)),
            pl.BlockSpec((Q_BLK, 128), lambda b, q: (q + cb, 0)),
            pl.BlockSpec((Q_BLK, 128), lambda b, q: (q + cb, 0)),
            pl.BlockSpec((LT, 128), lambda b, q: (0, 0)),
            pl.BlockSpec((LT, 128), lambda b, q: (0, 0)),
            pl.BlockSpec((8, 128), lambda b, q: (0, 0)),
            pl.BlockSpec((128, 128), lambda b, q: (0, 0)),
        ],
        out_specs=pl.BlockSpec((1, Q_BLK, 512), lambda b, q: (b, q, 0)),
        out_shape=jax.ShapeDtypeStruct((BATCH, SEQ, 512), BF16),
        scratch_shapes=[pltpu.VMEM((LT, 128), BF16)] * 3,
        compiler_params=_cparams(("arbitrary", "arbitrary")),
        name="gqa_attn",
    )(p, p, p, p, p, cos128, sin128, cos128, sin128, gvec, ones_blk)


def _odd_out_kernel(hf_ref, hr_ref, gc_ref, gd_ref, at_ref, h_ref, mod_ref, wo_ref, pg_ref, o_ref):
    chains = range(PAIR)
    y_c = [((hf_ref[h].astype(F32) + hr_ref[h].astype(F32)) * _silu(gc_ref[h].astype(F32))).astype(BF16)
           for h in chains]
    y_d = [(at_ref[h].astype(F32) * _silu(gd_ref[h].astype(F32))).astype(BF16) for h in chains]
    out = [_dot(y_c[h], wo_ref[0:512, :]) + _dot(y_d[h], wo_ref[512:1024, :]) for h in chains]
    for h in chains:
        gt = mod_ref[pl.ds(pl.program_id(0) * PAIR + h, 1), 2 * D_MODEL:]
        o_ref[h] = _finish(h_ref[h], out[h], gt, pg_ref[...])


def _odd_out(hf, hr, p, attn, h_cat, mod, w_out, post_g):
    cb = CTX_LEN // TOK_BLK
    cat = lambda c: pl.BlockSpec((PAIR, TOK_BLK, 512), lambda b, t: (b, t + cb, c))
    full = lambda a: pl.BlockSpec(a.shape, lambda b, t: (0,) * a.ndim)
    pg2 = post_g.reshape(1, D_MODEL)
    return pl.pallas_call(
        _odd_out_kernel,
        grid=(BATCH // PAIR, SEQ // TOK_BLK),
        in_specs=[cat(0), cat(0), cat(1), cat(4),
                  pl.BlockSpec((PAIR, TOK_BLK, 512), lambda b, t: (b, t, 0)),
                  pl.BlockSpec((PAIR, TOK_BLK, D_MODEL), lambda b, t: (b, t + cb, 0)),
                  full(mod), full(w_out), full(pg2)],
        out_specs=pl.BlockSpec((PAIR, TOK_BLK, D_MODEL), lambda b, t: (b, t, 0)),
        out_shape=jax.ShapeDtypeStruct((BATCH, SEQ, D_MODEL), F32),
        compiler_params=_cparams(("arbitrary", "arbitrary")),
        name="odd_out",
    )(hf, hr, p, p, attn, h_cat, mod, w_out, pg2)


def _pair_swap(w):
    n = w.shape[1]
    r = np.zeros((n, n), np.float32)
    r[np.arange(1, n, 2), np.arange(0, n, 2)] = -1.0
    r[np.arange(0, n, 2), np.arange(1, n, 2)] = 1.0
    return jnp.dot(w, jnp.asarray(r, BF16), preferred_element_type=F32).astype(BF16)


def _interleave_kv_groups(w, axis):
    if axis == 0:
        return w.reshape(2, 4, HEAD_DIM, w.shape[1]).transpose(1, 0, 2, 3).reshape(w.shape)
    return w.reshape(w.shape[0], 2, 4, HEAD_DIM).transpose(0, 2, 1, 3).reshape(w.shape)


def _odd_w_in(w):
    w = w.astype(BF16)
    x, gc = w[:, 0:512], w[:, 512:1024]
    q = _interleave_kv_groups(w[:, 1024:1536], 1)
    k, v = w[:, 1536:1664], w[:, 1664:1792]
    gd = _interleave_kv_groups(w[:, 1792:2304], 1)
    return jnp.concatenate([x, gc, q, _pair_swap(q), gd, k, _pair_swap(k), v], axis=1)


def _rope_tables():
    t = np.arange(SEQ)
    row = (t // GRID_W).astype(np.float32)
    col = (t % GRID_W).astype(np.float32)
    half = HEAD_DIM // 2
    inv = (ROPE_THETA ** (-np.arange(0, half, 2, dtype=np.float32) / half)).astype(np.float32)
    ang = np.concatenate([row[:, None] * inv, col[:, None] * inv], axis=-1)
    cos = np.repeat(np.cos(ang), 2, axis=-1)
    sin = np.repeat(np.sin(ang), 2, axis=-1)
    cos = np.concatenate([np.ones((CTX_LEN, HEAD_DIM), np.float32), cos], axis=0)
    sin = np.concatenate([np.zeros((CTX_LEN, HEAD_DIM), np.float32), sin], axis=0)
    return np.tile(cos, (1, 2)).astype(np.float32), np.tile(sin, (1, 2)).astype(np.float32)


def _swap_pairs_vec(g):
    g2 = g.reshape(-1, 2)
    return jnp.stack([g2[:, 1], g2[:, 0]], axis=-1).reshape(g.shape)


def _block_diag(w):
    n, a, b = w.shape
    return jnp.einsum('mn,nij->minj', jnp.eye(n, dtype=w.dtype), w).reshape(n * a, n * b)


def kernel(x, c, ctx, c_ctx, ada_w, ada_b, pre_g, post_g, ev_w_in, ev_w_out, s5_lam_re, s5_lam_im, s5_log_dt,
           s5_b_re, s5_b_im, s5_c_re, s5_c_im, s5_d, s5_w_glu, s5_b_glu, na_rel_bias, od_w_in, od_w_out,
           lru_conv_w, lru_conv_b, lru_lam, lru_w_a, lru_b_a, lru_w_x, lru_b_x, gqa_q_norm, gqa_k_norm):
    pf_np, pr_np = _scan_perms()
    pf, pr = jnp.asarray(pf_np, BF16), jnp.asarray(pr_np, BF16)
    pft, prt = jnp.asarray(pf_np.T, BF16), jnp.asarray(pr_np.T, BF16)
    ph = jnp.asarray(_halo_perm(), BF16)

    c8 = jnp.concatenate([c, c_ctx[None], jnp.zeros((3, D_MODEL), F32)], axis=0)
    mod = _adaln(c8, ada_w, ada_b)

    col_scale = np.ones((EVEN_IN,), np.float32)
    col_scale[1024:1536] = HEAD_DIM ** -0.5 * LOG2E
    p0 = _inproj((ctx, x), mod[0], pre_g[0], (ev_w_in[0] * jnp.asarray(col_scale)).astype(BF16))
    bcat, ccat, lre8, lim8 = _s5_weights(s5_lam_re[0], s5_lam_im[0], s5_log_dt[0], s5_b_re[0], s5_b_im[0],
                                         s5_c_re[0], s5_c_im[0])
    yf, yr = _s5(p0, pf, pr, pft, prt, bcat, ccat, lre8, lim8)
    attn0 = _na(p0, _na_bias_table(na_rel_bias[0]))
    h1 = _even_out(yf, yr, p0, attn0, ctx, x, mod[0], s5_d[0], s5_w_glu[0].astype(BF16), s5_b_glu[0],
                   ev_w_out[0].astype(BF16), post_g[0])

    p1 = _inproj(h1, mod[1], pre_g[1], _odd_w_in(od_w_in[0]))
    wg = _lru_gate_weights(lru_w_a[0], lru_w_x[0])
    bg = jnp.concatenate([lru_b_a[0], lru_b_x[0]], axis=1)
    hf, hr = _lru(p1, pf, pr, ph, pft, prt, _conv_coef(lru_conv_w[0]), lru_conv_b[0], wg, bg, lru_lam[0])
    cos_np, sin_np = _rope_tables()
    gq, gk = gqa_q_norm[0], gqa_k_norm[0]
    gvec = jnp.stack([jnp.tile(gq, 2), jnp.tile(_swap_pairs_vec(gq), 2),
                      jnp.tile(gk, 2), jnp.tile(_swap_pairs_vec(gk), 2)] + [jnp.zeros((128,), F32)] * 4)
    ones_np = np.kron(np.eye(2, dtype=np.float32), np.ones((HEAD_DIM, HEAD_DIM), np.float32))
    attn1 = _gqa(p1, jnp.asarray(cos_np), jnp.asarray(sin_np), gvec, jnp.asarray(ones_np, BF16))
    w_out1 = od_w_out[0].astype(BF16)
    w_out1 = jnp.concatenate([w_out1[:512], _interleave_kv_groups(w_out1[512:], 0)], axis=0)
    return _odd_out(hf, hr, p1, attn1, h1, mod[1], w_out1, post_g[1])
```

```python
import functools
import math

import numpy as np
import jax
import jax.numpy as jnp
from jax import lax
from jax.experimental import pallas as pl
from jax.experimental.pallas import tpu as pltpu

F32 = jnp.float32
BF16 = jnp.bfloat16
HIGHEST = lax.Precision.HIGHEST

D_MODEL = 1024
BATCH = 4
SEQ = 4096
GRID_W = 64
CTX_LEN = 256
LT = CTX_LEN + SEQ
HEAD_DIM = 64
EPS = 1e-6
S5_WIDTH = 512
S5_GROUP = 16
S5_GROUPS = 32
S5_STATE = 64
NA_HEADS = 8
NA_ROWS = 8
NA_COLS = 16
LRU_WIDTH = 512
LRU_BLOCKS = 8
LRU_BLOCK = 64
LRU_C = 8.0
GQA_HEADS = 8
ROPE_THETA = 10000.0
EVEN_IN = 3072
ODD_IN = 2944

TOK_BLK = 256
SCAN_BLK = 64
N_SCAN = LT // SCAN_BLK
CTX_SCAN = CTX_LEN // SCAN_BLK
Q_BLK = 256
NEG = -1e30
LOG2E = math.log2(math.e)
VMEM_LIMIT = 56 * 1024 * 1024


def _cparams(sem):
    return pltpu.CompilerParams(dimension_semantics=sem, vmem_limit_bytes=VMEM_LIMIT)


def _dot(a, b):
    return jnp.dot(a, b, preferred_element_type=F32)


def _dot_nt(a, b):
    return lax.dot_general(a, b, (((1,), (1,)), ((), ())), preferred_element_type=F32)


def _adaln_kernel(c_ref, w_ref, b_ref, o_ref):
    c = c_ref[...]
    s = c * jax.nn.sigmoid(c)
    o_ref[0] = jnp.dot(s, w_ref[0], preferred_element_type=F32, precision=HIGHEST) + b_ref[0]


def _adaln(c8, ada_w, ada_b):
    depth = ada_w.shape[0]
    nb = 3 * D_MODEL // 1024
    return pl.pallas_call(
        _adaln_kernel,
        grid=(depth, nb),
        in_specs=[
            pl.BlockSpec((8, D_MODEL), lambda i, n: (0, 0)),
            pl.BlockSpec((1, D_MODEL, 1024), lambda i, n: (i, 0, n)),
            pl.BlockSpec((1, 1, 1024), lambda i, n: (i, 0, n)),
        ],
        out_specs=pl.BlockSpec((1, 8, 1024), lambda i, n: (i, 0, n)),
        out_shape=jax.ShapeDtypeStruct((depth, 8, 3 * D_MODEL), F32),
        compiler_params=_cparams(("arbitrary", "arbitrary")),
        name="adaln",
    )(c8, ada_w, ada_b.reshape(depth, 1, 3 * D_MODEL))


PAIR = 2


def _cat_specs(width):
    return [pl.BlockSpec((PAIR, TOK_BLK, width), lambda b, t: (b, 0, 0)),
            pl.BlockSpec((PAIR, TOK_BLK, width), lambda b, t: (b, jnp.maximum(t - 1, 0), 0))]


def _mod_row(mod_ref, h, tb):
    row = jnp.where(tb == 0, BATCH, pl.program_id(0) * PAIR + h)
    return mod_ref[pl.ds(row, 1), :]


def _inproj_kernel(*refs, two_src):
    tb = pl.program_id(1)
    if two_src:
        c_ref, x_ref, mod_ref, g_ref, w_ref, o_ref = refs
    else:
        x_ref, mod_ref, g_ref, w_ref, o_ref = refs
    ys = []
    for h in range(PAIR):
        x = jnp.where(tb == 0, c_ref[h], x_ref[h]) if two_src else x_ref[h]
        r = lax.rsqrt(jnp.mean(x * x, axis=-1, keepdims=True) + EPS)
        m = _mod_row(mod_ref, h, tb)
        y = (x * r) * g_ref[...]
        ys.append((y * (1.0 + m[:, D_MODEL:2 * D_MODEL]) + m[:, :D_MODEL]).astype(BF16))
    for h in range(PAIR):
        o_ref[h] = _dot(ys[h], w_ref[...]).astype(BF16)


def _inproj(src, mod, g, w_bf):
    n = w_bf.shape[1]
    two_src = isinstance(src, tuple)
    if two_src:
        src_specs = _cat_specs(D_MODEL)
    else:
        src_specs = [pl.BlockSpec((PAIR, TOK_BLK, D_MODEL), lambda b, t: (b, t, 0))]
        src = (src,)
    return pl.pallas_call(
        functools.partial(_inproj_kernel, two_src=two_src),
        grid=(BATCH // PAIR, LT // TOK_BLK),
        in_specs=src_specs + [
            pl.BlockSpec((8, 3 * D_MODEL), lambda b, t: (0, 0)),
            pl.BlockSpec((1, D_MODEL), lambda b, t: (0, 0)),
            pl.BlockSpec((D_MODEL, n), lambda b, t: (0, 0)),
        ],
        out_specs=pl.BlockSpec((PAIR, TOK_BLK, n), lambda b, t: (b, t, 0)),
        out_shape=jax.ShapeDtypeStruct((BATCH, LT, n), BF16),
        compiler_params=_cparams(("arbitrary", "arbitrary")),
        name="inproj",
    )(*src, mod, g.reshape(1, D_MODEL), w_bf)


def _fwd_blk(i):
    return i


def _rev_blk(i):
    return jnp.where(i < CTX_SCAN, CTX_SCAN - 1 - i, N_SCAN + CTX_SCAN - 1 - i)


def _scan_perms():
    t = SCAN_BLK
    pf = np.zeros((8 * t, BATCH * t), np.float32)
    pr = np.zeros((8 * t, BATCH * t), np.float32)
    for tt in range(t):
        for b in range(BATCH):
            pf[tt * 8 + b, b * t + tt] = 1.0
            pr[tt * 8 + 4 + b, b * t + (t - 1 - tt)] = 1.0
    return pf, pr


def _halo_perm():
    ph = np.zeros((32, 16 * 4 * BATCH), np.float32)
    for b in range(BATCH):
        ph[1 * 8 + b, 0 * 64 + b * 16 + 15] = 1.0
        ph[2 * 8 + b, 1 * 64 + b * 16 + 0] = 1.0
        ph[3 * 8 + b, 1 * 64 + b * 16 + 1] = 1.0
        ph[0 * 8 + 4 + b, 3 * 64 + b * 16 + 1] = 1.0
        ph[1 * 8 + 4 + b, 3 * 64 + b * 16 + 0] = 1.0
        ph[2 * 8 + 4 + b, 2 * 64 + b * 16 + 15] = 1.0
    return ph


def _conv_coef(conv_w):
    zero = jnp.zeros((1, LRU_WIDTH), conv_w.dtype)
    fwd = jnp.concatenate([zero, conv_w], axis=0)
    rev = jnp.concatenate([conv_w[::-1], zero], axis=0)
    return jnp.concatenate([jnp.broadcast_to(fwd[:, None], (5, 4, LRU_WIDTH)),
                            jnp.broadcast_to(rev[:, None], (5, 4, LRU_WIDTH))], axis=1)


def _s5_kernel(uf_ref, ur_ref, pf_ref, pr_ref, pft_ref, prt_ref, bcat_ref, ccat_ref, lre_ref, lim_ref,
               yf_ref, yr_ref, buf0_ref, buf1_ref, st_ref):
    i = pl.program_id(0)

    @pl.when(i == 0)
    def _():
        st_ref[...] = jnp.zeros_like(st_ref)
        buf0_ref[...] = jnp.zeros_like(buf0_ref)
        buf1_ref[...] = jnp.zeros_like(buf1_ref)

    nrow = 8 * SCAN_BLK

    def step(buf_a, buf_b):
        vals = {}

        def perm():
            uf = uf_ref[...].reshape(BATCH * SCAN_BLK, S5_WIDTH)
            ur = ur_ref[...].reshape(BATCH * SCAN_BLK, S5_WIDTH)
            vals['u_f'] = _dot(pf_ref[...], uf).astype(BF16)
            vals['u_r'] = _dot(pr_ref[...], ur).astype(BF16)

        def readout(j):
            is_f = (lax.broadcasted_iota(jnp.int32, (nrow, 128), 0) & 7) < 4
            yj = _dot(buf_a[:, j * 1024:(j + 1) * 1024].astype(BF16), ccat_ref[j])
            vals['y%d' % j] = jnp.where(is_f, yj[:, :128], yj[:, 128:]).astype(BF16)

        def project(j):
            lhs = jnp.concatenate([vals['u_f'][:, j * 128:(j + 1) * 128], vals['u_r'][:, j * 128:(j + 1) * 128]],
                                  axis=1)
            buf_a[:, j * 1024:(j + 1) * 1024] = _dot(lhs, bcat_ref[j])

        def unperm(p_ref, o_ref):
            yp = jnp.concatenate([vals['y%d' % j] for j in range(4)], axis=1)
            o_ref[...] = _dot(p_ref[...], yp).astype(BF16).reshape(BATCH, SCAN_BLK, S5_WIDTH)

        mxu = [perm]
        for j in range(4):
            mxu += [functools.partial(readout, j), functools.partial(project, j)]
        mxu += [functools.partial(unperm, pft_ref, yf_ref), functools.partial(unperm, prt_ref, yr_ref)]

        def scan(j, t0):
            c_re = slice(j * 1024, j * 1024 + 512)
            c_im = slice(j * 1024 + 512, (j + 1) * 1024)
            lre = lre_ref[j]
            lim = lim_ref[j]
            hre = st_ref[:, c_re]
            him = st_ref[:, c_im]
            for t in range(t0, t0 + 16):
                rows = slice(t * 8, (t + 1) * 8)
                nre = lre * hre - lim * him + buf_b[rows, c_re]
                nim = lre * him + lim * hre + buf_b[rows, c_im]
                buf_b[rows, c_re] = nre
                buf_b[rows, c_im] = nim
                hre, him = nre, nim
            st_ref[:, c_re] = hre
            st_ref[:, c_im] = him

        vpu = [functools.partial(scan, j, t0) for j in range(4) for t0 in range(0, SCAN_BLK, 16)]

        per_piece = [1, 2, 1, 2, 1, 2, 1, 2, 1, 2, 1]
        for piece, n_scan in zip(mxu, per_piece):
            piece()
            for _ in range(n_scan):
                vpu.pop(0)()

    @pl.when(i % 2 == 0)
    def _():
        step(buf0_ref, buf1_ref)

    @pl.when(i % 2 == 1)
    def _():
        step(buf1_ref, buf0_ref)


def _s5_weights(lam_re, lam_im, log_dt, b_re, b_im, c_re, c_im):
    a = lam_re.astype(F32) * jnp.exp(log_dt.astype(F32))[..., None]
    b = lam_im.astype(F32) * jnp.exp(log_dt.astype(F32))[..., None]
    lbr = jnp.exp(a) * jnp.cos(b)
    lbi = jnp.exp(a) * jnp.sin(b)
    nr = jnp.expm1(a) * jnp.cos(b) - 2.0 * jnp.sin(0.5 * b) ** 2
    d2 = lam_re * lam_re + lam_im * lam_im
    qr = (nr * lam_re + lbi * lam_im) / d2
    qi = (lbi * lam_re - nr * lam_im) / d2
    bbr = qr[..., None] * b_re - qi[..., None] * b_im
    bbi = qr[..., None] * b_im + qi[..., None] * b_re
    same = jnp.asarray(np.eye(8, dtype=np.float32))
    bb = jnp.stack([bbr, bbi], axis=1).reshape(2, 2, 4, 8, S5_STATE, S5_GROUP)
    bb = bb.transpose(2, 0, 5, 1, 3, 4)[:, :, None] * same[:, None, None, :, None]
    bcat = bb.reshape(4, 2 * 128, 2 * 512).astype(BF16)
    cc = jnp.stack([c_re.astype(F32), -c_im.astype(F32)], axis=1).reshape(2, 2, 4, 8, S5_GROUP, S5_STATE)
    cc = cc.transpose(2, 1, 3, 5, 0, 4)[:, :, :, :, :, None] * same[:, None, None, :, None]
    ccat = cc.reshape(4, 2 * 512, 2 * 128).astype(BF16)
    lre8 = jnp.repeat(lbr.reshape(2, 4, 512).transpose(1, 0, 2), 4, axis=1)
    lim8 = jnp.repeat(lbi.reshape(2, 4, 512).transpose(1, 0, 2), 4, axis=1)
    return bcat, ccat, lre8, lim8


def _s5(p, pf, pr, pft, prt, bcat, ccat, lre8, lim8):
    nrow = 8 * SCAN_BLK
    const2 = lambda i: (0, 0)
    const3 = lambda i: (0, 0, 0)
    proj = lambda i: jnp.minimum(i, N_SCAN - 1)
    read = lambda i: jnp.clip(i - 2, 0, N_SCAN - 1)
    return pl.pallas_call(
        _s5_kernel,
        grid=(N_SCAN + 2,),
        in_specs=[
            pl.BlockSpec((BATCH, SCAN_BLK, S5_WIDTH), lambda i: (0, _fwd_blk(proj(i)), 0)),
            pl.BlockSpec((BATCH, SCAN_BLK, S5_WIDTH), lambda i: (0, _rev_blk(proj(i)), 0)),
            pl.BlockSpec(pf.shape, const2), pl.BlockSpec(pr.shape, const2),
            pl.BlockSpec(pft.shape, const2), pl.BlockSpec(prt.shape, const2),
            pl.BlockSpec(bcat.shape, const3), pl.BlockSpec(ccat.shape, const3),
            pl.BlockSpec(lre8.shape, const3), pl.BlockSpec(lim8.shape, const3),
        ],
        out_specs=[
            pl.BlockSpec((BATCH, SCAN_BLK, S5_WIDTH), lambda i: (0, _fwd_blk(read(i)), 0)),
            pl.BlockSpec((BATCH, SCAN_BLK, S5_WIDTH), lambda i: (0, _rev_blk(read(i)), 0)),
        ],
        out_shape=[jax.ShapeDtypeStruct((BATCH, LT, S5_WIDTH), BF16)] * 2,
        scratch_shapes=[pltpu.VMEM((nrow, 4096), F32), pltpu.VMEM((nrow, 4096), F32),
                        pltpu.VMEM((8, 4096), F32)],
        compiler_params=_cparams(("arbitrary",)),
        name="s5_scan",
    )(p, p, pf, pr, pft, prt, bcat, ccat, lre8, lim8)


def _na_bias_table(rel_bias):
    w = np.arange(GRID_W)
    cs = np.clip(w - NA_COLS // 2, 0, GRID_W - NA_COLS)
    cp = np.arange(GRID_W)
    valid = (cp[None, :] >= cs[:, None]) & (cp[None, :] < cs[:, None] + NA_COLS)
    dc = cp[None, :] - w[:, None] + (NA_COLS - 1)
    n_dc = 2 * NA_COLS - 1
    onehot = ((dc[None] == np.arange(n_dc)[:, None, None]) & valid[None]).astype(np.float32)
    n_dr = 2 * NA_ROWS - 1
    oh2 = np.zeros((2, n_dc, GRID_W, 2, GRID_W), np.float32)
    oh2[0, :, :, 0, :] = onehot
    oh2[1, :, :, 1, :] = onehot
    rb = rel_bias.astype(F32)
    rb2 = jnp.concatenate([rb[:, :n_dr - 1], rb[:, 1:]], axis=2).reshape(NA_HEADS * (n_dr - 1), 2 * n_dc)
    band2 = jnp.dot(rb2, jnp.asarray(oh2.reshape(2 * n_dc, 2 * GRID_W * GRID_W)), precision=HIGHEST)
    neg2 = np.tile(np.where(valid, 0.0, NEG).astype(np.float32), (1, 2))
    band2 = band2.reshape(NA_HEADS, n_dr - 1, GRID_W, 2 * GRID_W) * LOG2E + jnp.asarray(neg2)
    tabs = [jnp.concatenate([band2[:, off + 2 * i2] for i2 in range(NA_ROWS // 2)], axis=-1)
            for off in range(8)]
    tabs.append(jnp.full_like(tabs[0], NEG))
    return jnp.stack(tabs).reshape(9, NA_HEADS // 2, 128, NA_ROWS * GRID_W)


NA_STEP_ROWS = TOK_BLK // GRID_W


def _na_kernel(q_ref, k_ref, v_ref, bias_ref, o_ref):
    s_idx = pl.program_id(1)
    is_ctx = s_idx == 0
    lo = lax.broadcasted_iota(jnp.int32, (GRID_W, 128), 1) < HEAD_DIM
    nwin = NA_ROWS * GRID_W
    for rr in range(NA_STEP_ROWS):
        r = jnp.maximum(s_idx - 1, 0) * NA_STEP_ROWS + rr
        start = jnp.clip(r - NA_ROWS // 2, 0, SEQ // GRID_W - NA_ROWS)
        koff = pl.multiple_of(CTX_LEN + start * GRID_W, GRID_W)
        off = jnp.where(is_ctx, 8, start - r + NA_ROWS - 1)
        rows = slice(rr * GRID_W, (rr + 1) * GRID_W)
        pairs = range(NA_HEADS // 2)
        cols = [slice(hp * 128, (hp + 1) * 128) for hp in pairs]
        s_loc, s_ctx, m, p_loc, p_ctx, l, o = [], [], [], [], [], [], []
        for hp in pairs:
            qp = q_ref[0, rows, cols[hp]]
            zero = jnp.zeros_like(qp)
            q2 = jnp.concatenate([jnp.where(lo, qp, zero), jnp.where(lo, zero, qp)], axis=0)
            s_loc.append(_dot_nt(q2, k_ref[0, pl.ds(koff, nwin), cols[hp]]) + bias_ref[off, hp])
            s_ctx.append(_dot_nt(q2, k_ref[0, 0:CTX_LEN, cols[hp]]))
        for hp in pairs:
            m.append(jnp.maximum(jnp.max(s_loc[hp], axis=-1, keepdims=True),
                                 jnp.max(s_ctx[hp], axis=-1, keepdims=True)))
        for hp in pairs:
            pl_, pc_ = jnp.exp2(s_loc[hp] - m[hp]), jnp.exp2(s_ctx[hp] - m[hp])
            l.append(jnp.sum(pl_, axis=-1, keepdims=True) + jnp.sum(pc_, axis=-1, keepdims=True))
            p_loc.append(pl_.astype(BF16))
            p_ctx.append(pc_.astype(BF16))
        for hp in pairs:
            o.append(_dot(p_loc[hp], v_ref[0, pl.ds(koff, nwin), cols[hp]]) +
                     _dot(p_ctx[hp], v_ref[0, 0:CTX_LEN, cols[hp]]))
        for hp in pairs:
            oh = o[hp] / l[hp]
            o_ref[0, rows, cols[hp]] = jnp.where(lo, oh[:GRID_W], oh[GRID_W:]).astype(BF16)


def _na(p, bias_tb):
    return pl.pallas_call(
        _na_kernel,
        grid=(BATCH, LT // TOK_BLK),
        in_specs=[
            pl.BlockSpec((1, TOK_BLK, 512), lambda b, s: (b, s, 2)),
            pl.BlockSpec((1, LT, 512), lambda b, s: (b, 0, 3)),
            pl.BlockSpec((1, LT, 512), lambda b, s: (b, 0, 4)),
            pl.BlockSpec(bias_tb.shape, lambda b, s: (0, 0, 0, 0)),
        ],
        out_specs=pl.BlockSpec((1, TOK_BLK, 512), lambda b, s: (b, s, 0)),
        out_shape=jax.ShapeDtypeStruct((BATCH, LT, 512), BF16),
        compiler_params=_cparams(("arbitrary", "arbitrary")),
        name="na_attn",
    )(p, p, p, bias_tb)


def _gelu_tanh(x):
    return 0.5 * x * (1.0 + jnp.tanh(math.sqrt(2.0 / math.pi) * (x + 0.044715 * (x * x * x))))


def _sigmoid(x):
    return 0.5 + 0.5 * jnp.tanh(0.5 * x)


def _silu(x):
    return x * _sigmoid(x)


def _finish(h, out, gt, pg):
    r = lax.rsqrt(jnp.mean(out * out, axis=-1, keepdims=True) + EPS)
    return h + gt * ((out * r) * pg)


def _even_out_kernel(yf_ref, yr_ref, u_ref, ga_ref, gb_ref, at_ref, hc_ref, hl_ref, mod_ref, d_ref, wg_ref,
                     bg_ref, wo_ref, pg_ref, o_ref):
    tb = pl.program_id(1)
    chains = range(PAIR)
    y = [_gelu_tanh(d_ref[...] * u_ref[h].astype(F32) + yf_ref[h].astype(F32) + yr_ref[h].astype(F32))
         for h in chains]
    glu = [_dot(y[h].astype(BF16), wg_ref[...]) for h in chains]
    y_a = [(y[h] * _sigmoid(glu[h] + bg_ref[...]) * _silu(ga_ref[h].astype(F32))).astype(BF16) for h in chains]
    y_b = [(at_ref[h].astype(F32) * _silu(gb_ref[h].astype(F32))).astype(BF16) for h in chains]
    out = [_dot(y_a[h], wo_ref[0:512, :]) + _dot(y_b[h], wo_ref[512:1024, :]) for h in chains]
    for h in chains:
        res = jnp.where(tb == 0, hc_ref[h], hl_ref[h])
        o_ref[h] = _finish(res, out[h], _mod_row(mod_ref, h, tb)[:, 2 * D_MODEL:], pg_ref[...])


def _even_out(yf, yr, p, attn, ctx, x, mod, d_skip, w_glu, b_glu, w_out, post_g):
    tok = lambda c: pl.BlockSpec((PAIR, TOK_BLK, 512), lambda b, t: (b, t, c))
    full = lambda a: pl.BlockSpec(a.shape, lambda b, t: (0,) * a.ndim)
    d2, bg2, pg2 = d_skip.reshape(1, 512), b_glu.reshape(1, 512), post_g.reshape(1, D_MODEL)
    return pl.pallas_call(
        _even_out_kernel,
        grid=(BATCH // PAIR, LT // TOK_BLK),
        in_specs=[tok(0), tok(0), tok(0), tok(1), tok(5), tok(0)] + _cat_specs(D_MODEL) +
                 [full(mod), full(d2), full(w_glu), full(bg2), full(w_out), full(pg2)],
        out_specs=pl.BlockSpec((PAIR, TOK_BLK, D_MODEL), lambda b, t: (b, t, 0)),
        out_shape=jax.ShapeDtypeStruct((BATCH, LT, D_MODEL), F32),
        compiler_params=_cparams(("arbitrary", "arbitrary")),
        name="even_out",
    )(yf, yr, p, p, p, attn, ctx, x, mod, d2, w_glu, bg2, w_out, pg2)


def _softplus(z):
    return jnp.maximum(z, 0.0) + jnp.log1p(jnp.exp(-jnp.abs(z)))


def _lru_kernel(xf_ref, xfp_ref, xfn_ref, xr_ref, xrp_ref, xrn_ref, pf_ref, pr_ref, ph_ref, pft_ref, prt_ref,
                coef_ref, cb_ref, wg_ref, bg_ref, lam_ref, hf_ref, hr_ref, a0_ref, b0_ref, a1_ref, b1_ref, st_ref):
    i = pl.program_id(0)

    @pl.when(i == 0)
    def _():
        for ref in (st_ref, a0_ref, b0_ref, a1_ref, b1_ref):
            ref[...] = jnp.zeros_like(ref)

    nrow = 8 * SCAN_BLK
    half = nrow // 2
    blk = jnp.minimum(i, N_SCAN - 1)

    def halo(prev_ref, next_ref, b):
        prev_ok = jnp.logical_and(b != 0, b != CTX_SCAN)
        next_ok = jnp.logical_and(b != CTX_SCAN - 1, b != N_SCAN - 1)
        pv = prev_ref[...].reshape(BATCH * 16, LRU_WIDTH)
        nx = next_ref[...].reshape(BATCH * 16, LRU_WIDTH)
        return [jnp.where(prev_ok, pv, jnp.zeros_like(pv)), jnp.where(next_ok, nx, jnp.zeros_like(nx))]

    def step(a_w, b_w, a_s, b_s):
        vals = {}

        def unperm():
            hs = b_w[...].astype(BF16)
            hf_ref[...] = _dot(pft_ref[...], hs).astype(BF16).reshape(BATCH, SCAN_BLK, LRU_WIDTH)
            hr_ref[...] = _dot(prt_ref[...], hs).astype(BF16).reshape(BATCH, SCAN_BLK, LRU_WIDTH)

        def perm():
            hal = jnp.concatenate(halo(xfp_ref, xfn_ref, _fwd_blk(blk)) + halo(xrp_ref, xrn_ref, _rev_blk(blk)),
                                  axis=0)
            xh = _dot(ph_ref[...], hal)
            xp = (_dot(pf_ref[...], xf_ref[...].reshape(BATCH * SCAN_BLK, LRU_WIDTH)) +
                  _dot(pr_ref[...], xr_ref[...].reshape(BATCH * SCAN_BLK, LRU_WIDTH)))
            vals['x_ext'] = jnp.concatenate([xh[:16], xp, xh[16:]], axis=0)

        def conv(h):
            xc = jnp.zeros((SCAN_BLK // 2, 8, LRU_WIDTH), F32) + cb_ref[...]
            for s in range(5):
                r0 = h * half + s * 8
                xc = xc + vals['x_ext'][r0:r0 + half].reshape(SCAN_BLK // 2, 8, LRU_WIDTH) * coef_ref[s]
            vals['xc%d' % h] = xc.reshape(half, LRU_WIDTH)

        def gates(h):
            is_f = (lax.broadcasted_iota(jnp.int32, (half, 1), 0) & 7) < 4
            xcb = vals['xc%d' % h].astype(BF16)
            zero = jnp.zeros_like(xcb[:, :256])
            pre_r, pre_i = [], []
            for c in range(2):
                xt = xcb[:, c * 256:(c + 1) * 256]
                lhs = jnp.concatenate([jnp.where(is_f, xt, zero), jnp.where(is_f, zero, xt)], axis=1)
                pre = _dot(lhs, wg_ref[c])
                pre_r.append(pre[:, :256])
                pre_i.append(pre[:, 256:])
            vals['pr%d' % h] = jnp.concatenate(pre_r, axis=1)
            vals['pi%d' % h] = jnp.concatenate(pre_i, axis=1)

        def elem(h):
            rows = slice(h * half, (h + 1) * half)
            shp = (SCAN_BLK // 2, 8, LRU_WIDTH)
            t_r = jnp.tanh(vals['pr%d' % h].reshape(shp) + bg_ref[:, :LRU_WIDTH])
            t_i = jnp.tanh(vals['pi%d' % h].reshape(shp) + bg_ref[:, LRU_WIDTH:])
            c = (-0.5 * LRU_C) * _softplus(-lam_ref[...])
            u = 1.0 + t_r
            a = jnp.exp2((c * LOG2E) * u)
            a_w[rows, :] = a.reshape(half, LRU_WIDTH)
            one_m_a2 = jnp.maximum(jnp.tanh((-c) * u) * (a * a + 1.0), 1e-37)
            mult = one_m_a2 * lax.rsqrt(one_m_a2)
            b = (mult * vals['xc%d' % h].reshape(shp)) * (0.5 + 0.5 * t_i)
            b_w[rows, :] = b.reshape(half, LRU_WIDTH)

        def scan(t0):
            h = st_ref[...]
            for t in range(t0, t0 + 16):
                rows = slice(t * 8, (t + 1) * 8)
                h = a_s[rows, :] * h + b_s[rows, :]
                b_s[rows, :] = h
            st_ref[...] = h

        for piece in (unperm, functools.partial(scan, 0), perm, functools.partial(conv, 0),
                      functools.partial(conv, 1), functools.partial(scan, 16), functools.partial(gates, 0),
                      functools.partial(gates, 1), functools.partial(scan, 32), functools.partial(elem, 0),
                      functools.partial(elem, 1), functools.partial(scan, 48)):
            piece()

    @pl.when(i % 2 == 0)
    def _():
        step(a0_ref, b0_ref, a1_ref, b1_ref)

    @pl.when(i % 2 == 1)
    def _():
        step(a1_ref, b1_ref, a0_ref, b0_ref)


def _lru_gate_weights(w_a, w_x):
    w = jnp.stack([w_a, w_x], axis=1).reshape(2, 2, 2, 4, LRU_BLOCK, LRU_BLOCK)
    same = jnp.asarray(np.eye(4, dtype=np.float32))
    w = w.transpose(2, 0, 3, 4, 1, 5)[:, :, :, :, :, None] * same[:, None, None, :, None]
    return (0.5 * w).reshape(2, 512, 512).astype(BF16)


def _lru(p, pf, pr, ph, pft, prt, coef, conv_b, wg, bg, lam):
    nrow = 8 * SCAN_BLK
    full = lambda a: pl.BlockSpec(a.shape, lambda i: (0,) * a.ndim)
    proj = lambda i: jnp.minimum(i, N_SCAN - 1)
    read = lambda i: jnp.clip(i - 2, 0, N_SCAN - 1)
    cur = lambda f: pl.BlockSpec((BATCH, SCAN_BLK, LRU_WIDTH), lambda i: (0, f(proj(i)), 0))
    prev = lambda f: pl.BlockSpec((BATCH, 16, LRU_WIDTH), lambda i: (0, jnp.maximum(f(proj(i)) * 4 - 1, 0), 0))
    nxt = lambda f: pl.BlockSpec((BATCH, 16, LRU_WIDTH),
                                 lambda i: (0, jnp.minimum(f(proj(i)) * 4 + 4, LT // 16 - 1), 0))
    cb2 = conv_b.reshape(1, LRU_WIDTH)
    return pl.pallas_call(
        _lru_kernel,
        grid=(N_SCAN + 2,),
        in_specs=[cur(_fwd_blk), prev(_fwd_blk), nxt(_fwd_blk), cur(_rev_blk), prev(_rev_blk), nxt(_rev_blk),
                  full(pf), full(pr), full(ph), full(pft), full(prt), full(coef), full(cb2), full(wg), full(bg),
                  full(lam)],
        out_specs=[
            pl.BlockSpec((BATCH, SCAN_BLK, LRU_WIDTH), lambda i: (0, _fwd_blk(read(i)), 0)),
            pl.BlockSpec((BATCH, SCAN_BLK, LRU_WIDTH), lambda i: (0, _rev_blk(read(i)), 0)),
        ],
        out_shape=[jax.ShapeDtypeStruct((BATCH, LT, LRU_WIDTH), BF16)] * 2,
        scratch_shapes=[pltpu.VMEM((nrow, LRU_WIDTH), F32)] * 4 + [pltpu.VMEM((8, LRU_WIDTH), F32)],
        compiler_params=_cparams(("arbitrary",)),
        name="lru_scan",
    )(p, p, p, p, p, p, pf, pr, ph, pft, prt, coef, cb2, wg, bg, lam)


def _norm_rope(x, xrot, g, grot, cos, sin, ones_blk):
    x = x.astype(F32)
    xrot = xrot.astype(F32)
    ms = _dot((x * x).astype(BF16), ones_blk) * (1.0 / HEAD_DIM)
    rs = lax.rsqrt(ms + EPS)
    return rs * ((x * g) * cos + (xrot * grot) * sin)


def _gqa_kernel(q_ref, qr_ref, k_ref, kr_ref, v_ref, cq_ref, sq_ref, ck_ref, sk_ref, g_ref, ones_ref,
                o_ref, kn_ref, vlo_ref, vhi_ref):
    qb = pl.program_id(1)
    ones_blk = ones_ref[...]

    @pl.when(qb == 0)
    def _():
        kn = _norm_rope(k_ref[0], kr_ref[0], g_ref[2:3, :], g_ref[3:4, :], ck_ref[...], sk_ref[...], ones_blk)
        kn_ref[...] = kn.astype(BF16)
        v = v_ref[0]
        lo_k = lax.broadcasted_iota(jnp.int32, (LT, 128), 1) < HEAD_DIM
        one = jnp.ones_like(v)
        vlo_ref[...] = jnp.where(lo_k, v, one)
        vhi_ref[...] = jnp.where(lo_k, one, v)

    lo = lax.broadcasted_iota(jnp.int32, (Q_BLK, 128), 1) < HEAD_DIM
    kn = kn_ref[...]
    cos = cq_ref[...]
    sin = sq_ref[...]
    n_pb = GQA_HEADS // 2

    def scores(pb):
        cols = slice(pb * 128, (pb + 1) * 128)
        qn = _norm_rope(q_ref[0, :, cols], qr_ref[0, :, cols], g_ref[0:1, :], g_ref[1:2, :], cos, sin, ones_blk)
        qn = (qn * (HEAD_DIM ** -0.5 * LOG2E)).astype(BF16)
        zero = jnp.zeros_like(qn)
        qs = jnp.concatenate([jnp.where(lo, qn, zero), jnp.where(lo, zero, qn)], axis=0)
        return _dot_nt(qs, kn)

    s_next = scores(0)
    for pb in range(n_pb):
        cols = slice(pb * 128, (pb + 1) * 128)
        s = s_next
        if pb + 1 < n_pb:
            s_next = scores(pb + 1)
        m = jnp.max(s, axis=-1, keepdims=True)
        p = jnp.exp2(s - m).astype(BF16)
        o_lo = _dot(p[:Q_BLK], vlo_ref[...])
        o_hi = _dot(p[Q_BLK:], vhi_ref[...])
        num = jnp.where(lo, o_lo, o_hi)
        den = pltpu.roll(jnp.where(lo, o_hi, o_lo), HEAD_DIM, axis=1)
        o_ref[0, :, cols] = (num / den).astype(BF16)


def _gqa(p, cos128, sin128, gvec, ones_blk):
    nqb = SEQ // Q_BLK
    cb = CTX_LEN // Q_BLK
    return pl.pallas_call(
        _gqa_kernel,
        grid=(BATCH, nqb),
        in_specs=[
            pl.BlockSpec((1, Q_BLK, 512), lambda b, q: (b, q + cb, 2)),
            pl.BlockSpec((1, Q_BLK, 512), lambda b, q: (b, q + cb, 3)),
            pl.BlockSpec((1, LT, 128), lambda b, q: (b, 0, 20)),
            pl.BlockSpec((1, LT, 128), lambda b, q: (b, 0, 21)),
            pl.BlockSpec((1, LT, 128), lambda b, q: (b, 0, 22)),
            pl.BlockSpec((Q_BLK, 128), lambda b, q: (q + cb, 0)),
            pl.BlockSpec((Q_BLK, 128), lambda b, q: (q + cb, 0)),
            pl.BlockSpec((LT, 128), lambda b, q: (0, 0)),
            pl.BlockSpec((LT, 128), lambda b, q: (0, 0)),
            pl.BlockSpec((8, 128), lambda b, q: (0, 0)),
            pl.BlockSpec((128, 128), lambda b, q: (0, 0)),
        ],
        out_specs=pl.BlockSpec((1, Q_BLK, 512), lambda b, q: (b, q, 0)),
        out_shape=jax.ShapeDtypeStruct((BATCH, SEQ, 512), BF16),
        scratch_shapes=[pltpu.VMEM((LT, 128), BF16)] * 3,
        compiler_params=_cparams(("arbitrary", "arbitrary")),
        name="gqa_attn",
    )(p, p, p, p, p, cos128, sin128, cos128, sin128, gvec, ones_blk)


def _odd_out_kernel(hf_ref, hr_ref, gc_ref, gd_ref, at_ref, h_ref, mod_ref, wo_ref, pg_ref, o_ref):
    chains = range(PAIR)
    y_c = [((hf_ref[h].astype(F32) + hr_ref[h].astype(F32)) * _silu(gc_ref[h].astype(F32))).astype(BF16)
           for h in chains]
    y_d = [(at_ref[h].astype(F32) * _silu(gd_ref[h].astype(F32))).astype(BF16) for h in chains]
    out = [_dot(y_c[h], wo_ref[0:512, :]) + _dot(y_d[h], wo_ref[512:1024, :]) for h in chains]
    for h in chains:
        gt = mod_ref[pl.ds(pl.program_id(0) * PAIR + h, 1), 2 * D_MODEL:]
        o_ref[h] = _finish(h_ref[h], out[h], gt, pg_ref[...])


def _odd_out(hf, hr, p, attn, h_cat, mod, w_out, post_g):
    cb = CTX_LEN // TOK_BLK
    cat = lambda c: pl.BlockSpec((PAIR, TOK_BLK, 512), lambda b, t: (b, t + cb, c))
    full = lambda a: pl.BlockSpec(a.shape, lambda b, t: (0,) * a.ndim)
    pg2 = post_g.reshape(1, D_MODEL)
    return pl.pallas_call(
        _odd_out_kernel,
        grid=(BATCH // PAIR, SEQ // TOK_BLK),
        in_specs=[cat(0), cat(0), cat(1), cat(4),
                  pl.BlockSpec((PAIR, TOK_BLK, 512), lambda b, t: (b, t, 0)),
                  pl.BlockSpec((PAIR, TOK_BLK, D_MODEL), lambda b, t: (b, t + cb, 0)),
                  full(mod), full(w_out), full(pg2)],
        out_specs=pl.BlockSpec((PAIR, TOK_BLK, D_MODEL), lambda b, t: (b, t, 0)),
        out_shape=jax.ShapeDtypeStruct((BATCH, SEQ, D_MODEL), F32),
        compiler_params=_cparams(("arbitrary", "arbitrary")),
        name="odd_out",
    )(hf, hr, p, p, attn, h_cat, mod, w_out, pg2)


def _pair_swap(w):
    n = w.shape[1]
    r = np.zeros((n, n), np.float32)
    r[np.arange(1, n, 2), np.arange(0, n, 2)] = -1.0
    r[np.arange(0, n, 2), np.arange(1, n, 2)] = 1.0
    return jnp.dot(w, jnp.asarray(r, BF16), preferred_element_type=F32).astype(BF16)


def _interleave_kv_groups(w, axis):
    if axis == 0:
        return w.reshape(2, 4, HEAD_DIM, w.shape[1]).transpose(1, 0, 2, 3).reshape(w.shape)
    return w.reshape(w.shape[0], 2, 4, HEAD_DIM).transpose(0, 2, 1, 3).reshape(w.shape)


def _odd_w_in(w):
    w = w.astype(BF16)
    x, gc = w[:, 0:512], w[:, 512:1024]
    q = _interleave_kv_groups(w[:, 1024:1536], 1)
    k, v = w[:, 1536:1664], w[:, 1664:1792]
    gd = _interleave_kv_groups(w[:, 1792:2304], 1)
    return jnp.concatenate([x, gc, q, _pair_swap(q), gd, k, _pair_swap(k), v], axis=1)


def _rope_tables():
    t = np.arange(SEQ)
    row = (t // GRID_W).astype(np.float32)
    col = (t % GRID_W).astype(np.float32)
    half = HEAD_DIM // 2
    inv = (ROPE_THETA ** (-np.arange(0, half, 2, dtype=np.float32) / half)).astype(np.float32)
    ang = np.concatenate([row[:, None] * inv, col[:, None] * inv], axis=-1)
    cos = np.repeat(np.cos(ang), 2, axis=-1)
    sin = np.repeat(np.sin(ang), 2, axis=-1)
    cos = np.concatenate([np.ones((CTX_LEN, HEAD_DIM), np.float32), cos], axis=0)
    sin = np.concatenate([np.zeros((CTX_LEN, HEAD_DIM), np.float32), sin], axis=0)
    return np.tile(cos, (1, 2)).astype(np.float32), np.tile(sin, (1, 2)).astype(np.float32)


def _swap_pairs_vec(g):
    g2 = g.reshape(-1, 2)
    return jnp.stack([g2[:, 1], g2[:, 0]], axis=-1).reshape(g.shape)


def _block_diag(w):
    n, a, b = w.shape
    return jnp.einsum('mn,nij->minj', jnp.eye(n, dtype=w.dtype), w).reshape(n * a, n * b)


def kernel(x, c, ctx, c_ctx, ada_w, ada_b, pre_g, post_g, ev_w_in, ev_w_out, s5_lam_re, s5_lam_im, s5_log_dt,
           s5_b_re, s5_b_im, s5_c_re, s5_c_im, s5_d, s5_w_glu, s5_b_glu, na_rel_bias, od_w_in, od_w_out,
           lru_conv_w, lru_conv_b, lru_lam, lru_w_a, lru_b_a, lru_w_x, lru_b_x, gqa_q_norm, gqa_k_norm):
    pf_np, pr_np = _scan_perms()
    pf, pr = jnp.asarray(pf_np, BF16), jnp.asarray(pr_np, BF16)
    pft, prt = jnp.asarray(pf_np.T, BF16), jnp.asarray(pr_np.T, BF16)
    ph = jnp.asarray(_halo_perm(), BF16)

    c8 = jnp.concatenate([c, c_ctx[None], jnp.zeros((3, D_MODEL), F32)], axis=0)
    mod = _adaln(c8, ada_w, ada_b)

    col_scale = np.ones((EVEN_IN,), np.float32)
    col_scale[1024:1536] = HEAD_DIM ** -0.5 * LOG2E
    p0 = _inproj((ctx, x), mod[0], pre_g[0], (ev_w_in[0] * jnp.asarray(col_scale)).astype(BF16))
    bcat, ccat, lre8, lim8 = _s5_weights(s5_lam_re[0], s5_lam_im[0], s5_log_dt[0], s5_b_re[0], s5_b_im[0],
                                         s5_c_re[0], s5_c_im[0])
    yf, yr = _s5(p0, pf, pr, pft, prt, bcat, ccat, lre8, lim8)
    attn0 = _na(p0, _na_bias_table(na_rel_bias[0]))
    h1 = _even_out(yf, yr, p0, attn0, ctx, x, mod[0], s5_d[0], s5_w_glu[0].astype(BF16), s5_b_glu[0],
                   ev_w_out[0].astype(BF16), post_g[0])

    p1 = _inproj(h1, mod[1], pre_g[1], _odd_w_in(od_w_in[0]))
    wg = _lru_gate_weights(lru_w_a[0], lru_w_x[0])
    bg = jnp.repeat(0.5 * jnp.concatenate([lru_b_a[0], lru_b_x[0]], axis=1), 4, axis=0)
    lam8 = jnp.repeat(lru_lam[0], 4, axis=0)
    hf, hr = _lru(p1, pf, pr, ph, pft, prt, _conv_coef(lru_conv_w[0]), lru_conv_b[0], wg, bg, lam8)
    cos_np, sin_np = _rope_tables()
    gq, gk = gqa_q_norm[0], gqa_k_norm[0]
    gvec = jnp.stack([jnp.tile(gq, 2), jnp.tile(_swap_pairs_vec(gq), 2),
                      jnp.tile(gk, 2), jnp.tile(_swap_pairs_vec(gk), 2)] + [jnp.zeros((128,), F32)] * 4)
    ones_np = np.kron(np.eye(2, dtype=np.float32), np.ones((HEAD_DIM, HEAD_DIM), np.float32))
    attn1 = _gqa(p1, jnp.asarray(cos_np), jnp.asarray(sin_np), gvec, jnp.asarray(ones_np, BF16))
    w_out1 = od_w_out[0].astype(BF16)
    w_out1 = jnp.concatenate([w_out1[:512], _interleave_kv_groups(w_out1[512:], 0)], axis=0)
    return _odd_out(hf, hr, p1, attn1, h1, mod[1], w_out1, post_g[1])
```

```python
import functools
import math

import numpy as np
import jax
import jax.numpy as jnp
from jax import lax
from jax.experimental import pallas as pl
from jax.experimental.pallas import tpu as pltpu

F32 = jnp.float32
BF16 = jnp.bfloat16
HIGHEST = lax.Precision.HIGHEST

D_MODEL = 1024
BATCH = 4
SEQ = 4096
GRID_W = 64
CTX_LEN = 256
LT = CTX_LEN + SEQ
HEAD_DIM = 64
EPS = 1e-6
S5_WIDTH = 512
S5_GROUP = 16
S5_GROUPS = 32
S5_STATE = 64
NA_HEADS = 8
NA_ROWS = 8
NA_COLS = 16
LRU_WIDTH = 512
LRU_BLOCKS = 8
LRU_BLOCK = 64
LRU_C = 8.0
GQA_HEADS = 8
ROPE_THETA = 10000.0
EVEN_IN = 3072
ODD_IN = 2944

TOK_BLK = 256
SCAN_BLK = 64
N_SCAN = LT // SCAN_BLK
CTX_SCAN = CTX_LEN // SCAN_BLK
Q_BLK = 256
NEG = -1e30
LOG2E = math.log2(math.e)
VMEM_LIMIT = 56 * 1024 * 1024


def _cparams(sem):
    return pltpu.CompilerParams(dimension_semantics=sem, vmem_limit_bytes=VMEM_LIMIT)


def _dot(a, b):
    return jnp.dot(a, b, preferred_element_type=F32)


def _dot_nt(a, b):
    return lax.dot_general(a, b, (((1,), (1,)), ((), ())), preferred_element_type=F32)


def _adaln_kernel(c_ref, w_ref, b_ref, o_ref):
    c = c_ref[...]
    s = c * jax.nn.sigmoid(c)
    o_ref[0] = jnp.dot(s, w_ref[0], preferred_element_type=F32, precision=HIGHEST) + b_ref[0]


def _adaln(c8, ada_w, ada_b):
    depth = ada_w.shape[0]
    nb = 3 * D_MODEL // 1024
    return pl.pallas_call(
        _adaln_kernel,
        grid=(depth, nb),
        in_specs=[
            pl.BlockSpec((8, D_MODEL), lambda i, n: (0, 0)),
            pl.BlockSpec((1, D_MODEL, 1024), lambda i, n: (i, 0, n)),
            pl.BlockSpec((1, 1, 1024), lambda i, n: (i, 0, n)),
        ],
        out_specs=pl.BlockSpec((1, 8, 1024), lambda i, n: (i, 0, n)),
        out_shape=jax.ShapeDtypeStruct((depth, 8, 3 * D_MODEL), F32),
        compiler_params=_cparams(("arbitrary", "arbitrary")),
        name="adaln",
    )(c8, ada_w, ada_b.reshape(depth, 1, 3 * D_MODEL))


PAIR = 2


def _cat_specs(width):
    return [pl.BlockSpec((PAIR, TOK_BLK, width), lambda b, t: (b, 0, 0)),
            pl.BlockSpec((PAIR, TOK_BLK, width), lambda b, t: (b, jnp.maximum(t - 1, 0), 0))]


def _mod_row(mod_ref, h, tb):
    row = jnp.where(tb == 0, BATCH, pl.program_id(0) * PAIR + h)
    return mod_ref[pl.ds(row, 1), :]


def _inproj_kernel(*refs, two_src):
    tb = pl.program_id(1)
    if two_src:
        c_ref, x_ref, mod_ref, g_ref, w_ref, o_ref = refs
    else:
        x_ref, mod_ref, g_ref, w_ref, o_ref = refs
    ys = []
    for h in range(PAIR):
        x = jnp.where(tb == 0, c_ref[h], x_ref[h]) if two_src else x_ref[h]
        r = lax.rsqrt(jnp.mean(x * x, axis=-1, keepdims=True) + EPS)
        m = _mod_row(mod_ref, h, tb)
        y = (x * r) * g_ref[...]
        ys.append((y * (1.0 + m[:, D_MODEL:2 * D_MODEL]) + m[:, :D_MODEL]).astype(BF16))
    for h in range(PAIR):
        o_ref[h] = _dot(ys[h], w_ref[...]).astype(BF16)


def _inproj(src, mod, g, w_bf):
    n = w_bf.shape[1]
    two_src = isinstance(src, tuple)
    if two_src:
        src_specs = _cat_specs(D_MODEL)
    else:
        src_specs = [pl.BlockSpec((PAIR, TOK_BLK, D_MODEL), lambda b, t: (b, t, 0))]
        src = (src,)
    return pl.pallas_call(
        functools.partial(_inproj_kernel, two_src=two_src),
        grid=(BATCH // PAIR, LT // TOK_BLK),
        in_specs=src_specs + [
            pl.BlockSpec((8, 3 * D_MODEL), lambda b, t: (0, 0)),
            pl.BlockSpec((1, D_MODEL), lambda b, t: (0, 0)),
            pl.BlockSpec((D_MODEL, n), lambda b, t: (0, 0)),
        ],
        out_specs=pl.BlockSpec((PAIR, TOK_BLK, n), lambda b, t: (b, t, 0)),
        out_shape=jax.ShapeDtypeStruct((BATCH, LT, n), BF16),
        compiler_params=_cparams(("arbitrary", "arbitrary")),
        name="inproj",
    )(*src, mod, g.reshape(1, D_MODEL), w_bf)


def _fwd_blk(i):
    return i


def _rev_blk(i):
    return jnp.where(i < CTX_SCAN, CTX_SCAN - 1 - i, N_SCAN + CTX_SCAN - 1 - i)


def _scan_perms():
    t = SCAN_BLK
    pf = np.zeros((8 * t, BATCH * t), np.float32)
    pr = np.zeros((8 * t, BATCH * t), np.float32)
    for tt in range(t):
        for b in range(BATCH):
            pf[tt * 8 + b, b * t + tt] = 1.0
            pr[tt * 8 + 4 + b, b * t + (t - 1 - tt)] = 1.0
    return pf, pr


def _halo_perm():
    ph = np.zeros((32, 16 * 4 * BATCH), np.float32)
    for b in range(BATCH):
        ph[1 * 8 + b, 0 * 64 + b * 16 + 15] = 1.0
        ph[2 * 8 + b, 1 * 64 + b * 16 + 0] = 1.0
        ph[3 * 8 + b, 1 * 64 + b * 16 + 1] = 1.0
        ph[0 * 8 + 4 + b, 3 * 64 + b * 16 + 1] = 1.0
        ph[1 * 8 + 4 + b, 3 * 64 + b * 16 + 0] = 1.0
        ph[2 * 8 + 4 + b, 2 * 64 + b * 16 + 15] = 1.0
    return ph


def _conv_coef(conv_w):
    zero = jnp.zeros((1, LRU_WIDTH), conv_w.dtype)
    fwd = jnp.concatenate([zero, conv_w], axis=0)
    rev = jnp.concatenate([conv_w[::-1], zero], axis=0)
    return jnp.concatenate([jnp.broadcast_to(fwd[:, None], (5, 4, LRU_WIDTH)),
                            jnp.broadcast_to(rev[:, None], (5, 4, LRU_WIDTH))], axis=1)


def _s5_kernel(uf_ref, ur_ref, pf_ref, pr_ref, pft_ref, prt_ref, bc_ref, rep_ref, same_ref, lre_ref, lim_ref,
               yf_ref, yr_ref, buf0_ref, buf1_ref, st_ref, bcat_ref, ccat_ref):
    i = pl.program_id(0)

    @pl.when(i == 0)
    def _():
        st_ref[...] = jnp.zeros_like(st_ref)
        buf0_ref[...] = jnp.zeros_like(buf0_ref)
        buf1_ref[...] = jnp.zeros_like(buf1_ref)
        same = same_ref[...].astype(F32)
        for j in range(4):
            bcat_ref[j] = (_dot(rep_ref[...], bc_ref[0, j]) * same).astype(BF16)
            ccat_ref[j] = (_dot(rep_ref[...], bc_ref[1, j]) * same).astype(BF16)

    nrow = 8 * SCAN_BLK

    def step(buf_a, buf_b):
        vals = {}

        def perm():
            uf = uf_ref[...].reshape(BATCH * SCAN_BLK, S5_WIDTH)
            ur = ur_ref[...].reshape(BATCH * SCAN_BLK, S5_WIDTH)
            vals['u_f'] = _dot(pf_ref[...], uf).astype(BF16)
            vals['u_r'] = _dot(pr_ref[...], ur).astype(BF16)

        def readout(j):
            is_f = (lax.broadcasted_iota(jnp.int32, (nrow, 128), 0) & 7) < 4
            yj = _dot_nt(buf_a[:, j * 1024:(j + 1) * 1024].astype(BF16), ccat_ref[j])
            vals['y%d' % j] = jnp.where(is_f, yj[:, :128], yj[:, 128:]).astype(BF16)

        def project(j):
            lhs = jnp.concatenate([vals['u_f'][:, j * 128:(j + 1) * 128], vals['u_r'][:, j * 128:(j + 1) * 128]],
                                  axis=1)
            buf_a[:, j * 1024:(j + 1) * 1024] = _dot(lhs, bcat_ref[j])

        def unperm(p_ref, o_ref):
            yp = jnp.concatenate([vals['y%d' % j] for j in range(4)], axis=1)
            o_ref[...] = _dot(p_ref[...], yp).astype(BF16).reshape(BATCH, SCAN_BLK, S5_WIDTH)

        mxu = [perm]
        for j in range(4):
            mxu += [functools.partial(readout, j), functools.partial(project, j)]
        mxu += [functools.partial(unperm, pft_ref, yf_ref), functools.partial(unperm, prt_ref, yr_ref)]

        def scan(j, t0):
            c_re = slice(j * 1024, j * 1024 + 512)
            c_im = slice(j * 1024 + 512, (j + 1) * 1024)
            lre = lre_ref[j]
            lim = lim_ref[j]
            hre = st_ref[:, c_re]
            him = st_ref[:, c_im]
            for t in range(t0, t0 + 16):
                rows = slice(t * 8, (t + 1) * 8)
                nre = lre * hre - lim * him + buf_b[rows, c_re]
                nim = lre * him + lim * hre + buf_b[rows, c_im]
                buf_b[rows, c_re] = nre
                buf_b[rows, c_im] = nim
                hre, him = nre, nim
            st_ref[:, c_re] = hre
            st_ref[:, c_im] = him

        vpu = [functools.partial(scan, j, t0) for j in range(4) for t0 in range(0, SCAN_BLK, 16)]

        per_piece = [1, 2, 1, 2, 1, 2, 1, 2, 1, 2, 1]
        for piece, n_scan in zip(mxu, per_piece):
            piece()
            for _ in range(n_scan):
                vpu.pop(0)()

    @pl.when(i % 2 == 0)
    def _():
        step(buf0_ref, buf1_ref)

    @pl.when(i % 2 == 1)
    def _():
        step(buf1_ref, buf0_ref)


def _s5_weights(lam_re, lam_im, log_dt, b_re, b_im, c_re, c_im):
    a = lam_re.astype(F32) * jnp.exp(log_dt.astype(F32))[..., None]
    b = lam_im.astype(F32) * jnp.exp(log_dt.astype(F32))[..., None]
    lbr = jnp.exp(a) * jnp.cos(b)
    lbi = jnp.exp(a) * jnp.sin(b)
    nr = jnp.expm1(a) * jnp.cos(b) - 2.0 * jnp.sin(0.5 * b) ** 2
    d2 = lam_re * lam_re + lam_im * lam_im
    qr = (nr * lam_re + lbi * lam_im) / d2
    qi = (lbi * lam_re - nr * lam_im) / d2
    bbr = qr[..., None] * b_re - qi[..., None] * b_im
    bbi = qr[..., None] * b_im + qi[..., None] * b_re
    bb = jnp.stack([bbr, bbi], axis=1).reshape(2, 2, 4, 8, S5_STATE, S5_GROUP)
    bb = bb.transpose(2, 0, 5, 1, 3, 4).reshape(4, 2 * S5_GROUP, 2 * 512)
    cc = jnp.stack([c_re.astype(F32), -c_im.astype(F32)], axis=1).reshape(2, 2, 4, 8, S5_GROUP, S5_STATE)
    cc = cc.transpose(2, 0, 4, 1, 3, 5).reshape(4, 2 * S5_GROUP, 2 * 512)
    lre8 = jnp.repeat(lbr.reshape(2, 4, 512).transpose(1, 0, 2), 4, axis=1)
    lim8 = jnp.repeat(lbi.reshape(2, 4, 512).transpose(1, 0, 2), 4, axis=1)
    return jnp.stack([bb, cc]).astype(BF16), lre8, lim8


def _s5_expanders():
    rows = np.arange(2 * 128)
    d, g, h = rows // 128, (rows // S5_GROUP) % 8, rows % S5_GROUP
    k = np.arange(2 * S5_GROUP)
    rep = ((d[:, None] == k[None, :] // S5_GROUP) & (h[:, None] == k[None, :] % S5_GROUP)).astype(np.float32)
    cols = np.arange(2 * 512)
    same = (g[:, None] == (cols[None, :] // S5_STATE) % 8).astype(np.float32)
    return rep, same


def _s5(p, pf, pr, pft, prt, bc, rep, same, lre8, lim8):
    nrow = 8 * SCAN_BLK
    const2 = lambda i: (0, 0)
    const3 = lambda i: (0, 0, 0)
    proj = lambda i: jnp.minimum(i, N_SCAN - 1)
    read = lambda i: jnp.clip(i - 2, 0, N_SCAN - 1)
    return pl.pallas_call(
        _s5_kernel,
        grid=(N_SCAN + 2,),
        in_specs=[
            pl.BlockSpec((BATCH, SCAN_BLK, S5_WIDTH), lambda i: (0, _fwd_blk(proj(i)), 0)),
            pl.BlockSpec((BATCH, SCAN_BLK, S5_WIDTH), lambda i: (0, _rev_blk(proj(i)), 0)),
            pl.BlockSpec(pf.shape, const2), pl.BlockSpec(pr.shape, const2),
            pl.BlockSpec(pft.shape, const2), pl.BlockSpec(prt.shape, const2),
            pl.BlockSpec(bc.shape, lambda i: (0, 0, 0, 0)), pl.BlockSpec(rep.shape, const2),
            pl.BlockSpec(same.shape, const2),
            pl.BlockSpec(lre8.shape, const3), pl.BlockSpec(lim8.shape, const3),
        ],
        out_specs=[
            pl.BlockSpec((BATCH, SCAN_BLK, S5_WIDTH), lambda i: (0, _fwd_blk(read(i)), 0)),
            pl.BlockSpec((BATCH, SCAN_BLK, S5_WIDTH), lambda i: (0, _rev_blk(read(i)), 0)),
        ],
        out_shape=[jax.ShapeDtypeStruct((BATCH, LT, S5_WIDTH), BF16)] * 2,
        scratch_shapes=[pltpu.VMEM((nrow, 4096), F32), pltpu.VMEM((nrow, 4096), F32),
                        pltpu.VMEM((8, 4096), F32),
                        pltpu.VMEM((4, 256, 1024), BF16), pltpu.VMEM((4, 256, 1024), BF16)],
        compiler_params=_cparams(("arbitrary",)),
        name="s5_scan",
    )(p, p, pf, pr, pft, prt, bc, rep, same, lre8, lim8)


def _na_bias_table(rel_bias):
    w = np.arange(GRID_W)
    cs = np.clip(w - NA_COLS // 2, 0, GRID_W - NA_COLS)
    cp = np.arange(GRID_W)
    valid = (cp[None, :] >= cs[:, None]) & (cp[None, :] < cs[:, None] + NA_COLS)
    dc = cp[None, :] - w[:, None] + (NA_COLS - 1)
    n_dc = 2 * NA_COLS - 1
    onehot = ((dc[None] == np.arange(n_dc)[:, None, None]) & valid[None]).astype(np.float32)
    n_dr = 2 * NA_ROWS - 1
    oh2 = np.zeros((2, n_dc, GRID_W, 2, GRID_W), np.float32)
    oh2[0, :, :, 0, :] = onehot
    oh2[1, :, :, 1, :] = onehot
    rb = rel_bias.astype(F32)
    rb2 = jnp.concatenate([rb[:, :n_dr - 1], rb[:, 1:]], axis=2).reshape(NA_HEADS * (n_dr - 1), 2 * n_dc)
    band2 = jnp.dot(rb2, jnp.asarray(oh2.reshape(2 * n_dc, 2 * GRID_W * GRID_W)), precision=HIGHEST)
    neg2 = np.tile(np.where(valid, 0.0, NEG).astype(np.float32), (1, 2))
    return band2.reshape(NA_HEADS, n_dr - 1, GRID_W, 2 * GRID_W) * LOG2E + jnp.asarray(neg2)


def _fill_bias_tables(band_ref, bias_ref):
    for off in range(8):
        for i2 in range(NA_ROWS // 2):
            bias_ref[off, :, :, i2 * 128:(i2 + 1) * 128] = band_ref[:, off + 2 * i2].reshape(NA_HEADS // 2, 128, 128)
    bias_ref[8] = jnp.full(bias_ref.shape[1:], NEG, F32)


NA_STEP_ROWS = TOK_BLK // GRID_W


def _na_kernel(q_ref, k_ref, v_ref, band_ref, o_ref, bias_ref):
    s_idx = pl.program_id(1)
    is_ctx = s_idx == 0

    @pl.when(jnp.logical_and(pl.program_id(0) == 0, s_idx == 0))
    def _():
        _fill_bias_tables(band_ref, bias_ref)

    lo = lax.broadcasted_iota(jnp.int32, (GRID_W, 128), 1) < HEAD_DIM
    nwin = NA_ROWS * GRID_W
    for rr in range(NA_STEP_ROWS):
        r = jnp.maximum(s_idx - 1, 0) * NA_STEP_ROWS + rr
        start = jnp.clip(r - NA_ROWS // 2, 0, SEQ // GRID_W - NA_ROWS)
        koff = pl.multiple_of(CTX_LEN + start * GRID_W, GRID_W)
        off = jnp.where(is_ctx, 8, start - r + NA_ROWS - 1)
        rows = slice(rr * GRID_W, (rr + 1) * GRID_W)
        pairs = range(NA_HEADS // 2)
        cols = [slice(hp * 128, (hp + 1) * 128) for hp in pairs]
        s_loc, s_ctx, m, p_loc, p_ctx, l, o = [], [], [], [], [], [], []
        for hp in pairs:
            qp = q_ref[0, rows, cols[hp]]
            zero = jnp.zeros_like(qp)
            q2 = jnp.concatenate([jnp.where(lo, qp, zero), jnp.where(lo, zero, qp)], axis=0)
            s_loc.append(_dot_nt(q2, k_ref[0, pl.ds(koff, nwin), cols[hp]]) + bias_ref[off, hp])
            s_ctx.append(_dot_nt(q2, k_ref[0, 0:CTX_LEN, cols[hp]]))
        for hp in pairs:
            m.append(jnp.maximum(jnp.max(s_loc[hp], axis=-1, keepdims=True),
                                 jnp.max(s_ctx[hp], axis=-1, keepdims=True)))
        for hp in pairs:
            pl_, pc_ = jnp.exp2(s_loc[hp] - m[hp]), jnp.exp2(s_ctx[hp] - m[hp])
            l.append(jnp.sum(pl_, axis=-1, keepdims=True) + jnp.sum(pc_, axis=-1, keepdims=True))
            p_loc.append(pl_.astype(BF16))
            p_ctx.append(pc_.astype(BF16))
        for hp in pairs:
            o.append(_dot(p_loc[hp], v_ref[0, pl.ds(koff, nwin), cols[hp]]) +
                     _dot(p_ctx[hp], v_ref[0, 0:CTX_LEN, cols[hp]]))
        for hp in pairs:
            oh = o[hp] / l[hp]
            o_ref[0, rows, cols[hp]] = jnp.where(lo, oh[:GRID_W], oh[GRID_W:]).astype(BF16)


def _na(p, band2):
    return pl.pallas_call(
        _na_kernel,
        grid=(BATCH, LT // TOK_BLK),
        in_specs=[
            pl.BlockSpec((1, TOK_BLK, 512), lambda b, s: (b, s, 2)),
            pl.BlockSpec((1, LT, 512), lambda b, s: (b, 0, 3)),
            pl.BlockSpec((1, LT, 512), lambda b, s: (b, 0, 4)),
            pl.BlockSpec(band2.shape, lambda b, s: (0, 0, 0, 0)),
        ],
        out_specs=pl.BlockSpec((1, TOK_BLK, 512), lambda b, s: (b, s, 0)),
        out_shape=jax.ShapeDtypeStruct((BATCH, LT, 512), BF16),
        scratch_shapes=[pltpu.VMEM((9, NA_HEADS // 2, 128, NA_ROWS * GRID_W), F32)],
        compiler_params=_cparams(("arbitrary", "arbitrary")),
        name="na_attn",
    )(p, p, p, band2)


def _gelu_tanh(x):
    return 0.5 * x * (1.0 + jnp.tanh(math.sqrt(2.0 / math.pi) * (x + 0.044715 * (x * x * x))))


def _silu(x):
    h = 0.5 * x
    return h + h * jnp.tanh(h)


def _finish(h, out, gt, pg):
    r = lax.rsqrt(jnp.mean(out * out, axis=-1, keepdims=True) + EPS)
    return h + (out * r) * (gt * pg)


def _even_out_kernel(yf_ref, yr_ref, u_ref, ga_ref, gb_ref, at_ref, hc_ref, hl_ref, mod_ref, d_ref, wg_ref,
                     bg_ref, wo_ref, pg_ref, o_ref):
    tb = pl.program_id(1)
    chains = range(PAIR)
    y = [_gelu_tanh(d_ref[...] * u_ref[h].astype(F32) + yf_ref[h].astype(F32) + yr_ref[h].astype(F32))
         for h in chains]
    glu = [_dot(y[h].astype(BF16), wg_ref[...]) for h in chains]
    y_a = [(y[h] * (0.5 + 0.5 * jnp.tanh(glu[h] + bg_ref[...])) * _silu(ga_ref[h].astype(F32))).astype(BF16)
           for h in chains]
    y_b = [(at_ref[h].astype(F32) * _silu(gb_ref[h].astype(F32))).astype(BF16) for h in chains]
    out = [_dot(y_a[h], wo_ref[0:512, :]) + _dot(y_b[h], wo_ref[512:1024, :]) for h in chains]
    for h in chains:
        res = jnp.where(tb == 0, hc_ref[h], hl_ref[h])
        o_ref[h] = _finish(res, out[h], _mod_row(mod_ref, h, tb)[:, 2 * D_MODEL:], pg_ref[...])


def _even_out(yf, yr, p, attn, ctx, x, mod, d_skip, w_glu, b_glu, w_out, post_g):
    tok = lambda c: pl.BlockSpec((PAIR, TOK_BLK, 512), lambda b, t: (b, t, c))
    full = lambda a: pl.BlockSpec(a.shape, lambda b, t: (0,) * a.ndim)
    d2, bg2, pg2 = d_skip.reshape(1, 512), b_glu.reshape(1, 512), post_g.reshape(1, D_MODEL)
    return pl.pallas_call(
        _even_out_kernel,
        grid=(BATCH // PAIR, LT // TOK_BLK),
        in_specs=[tok(0), tok(0), tok(0), tok(1), tok(5), tok(0)] + _cat_specs(D_MODEL) +
                 [full(mod), full(d2), full(w_glu), full(bg2), full(w_out), full(pg2)],
        out_specs=pl.BlockSpec((PAIR, TOK_BLK, D_MODEL), lambda b, t: (b, t, 0)),
        out_shape=jax.ShapeDtypeStruct((BATCH, LT, D_MODEL), F32),
        compiler_params=_cparams(("arbitrary", "arbitrary")),
        name="even_out",
    )(yf, yr, p, p, p, attn, ctx, x, mod, d2, w_glu, bg2, w_out, pg2)


def _softplus(z):
    return jnp.maximum(z, 0.0) + jnp.log1p(jnp.exp(-jnp.abs(z)))


def _lru_kernel(xf_ref, xfp_ref, xfn_ref, xr_ref, xrp_ref, xrn_ref, pf_ref, pr_ref, ph_ref, pft_ref, prt_ref,
                coef_ref, cb_ref, wg_ref, bg_ref, lam_ref, hf_ref, hr_ref, a0_ref, b0_ref, a1_ref, b1_ref, st_ref):
    i = pl.program_id(0)

    @pl.when(i == 0)
    def _():
        for ref in (st_ref, a0_ref, b0_ref, a1_ref, b1_ref):
            ref[...] = jnp.zeros_like(ref)

    nrow = 8 * SCAN_BLK
    half = nrow // 2
    blk = jnp.minimum(i, N_SCAN - 1)

    def halo(prev_ref, next_ref, b):
        prev_ok = jnp.logical_and(b != 0, b != CTX_SCAN)
        next_ok = jnp.logical_and(b != CTX_SCAN - 1, b != N_SCAN - 1)
        pv = prev_ref[...].reshape(BATCH * 16, LRU_WIDTH)
        nx = next_ref[...].reshape(BATCH * 16, LRU_WIDTH)
        return [jnp.where(prev_ok, pv, jnp.zeros_like(pv)), jnp.where(next_ok, nx, jnp.zeros_like(nx))]

    def step(a_w, b_w, a_s, b_s):
        vals = {}

        def unperm():
            hs = b_w[...].astype(BF16)
            hf_ref[...] = _dot(pft_ref[...], hs).astype(BF16).reshape(BATCH, SCAN_BLK, LRU_WIDTH)
            hr_ref[...] = _dot(prt_ref[...], hs).astype(BF16).reshape(BATCH, SCAN_BLK, LRU_WIDTH)

        def perm():
            hal = jnp.concatenate(halo(xfp_ref, xfn_ref, _fwd_blk(blk)) + halo(xrp_ref, xrn_ref, _rev_blk(blk)),
                                  axis=0)
            xh = _dot(ph_ref[...], hal)
            xp = (_dot(pf_ref[...], xf_ref[...].reshape(BATCH * SCAN_BLK, LRU_WIDTH)) +
                  _dot(pr_ref[...], xr_ref[...].reshape(BATCH * SCAN_BLK, LRU_WIDTH)))
            vals['x_ext'] = jnp.concatenate([xh[:16], xp, xh[16:]], axis=0)

        def conv(h):
            xc = jnp.zeros((SCAN_BLK // 2, 8, LRU_WIDTH), F32) + cb_ref[...]
            for s in range(5):
                r0 = h * half + s * 8
                xc = xc + vals['x_ext'][r0:r0 + half].reshape(SCAN_BLK // 2, 8, LRU_WIDTH) * coef_ref[s]
            vals['xc%d' % h] = xc.reshape(half, LRU_WIDTH)

        def gates(h):
            is_f = (lax.broadcasted_iota(jnp.int32, (half, 1), 0) & 7) < 4
            xcb = vals['xc%d' % h].astype(BF16)
            zero = jnp.zeros_like(xcb[:, :256])
            pre_r, pre_i = [], []
            for c in range(2):
                xt = xcb[:, c * 256:(c + 1) * 256]
                lhs = jnp.concatenate([jnp.where(is_f, xt, zero), jnp.where(is_f, zero, xt)], axis=1)
                pre = _dot(lhs, wg_ref[c])
                pre_r.append(pre[:, :256])
                pre_i.append(pre[:, 256:])
            vals['pr%d' % h] = jnp.concatenate(pre_r, axis=1)
            vals['pi%d' % h] = jnp.concatenate(pre_i, axis=1)

        def elem(h):
            rows = slice(h * half, (h + 1) * half)
            shp = (SCAN_BLK // 2, 8, LRU_WIDTH)
            t_r = jnp.tanh(vals['pr%d' % h].reshape(shp) + bg_ref[:, :LRU_WIDTH])
            t_i = jnp.tanh(vals['pi%d' % h].reshape(shp) + bg_ref[:, LRU_WIDTH:])
            c = (-0.5 * LRU_C) * _softplus(-lam_ref[...])
            u = 1.0 + t_r
            a = jnp.exp2((c * LOG2E) * u)
            a_w[rows, :] = a.reshape(half, LRU_WIDTH)
            one_m_a2 = jnp.maximum(jnp.tanh((-c) * u) * (a * a + 1.0), 1e-37)
            mult = one_m_a2 * lax.rsqrt(one_m_a2)
            b = (mult * vals['xc%d' % h].reshape(shp)) * (0.5 + 0.5 * t_i)
            b_w[rows, :] = b.reshape(half, LRU_WIDTH)

        def scan(t0):
            h = st_ref[...]
            for t in range(t0, t0 + 16):
                rows = slice(t * 8, (t + 1) * 8)
                h = a_s[rows, :] * h + b_s[rows, :]
                b_s[rows, :] = h
            st_ref[...] = h

        for piece in (unperm, functools.partial(scan, 0), perm, functools.partial(conv, 0),
                      functools.partial(conv, 1), functools.partial(scan, 16), functools.partial(gates, 0),
                      functools.partial(gates, 1), functools.partial(scan, 32), functools.partial(elem, 0),
                      functools.partial(elem, 1), functools.partial(scan, 48)):
            piece()

    @pl.when(i % 2 == 0)
    def _():
        step(a0_ref, b0_ref, a1_ref, b1_ref)

    @pl.when(i % 2 == 1)
    def _():
        step(a1_ref, b1_ref, a0_ref, b0_ref)


def _lru_gate_weights(w_a, w_x):
    w = jnp.stack([w_a, w_x], axis=1).reshape(2, 2, 2, 4, LRU_BLOCK, LRU_BLOCK)
    same = jnp.asarray(np.eye(4, dtype=np.float32))
    w = w.transpose(2, 0, 3, 4, 1, 5)[:, :, :, :, :, None] * same[:, None, None, :, None]
    return (0.5 * w).reshape(2, 512, 512).astype(BF16)


def _lru(p, pf, pr, ph, pft, prt, coef, conv_b, wg, bg, lam):
    nrow = 8 * SCAN_BLK
    full = lambda a: pl.BlockSpec(a.shape, lambda i: (0,) * a.ndim)
    proj = lambda i: jnp.minimum(i, N_SCAN - 1)
    read = lambda i: jnp.clip(i - 2, 0, N_SCAN - 1)
    cur = lambda f: pl.BlockSpec((BATCH, SCAN_BLK, LRU_WIDTH), lambda i: (0, f(proj(i)), 0))
    prev = lambda f: pl.BlockSpec((BATCH, 16, LRU_WIDTH), lambda i: (0, jnp.maximum(f(proj(i)) * 4 - 1, 0), 0))
    nxt = lambda f: pl.BlockSpec((BATCH, 16, LRU_WIDTH),
                                 lambda i: (0, jnp.minimum(f(proj(i)) * 4 + 4, LT // 16 - 1), 0))
    cb2 = conv_b.reshape(1, LRU_WIDTH)
    return pl.pallas_call(
        _lru_kernel,
        grid=(N_SCAN + 2,),
        in_specs=[cur(_fwd_blk), prev(_fwd_blk), nxt(_fwd_blk), cur(_rev_blk), prev(_rev_blk), nxt(_rev_blk),
                  full(pf), full(pr), full(ph), full(pft), full(prt), full(coef), full(cb2), full(wg), full(bg),
                  full(lam)],
        out_specs=[
            pl.BlockSpec((BATCH, SCAN_BLK, LRU_WIDTH), lambda i: (0, _fwd_blk(read(i)), 0)),
            pl.BlockSpec((BATCH, SCAN_BLK, LRU_WIDTH), lambda i: (0, _rev_blk(read(i)), 0)),
        ],
        out_shape=[jax.ShapeDtypeStruct((BATCH, LT, LRU_WIDTH), BF16)] * 2,
        scratch_shapes=[pltpu.VMEM((nrow, LRU_WIDTH), F32)] * 4 + [pltpu.VMEM((8, LRU_WIDTH), F32)],
        compiler_params=_cparams(("arbitrary",)),
        name="lru_scan",
    )(p, p, p, p, p, p, pf, pr, ph, pft, prt, coef, cb2, wg, bg, lam)


def _norm_rope(x, xrot, g, grot, cos, sin, ones_blk):
    x = x.astype(F32)
    xrot = xrot.astype(F32)
    ms = _dot((x * x).astype(BF16), ones_blk) * (1.0 / HEAD_DIM)
    rs = lax.rsqrt(ms + EPS)
    return rs * ((x * g) * cos + (xrot * grot) * sin)


def _gqa_kernel(q_ref, qr_ref, k_ref, kr_ref, v_ref, cq_ref, sq_ref, ck_ref, sk_ref, g_ref, ones_ref,
                o_ref, kn_ref, vlo_ref, vhi_ref):
    qb = pl.program_id(1)
    ones_blk = ones_ref[...]

    @pl.when(qb == 0)
    def _():
        kn = _norm_rope(k_ref[0], kr_ref[0], g_ref[2:3, :], g_ref[3:4, :], ck_ref[...], sk_ref[...], ones_blk)
        kn_ref[...] = kn.astype(BF16)
        v = v_ref[0]
        lo_k = lax.broadcasted_iota(jnp.int32, (LT, 128), 1) < HEAD_DIM
        one = jnp.ones_like(v)
        vlo_ref[...] = jnp.where(lo_k, v, one)
        vhi_ref[...] = jnp.where(lo_k, one, v)

    lo = lax.broadcasted_iota(jnp.int32, (Q_BLK, 128), 1) < HEAD_DIM
    kn = kn_ref[...]
    cos = cq_ref[...]
    sin = sq_ref[...]
    n_pb = GQA_HEADS // 2

    def scores(pb):
        cols = slice(pb * 128, (pb + 1) * 128)
        qn = _norm_rope(q_ref[0, :, cols], qr_ref[0, :, cols], g_ref[0:1, :], g_ref[1:2, :], cos, sin, ones_blk)
        qn = (qn * (HEAD_DIM ** -0.5 * LOG2E)).astype(BF16)
        zero = jnp.zeros_like(qn)
        qs = jnp.concatenate([jnp.where(lo, qn, zero), jnp.where(lo, zero, qn)], axis=0)
        return _dot_nt(qs, kn)

    s_next = scores(0)
    for pb in range(n_pb):
        cols = slice(pb * 128, (pb + 1) * 128)
        s = s_next
        if pb + 1 < n_pb:
            s_next = scores(pb + 1)
        m = jnp.max(s, axis=-1, keepdims=True)
        p = jnp.exp2(s - m).astype(BF16)
        o_lo = _dot(p[:Q_BLK], vlo_ref[...])
        o_hi = _dot(p[Q_BLK:], vhi_ref[...])
        num = jnp.where(lo, o_lo, o_hi)
        den = pltpu.roll(jnp.where(lo, o_hi, o_lo), HEAD_DIM, axis=1)
        o_ref[0, :, cols] = (num / den).astype(BF16)


def _gqa(p, cos128, sin128, gvec, ones_blk):
    nqb = SEQ // Q_BLK
    cb = CTX_LEN // Q_BLK
    return pl.pallas_call(
        _gqa_kernel,
        grid=(BATCH, nqb),
        in_specs=[
            pl.BlockSpec((1, Q_BLK, 512), lambda b, q: (b, q + cb, 2)),
            pl.BlockSpec((1, Q_BLK, 512), lambda b, q: (b, q + cb, 3)),
            pl.BlockSpec((1, LT, 128), lambda b, q: (b, 0, 20)),
            pl.BlockSpec((1, LT, 128), lambda b, q: (b, 0, 21)),
            pl.BlockSpec((1, LT, 128), lambda b, q: (b, 0, 22)),
            pl.BlockSpec((Q_BLK, 128), lambda b, q: (q + cb, 0)),
            pl.BlockSpec((Q_BLK, 128), lambda b, q: (q + cb, 0)),
            pl.BlockSpec((LT, 128), lambda b, q: (0, 0)),
            pl.BlockSpec((LT, 128), lambda b, q: (0, 0)),
            pl.BlockSpec((8, 128), lambda b, q: (0, 0)),
            pl.BlockSpec((128, 128), lambda b, q: (0, 0)),
        ],
        out_specs=pl.BlockSpec((1, Q_BLK, 512), lambda b, q: (b, q, 0)),
        out_shape=jax.ShapeDtypeStruct((BATCH, SEQ, 512), BF16),
        scratch_shapes=[pltpu.VMEM((LT, 128), BF16)] * 3,
        compiler_params=_cparams(("arbitrary", "arbitrary")),
        name="gqa_attn",
    )(p, p, p, p, p, cos128, sin128, cos128, sin128, gvec, ones_blk)


def _odd_out_kernel(hf_ref, hr_ref, gc_ref, gd_ref, at_ref, h_ref, mod_ref, wo_ref, pg_ref, o_ref):
    chains = range(PAIR)
    y_c = [((hf_ref[h].astype(F32) + hr_ref[h].astype(F32)) * _silu(gc_ref[h].astype(F32))).astype(BF16)
           for h in chains]
    y_d = [(at_ref[h].astype(F32) * _silu(gd_ref[h].astype(F32))).astype(BF16) for h in chains]
    out = [_dot(y_c[h], wo_ref[0:512, :]) + _dot(y_d[h], wo_ref[512:1024, :]) for h in chains]
    for h in chains:
        gt = mod_ref[pl.ds(pl.program_id(0) * PAIR + h, 1), 2 * D_MODEL:]
        o_ref[h] = _finish(h_ref[h], out[h], gt, pg_ref[...])


def _odd_out(hf, hr, p, attn, h_cat, mod, w_out, post_g):
    cb = CTX_LEN // TOK_BLK
    cat = lambda c: pl.BlockSpec((PAIR, TOK_BLK, 512), lambda b, t: (b, t + cb, c))
    full = lambda a: pl.BlockSpec(a.shape, lambda b, t: (0,) * a.ndim)
    pg2 = post_g.reshape(1, D_MODEL)
    return pl.pallas_call(
        _odd_out_kernel,
        grid=(BATCH // PAIR, SEQ // TOK_BLK),
        in_specs=[cat(0), cat(0), cat(1), cat(4),
                  pl.BlockSpec((PAIR, TOK_BLK, 512), lambda b, t: (b, t, 0)),
                  pl.BlockSpec((PAIR, TOK_BLK, D_MODEL), lambda b, t: (b, t + cb, 0)),
                  full(mod), full(w_out), full(pg2)],
        out_specs=pl.BlockSpec((PAIR, TOK_BLK, D_MODEL), lambda b, t: (b, t, 0)),
        out_shape=jax.ShapeDtypeStruct((BATCH, SEQ, D_MODEL), F32),
        compiler_params=_cparams(("arbitrary", "arbitrary")),
        name="odd_out",
    )(hf, hr, p, p, attn, h_cat, mod, w_out, pg2)


def _pair_swap(w):
    n = w.shape[1]
    r = np.zeros((n, n), np.float32)
    r[np.arange(1, n, 2), np.arange(0, n, 2)] = -1.0
    r[np.arange(0, n, 2), np.arange(1, n, 2)] = 1.0
    return jnp.dot(w, jnp.asarray(r, BF16), preferred_element_type=F32).astype(BF16)


def _interleave_kv_groups(w, axis):
    if axis == 0:
        return w.reshape(2, 4, HEAD_DIM, w.shape[1]).transpose(1, 0, 2, 3).reshape(w.shape)
    return w.reshape(w.shape[0], 2, 4, HEAD_DIM).transpose(0, 2, 1, 3).reshape(w.shape)


def _odd_w_in(w):
    w = w.astype(BF16)
    x, gc = w[:, 0:512], w[:, 512:1024]
    q = _interleave_kv_groups(w[:, 1024:1536], 1)
    k, v = w[:, 1536:1664], w[:, 1664:1792]
    gd = _interleave_kv_groups(w[:, 1792:2304], 1)
    return jnp.concatenate([x, gc, q, _pair_swap(q), gd, k, _pair_swap(k), v], axis=1)


def _rope_tables():
    t = np.arange(SEQ)
    row = (t // GRID_W).astype(np.float32)
    col = (t % GRID_W).astype(np.float32)
    half = HEAD_DIM // 2
    inv = (ROPE_THETA ** (-np.arange(0, half, 2, dtype=np.float32) / half)).astype(np.float32)
    ang = np.concatenate([row[:, None] * inv, col[:, None] * inv], axis=-1)
    cos = np.repeat(np.cos(ang), 2, axis=-1)
    sin = np.repeat(np.sin(ang), 2, axis=-1)
    cos = np.concatenate([np.ones((CTX_LEN, HEAD_DIM), np.float32), cos], axis=0)
    sin = np.concatenate([np.zeros((CTX_LEN, HEAD_DIM), np.float32), sin], axis=0)
    return np.tile(cos, (1, 2)).astype(np.float32), np.tile(sin, (1, 2)).astype(np.float32)


def _swap_pairs_vec(g):
    g2 = g.reshape(-1, 2)
    return jnp.stack([g2[:, 1], g2[:, 0]], axis=-1).reshape(g.shape)


def kernel(x, c, ctx, c_ctx, ada_w, ada_b, pre_g, post_g, ev_w_in, ev_w_out, s5_lam_re, s5_lam_im, s5_log_dt,
           s5_b_re, s5_b_im, s5_c_re, s5_c_im, s5_d, s5_w_glu, s5_b_glu, na_rel_bias, od_w_in, od_w_out,
           lru_conv_w, lru_conv_b, lru_lam, lru_w_a, lru_b_a, lru_w_x, lru_b_x, gqa_q_norm, gqa_k_norm):
    pf_np, pr_np = _scan_perms()
    pf, pr = jnp.asarray(pf_np, BF16), jnp.asarray(pr_np, BF16)
    pft, prt = jnp.asarray(pf_np.T, BF16), jnp.asarray(pr_np.T, BF16)
    ph = jnp.asarray(_halo_perm(), BF16)

    c8 = jnp.concatenate([c, c_ctx[None], jnp.zeros((3, D_MODEL), F32)], axis=0)
    mod = _adaln(c8, ada_w, ada_b)

    col_scale = np.ones((EVEN_IN,), np.float32)
    col_scale[1024:1536] = HEAD_DIM ** -0.5 * LOG2E
    p0 = _inproj((ctx, x), mod[0], pre_g[0], (ev_w_in[0] * jnp.asarray(col_scale)).astype(BF16))
    bc, lre8, lim8 = _s5_weights(s5_lam_re[0], s5_lam_im[0], s5_log_dt[0], s5_b_re[0], s5_b_im[0],
                                 s5_c_re[0], s5_c_im[0])
    rep_np, same_np = _s5_expanders()
    yf, yr = _s5(p0, pf, pr, pft, prt, bc, jnp.asarray(rep_np, BF16), jnp.asarray(same_np, BF16), lre8, lim8)
    attn0 = _na(p0, _na_bias_table(na_rel_bias[0]))
    h1 = _even_out(yf, yr, p0, attn0, ctx, x, mod[0], s5_d[0], (0.5 * s5_w_glu[0]).astype(BF16), 0.5 * s5_b_glu[0],
                   ev_w_out[0].astype(BF16), post_g[0])

    p1 = _inproj(h1, mod[1], pre_g[1], _odd_w_in(od_w_in[0]))
    wg = _lru_gate_weights(lru_w_a[0], lru_w_x[0])
    bg = jnp.repeat(0.5 * jnp.concatenate([lru_b_a[0], lru_b_x[0]], axis=1), 4, axis=0)
    lam8 = jnp.repeat(lru_lam[0], 4, axis=0)
    hf, hr = _lru(p1, pf, pr, ph, pft, prt, _conv_coef(lru_conv_w[0]), lru_conv_b[0], wg, bg, lam8)
    cos_np, sin_np = _rope_tables()
    gq, gk = gqa_q_norm[0], gqa_k_norm[0]
    gvec = jnp.stack([jnp.tile(gq, 2), jnp.tile(_swap_pairs_vec(gq), 2),
                      jnp.tile(gk, 2), jnp.tile(_swap_pairs_vec(gk), 2)] + [jnp.zeros((128,), F32)] * 4)
    ones_np = np.kron(np.eye(2, dtype=np.float32), np.ones((HEAD_DIM, HEAD_DIM), np.float32))
    attn1 = _gqa(p1, jnp.asarray(cos_np), jnp.asarray(sin_np), gvec, jnp.asarray(ones_np, BF16))
    w_out1 = od_w_out[0].astype(BF16)
    w_out1 = jnp.concatenate([w_out1[:512], _interleave_kv_groups(w_out1[512:], 0)], axis=0)
    return _odd_out(hf, hr, p1, attn1, h1, mod[1], w_out1, post_g[1])
```

```python
import functools
import math

import numpy as np
import jax
import jax.numpy as jnp
from jax import lax
from jax.experimental import pallas as pl
from jax.experimental.pallas import tpu as pltpu

F32 = jnp.float32
BF16 = jnp.bfloat16
HIGHEST = lax.Precision.HIGHEST

D_MODEL = 1024
BATCH = 4
SEQ = 4096
GRID_W = 64
CTX_LEN = 256
LT = CTX_LEN + SEQ
HEAD_DIM = 64
EPS = 1e-6
S5_WIDTH = 512
S5_GROUP = 16
S5_GROUPS = 32
S5_STATE = 64
NA_HEADS = 8
NA_ROWS = 8
NA_COLS = 16
LRU_WIDTH = 512
LRU_BLOCKS = 8
LRU_BLOCK = 64
LRU_C = 8.0
GQA_HEADS = 8
ROPE_THETA = 10000.0
EVEN_IN = 3072
ODD_IN = 2304

TOK_BLK = 256
SCAN_BLK = 64
N_SCAN = LT // SCAN_BLK
CTX_SCAN = CTX_LEN // SCAN_BLK
Q_BLK = 256
NEG = -1e30
LOG2E = math.log2(math.e)
VMEM_LIMIT = 56 * 1024 * 1024


def _cparams(sem):
    return pltpu.CompilerParams(dimension_semantics=sem, vmem_limit_bytes=VMEM_LIMIT)


def _dot(a, b):
    return jnp.dot(a, b, preferred_element_type=F32)


def _dot_nt(a, b):
    return lax.dot_general(a, b, (((1,), (1,)), ((), ())), preferred_element_type=F32)


def _adaln_kernel(c_ref, w_ref, b_ref, o_ref):
    c = c_ref[...]
    s = c * jax.nn.sigmoid(c)
    o_ref[0] = jnp.dot(s, w_ref[0], preferred_element_type=F32, precision=HIGHEST) + b_ref[0]


def _adaln(c8, ada_w, ada_b):
    depth = ada_w.shape[0]
    nb = 3 * D_MODEL // 1024
    return pl.pallas_call(
        _adaln_kernel,
        grid=(depth, nb),
        in_specs=[
            pl.BlockSpec((8, D_MODEL), lambda i, n: (0, 0)),
            pl.BlockSpec((1, D_MODEL, 1024), lambda i, n: (i, 0, n)),
            pl.BlockSpec((1, 1, 1024), lambda i, n: (i, 0, n)),
        ],
        out_specs=pl.BlockSpec((1, 8, 1024), lambda i, n: (i, 0, n)),
        out_shape=jax.ShapeDtypeStruct((depth, 8, 3 * D_MODEL), F32),
        compiler_params=_cparams(("arbitrary", "arbitrary")),
        name="adaln",
    )(c8, ada_w, ada_b.reshape(depth, 1, 3 * D_MODEL))


PAIR = 2


def _cat_specs(width):
    return [pl.BlockSpec((PAIR, TOK_BLK, width), lambda b, t: (b, 0, 0)),
            pl.BlockSpec((PAIR, TOK_BLK, width), lambda b, t: (b, jnp.maximum(t - 1, 0), 0))]


def _mod_row(mod_ref, h, tb):
    row = jnp.where(tb == 0, BATCH, pl.program_id(0) * PAIR + h)
    return mod_ref[pl.ds(row, 1), :]


def _inproj_kernel(*refs, two_src):
    tb = pl.program_id(1)
    if two_src:
        c_ref, x_ref, mod_ref, g_ref, w_ref, o_ref = refs
    else:
        x_ref, mod_ref, g_ref, w_ref, o_ref = refs
    ys = []
    for h in range(PAIR):
        x = jnp.where(tb == 0, c_ref[h], x_ref[h]) if two_src else x_ref[h]
        r = lax.rsqrt(jnp.mean(x * x, axis=-1, keepdims=True) + EPS)
        m = _mod_row(mod_ref, h, tb)
        y = (x * r) * g_ref[...]
        ys.append((y * (1.0 + m[:, D_MODEL:2 * D_MODEL]) + m[:, :D_MODEL]).astype(BF16))
    for h in range(PAIR):
        o_ref[h] = _dot(ys[h], w_ref[...]).astype(BF16)


def _inproj(src, mod, g, w_bf):
    n = w_bf.shape[1]
    two_src = isinstance(src, tuple)
    if two_src:
        src_specs = _cat_specs(D_MODEL)
    else:
        src_specs = [pl.BlockSpec((PAIR, TOK_BLK, D_MODEL), lambda b, t: (b, t, 0))]
        src = (src,)
    return pl.pallas_call(
        functools.partial(_inproj_kernel, two_src=two_src),
        grid=(BATCH // PAIR, LT // TOK_BLK),
        in_specs=src_specs + [
            pl.BlockSpec((8, 3 * D_MODEL), lambda b, t: (0, 0)),
            pl.BlockSpec((1, D_MODEL), lambda b, t: (0, 0)),
            pl.BlockSpec((D_MODEL, n), lambda b, t: (0, 0)),
        ],
        out_specs=pl.BlockSpec((PAIR, TOK_BLK, n), lambda b, t: (b, t, 0)),
        out_shape=jax.ShapeDtypeStruct((BATCH, LT, n), BF16),
        compiler_params=_cparams(("arbitrary", "arbitrary")),
        name="inproj",
    )(*src, mod, g.reshape(1, D_MODEL), w_bf)


def _fwd_blk(i):
    return i


def _rev_blk(i):
    return jnp.where(i < CTX_SCAN, CTX_SCAN - 1 - i, N_SCAN + CTX_SCAN - 1 - i)


def _scan_perms():
    t = SCAN_BLK
    pf = np.zeros((8 * t, BATCH * t), np.float32)
    pr = np.zeros((8 * t, BATCH * t), np.float32)
    for tt in range(t):
        for b in range(BATCH):
            pf[tt * 8 + b, b * t + tt] = 1.0
            pr[tt * 8 + 4 + b, b * t + (t - 1 - tt)] = 1.0
    return pf, pr


def _halo_perm():
    ph = np.zeros((32, 16 * 4 * BATCH), np.float32)
    for b in range(BATCH):
        ph[1 * 8 + b, 0 * 64 + b * 16 + 15] = 1.0
        ph[2 * 8 + b, 1 * 64 + b * 16 + 0] = 1.0
        ph[3 * 8 + b, 1 * 64 + b * 16 + 1] = 1.0
        ph[0 * 8 + 4 + b, 3 * 64 + b * 16 + 1] = 1.0
        ph[1 * 8 + 4 + b, 3 * 64 + b * 16 + 0] = 1.0
        ph[2 * 8 + 4 + b, 2 * 64 + b * 16 + 15] = 1.0
    return ph


def _conv_coef(conv_w):
    zero = jnp.zeros((1, LRU_WIDTH), conv_w.dtype)
    fwd = jnp.concatenate([zero, conv_w], axis=0)
    rev = jnp.concatenate([conv_w[::-1], zero], axis=0)
    return jnp.concatenate([jnp.broadcast_to(fwd[:, None], (5, 4, LRU_WIDTH)),
                            jnp.broadcast_to(rev[:, None], (5, 4, LRU_WIDTH))], axis=1)


def _s5_kernel(uf_ref, ur_ref, pf_ref, pr_ref, pft_ref, prt_ref, bc_ref, rep_ref, same_ref, lre_ref, lim_ref,
               yf_ref, yr_ref, buf0_ref, buf1_ref, st_ref, bcat_ref, ccat_ref):
    i = pl.program_id(0)

    @pl.when(i == 0)
    def _():
        st_ref[...] = jnp.zeros_like(st_ref)
        buf0_ref[...] = jnp.zeros_like(buf0_ref)
        buf1_ref[...] = jnp.zeros_like(buf1_ref)
        same = same_ref[...].astype(F32)
        for j in range(4):
            bcat_ref[j] = (_dot(rep_ref[...], bc_ref[0, j]) * same).astype(BF16)
            ccat_ref[j] = (_dot(rep_ref[...], bc_ref[1, j]) * same).T.astype(BF16)

    nrow = 8 * SCAN_BLK

    def step(buf_a, buf_b):
        vals = {}

        def perm():
            uf = uf_ref[...].reshape(BATCH * SCAN_BLK, S5_WIDTH)
            ur = ur_ref[...].reshape(BATCH * SCAN_BLK, S5_WIDTH)
            vals['u_f'] = _dot(pf_ref[...], uf).astype(BF16)
            vals['u_r'] = _dot(pr_ref[...], ur).astype(BF16)

        def readout(j):
            is_f = (lax.broadcasted_iota(jnp.int32, (nrow, 128), 0) & 7) < 4
            yj = _dot(buf_a[:, j * 1024:(j + 1) * 1024].astype(BF16), ccat_ref[j])
            vals['y%d' % j] = jnp.where(is_f, yj[:, :128], yj[:, 128:]).astype(BF16)

        def project(j):
            lhs = jnp.concatenate([vals['u_f'][:, j * 128:(j + 1) * 128], vals['u_r'][:, j * 128:(j + 1) * 128]],
                                  axis=1)
            buf_a[:, j * 1024:(j + 1) * 1024] = _dot(lhs, bcat_ref[j])

        def unperm(p_ref, o_ref):
            yp = jnp.concatenate([vals['y%d' % j] for j in range(4)], axis=1)
            o_ref[...] = _dot(p_ref[...], yp).astype(BF16).reshape(BATCH, SCAN_BLK, S5_WIDTH)

        mxu = [perm]
        for j in range(4):
            mxu += [functools.partial(readout, j), functools.partial(project, j)]
        mxu += [functools.partial(unperm, pft_ref, yf_ref), functools.partial(unperm, prt_ref, yr_ref)]

        def scan(j, t0):
            c_re = slice(j * 1024, j * 1024 + 512)
            c_im = slice(j * 1024 + 512, (j + 1) * 1024)
            lre = lre_ref[j]
            lim = lim_ref[j]
            hre = st_ref[:, c_re]
            him = st_ref[:, c_im]
            for t in range(t0, t0 + 16):
                rows = slice(t * 8, (t + 1) * 8)
                nre = lre * hre - lim * him + buf_b[rows, c_re]
                nim = lre * him + lim * hre + buf_b[rows, c_im]
                buf_b[rows, c_re] = nre
                buf_b[rows, c_im] = nim
                hre, him = nre, nim
            st_ref[:, c_re] = hre
            st_ref[:, c_im] = him

        vpu = [functools.partial(scan, j, t0) for j in range(4) for t0 in range(0, SCAN_BLK, 16)]

        per_piece = [1, 2, 1, 2, 1, 2, 1, 2, 1, 2, 1]
        for piece, n_scan in zip(mxu, per_piece):
            piece()
            for _ in range(n_scan):
                vpu.pop(0)()

    @pl.when(i % 2 == 0)
    def _():
        step(buf0_ref, buf1_ref)

    @pl.when(i % 2 == 1)
    def _():
        step(buf1_ref, buf0_ref)


def _s5_weights(lam_re, lam_im, log_dt, b_re, b_im, c_re, c_im):
    a = lam_re.astype(F32) * jnp.exp(log_dt.astype(F32))[..., None]
    b = lam_im.astype(F32) * jnp.exp(log_dt.astype(F32))[..., None]
    lbr = jnp.exp(a) * jnp.cos(b)
    lbi = jnp.exp(a) * jnp.sin(b)
    nr = jnp.expm1(a) * jnp.cos(b) - 2.0 * jnp.sin(0.5 * b) ** 2
    d2 = lam_re * lam_re + lam_im * lam_im
    qr = (nr * lam_re + lbi * lam_im) / d2
    qi = (lbi * lam_re - nr * lam_im) / d2
    bbr = qr[..., None] * b_re - qi[..., None] * b_im
    bbi = qr[..., None] * b_im + qi[..., None] * b_re
    bb = jnp.stack([bbr, bbi], axis=1).reshape(2, 2, 4, 8, S5_STATE, S5_GROUP)
    bb = bb.transpose(2, 0, 5, 1, 3, 4).reshape(4, 2 * S5_GROUP, 2 * 512)
    cc = jnp.stack([c_re.astype(F32), -c_im.astype(F32)], axis=1).reshape(2, 2, 4, 8, S5_GROUP, S5_STATE)
    cc = cc.transpose(2, 0, 4, 1, 3, 5).reshape(4, 2 * S5_GROUP, 2 * 512)
    lre8 = jnp.repeat(lbr.reshape(2, 4, 512).transpose(1, 0, 2), 4, axis=1)
    lim8 = jnp.repeat(lbi.reshape(2, 4, 512).transpose(1, 0, 2), 4, axis=1)
    return jnp.stack([bb, cc]).astype(BF16), lre8, lim8


def _s5_expanders():
    rows = np.arange(2 * 128)
    d, g, h = rows // 128, (rows // S5_GROUP) % 8, rows % S5_GROUP
    k = np.arange(2 * S5_GROUP)
    rep = ((d[:, None] == k[None, :] // S5_GROUP) & (h[:, None] == k[None, :] % S5_GROUP)).astype(np.float32)
    cols = np.arange(2 * 512)
    same = (g[:, None] == (cols[None, :] // S5_STATE) % 8).astype(np.float32)
    return rep, same


def _s5(p, pf, pr, pft, prt, bc, rep, same, lre8, lim8):
    nrow = 8 * SCAN_BLK
    const2 = lambda i: (0, 0)
    const3 = lambda i: (0, 0, 0)
    proj = lambda i: jnp.minimum(i, N_SCAN - 1)
    read = lambda i: jnp.clip(i - 2, 0, N_SCAN - 1)
    return pl.pallas_call(
        _s5_kernel,
        grid=(N_SCAN + 2,),
        in_specs=[
            pl.BlockSpec((BATCH, SCAN_BLK, S5_WIDTH), lambda i: (0, _fwd_blk(proj(i)), 0)),
            pl.BlockSpec((BATCH, SCAN_BLK, S5_WIDTH), lambda i: (0, _rev_blk(proj(i)), 0)),
            pl.BlockSpec(pf.shape, const2), pl.BlockSpec(pr.shape, const2),
            pl.BlockSpec(pft.shape, const2), pl.BlockSpec(prt.shape, const2),
            pl.BlockSpec(bc.shape, lambda i: (0, 0, 0, 0)), pl.BlockSpec(rep.shape, const2),
            pl.BlockSpec(same.shape, const2),
            pl.BlockSpec(lre8.shape, const3), pl.BlockSpec(lim8.shape, const3),
        ],
        out_specs=[
            pl.BlockSpec((BATCH, SCAN_BLK, S5_WIDTH), lambda i: (0, _fwd_blk(read(i)), 0)),
            pl.BlockSpec((BATCH, SCAN_BLK, S5_WIDTH), lambda i: (0, _rev_blk(read(i)), 0)),
        ],
        out_shape=[jax.ShapeDtypeStruct((BATCH, LT, S5_WIDTH), BF16)] * 2,
        scratch_shapes=[pltpu.VMEM((nrow, 4096), F32), pltpu.VMEM((nrow, 4096), F32),
                        pltpu.VMEM((8, 4096), F32),
                        pltpu.VMEM((4, 256, 1024), BF16), pltpu.VMEM((4, 1024, 256), BF16)],
        compiler_params=_cparams(("arbitrary",)),
        name="s5_scan",
    )(p, p, pf, pr, pft, prt, bc, rep, same, lre8, lim8)


def _na_bias_table(rel_bias):
    w = np.arange(GRID_W)
    cs = np.clip(w - NA_COLS // 2, 0, GRID_W - NA_COLS)
    cp = np.arange(GRID_W)
    valid = (cp[None, :] >= cs[:, None]) & (cp[None, :] < cs[:, None] + NA_COLS)
    dc = cp[None, :] - w[:, None] + (NA_COLS - 1)
    n_dc = 2 * NA_COLS - 1
    onehot = ((dc[None] == np.arange(n_dc)[:, None, None]) & valid[None]).astype(np.float32)
    n_dr = 2 * NA_ROWS - 1
    oh2 = np.zeros((2, n_dc, GRID_W, 2, GRID_W), np.float32)
    oh2[0, :, :, 0, :] = onehot
    oh2[1, :, :, 1, :] = onehot
    rb = rel_bias.astype(F32)
    rb2 = jnp.concatenate([rb[:, :n_dr - 1], rb[:, 1:]], axis=2).reshape(NA_HEADS * (n_dr - 1), 2 * n_dc)
    band2 = jnp.dot(rb2, jnp.asarray(oh2.reshape(2 * n_dc, 2 * GRID_W * GRID_W)), precision=HIGHEST)
    neg2 = np.tile(np.where(valid, 0.0, NEG).astype(np.float32), (1, 2))
    return band2.reshape(NA_HEADS, n_dr - 1, GRID_W, 2 * GRID_W) * LOG2E + jnp.asarray(neg2)


def _fill_bias_tables(band_ref, bias_ref):
    for off in range(8):
        for i2 in range(NA_ROWS // 2):
            bias_ref[off, :, :, i2 * 128:(i2 + 1) * 128] = band_ref[:, off + 2 * i2].reshape(NA_HEADS // 2, 128, 128)
    bias_ref[8] = jnp.full(bias_ref.shape[1:], NEG, F32)


NA_STEP_ROWS = TOK_BLK // GRID_W


def _na_kernel(q_ref, k_ref, v_ref, band_ref, o_ref, bias_ref):
    s_idx = pl.program_id(1)
    is_ctx = s_idx == 0

    @pl.when(jnp.logical_and(pl.program_id(0) == 0, s_idx == 0))
    def _():
        _fill_bias_tables(band_ref, bias_ref)

    lo = lax.broadcasted_iota(jnp.int32, (GRID_W, 128), 1) < HEAD_DIM
    nwin = NA_ROWS * GRID_W
    for rr in range(NA_STEP_ROWS):
        r = jnp.maximum(s_idx - 1, 0) * NA_STEP_ROWS + rr
        start = jnp.clip(r - NA_ROWS // 2, 0, SEQ // GRID_W - NA_ROWS)
        koff = pl.multiple_of(CTX_LEN + start * GRID_W, GRID_W)
        off = jnp.where(is_ctx, 8, start - r + NA_ROWS - 1)
        rows = slice(rr * GRID_W, (rr + 1) * GRID_W)
        pairs = range(NA_HEADS // 2)
        cols = [slice(hp * 128, (hp + 1) * 128) for hp in pairs]
        s_loc, s_ctx, m, p_loc, p_ctx, l, o = [], [], [], [], [], [], []
        for hp in pairs:
            qp = q_ref[0, rows, cols[hp]]
            zero = jnp.zeros_like(qp)
            q2 = jnp.concatenate([jnp.where(lo, qp, zero), jnp.where(lo, zero, qp)], axis=0)
            s_loc.append(_dot_nt(q2, k_ref[0, pl.ds(koff, nwin), cols[hp]]) + bias_ref[off, hp])
            s_ctx.append(_dot_nt(q2, k_ref[0, 0:CTX_LEN, cols[hp]]))
        for hp in pairs:
            m.append(jnp.maximum(jnp.max(s_loc[hp], axis=-1, keepdims=True),
                                 jnp.max(s_ctx[hp], axis=-1, keepdims=True)))
        for hp in pairs:
            pl_, pc_ = jnp.exp2(s_loc[hp] - m[hp]), jnp.exp2(s_ctx[hp] - m[hp])
            l.append(jnp.sum(pl_, axis=-1, keepdims=True) + jnp.sum(pc_, axis=-1, keepdims=True))
            p_loc.append(pl_.astype(BF16))
            p_ctx.append(pc_.astype(BF16))
        for hp in pairs:
            o.append(_dot(p_loc[hp], v_ref[0, pl.ds(koff, nwin), cols[hp]]) +
                     _dot(p_ctx[hp], v_ref[0, 0:CTX_LEN, cols[hp]]))
        for hp in pairs:
            oh = o[hp] / l[hp]
            o_ref[0, rows, cols[hp]] = jnp.where(lo, oh[:GRID_W], oh[GRID_W:]).astype(BF16)


def _na(p, band2):
    return pl.pallas_call(
        _na_kernel,
        grid=(BATCH, LT // TOK_BLK),
        in_specs=[
            pl.BlockSpec((1, TOK_BLK, 512), lambda b, s: (b, s, 2)),
            pl.BlockSpec((1, LT, 512), lambda b, s: (b, 0, 3)),
            pl.BlockSpec((1, LT, 512), lambda b, s: (b, 0, 4)),
            pl.BlockSpec(band2.shape, lambda b, s: (0, 0, 0, 0)),
        ],
        out_specs=pl.BlockSpec((1, TOK_BLK, 512), lambda b, s: (b, s, 0)),
        out_shape=jax.ShapeDtypeStruct((BATCH, LT, 512), BF16),
        scratch_shapes=[pltpu.VMEM((9, NA_HEADS // 2, 128, NA_ROWS * GRID_W), F32)],
        compiler_params=_cparams(("arbitrary", "arbitrary")),
        name="na_attn",
    )(p, p, p, band2)


def _gelu_tanh(x):
    return 0.5 * x * (1.0 + jnp.tanh(math.sqrt(2.0 / math.pi) * (x + 0.044715 * (x * x * x))))


def _silu(x):
    h = 0.5 * x
    return h + h * jnp.tanh(h)


def _finish(h, out, gt, pg):
    r = lax.rsqrt(jnp.mean(out * out, axis=-1, keepdims=True) + EPS)
    return h + (out * r) * (gt * pg)


def _even_out_kernel(yf_ref, yr_ref, u_ref, ga_ref, gb_ref, at_ref, hc_ref, hl_ref, mod_ref, d_ref, wg_ref,
                     bg_ref, wo_ref, pg_ref, o_ref):
    tb = pl.program_id(1)
    chains = range(PAIR)
    y = [_gelu_tanh(d_ref[...] * u_ref[h].astype(F32) + yf_ref[h].astype(F32) + yr_ref[h].astype(F32))
         for h in chains]
    glu = [_dot(y[h].astype(BF16), wg_ref[...]) for h in chains]
    y_a = [(y[h] * (0.5 + 0.5 * jnp.tanh(glu[h] + bg_ref[...])) * _silu(ga_ref[h].astype(F32))).astype(BF16)
           for h in chains]
    y_b = [(at_ref[h].astype(F32) * _silu(gb_ref[h].astype(F32))).astype(BF16) for h in chains]
    out = [_dot(y_a[h], wo_ref[0:512, :]) + _dot(y_b[h], wo_ref[512:1024, :]) for h in chains]
    for h in chains:
        res = jnp.where(tb == 0, hc_ref[h], hl_ref[h])
        o_ref[h] = _finish(res, out[h], _mod_row(mod_ref, h, tb)[:, 2 * D_MODEL:], pg_ref[...])


def _even_out(yf, yr, p, attn, ctx, x, mod, d_skip, w_glu, b_glu, w_out, post_g):
    tok = lambda c: pl.BlockSpec((PAIR, TOK_BLK, 512), lambda b, t: (b, t, c))
    full = lambda a: pl.BlockSpec(a.shape, lambda b, t: (0,) * a.ndim)
    d2, bg2, pg2 = d_skip.reshape(1, 512), b_glu.reshape(1, 512), post_g.reshape(1, D_MODEL)
    return pl.pallas_call(
        _even_out_kernel,
        grid=(BATCH // PAIR, LT // TOK_BLK),
        in_specs=[tok(0), tok(0), tok(0), tok(1), tok(5), tok(0)] + _cat_specs(D_MODEL) +
                 [full(mod), full(d2), full(w_glu), full(bg2), full(w_out), full(pg2)],
        out_specs=pl.BlockSpec((PAIR, TOK_BLK, D_MODEL), lambda b, t: (b, t, 0)),
        out_shape=jax.ShapeDtypeStruct((BATCH, LT, D_MODEL), F32),
        compiler_params=_cparams(("arbitrary", "arbitrary")),
        name="even_out",
    )(yf, yr, p, p, p, attn, ctx, x, mod, d2, w_glu, bg2, w_out, pg2)


def _softplus(z):
    return jnp.maximum(z, 0.0) + jnp.log1p(jnp.exp(-jnp.abs(z)))


def _lru_kernel(xf_ref, xfp_ref, xfn_ref, xr_ref, xrp_ref, xrn_ref, pf_ref, pr_ref, ph_ref, pft_ref, prt_ref,
                coef_ref, cb_ref, wg_ref, bg_ref, lam_ref, hf_ref, hr_ref, a0_ref, b0_ref, a1_ref, b1_ref, st_ref):
    i = pl.program_id(0)

    @pl.when(i == 0)
    def _():
        for ref in (st_ref, a0_ref, b0_ref, a1_ref, b1_ref):
            ref[...] = jnp.zeros_like(ref)

    nrow = 8 * SCAN_BLK
    half = nrow // 2
    blk = jnp.minimum(i, N_SCAN - 1)

    def halo(prev_ref, next_ref, b):
        prev_ok = jnp.logical_and(b != 0, b != CTX_SCAN)
        next_ok = jnp.logical_and(b != CTX_SCAN - 1, b != N_SCAN - 1)
        pv = prev_ref[...].reshape(BATCH * 16, LRU_WIDTH)
        nx = next_ref[...].reshape(BATCH * 16, LRU_WIDTH)
        return [jnp.where(prev_ok, pv, jnp.zeros_like(pv)), jnp.where(next_ok, nx, jnp.zeros_like(nx))]

    def step(a_w, b_w, a_s, b_s):
        vals = {}

        def unperm():
            hs = b_w[...].astype(BF16)
            hf_ref[...] = _dot(pft_ref[...], hs).astype(BF16).reshape(BATCH, SCAN_BLK, LRU_WIDTH)
            hr_ref[...] = _dot(prt_ref[...], hs).astype(BF16).reshape(BATCH, SCAN_BLK, LRU_WIDTH)

        def perm():
            hal = jnp.concatenate(halo(xfp_ref, xfn_ref, _fwd_blk(blk)) + halo(xrp_ref, xrn_ref, _rev_blk(blk)),
                                  axis=0)
            xh = _dot(ph_ref[...], hal)
            xp = (_dot(pf_ref[...], xf_ref[...].reshape(BATCH * SCAN_BLK, LRU_WIDTH)) +
                  _dot(pr_ref[...], xr_ref[...].reshape(BATCH * SCAN_BLK, LRU_WIDTH)))
            vals['x_ext'] = jnp.concatenate([xh[:16], xp, xh[16:]], axis=0)

        def conv(h):
            xc = jnp.zeros((SCAN_BLK // 2, 8, LRU_WIDTH), F32) + cb_ref[...]
            for s in range(5):
                r0 = h * half + s * 8
                xc = xc + vals['x_ext'][r0:r0 + half].reshape(SCAN_BLK // 2, 8, LRU_WIDTH) * coef_ref[s]
            vals['xc%d' % h] = xc.reshape(half, LRU_WIDTH)

        def gates(h):
            is_f = (lax.broadcasted_iota(jnp.int32, (half, 1), 0) & 7) < 4
            xcb = vals['xc%d' % h].astype(BF16)
            zero = jnp.zeros_like(xcb[:, :256])
            pre_r, pre_i = [], []
            for c in range(2):
                xt = xcb[:, c * 256:(c + 1) * 256]
                lhs = jnp.concatenate([jnp.where(is_f, xt, zero), jnp.where(is_f, zero, xt)], axis=1)
                pre = _dot(lhs, wg_ref[c])
                pre_r.append(pre[:, :256])
                pre_i.append(pre[:, 256:])
            vals['pr%d' % h] = jnp.concatenate(pre_r, axis=1)
            vals['pi%d' % h] = jnp.concatenate(pre_i, axis=1)

        def elem(h):
            rows = slice(h * half, (h + 1) * half)
            shp = (SCAN_BLK // 2, 8, LRU_WIDTH)
            t_r = jnp.tanh(vals['pr%d' % h].reshape(shp) + bg_ref[:, :LRU_WIDTH])
            t_i = jnp.tanh(vals['pi%d' % h].reshape(shp) + bg_ref[:, LRU_WIDTH:])
            c = (-0.5 * LRU_C) * _softplus(-lam_ref[...])
            u = 1.0 + t_r
            a = jnp.exp2((c * LOG2E) * u)
            a_w[rows, :] = a.reshape(half, LRU_WIDTH)
            one_m_a2 = jnp.maximum(jnp.tanh((-c) * u) * (a * a + 1.0), 1e-37)
            mult = one_m_a2 * lax.rsqrt(one_m_a2)
            b = (mult * vals['xc%d' % h].reshape(shp)) * (0.5 + 0.5 * t_i)
            b_w[rows, :] = b.reshape(half, LRU_WIDTH)

        def scan(t0):
            h = st_ref[...]
            for t in range(t0, t0 + 16):
                rows = slice(t * 8, (t + 1) * 8)
                h = a_s[rows, :] * h + b_s[rows, :]
                b_s[rows, :] = h
            st_ref[...] = h

        for piece in (unperm, functools.partial(scan, 0), perm, functools.partial(conv, 0),
                      functools.partial(conv, 1), functools.partial(scan, 16), functools.partial(gates, 0),
                      functools.partial(gates, 1), functools.partial(scan, 32), functools.partial(elem, 0),
                      functools.partial(elem, 1), functools.partial(scan, 48)):
            piece()

    @pl.when(i % 2 == 0)
    def _():
        step(a0_ref, b0_ref, a1_ref, b1_ref)

    @pl.when(i % 2 == 1)
    def _():
        step(a1_ref, b1_ref, a0_ref, b0_ref)


def _lru_gate_weights(w_a, w_x):
    w = jnp.stack([w_a, w_x], axis=1).reshape(2, 2, 2, 4, LRU_BLOCK, LRU_BLOCK)
    same = jnp.asarray(np.eye(4, dtype=np.float32))
    w = w.transpose(2, 0, 3, 4, 1, 5)[:, :, :, :, :, None] * same[:, None, None, :, None]
    return (0.5 * w).reshape(2, 512, 512).astype(BF16)


def _lru(p, pf, pr, ph, pft, prt, coef, conv_b, wg, bg, lam):
    nrow = 8 * SCAN_BLK
    full = lambda a: pl.BlockSpec(a.shape, lambda i: (0,) * a.ndim)
    proj = lambda i: jnp.minimum(i, N_SCAN - 1)
    read = lambda i: jnp.clip(i - 2, 0, N_SCAN - 1)
    cur = lambda f: pl.BlockSpec((BATCH, SCAN_BLK, LRU_WIDTH), lambda i: (0, f(proj(i)), 0))
    prev = lambda f: pl.BlockSpec((BATCH, 16, LRU_WIDTH), lambda i: (0, jnp.maximum(f(proj(i)) * 4 - 1, 0), 0))
    nxt = lambda f: pl.BlockSpec((BATCH, 16, LRU_WIDTH),
                                 lambda i: (0, jnp.minimum(f(proj(i)) * 4 + 4, LT // 16 - 1), 0))
    cb2 = conv_b.reshape(1, LRU_WIDTH)
    return pl.pallas_call(
        _lru_kernel,
        grid=(N_SCAN + 2,),
        in_specs=[cur(_fwd_blk), prev(_fwd_blk), nxt(_fwd_blk), cur(_rev_blk), prev(_rev_blk), nxt(_rev_blk),
                  full(pf), full(pr), full(ph), full(pft), full(prt), full(coef), full(cb2), full(wg), full(bg),
                  full(lam)],
        out_specs=[
            pl.BlockSpec((BATCH, SCAN_BLK, LRU_WIDTH), lambda i: (0, _fwd_blk(read(i)), 0)),
            pl.BlockSpec((BATCH, SCAN_BLK, LRU_WIDTH), lambda i: (0, _rev_blk(read(i)), 0)),
        ],
        out_shape=[jax.ShapeDtypeStruct((BATCH, LT, LRU_WIDTH), BF16)] * 2,
        scratch_shapes=[pltpu.VMEM((nrow, LRU_WIDTH), F32)] * 4 + [pltpu.VMEM((8, LRU_WIDTH), F32)],
        compiler_params=_cparams(("arbitrary",)),
        name="lru_scan",
    )(p, p, p, p, p, p, pf, pr, ph, pft, prt, coef, cb2, wg, bg, lam)


def _norm_rope(xb, g, grot, cos, sin, ones_blk, swap):
    x = xb.astype(F32)
    xrot = _dot(xb, swap)
    ms = _dot((x * x).astype(BF16), ones_blk) * (1.0 / HEAD_DIM)
    rs = lax.rsqrt(ms + EPS)
    return rs * ((x * g) * cos + (xrot * grot) * sin)


def _gqa_kernel(q_ref, k_ref, v_ref, cq_ref, sq_ref, ck_ref, sk_ref, g_ref, ones_ref, swap_ref,
                o_ref, kn_ref, vlo_ref, vhi_ref):
    qb = pl.program_id(1)
    ones_blk = ones_ref[...]
    swap = swap_ref[...]

    @pl.when(qb == 0)
    def _():
        kn = _norm_rope(k_ref[0], g_ref[2:3, :], g_ref[3:4, :], ck_ref[...], sk_ref[...], ones_blk, swap)
        kn_ref[...] = kn.astype(BF16)
        v = v_ref[0]
        lo_k = lax.broadcasted_iota(jnp.int32, (LT, 128), 1) < HEAD_DIM
        one = jnp.ones_like(v)
        vlo_ref[...] = jnp.where(lo_k, v, one)
        vhi_ref[...] = jnp.where(lo_k, one, v)

    lo = lax.broadcasted_iota(jnp.int32, (Q_BLK, 128), 1) < HEAD_DIM
    kn = kn_ref[...]
    cos = cq_ref[...]
    sin = sq_ref[...]
    n_pb = GQA_HEADS // 2

    def scores(pb):
        cols = slice(pb * 128, (pb + 1) * 128)
        qn = _norm_rope(q_ref[0, :, cols], g_ref[0:1, :], g_ref[1:2, :], cos, sin, ones_blk, swap)
        qn = (qn * (HEAD_DIM ** -0.5 * LOG2E)).astype(BF16)
        zero = jnp.zeros_like(qn)
        qs = jnp.concatenate([jnp.where(lo, qn, zero), jnp.where(lo, zero, qn)], axis=0)
        return _dot_nt(qs, kn)

    s_next = scores(0)
    for pb in range(n_pb):
        cols = slice(pb * 128, (pb + 1) * 128)
        s = s_next
        if pb + 1 < n_pb:
            s_next = scores(pb + 1)
        m = jnp.max(s, axis=-1, keepdims=True)
        p = jnp.exp2(s - m).astype(BF16)
        o_lo = _dot(p[:Q_BLK], vlo_ref[...])
        o_hi = _dot(p[Q_BLK:], vhi_ref[...])
        num = jnp.where(lo, o_lo, o_hi)
        den = pltpu.roll(jnp.where(lo, o_hi, o_lo), HEAD_DIM, axis=1)
        o_ref[0, :, cols] = (num / den).astype(BF16)


def _gqa(p, cos128, sin128, gvec, ones_blk, swap):
    nqb = SEQ // Q_BLK
    cb = CTX_LEN // Q_BLK
    return pl.pallas_call(
        _gqa_kernel,
        grid=(BATCH, nqb),
        in_specs=[
            pl.BlockSpec((1, Q_BLK, 512), lambda b, q: (b, q + cb, 2)),
            pl.BlockSpec((1, LT, 128), lambda b, q: (b, 0, 16)),
            pl.BlockSpec((1, LT, 128), lambda b, q: (b, 0, 17)),
            pl.BlockSpec((Q_BLK, 128), lambda b, q: (q + cb, 0)),
            pl.BlockSpec((Q_BLK, 128), lambda b, q: (q + cb, 0)),
            pl.BlockSpec((LT, 128), lambda b, q: (0, 0)),
            pl.BlockSpec((LT, 128), lambda b, q: (0, 0)),
            pl.BlockSpec((8, 128), lambda b, q: (0, 0)),
            pl.BlockSpec((128, 128), lambda b, q: (0, 0)),
            pl.BlockSpec((128, 128), lambda b, q: (0, 0)),
        ],
        out_specs=pl.BlockSpec((1, Q_BLK, 512), lambda b, q: (b, q, 0)),
        out_shape=jax.ShapeDtypeStruct((BATCH, SEQ, 512), BF16),
        scratch_shapes=[pltpu.VMEM((LT, 128), BF16)] * 3,
        compiler_params=_cparams(("arbitrary", "arbitrary")),
        name="gqa_attn",
    )(p, p, p, cos128, sin128, cos128, sin128, gvec, ones_blk, swap)


def _odd_out_kernel(hf_ref, hr_ref, gc_ref, gd_ref, at_ref, h_ref, mod_ref, wo_ref, pg_ref, o_ref):
    chains = range(PAIR)
    y_c = [((hf_ref[h].astype(F32) + hr_ref[h].astype(F32)) * _silu(gc_ref[h].astype(F32))).astype(BF16)
           for h in chains]
    y_d = [(at_ref[h].astype(F32) * _silu(gd_ref[h].astype(F32))).astype(BF16) for h in chains]
    out = [_dot(y_c[h], wo_ref[0:512, :]) + _dot(y_d[h], wo_ref[512:1024, :]) for h in chains]
    for h in chains:
        gt = mod_ref[pl.ds(pl.program_id(0) * PAIR + h, 1), 2 * D_MODEL:]
        o_ref[h] = _finish(h_ref[h], out[h], gt, pg_ref[...])


def _odd_out(hf, hr, p, attn, h_cat, mod, w_out, post_g):
    cb = CTX_LEN // TOK_BLK
    cat = lambda c: pl.BlockSpec((PAIR, TOK_BLK, 512), lambda b, t: (b, t + cb, c))
    full = lambda a: pl.BlockSpec(a.shape, lambda b, t: (0,) * a.ndim)
    pg2 = post_g.reshape(1, D_MODEL)
    return pl.pallas_call(
        _odd_out_kernel,
        grid=(BATCH // PAIR, SEQ // TOK_BLK),
        in_specs=[cat(0), cat(0), cat(1), cat(3),
                  pl.BlockSpec((PAIR, TOK_BLK, 512), lambda b, t: (b, t, 0)),
                  pl.BlockSpec((PAIR, TOK_BLK, D_MODEL), lambda b, t: (b, t + cb, 0)),
                  full(mod), full(w_out), full(pg2)],
        out_specs=pl.BlockSpec((PAIR, TOK_BLK, D_MODEL), lambda b, t: (b, t, 0)),
        out_shape=jax.ShapeDtypeStruct((BATCH, SEQ, D_MODEL), F32),
        compiler_params=_cparams(("arbitrary", "arbitrary")),
        name="odd_out",
    )(hf, hr, p, p, attn, h_cat, mod, w_out, pg2)


def _pair_swap_matrix(n):
    r = np.zeros((n, n), np.float32)
    r[np.arange(1, n, 2), np.arange(0, n, 2)] = -1.0
    r[np.arange(0, n, 2), np.arange(1, n, 2)] = 1.0
    return r


def _interleave_kv_groups(w, axis):
    if axis == 0:
        return w.reshape(2, 4, HEAD_DIM, w.shape[1]).transpose(1, 0, 2, 3).reshape(w.shape)
    return w.reshape(w.shape[0], 2, 4, HEAD_DIM).transpose(0, 2, 1, 3).reshape(w.shape)


def _odd_w_in(w):
    w = w.astype(BF16)
    x, gc = w[:, 0:512], w[:, 512:1024]
    q = _interleave_kv_groups(w[:, 1024:1536], 1)
    k, v = w[:, 1536:1664], w[:, 1664:1792]
    gd = _interleave_kv_groups(w[:, 1792:2304], 1)
    return jnp.concatenate([x, gc, q, gd, k, v], axis=1)


def _rope_tables():
    t = np.arange(SEQ)
    row = (t // GRID_W).astype(np.float32)
    col = (t % GRID_W).astype(np.float32)
    half = HEAD_DIM // 2
    inv = (ROPE_THETA ** (-np.arange(0, half, 2, dtype=np.float32) / half)).astype(np.float32)
    ang = np.concatenate([row[:, None] * inv, col[:, None] * inv], axis=-1)
    cos = np.repeat(np.cos(ang), 2, axis=-1)
    sin = np.repeat(np.sin(ang), 2, axis=-1)
    cos = np.concatenate([np.ones((CTX_LEN, HEAD_DIM), np.float32), cos], axis=0)
    sin = np.concatenate([np.zeros((CTX_LEN, HEAD_DIM), np.float32), sin], axis=0)
    return np.tile(cos, (1, 2)).astype(np.float32), np.tile(sin, (1, 2)).astype(np.float32)


def _swap_pairs_vec(g):
    g2 = g.reshape(-1, 2)
    return jnp.stack([g2[:, 1], g2[:, 0]], axis=-1).reshape(g.shape)


def kernel(x, c, ctx, c_ctx, ada_w, ada_b, pre_g, post_g, ev_w_in, ev_w_out, s5_lam_re, s5_lam_im, s5_log_dt,
           s5_b_re, s5_b_im, s5_c_re, s5_c_im, s5_d, s5_w_glu, s5_b_glu, na_rel_bias, od_w_in, od_w_out,
           lru_conv_w, lru_conv_b, lru_lam, lru_w_a, lru_b_a, lru_w_x, lru_b_x, gqa_q_norm, gqa_k_norm):
    pf_np, pr_np = _scan_perms()
    pf, pr = jnp.asarray(pf_np, BF16), jnp.asarray(pr_np, BF16)
    pft, prt = jnp.asarray(pf_np.T, BF16), jnp.asarray(pr_np.T, BF16)
    ph = jnp.asarray(_halo_perm(), BF16)

    c8 = jnp.concatenate([c, c_ctx[None], jnp.zeros((3, D_MODEL), F32)], axis=0)
    mod = _adaln(c8, ada_w, ada_b)

    col_scale = np.ones((EVEN_IN,), np.float32)
    col_scale[1024:1536] = HEAD_DIM ** -0.5 * LOG2E
    p0 = _inproj((ctx, x), mod[0], pre_g[0], (ev_w_in[0] * jnp.asarray(col_scale)).astype(BF16))
    bc, lre8, lim8 = _s5_weights(s5_lam_re[0], s5_lam_im[0], s5_log_dt[0], s5_b_re[0], s5_b_im[0],
                                 s5_c_re[0], s5_c_im[0])
    rep_np, same_np = _s5_expanders()
    yf, yr = _s5(p0, pf, pr, pft, prt, bc, jnp.asarray(rep_np, BF16), jnp.asarray(same_np, BF16), lre8, lim8)
    attn0 = _na(p0, _na_bias_table(na_rel_bias[0]))
    h1 = _even_out(yf, yr, p0, attn0, ctx, x, mod[0], s5_d[0], (0.5 * s5_w_glu[0]).astype(BF16), 0.5 * s5_b_glu[0],
                   ev_w_out[0].astype(BF16), post_g[0])

    p1 = _inproj(h1, mod[1], pre_g[1], _odd_w_in(od_w_in[0]))
    wg = _lru_gate_weights(lru_w_a[0], lru_w_x[0])
    bg = jnp.repeat(0.5 * jnp.concatenate([lru_b_a[0], lru_b_x[0]], axis=1), 4, axis=0)
    lam8 = jnp.repeat(lru_lam[0], 4, axis=0)
    hf, hr = _lru(p1, pf, pr, ph, pft, prt, _conv_coef(lru_conv_w[0]), lru_conv_b[0], wg, bg, lam8)
    cos_np, sin_np = _rope_tables()
    gq, gk = gqa_q_norm[0], gqa_k_norm[0]
    gvec = jnp.stack([jnp.tile(gq, 2), jnp.tile(_swap_pairs_vec(gq), 2),
                      jnp.tile(gk, 2), jnp.tile(_swap_pairs_vec(gk), 2)] + [jnp.zeros((128,), F32)] * 4)
    ones_np = np.kron(np.eye(2, dtype=np.float32), np.ones((HEAD_DIM, HEAD_DIM), np.float32))
    attn1 = _gqa(p1, jnp.asarray(cos_np), jnp.asarray(sin_np), gvec, jnp.asarray(ones_np, BF16),
                 jnp.asarray(_pair_swap_matrix(128), BF16))
    w_out1 = od_w_out[0].astype(BF16)
    w_out1 = jnp.concatenate([w_out1[:512], _interleave_kv_groups(w_out1[512:], 0)], axis=0)
    return _odd_out(hf, hr, p1, attn1, h1, mod[1], w_out1, post_g[1])
```

```python
import functools
import math

import numpy as np
import jax
import jax.numpy as jnp
from jax import lax
from jax.experimental import pallas as pl
from jax.experimental.pallas import tpu as pltpu

F32 = jnp.float32
BF16 = jnp.bfloat16
HIGHEST = lax.Precision.HIGHEST

D_MODEL = 1024
BATCH = 4
SEQ = 4096
GRID_W = 64
CTX_LEN = 256
LT = CTX_LEN + SEQ
HEAD_DIM = 64
EPS = 1e-6
S5_WIDTH = 512
S5_GROUP = 16
S5_GROUPS = 32
S5_STATE = 64
NA_HEADS = 8
NA_ROWS = 8
NA_COLS = 16
LRU_WIDTH = 512
LRU_BLOCKS = 8
LRU_BLOCK = 64
LRU_C = 8.0
GQA_HEADS = 8
ROPE_THETA = 10000.0
EVEN_IN = 3072
ODD_IN = 2304

TOK_BLK = 256
SCAN_BLK = 64
N_SCAN = LT // SCAN_BLK
CTX_SCAN = CTX_LEN // SCAN_BLK
Q_BLK = 256
NEG = -1e30
LOG2E = math.log2(math.e)
VMEM_LIMIT = 56 * 1024 * 1024


def _cparams(sem):
    return pltpu.CompilerParams(dimension_semantics=sem, vmem_limit_bytes=VMEM_LIMIT)


def _dot(a, b):
    return jnp.dot(a, b, preferred_element_type=F32)


def _dot_nt(a, b):
    return lax.dot_general(a, b, (((1,), (1,)), ((), ())), preferred_element_type=F32)


def _adaln_kernel(c_ref, w_ref, b_ref, o_ref):
    c = c_ref[...]
    s = c * jax.nn.sigmoid(c)
    o_ref[0] = jnp.dot(s, w_ref[0], preferred_element_type=F32, precision=HIGHEST) + b_ref[0]


def _adaln(c8, ada_w, ada_b):
    depth = ada_w.shape[0]
    nb = 3 * D_MODEL // 1024
    return pl.pallas_call(
        _adaln_kernel,
        grid=(depth, nb),
        in_specs=[
            pl.BlockSpec((8, D_MODEL), lambda i, n: (0, 0)),
            pl.BlockSpec((1, D_MODEL, 1024), lambda i, n: (i, 0, n)),
            pl.BlockSpec((1, 1, 1024), lambda i, n: (i, 0, n)),
        ],
        out_specs=pl.BlockSpec((1, 8, 1024), lambda i, n: (i, 0, n)),
        out_shape=jax.ShapeDtypeStruct((depth, 8, 3 * D_MODEL), F32),
        compiler_params=_cparams(("arbitrary", "arbitrary")),
        name="adaln",
    )(c8, ada_w, ada_b.reshape(depth, 1, 3 * D_MODEL))


PAIR = 2


def _cat_specs(width):
    return [pl.BlockSpec((PAIR, TOK_BLK, width), lambda b, t: (b, 0, 0)),
            pl.BlockSpec((PAIR, TOK_BLK, width), lambda b, t: (b, jnp.maximum(t - 1, 0), 0))]


def _mod_row(mod_ref, h, tb):
    row = jnp.where(tb == 0, BATCH, pl.program_id(0) * PAIR + h)
    return mod_ref[pl.ds(row, 1), :]


def _inproj_kernel(*refs, two_src):
    tb = pl.program_id(1)
    if two_src:
        c_ref, x_ref, mod_ref, g_ref, w_ref, o_ref = refs
    else:
        x_ref, mod_ref, g_ref, w_ref, o_ref = refs
    ys = []
    for h in range(PAIR):
        x = jnp.where(tb == 0, c_ref[h], x_ref[h]) if two_src else x_ref[h]
        r = lax.rsqrt(jnp.mean(x * x, axis=-1, keepdims=True) + EPS)
        m = _mod_row(mod_ref, h, tb)
        y = (x * r) * g_ref[...]
        ys.append((y * (1.0 + m[:, D_MODEL:2 * D_MODEL]) + m[:, :D_MODEL]).astype(BF16))
    for h in range(PAIR):
        o_ref[h] = _dot(ys[h], w_ref[...]).astype(BF16)


def _inproj(src, mod, g, w_bf):
    n = w_bf.shape[1]
    two_src = isinstance(src, tuple)
    if two_src:
        src_specs = _cat_specs(D_MODEL)
    else:
        src_specs = [pl.BlockSpec((PAIR, TOK_BLK, D_MODEL), lambda b, t: (b, t, 0))]
        src = (src,)
    return pl.pallas_call(
        functools.partial(_inproj_kernel, two_src=two_src),
        grid=(BATCH // PAIR, LT // TOK_BLK),
        in_specs=src_specs + [
            pl.BlockSpec((8, 3 * D_MODEL), lambda b, t: (0, 0)),
            pl.BlockSpec((1, D_MODEL), lambda b, t: (0, 0)),
            pl.BlockSpec((D_MODEL, n), lambda b, t: (0, 0)),
        ],
        out_specs=pl.BlockSpec((PAIR, TOK_BLK, n), lambda b, t: (b, t, 0)),
        out_shape=jax.ShapeDtypeStruct((BATCH, LT, n), BF16),
        compiler_params=_cparams(("arbitrary", "arbitrary")),
        name="inproj",
    )(*src, mod, g.reshape(1, D_MODEL), w_bf)


def _fwd_blk(i):
    return i


def _rev_blk(i):
    return jnp.where(i < CTX_SCAN, CTX_SCAN - 1 - i, N_SCAN + CTX_SCAN - 1 - i)


def _scan_perms():
    t = SCAN_BLK
    pf = np.zeros((8 * t, BATCH * t), np.float32)
    pr = np.zeros((8 * t, BATCH * t), np.float32)
    for tt in range(t):
        for b in range(BATCH):
            pf[tt * 8 + b, b * t + tt] = 1.0
            pr[tt * 8 + 4 + b, b * t + (t - 1 - tt)] = 1.0
    return pf, pr


def _halo_perm():
    ph = np.zeros((32, 16 * 4 * BATCH), np.float32)
    for b in range(BATCH):
        ph[1 * 8 + b, 0 * 64 + b * 16 + 15] = 1.0
        ph[2 * 8 + b, 1 * 64 + b * 16 + 0] = 1.0
        ph[3 * 8 + b, 1 * 64 + b * 16 + 1] = 1.0
        ph[0 * 8 + 4 + b, 3 * 64 + b * 16 + 1] = 1.0
        ph[1 * 8 + 4 + b, 3 * 64 + b * 16 + 0] = 1.0
        ph[2 * 8 + 4 + b, 2 * 64 + b * 16 + 15] = 1.0
    return ph


def _conv_coef(conv_w):
    zero = jnp.zeros((1, LRU_WIDTH), conv_w.dtype)
    fwd = jnp.concatenate([zero, conv_w], axis=0)
    rev = jnp.concatenate([conv_w[::-1], zero], axis=0)
    return jnp.concatenate([jnp.broadcast_to(fwd[:, None], (5, 4, LRU_WIDTH)),
                            jnp.broadcast_to(rev[:, None], (5, 4, LRU_WIDTH))], axis=1)


def _s5_kernel(uf_ref, ur_ref, pf_ref, pr_ref, pft_ref, prt_ref, bc_ref, rep_ref, same_ref, lre_ref, lim_ref,
               yf_ref, yr_ref, buf0_ref, buf1_ref, st_ref, bcat_ref, ccat_ref):
    i = pl.program_id(0)

    @pl.when(i == 0)
    def _():
        st_ref[...] = jnp.zeros_like(st_ref)
        buf0_ref[...] = jnp.zeros_like(buf0_ref)
        buf1_ref[...] = jnp.zeros_like(buf1_ref)
        same = same_ref[...].astype(F32)
        for j in range(4):
            bcat_ref[j] = (_dot(rep_ref[...], bc_ref[0, j]) * same).astype(BF16)
            ccat_ref[j] = (_dot(rep_ref[...], bc_ref[1, j]) * same).T.astype(BF16)

    nrow = 8 * SCAN_BLK

    def step(buf_a, buf_b):
        vals = {}

        def perm():
            uf = uf_ref[...].reshape(BATCH * SCAN_BLK, S5_WIDTH)
            ur = ur_ref[...].reshape(BATCH * SCAN_BLK, S5_WIDTH)
            vals['u_f'] = _dot(pf_ref[...], uf).astype(BF16)
            vals['u_r'] = _dot(pr_ref[...], ur).astype(BF16)

        def readout(j):
            is_f = (lax.broadcasted_iota(jnp.int32, (nrow, 128), 0) & 7) < 4
            yj = _dot(buf_a[:, j * 1024:(j + 1) * 1024].astype(BF16), ccat_ref[j])
            vals['y%d' % j] = jnp.where(is_f, yj[:, :128], yj[:, 128:]).astype(BF16)

        def project(j):
            lhs = jnp.concatenate([vals['u_f'][:, j * 128:(j + 1) * 128], vals['u_r'][:, j * 128:(j + 1) * 128]],
                                  axis=1)
            buf_a[:, j * 1024:(j + 1) * 1024] = _dot(lhs, bcat_ref[j])

        def unperm(p_ref, o_ref):
            yp = jnp.concatenate([vals['y%d' % j] for j in range(4)], axis=1)
            o_ref[...] = _dot(p_ref[...], yp).astype(BF16).reshape(BATCH, SCAN_BLK, S5_WIDTH)

        mxu = [perm]
        for j in range(4):
            mxu += [functools.partial(readout, j), functools.partial(project, j)]
        mxu += [functools.partial(unperm, pft_ref, yf_ref), functools.partial(unperm, prt_ref, yr_ref)]

        def scan(j, t0):
            c_re = slice(j * 1024, j * 1024 + 512)
            c_im = slice(j * 1024 + 512, (j + 1) * 1024)
            lre = lre_ref[j]
            lim = lim_ref[j]
            hre = st_ref[:, c_re]
            him = st_ref[:, c_im]
            for t in range(t0, t0 + 16):
                rows = slice(t * 8, (t + 1) * 8)
                nre = lre * hre - lim * him + buf_b[rows, c_re]
                nim = lre * him + lim * hre + buf_b[rows, c_im]
                buf_b[rows, c_re] = nre
                buf_b[rows, c_im] = nim
                hre, him = nre, nim
            st_ref[:, c_re] = hre
            st_ref[:, c_im] = him

        vpu = [functools.partial(scan, j, t0) for j in range(4) for t0 in range(0, SCAN_BLK, 16)]

        per_piece = [1, 2, 1, 2, 1, 2, 1, 2, 1, 2, 1]
        for piece, n_scan in zip(mxu, per_piece):
            piece()
            for _ in range(n_scan):
                vpu.pop(0)()

    @pl.when(i % 2 == 0)
    def _():
        step(buf0_ref, buf1_ref)

    @pl.when(i % 2 == 1)
    def _():
        step(buf1_ref, buf0_ref)


def _s5_weights(lam_re, lam_im, log_dt, b_re, b_im, c_re, c_im):
    a = lam_re.astype(F32) * jnp.exp(log_dt.astype(F32))[..., None]
    b = lam_im.astype(F32) * jnp.exp(log_dt.astype(F32))[..., None]
    lbr = jnp.exp(a) * jnp.cos(b)
    lbi = jnp.exp(a) * jnp.sin(b)
    nr = jnp.expm1(a) * jnp.cos(b) - 2.0 * jnp.sin(0.5 * b) ** 2
    d2 = lam_re * lam_re + lam_im * lam_im
    qr = (nr * lam_re + lbi * lam_im) / d2
    qi = (lbi * lam_re - nr * lam_im) / d2
    bbr = qr[..., None] * b_re - qi[..., None] * b_im
    bbi = qr[..., None] * b_im + qi[..., None] * b_re
    bb = jnp.stack([bbr, bbi], axis=1).reshape(2, 2, 4, 8, S5_STATE, S5_GROUP)
    bb = bb.transpose(2, 0, 5, 1, 3, 4).reshape(4, 2 * S5_GROUP, 2 * 512)
    cc = jnp.stack([c_re.astype(F32), -c_im.astype(F32)], axis=1).reshape(2, 2, 4, 8, S5_GROUP, S5_STATE)
    cc = cc.transpose(2, 0, 4, 1, 3, 5).reshape(4, 2 * S5_GROUP, 2 * 512)
    lre8 = jnp.repeat(lbr.reshape(2, 4, 512).transpose(1, 0, 2), 4, axis=1)
    lim8 = jnp.repeat(lbi.reshape(2, 4, 512).transpose(1, 0, 2), 4, axis=1)
    return jnp.stack([bb, cc]).astype(BF16), lre8, lim8


def _s5_expanders():
    rows = np.arange(2 * 128)
    d, g, h = rows // 128, (rows // S5_GROUP) % 8, rows % S5_GROUP
    k = np.arange(2 * S5_GROUP)
    rep = ((d[:, None] == k[None, :] // S5_GROUP) & (h[:, None] == k[None, :] % S5_GROUP)).astype(np.float32)
    cols = np.arange(2 * 512)
    same = (g[:, None] == (cols[None, :] // S5_STATE) % 8).astype(np.float32)
    return rep, same


def _s5(p, pf, pr, pft, prt, bc, rep, same, lre8, lim8):
    nrow = 8 * SCAN_BLK
    const2 = lambda i: (0, 0)
    const3 = lambda i: (0, 0, 0)
    proj = lambda i: jnp.minimum(i, N_SCAN - 1)
    read = lambda i: jnp.clip(i - 2, 0, N_SCAN - 1)
    return pl.pallas_call(
        _s5_kernel,
        grid=(N_SCAN + 2,),
        in_specs=[
            pl.BlockSpec((BATCH, SCAN_BLK, S5_WIDTH), lambda i: (0, _fwd_blk(proj(i)), 0)),
            pl.BlockSpec((BATCH, SCAN_BLK, S5_WIDTH), lambda i: (0, _rev_blk(proj(i)), 0)),
            pl.BlockSpec(pf.shape, const2), pl.BlockSpec(pr.shape, const2),
            pl.BlockSpec(pft.shape, const2), pl.BlockSpec(prt.shape, const2),
            pl.BlockSpec(bc.shape, lambda i: (0, 0, 0, 0)), pl.BlockSpec(rep.shape, const2),
            pl.BlockSpec(same.shape, const2),
            pl.BlockSpec(lre8.shape, const3), pl.BlockSpec(lim8.shape, const3),
        ],
        out_specs=[
            pl.BlockSpec((BATCH, SCAN_BLK, S5_WIDTH), lambda i: (0, _fwd_blk(read(i)), 0)),
            pl.BlockSpec((BATCH, SCAN_BLK, S5_WIDTH), lambda i: (0, _rev_blk(read(i)), 0)),
        ],
        out_shape=[jax.ShapeDtypeStruct((BATCH, LT, S5_WIDTH), BF16)] * 2,
        scratch_shapes=[pltpu.VMEM((nrow, 4096), F32), pltpu.VMEM((nrow, 4096), F32),
                        pltpu.VMEM((8, 4096), F32),
                        pltpu.VMEM((4, 256, 1024), BF16), pltpu.VMEM((4, 1024, 256), BF16)],
        compiler_params=_cparams(("arbitrary",)),
        name="s5_scan",
    )(p, p, pf, pr, pft, prt, bc, rep, same, lre8, lim8)


def _na_bias_table(rel_bias):
    w = np.arange(GRID_W)
    cs = np.clip(w - NA_COLS // 2, 0, GRID_W - NA_COLS)
    cp = np.arange(GRID_W)
    valid = (cp[None, :] >= cs[:, None]) & (cp[None, :] < cs[:, None] + NA_COLS)
    dc = cp[None, :] - w[:, None] + (NA_COLS - 1)
    n_dc = 2 * NA_COLS - 1
    onehot = ((dc[None] == np.arange(n_dc)[:, None, None]) & valid[None]).astype(np.float32)
    n_dr = 2 * NA_ROWS - 1
    oh2 = np.zeros((2, n_dc, GRID_W, 2, GRID_W), np.float32)
    oh2[0, :, :, 0, :] = onehot
    oh2[1, :, :, 1, :] = onehot
    rb = rel_bias.astype(F32)
    rb2 = jnp.concatenate([rb[:, :n_dr - 1], rb[:, 1:]], axis=2).reshape(NA_HEADS * (n_dr - 1), 2 * n_dc)
    band2 = jnp.dot(rb2, jnp.asarray(oh2.reshape(2 * n_dc, 2 * GRID_W * GRID_W)), precision=HIGHEST)
    neg2 = np.tile(np.where(valid, 0.0, NEG).astype(np.float32), (1, 2))
    return band2.reshape(NA_HEADS, n_dr - 1, GRID_W, 2 * GRID_W) * LOG2E + jnp.asarray(neg2)


def _fill_bias_tables(band_ref, bias_ref):
    for off in range(8):
        for i2 in range(NA_ROWS // 2):
            bias_ref[off, :, :, i2 * 128:(i2 + 1) * 128] = band_ref[:, off + 2 * i2].reshape(NA_HEADS // 2, 128, 128)
    bias_ref[8] = jnp.full(bias_ref.shape[1:], NEG, F32)


NA_STEP_ROWS = TOK_BLK // GRID_W


def _na_kernel(q_ref, k_ref, v_ref, band_ref, o_ref, bias_ref):
    s_idx = pl.program_id(1)
    is_ctx = s_idx == 0

    @pl.when(jnp.logical_and(pl.program_id(0) == 0, s_idx == 0))
    def _():
        _fill_bias_tables(band_ref, bias_ref)

    lo = lax.broadcasted_iota(jnp.int32, (GRID_W, 128), 1) < HEAD_DIM
    nwin = NA_ROWS * GRID_W
    for rr in range(NA_STEP_ROWS):
        r = jnp.maximum(s_idx - 1, 0) * NA_STEP_ROWS + rr
        start = jnp.clip(r - NA_ROWS // 2, 0, SEQ // GRID_W - NA_ROWS)
        koff = pl.multiple_of(CTX_LEN + start * GRID_W, GRID_W)
        off = jnp.where(is_ctx, 8, start - r + NA_ROWS - 1)
        rows = slice(rr * GRID_W, (rr + 1) * GRID_W)
        pairs = range(NA_HEADS // 2)
        cols = [slice(hp * 128, (hp + 1) * 128) for hp in pairs]
        s_loc, s_ctx, m, p_loc, p_ctx, l, o = [], [], [], [], [], [], []
        for hp in pairs:
            qp = q_ref[0, rows, cols[hp]]
            zero = jnp.zeros_like(qp)
            q2 = jnp.concatenate([jnp.where(lo, qp, zero), jnp.where(lo, zero, qp)], axis=0)
            s_loc.append(_dot_nt(q2, k_ref[0, pl.ds(koff, nwin), cols[hp]]) + bias_ref[off, hp])
            s_ctx.append(_dot_nt(q2, k_ref[0, 0:CTX_LEN, cols[hp]]))
        for hp in pairs:
            m.append(jnp.maximum(jnp.max(s_loc[hp], axis=-1, keepdims=True),
                                 jnp.max(s_ctx[hp], axis=-1, keepdims=True)))
        for hp in pairs:
            pl_, pc_ = jnp.exp2(s_loc[hp] - m[hp]), jnp.exp2(s_ctx[hp] - m[hp])
            l.append(jnp.sum(pl_, axis=-1, keepdims=True) + jnp.sum(pc_, axis=-1, keepdims=True))
            p_loc.append(pl_.astype(BF16))
            p_ctx.append(pc_.astype(BF16))
        for hp in pairs:
            o.append(_dot(p_loc[hp], v_ref[0, pl.ds(koff, nwin), cols[hp]]) +
                     _dot(p_ctx[hp], v_ref[0, 0:CTX_LEN, cols[hp]]))
        for hp in pairs:
            oh = o[hp] / l[hp]
            o_ref[0, rows, cols[hp]] = jnp.where(lo, oh[:GRID_W], oh[GRID_W:]).astype(BF16)


def _na(p, band2):
    return pl.pallas_call(
        _na_kernel,
        grid=(BATCH, LT // TOK_BLK),
        in_specs=[
            pl.BlockSpec((1, TOK_BLK, 512), lambda b, s: (b, s, 2)),
            pl.BlockSpec((1, LT, 512), lambda b, s: (b, 0, 3)),
            pl.BlockSpec((1, LT, 512), lambda b, s: (b, 0, 4)),
            pl.BlockSpec(band2.shape, lambda b, s: (0, 0, 0, 0)),
        ],
        out_specs=pl.BlockSpec((1, TOK_BLK, 512), lambda b, s: (b, s, 0)),
        out_shape=jax.ShapeDtypeStruct((BATCH, LT, 512), BF16),
        scratch_shapes=[pltpu.VMEM((9, NA_HEADS // 2, 128, NA_ROWS * GRID_W), F32)],
        compiler_params=_cparams(("arbitrary", "arbitrary")),
        name="na_attn",
    )(p, p, p, band2)


def _gelu_tanh(x):
    return 0.5 * x * (1.0 + jnp.tanh(math.sqrt(2.0 / math.pi) * (x + 0.044715 * (x * x * x))))


def _silu(x):
    h = 0.5 * x
    return h + h * jnp.tanh(h)


def _finish(h, out, gt, pg):
    r = lax.rsqrt(jnp.mean(out * out, axis=-1, keepdims=True) + EPS)
    return h + (out * r) * (gt * pg)


def _even_out_kernel(yf_ref, yr_ref, u_ref, ga_ref, gb_ref, at_ref, hc_ref, hl_ref, mod_ref, d_ref, wg_ref,
                     bg_ref, wo_ref, pg_ref, o_ref):
    tb = pl.program_id(1)
    chains = range(PAIR)
    y = [_gelu_tanh(d_ref[...] * u_ref[h].astype(F32) + yf_ref[h].astype(F32) + yr_ref[h].astype(F32))
         for h in chains]
    glu = [_dot(y[h].astype(BF16), wg_ref[...]) for h in chains]
    y_a = [(y[h] * (0.5 + 0.5 * jnp.tanh(glu[h] + bg_ref[...])) * _silu(ga_ref[h].astype(F32))).astype(BF16)
           for h in chains]
    y_b = [(at_ref[h].astype(F32) * _silu(gb_ref[h].astype(F32))).astype(BF16) for h in chains]
    out = [_dot(y_a[h], wo_ref[0:512, :]) + _dot(y_b[h], wo_ref[512:1024, :]) for h in chains]
    for h in chains:
        res = jnp.where(tb == 0, hc_ref[h], hl_ref[h])
        o_ref[h] = _finish(res, out[h], _mod_row(mod_ref, h, tb)[:, 2 * D_MODEL:], pg_ref[...])


def _even_out(yf, yr, p, attn, ctx, x, mod, d_skip, w_glu, b_glu, w_out, post_g):
    tok = lambda c: pl.BlockSpec((PAIR, TOK_BLK, 512), lambda b, t: (b, t, c))
    full = lambda a: pl.BlockSpec(a.shape, lambda b, t: (0,) * a.ndim)
    d2, bg2, pg2 = d_skip.reshape(1, 512), b_glu.reshape(1, 512), post_g.reshape(1, D_MODEL)
    return pl.pallas_call(
        _even_out_kernel,
        grid=(BATCH // PAIR, LT // TOK_BLK),
        in_specs=[tok(0), tok(0), tok(0), tok(1), tok(5), tok(0)] + _cat_specs(D_MODEL) +
                 [full(mod), full(d2), full(w_glu), full(bg2), full(w_out), full(pg2)],
        out_specs=pl.BlockSpec((PAIR, TOK_BLK, D_MODEL), lambda b, t: (b, t, 0)),
        out_shape=jax.ShapeDtypeStruct((BATCH, LT, D_MODEL), F32),
        compiler_params=_cparams(("arbitrary", "arbitrary")),
        name="even_out",
    )(yf, yr, p, p, p, attn, ctx, x, mod, d2, w_glu, bg2, w_out, pg2)


def _softplus(z):
    return jnp.maximum(z, 0.0) + jnp.log1p(jnp.exp(-jnp.abs(z)))


def _lru_kernel(xf_ref, xfp_ref, xfn_ref, xr_ref, xrp_ref, xrn_ref, pf_ref, pr_ref, ph_ref, pft_ref, prt_ref,
                coef_ref, cb_ref, wg_ref, bg_ref, lam_ref, hf_ref, hr_ref, a0_ref, b0_ref, a1_ref, b1_ref, st_ref):
    i = pl.program_id(0)

    @pl.when(i == 0)
    def _():
        for ref in (st_ref, a0_ref, b0_ref, a1_ref, b1_ref):
            ref[...] = jnp.zeros_like(ref)

    nrow = 8 * SCAN_BLK
    half = nrow // 2
    blk = jnp.minimum(i, N_SCAN - 1)

    def halo(prev_ref, next_ref, b):
        prev_ok = jnp.logical_and(b != 0, b != CTX_SCAN)
        next_ok = jnp.logical_and(b != CTX_SCAN - 1, b != N_SCAN - 1)
        pv = prev_ref[...].reshape(BATCH * 16, LRU_WIDTH)
        nx = next_ref[...].reshape(BATCH * 16, LRU_WIDTH)
        return [jnp.where(prev_ok, pv, jnp.zeros_like(pv)), jnp.where(next_ok, nx, jnp.zeros_like(nx))]

    def step(a_w, b_w, a_s, b_s):
        vals = {}

        def unperm():
            hs = b_w[...].astype(BF16)
            hf_ref[...] = _dot(pft_ref[...], hs).astype(BF16).reshape(BATCH, SCAN_BLK, LRU_WIDTH)
            hr_ref[...] = _dot(prt_ref[...], hs).astype(BF16).reshape(BATCH, SCAN_BLK, LRU_WIDTH)

        def perm():
            hal = jnp.concatenate(halo(xfp_ref, xfn_ref, _fwd_blk(blk)) + halo(xrp_ref, xrn_ref, _rev_blk(blk)),
                                  axis=0)
            xh = _dot(ph_ref[...], hal)
            xp = (_dot(pf_ref[...], xf_ref[...].reshape(BATCH * SCAN_BLK, LRU_WIDTH)) +
                  _dot(pr_ref[...], xr_ref[...].reshape(BATCH * SCAN_BLK, LRU_WIDTH)))
            vals['x_ext'] = jnp.concatenate([xh[:16], xp, xh[16:]], axis=0)

        def conv(h):
            xc = jnp.zeros((SCAN_BLK // 2, 8, LRU_WIDTH), F32) + cb_ref[...]
            for s in range(5):
                r0 = h * half + s * 8
                xc = xc + vals['x_ext'][r0:r0 + half].reshape(SCAN_BLK // 2, 8, LRU_WIDTH) * coef_ref[s]
            vals['xc%d' % h] = xc.reshape(half, LRU_WIDTH)

        def gates(h):
            is_f = (lax.broadcasted_iota(jnp.int32, (half, 1), 0) & 7) < 4
            xcb = vals['xc%d' % h].astype(BF16)
            zero = jnp.zeros_like(xcb[:, :256])
            pre_r, pre_i = [], []
            for c in range(2):
                xt = xcb[:, c * 256:(c + 1) * 256]
                lhs = jnp.concatenate([jnp.where(is_f, xt, zero), jnp.where(is_f, zero, xt)], axis=1)
                pre = _dot(lhs, wg_ref[c])
                pre_r.append(pre[:, :256])
                pre_i.append(pre[:, 256:])
            vals['pr%d' % h] = jnp.concatenate(pre_r, axis=1)
            vals['pi%d' % h] = jnp.concatenate(pre_i, axis=1)

        def elem(h):
            rows = slice(h * half, (h + 1) * half)
            shp = (SCAN_BLK // 2, 8, LRU_WIDTH)
            t_r = jnp.tanh(vals['pr%d' % h].reshape(shp) + bg_ref[:, :LRU_WIDTH])
            t_i = jnp.tanh(vals['pi%d' % h].reshape(shp) + bg_ref[:, LRU_WIDTH:])
            c = (-0.5 * LRU_C) * _softplus(-lam_ref[...])
            u = 1.0 + t_r
            a = jnp.exp2((c * LOG2E) * u)
            a_w[rows, :] = a.reshape(half, LRU_WIDTH)
            one_m_a2 = jnp.maximum(jnp.tanh((-c) * u) * (a * a + 1.0), 1e-37)
            mult = one_m_a2 * lax.rsqrt(one_m_a2)
            b = (mult * vals['xc%d' % h].reshape(shp)) * (0.5 + 0.5 * t_i)
            b_w[rows, :] = b.reshape(half, LRU_WIDTH)

        def scan(t0):
            h = st_ref[...]
            for t in range(t0, t0 + 16):
                rows = slice(t * 8, (t + 1) * 8)
                h = a_s[rows, :] * h + b_s[rows, :]
                b_s[rows, :] = h
            st_ref[...] = h

        for piece in (unperm, functools.partial(scan, 0), perm, functools.partial(conv, 0),
                      functools.partial(conv, 1), functools.partial(scan, 16), functools.partial(gates, 0),
                      functools.partial(gates, 1), functools.partial(scan, 32), functools.partial(elem, 0),
                      functools.partial(elem, 1), functools.partial(scan, 48)):
            piece()

    @pl.when(i % 2 == 0)
    def _():
        step(a0_ref, b0_ref, a1_ref, b1_ref)

    @pl.when(i % 2 == 1)
    def _():
        step(a1_ref, b1_ref, a0_ref, b0_ref)


def _lru_gate_weights(w_a, w_x):
    w = jnp.stack([w_a, w_x], axis=1).reshape(2, 2, 2, 4, LRU_BLOCK, LRU_BLOCK)
    same = jnp.asarray(np.eye(4, dtype=np.float32))
    w = w.transpose(2, 0, 3, 4, 1, 5)[:, :, :, :, :, None] * same[:, None, None, :, None]
    return (0.5 * w).reshape(2, 512, 512).astype(BF16)


def _lru(p, pf, pr, ph, pft, prt, coef, conv_b, wg, bg, lam):
    nrow = 8 * SCAN_BLK
    full = lambda a: pl.BlockSpec(a.shape, lambda i: (0,) * a.ndim)
    proj = lambda i: jnp.minimum(i, N_SCAN - 1)
    read = lambda i: jnp.clip(i - 2, 0, N_SCAN - 1)
    cur = lambda f: pl.BlockSpec((BATCH, SCAN_BLK, LRU_WIDTH), lambda i: (0, f(proj(i)), 0))
    prev = lambda f: pl.BlockSpec((BATCH, 16, LRU_WIDTH), lambda i: (0, jnp.maximum(f(proj(i)) * 4 - 1, 0), 0))
    nxt = lambda f: pl.BlockSpec((BATCH, 16, LRU_WIDTH),
                                 lambda i: (0, jnp.minimum(f(proj(i)) * 4 + 4, LT // 16 - 1), 0))
    cb2 = conv_b.reshape(1, LRU_WIDTH)
    return pl.pallas_call(
        _lru_kernel,
        grid=(N_SCAN + 2,),
        in_specs=[cur(_fwd_blk), prev(_fwd_blk), nxt(_fwd_blk), cur(_rev_blk), prev(_rev_blk), nxt(_rev_blk),
                  full(pf), full(pr), full(ph), full(pft), full(prt), full(coef), full(cb2), full(wg), full(bg),
                  full(lam)],
        out_specs=[
            pl.BlockSpec((BATCH, SCAN_BLK, LRU_WIDTH), lambda i: (0, _fwd_blk(read(i)), 0)),
            pl.BlockSpec((BATCH, SCAN_BLK, LRU_WIDTH), lambda i: (0, _rev_blk(read(i)), 0)),
        ],
        out_shape=[jax.ShapeDtypeStruct((BATCH, LT, LRU_WIDTH), BF16)] * 2,
        scratch_shapes=[pltpu.VMEM((nrow, LRU_WIDTH), F32)] * 4 + [pltpu.VMEM((8, LRU_WIDTH), F32)],
        compiler_params=_cparams(("arbitrary",)),
        name="lru_scan",
    )(p, p, p, p, p, p, pf, pr, ph, pft, prt, coef, cb2, wg, bg, lam)


def _norm_rope(xb, g, grot, cos, sin, ones_blk, swap):
    x = xb.astype(F32)
    xrot = _dot(xb, swap)
    ms = _dot((x * x).astype(BF16), ones_blk) * (1.0 / HEAD_DIM)
    rs = lax.rsqrt(ms + EPS)
    return rs * ((x * g) * cos + (xrot * grot) * sin)


def _gqa_kernel(q_ref, k_ref, v_ref, cq_ref, sq_ref, ck_ref, sk_ref, g_ref, ones_ref, swap_ref,
                o_ref, kn_ref, vlo_ref, vhi_ref):
    qb = pl.program_id(1)
    ones_blk = ones_ref[...]
    swap = swap_ref[...]

    @pl.when(qb == 0)
    def _():
        kn = _norm_rope(k_ref[0], g_ref[2:3, :], g_ref[3:4, :], ck_ref[...], sk_ref[...], ones_blk, swap)
        kn_ref[...] = kn.astype(BF16)
        v = v_ref[0]
        lo_k = lax.broadcasted_iota(jnp.int32, (LT, 128), 1) < HEAD_DIM
        one = jnp.ones_like(v)
        vlo_ref[...] = jnp.where(lo_k, v, one)
        vhi_ref[...] = jnp.where(lo_k, one, v)

    lo = lax.broadcasted_iota(jnp.int32, (Q_BLK, 128), 1) < HEAD_DIM
    kn = kn_ref[...]
    cos = cq_ref[...]
    sin = sq_ref[...]
    n_pb = GQA_HEADS // 2

    q_blocks = [q_ref[0, :, pb * 128:(pb + 1) * 128] for pb in range(n_pb)]
    x_rot = [_dot(qb_, swap) for qb_ in q_blocks]
    msq = [_dot((qb_.astype(F32) * qb_.astype(F32)).astype(BF16), ones_blk) for qb_ in q_blocks]
    qs_all = []
    for pb in range(n_pb):
        rs = lax.rsqrt(msq[pb] * (1.0 / HEAD_DIM) + EPS) * (HEAD_DIM ** -0.5 * LOG2E)
        qn = (rs * ((q_blocks[pb].astype(F32) * g_ref[0:1, :]) * cos + (x_rot[pb] * g_ref[1:2, :]) * sin))
        qn = qn.astype(BF16)
        zero = jnp.zeros_like(qn)
        qs_all.append(jnp.concatenate([jnp.where(lo, qn, zero), jnp.where(lo, zero, qn)], axis=0))

    def scores(pb):
        return _dot_nt(qs_all[pb], kn)

    s_next = scores(0)
    for pb in range(n_pb):
        cols = slice(pb * 128, (pb + 1) * 128)
        s = s_next
        if pb + 1 < n_pb:
            s_next = scores(pb + 1)
        m = jnp.max(s, axis=-1, keepdims=True)
        p = jnp.exp2(s - m).astype(BF16)
        o_lo = _dot(p[:Q_BLK], vlo_ref[...])
        o_hi = _dot(p[Q_BLK:], vhi_ref[...])
        num = jnp.where(lo, o_lo, o_hi)
        den = pltpu.roll(jnp.where(lo, o_hi, o_lo), HEAD_DIM, axis=1)
        o_ref[0, :, cols] = (num / den).astype(BF16)


def _gqa(p, cos128, sin128, gvec, ones_blk, swap):
    nqb = SEQ // Q_BLK
    cb = CTX_LEN // Q_BLK
    return pl.pallas_call(
        _gqa_kernel,
        grid=(BATCH, nqb),
        in_specs=[
            pl.BlockSpec((1, Q_BLK, 512), lambda b, q: (b, q + cb, 2)),
            pl.BlockSpec((1, LT, 128), lambda b, q: (b, 0, 16)),
            pl.BlockSpec((1, LT, 128), lambda b, q: (b, 0, 17)),
            pl.BlockSpec((Q_BLK, 128), lambda b, q: (q + cb, 0)),
            pl.BlockSpec((Q_BLK, 128), lambda b, q: (q + cb, 0)),
            pl.BlockSpec((LT, 128), lambda b, q: (0, 0)),
            pl.BlockSpec((LT, 128), lambda b, q: (0, 0)),
            pl.BlockSpec((8, 128), lambda b, q: (0, 0)),
            pl.BlockSpec((128, 128), lambda b, q: (0, 0)),
            pl.BlockSpec((128, 128), lambda b, q: (0, 0)),
        ],
        out_specs=pl.BlockSpec((1, Q_BLK, 512), lambda b, q: (b, q, 0)),
        out_shape=jax.ShapeDtypeStruct((BATCH, SEQ, 512), BF16),
        scratch_shapes=[pltpu.VMEM((LT, 128), BF16)] * 3,
        compiler_params=_cparams(("arbitrary", "arbitrary")),
        name="gqa_attn",
    )(p, p, p, cos128, sin128, cos128, sin128, gvec, ones_blk, swap)


def _odd_out_kernel(hf_ref, hr_ref, gc_ref, gd_ref, at_ref, h_ref, mod_ref, wo_ref, pg_ref, o_ref):
    chains = range(PAIR)
    y_c = [((hf_ref[h].astype(F32) + hr_ref[h].astype(F32)) * _silu(gc_ref[h].astype(F32))).astype(BF16)
           for h in chains]
    y_d = [(at_ref[h].astype(F32) * _silu(gd_ref[h].astype(F32))).astype(BF16) for h in chains]
    out = [_dot(y_c[h], wo_ref[0:512, :]) + _dot(y_d[h], wo_ref[512:1024, :]) for h in chains]
    for h in chains:
        gt = mod_ref[pl.ds(pl.program_id(0) * PAIR + h, 1), 2 * D_MODEL:]
        o_ref[h] = _finish(h_ref[h], out[h], gt, pg_ref[...])


def _odd_out(hf, hr, p, attn, h_cat, mod, w_out, post_g):
    cb = CTX_LEN // TOK_BLK
    cat = lambda c: pl.BlockSpec((PAIR, TOK_BLK, 512), lambda b, t: (b, t + cb, c))
    full = lambda a: pl.BlockSpec(a.shape, lambda b, t: (0,) * a.ndim)
    pg2 = post_g.reshape(1, D_MODEL)
    return pl.pallas_call(
        _odd_out_kernel,
        grid=(BATCH // PAIR, SEQ // TOK_BLK),
        in_specs=[cat(0), cat(0), cat(1), cat(3),
                  pl.BlockSpec((PAIR, TOK_BLK, 512), lambda b, t: (b, t, 0)),
                  pl.BlockSpec((PAIR, TOK_BLK, D_MODEL), lambda b, t: (b, t + cb, 0)),
                  full(mod), full(w_out), full(pg2)],
        out_specs=pl.BlockSpec((PAIR, TOK_BLK, D_MODEL), lambda b, t: (b, t, 0)),
        out_shape=jax.ShapeDtypeStruct((BATCH, SEQ, D_MODEL), F32),
        compiler_params=_cparams(("arbitrary", "arbitrary")),
        name="odd_out",
    )(hf, hr, p, p, attn, h_cat, mod, w_out, pg2)


def _pair_swap_matrix(n):
    r = np.zeros((n, n), np.float32)
    r[np.arange(1, n, 2), np.arange(0, n, 2)] = -1.0
    r[np.arange(0, n, 2), np.arange(1, n, 2)] = 1.0
    return r


def _interleave_kv_groups(w, axis):
    if axis == 0:
        return w.reshape(2, 4, HEAD_DIM, w.shape[1]).transpose(1, 0, 2, 3).reshape(w.shape)
    return w.reshape(w.shape[0], 2, 4, HEAD_DIM).transpose(0, 2, 1, 3).reshape(w.shape)


def _odd_w_in(w):
    w = w.astype(BF16)
    x, gc = w[:, 0:512], w[:, 512:1024]
    q = _interleave_kv_groups(w[:, 1024:1536], 1)
    k, v = w[:, 1536:1664], w[:, 1664:1792]
    gd = _interleave_kv_groups(w[:, 1792:2304], 1)
    return jnp.concatenate([x, gc, q, gd, k, v], axis=1)


def _rope_tables():
    t = np.arange(SEQ)
    row = (t // GRID_W).astype(np.float32)
    col = (t % GRID_W).astype(np.float32)
    half = HEAD_DIM // 2
    inv = (ROPE_THETA ** (-np.arange(0, half, 2, dtype=np.float32) / half)).astype(np.float32)
    ang = np.concatenate([row[:, None] * inv, col[:, None] * inv], axis=-1)
    cos = np.repeat(np.cos(ang), 2, axis=-1)
    sin = np.repeat(np.sin(ang), 2, axis=-1)
    cos = np.concatenate([np.ones((CTX_LEN, HEAD_DIM), np.float32), cos], axis=0)
    sin = np.concatenate([np.zeros((CTX_LEN, HEAD_DIM), np.float32), sin], axis=0)
    return np.tile(cos, (1, 2)).astype(np.float32), np.tile(sin, (1, 2)).astype(np.float32)


def _swap_pairs_vec(g):
    g2 = g.reshape(-1, 2)
    return jnp.stack([g2[:, 1], g2[:, 0]], axis=-1).reshape(g.shape)


def kernel(x, c, ctx, c_ctx, ada_w, ada_b, pre_g, post_g, ev_w_in, ev_w_out, s5_lam_re, s5_lam_im, s5_log_dt,
           s5_b_re, s5_b_im, s5_c_re, s5_c_im, s5_d, s5_w_glu, s5_b_glu, na_rel_bias, od_w_in, od_w_out,
           lru_conv_w, lru_conv_b, lru_lam, lru_w_a, lru_b_a, lru_w_x, lru_b_x, gqa_q_norm, gqa_k_norm):
    pf_np, pr_np = _scan_perms()
    pf, pr = jnp.asarray(pf_np, BF16), jnp.asarray(pr_np, BF16)
    pft, prt = jnp.asarray(pf_np.T, BF16), jnp.asarray(pr_np.T, BF16)
    ph = jnp.asarray(_halo_perm(), BF16)

    c8 = jnp.concatenate([c, c_ctx[None], jnp.zeros((3, D_MODEL), F32)], axis=0)
    mod = _adaln(c8, ada_w, ada_b)

    col_scale = np.ones((EVEN_IN,), np.float32)
    col_scale[1024:1536] = HEAD_DIM ** -0.5 * LOG2E
    p0 = _inproj((ctx, x), mod[0], pre_g[0], (ev_w_in[0] * jnp.asarray(col_scale)).astype(BF16))
    bc, lre8, lim8 = _s5_weights(s5_lam_re[0], s5_lam_im[0], s5_log_dt[0], s5_b_re[0], s5_b_im[0],
                                 s5_c_re[0], s5_c_im[0])
    rep_np, same_np = _s5_expanders()
    yf, yr = _s5(p0, pf, pr, pft, prt, bc, jnp.asarray(rep_np, BF16), jnp.asarray(same_np, BF16), lre8, lim8)
    attn0 = _na(p0, _na_bias_table(na_rel_bias[0]))
    h1 = _even_out(yf, yr, p0, attn0, ctx, x, mod[0], s5_d[0], (0.5 * s5_w_glu[0]).astype(BF16), 0.5 * s5_b_glu[0],
                   ev_w_out[0].astype(BF16), post_g[0])

    p1 = _inproj(h1, mod[1], pre_g[1], _odd_w_in(od_w_in[0]))
    wg = _lru_gate_weights(lru_w_a[0], lru_w_x[0])
    bg = jnp.repeat(0.5 * jnp.concatenate([lru_b_a[0], lru_b_x[0]], axis=1), 4, axis=0)
    lam8 = jnp.repeat(lru_lam[0], 4, axis=0)
    hf, hr = _lru(p1, pf, pr, ph, pft, prt, _conv_coef(lru_conv_w[0]), lru_conv_b[0], wg, bg, lam8)
    cos_np, sin_np = _rope_tables()
    gq, gk = gqa_q_norm[0], gqa_k_norm[0]
    gvec = jnp.stack([jnp.tile(gq, 2), jnp.tile(_swap_pairs_vec(gq), 2),
                      jnp.tile(gk, 2), jnp.tile(_swap_pairs_vec(gk), 2)] + [jnp.zeros((128,), F32)] * 4)
    ones_np = np.kron(np.eye(2, dtype=np.float32), np.ones((HEAD_DIM, HEAD_DIM), np.float32))
    attn1 = _gqa(p1, jnp.asarray(cos_np), jnp.asarray(sin_np), gvec, jnp.asarray(ones_np, BF16),
                 jnp.asarray(_pair_swap_matrix(128), BF16))
    w_out1 = od_w_out[0].astype(BF16)
    w_out1 = jnp.concatenate([w_out1[:512], _interleave_kv_groups(w_out1[512:], 0)], axis=0)
    return _odd_out(hf, hr, p1, attn1, h1, mod[1], w_out1, post_g[1])
```

```python
import functools
import math

import numpy as np
import jax
import jax.numpy as jnp
from jax import lax
from jax.experimental import pallas as pl
from jax.experimental.pallas import tpu as pltpu

F32 = jnp.float32
BF16 = jnp.bfloat16
HIGHEST = lax.Precision.HIGHEST

D_MODEL = 1024
BATCH = 4
SEQ = 4096
GRID_W = 64
CTX_LEN = 256
LT = CTX_LEN + SEQ
HEAD_DIM = 64
EPS = 1e-6
S5_WIDTH = 512
S5_GROUP = 16
S5_GROUPS = 32
S5_STATE = 64
NA_HEADS = 8
NA_ROWS = 8
NA_COLS = 16
LRU_WIDTH = 512
LRU_BLOCKS = 8
LRU_BLOCK = 64
LRU_C = 8.0
GQA_HEADS = 8
ROPE_THETA = 10000.0
EVEN_IN = 3072
ODD_IN = 2304

TOK_BLK = 256
SCAN_BLK = 64
N_SCAN = LT // SCAN_BLK
CTX_SCAN = CTX_LEN // SCAN_BLK
Q_BLK = 256
NEG = -1e30
LOG2E = math.log2(math.e)
VMEM_LIMIT = 56 * 1024 * 1024


def _cparams(sem):
    return pltpu.CompilerParams(dimension_semantics=sem, vmem_limit_bytes=VMEM_LIMIT)


def _dot(a, b):
    return jnp.dot(a, b, preferred_element_type=F32)


def _dot_nt(a, b):
    return lax.dot_general(a, b, (((1,), (1,)), ((), ())), preferred_element_type=F32)


def _adaln_kernel(c_ref, w_ref, b_ref, o_ref):
    c = c_ref[...]
    s = c * jax.nn.sigmoid(c)
    o_ref[0] = jnp.dot(s, w_ref[0], preferred_element_type=F32, precision=HIGHEST) + b_ref[0]


def _adaln(c8, ada_w, ada_b):
    depth = ada_w.shape[0]
    nb = 3 * D_MODEL // 1024
    return pl.pallas_call(
        _adaln_kernel,
        grid=(depth, nb),
        in_specs=[
            pl.BlockSpec((8, D_MODEL), lambda i, n: (0, 0)),
            pl.BlockSpec((1, D_MODEL, 1024), lambda i, n: (i, 0, n)),
            pl.BlockSpec((1, 1, 1024), lambda i, n: (i, 0, n)),
        ],
        out_specs=pl.BlockSpec((1, 8, 1024), lambda i, n: (i, 0, n)),
        out_shape=jax.ShapeDtypeStruct((depth, 8, 3 * D_MODEL), F32),
        compiler_params=_cparams(("arbitrary", "arbitrary")),
        name="adaln",
    )(c8, ada_w, ada_b.reshape(depth, 1, 3 * D_MODEL))


PAIR = 4


def _cat_specs(width):
    return [pl.BlockSpec((PAIR, TOK_BLK, width), lambda b, t: (b, 0, 0)),
            pl.BlockSpec((PAIR, TOK_BLK, width), lambda b, t: (b, jnp.maximum(t - 1, 0), 0))]


def _mod_row(mod_ref, h, tb):
    row = jnp.where(tb == 0, BATCH, pl.program_id(0) * PAIR + h)
    return mod_ref[pl.ds(row, 1), :]


def _inproj_kernel(*refs, two_src):
    tb = pl.program_id(1)
    if two_src:
        c_ref, x_ref, mod_ref, g_ref, w_ref, o_ref = refs
    else:
        x_ref, mod_ref, g_ref, w_ref, o_ref = refs
    ys = []
    for h in range(PAIR):
        x = jnp.where(tb == 0, c_ref[h], x_ref[h]) if two_src else x_ref[h]
        r = lax.rsqrt(jnp.mean(x * x, axis=-1, keepdims=True) + EPS)
        m = _mod_row(mod_ref, h, tb)
        y = (x * r) * g_ref[...]
        ys.append((y * (1.0 + m[:, D_MODEL:2 * D_MODEL]) + m[:, :D_MODEL]).astype(BF16))
    for h in range(PAIR):
        o_ref[h] = _dot(ys[h], w_ref[...]).astype(BF16)


def _inproj(src, mod, g, w_bf):
    n = w_bf.shape[1]
    two_src = isinstance(src, tuple)
    if two_src:
        src_specs = _cat_specs(D_MODEL)
    else:
        src_specs = [pl.BlockSpec((PAIR, TOK_BLK, D_MODEL), lambda b, t: (b, t, 0))]
        src = (src,)
    return pl.pallas_call(
        functools.partial(_inproj_kernel, two_src=two_src),
        grid=(BATCH // PAIR, LT // TOK_BLK),
        in_specs=src_specs + [
            pl.BlockSpec((8, 3 * D_MODEL), lambda b, t: (0, 0)),
            pl.BlockSpec((1, D_MODEL), lambda b, t: (0, 0)),
            pl.BlockSpec((D_MODEL, n), lambda b, t: (0, 0)),
        ],
        out_specs=pl.BlockSpec((PAIR, TOK_BLK, n), lambda b, t: (b, t, 0)),
        out_shape=jax.ShapeDtypeStruct((BATCH, LT, n), BF16),
        compiler_params=_cparams(("arbitrary", "arbitrary")),
        name="inproj",
    )(*src, mod, g.reshape(1, D_MODEL), w_bf)


def _fwd_blk(i):
    return i


def _rev_blk(i):
    return jnp.where(i < CTX_SCAN, CTX_SCAN - 1 - i, N_SCAN + CTX_SCAN - 1 - i)


def _scan_perms():
    t = SCAN_BLK
    pf = np.zeros((8 * t, BATCH * t), np.float32)
    pr = np.zeros((8 * t, BATCH * t), np.float32)
    for tt in range(t):
        for b in range(BATCH):
            pf[tt * 8 + b, b * t + tt] = 1.0
            pr[tt * 8 + 4 + b, b * t + (t - 1 - tt)] = 1.0
    return pf, pr


def _halo_perm():
    ph = np.zeros((32, 16 * 4 * BATCH), np.float32)
    for b in range(BATCH):
        ph[1 * 8 + b, 0 * 64 + b * 16 + 15] = 1.0
        ph[2 * 8 + b, 1 * 64 + b * 16 + 0] = 1.0
        ph[3 * 8 + b, 1 * 64 + b * 16 + 1] = 1.0
        ph[0 * 8 + 4 + b, 3 * 64 + b * 16 + 1] = 1.0
        ph[1 * 8 + 4 + b, 3 * 64 + b * 16 + 0] = 1.0
        ph[2 * 8 + 4 + b, 2 * 64 + b * 16 + 15] = 1.0
    return ph


def _conv_coef(conv_w):
    zero = jnp.zeros((1, LRU_WIDTH), conv_w.dtype)
    fwd = jnp.concatenate([zero, conv_w], axis=0)
    rev = jnp.concatenate([conv_w[::-1], zero], axis=0)
    return jnp.concatenate([jnp.broadcast_to(fwd[:, None], (5, 4, LRU_WIDTH)),
                            jnp.broadcast_to(rev[:, None], (5, 4, LRU_WIDTH))], axis=1)


def _s5_kernel(uf_ref, ur_ref, pf_ref, pr_ref, pft_ref, prt_ref, bc_ref, rep_ref, same_ref, lre_ref, lim_ref,
               yf_ref, yr_ref, buf0_ref, buf1_ref, st_ref, bcat_ref, ccat_ref):
    i = pl.program_id(0)

    @pl.when(i == 0)
    def _():
        st_ref[...] = jnp.zeros_like(st_ref)
        buf0_ref[...] = jnp.zeros_like(buf0_ref)
        buf1_ref[...] = jnp.zeros_like(buf1_ref)
        same = same_ref[...].astype(F32)
        for j in range(4):
            bcat_ref[j] = (_dot(rep_ref[...], bc_ref[0, j]) * same).astype(BF16)
            ccat_ref[j] = (_dot(rep_ref[...], bc_ref[1, j]) * same).T.astype(BF16)

    nrow = 8 * SCAN_BLK

    def step(buf_a, buf_b):
        vals = {}

        def perm():
            uf = uf_ref[...].reshape(BATCH * SCAN_BLK, S5_WIDTH)
            ur = ur_ref[...].reshape(BATCH * SCAN_BLK, S5_WIDTH)
            vals['u_f'] = _dot(pf_ref[...], uf).astype(BF16)
            vals['u_r'] = _dot(pr_ref[...], ur).astype(BF16)

        def readout(j):
            is_f = (lax.broadcasted_iota(jnp.int32, (nrow, 128), 0) & 7) < 4
            yj = _dot(buf_a[:, j * 1024:(j + 1) * 1024].astype(BF16), ccat_ref[j])
            vals['y%d' % j] = jnp.where(is_f, yj[:, :128], yj[:, 128:]).astype(BF16)

        def project(j):
            lhs = jnp.concatenate([vals['u_f'][:, j * 128:(j + 1) * 128], vals['u_r'][:, j * 128:(j + 1) * 128]],
                                  axis=1)
            buf_a[:, j * 1024:(j + 1) * 1024] = _dot(lhs, bcat_ref[j])

        def unperm(p_ref, o_ref):
            yp = jnp.concatenate([vals['y%d' % j] for j in range(4)], axis=1)
            o_ref[...] = _dot(p_ref[...], yp).astype(BF16).reshape(BATCH, SCAN_BLK, S5_WIDTH)

        mxu = [perm]
        for j in range(4):
            mxu += [functools.partial(readout, j), functools.partial(project, j)]
        mxu += [functools.partial(unperm, pft_ref, yf_ref), functools.partial(unperm, prt_ref, yr_ref)]

        def scan(j, t0):
            c_re = slice(j * 1024, j * 1024 + 512)
            c_im = slice(j * 1024 + 512, (j + 1) * 1024)
            lre = lre_ref[j]
            lim = lim_ref[j]
            hre = st_ref[:, c_re]
            him = st_ref[:, c_im]
            for t in range(t0, t0 + 16):
                rows = slice(t * 8, (t + 1) * 8)
                nre = lre * hre - lim * him + buf_b[rows, c_re]
                nim = lre * him + lim * hre + buf_b[rows, c_im]
                buf_b[rows, c_re] = nre
                buf_b[rows, c_im] = nim
                hre, him = nre, nim
            st_ref[:, c_re] = hre
            st_ref[:, c_im] = him

        vpu = [functools.partial(scan, j, t0) for j in range(4) for t0 in range(0, SCAN_BLK, 16)]

        per_piece = [1, 2, 1, 2, 1, 2, 1, 2, 1, 2, 1]
        for piece, n_scan in zip(mxu, per_piece):
            piece()
            for _ in range(n_scan):
                vpu.pop(0)()

    @pl.when(i % 2 == 0)
    def _():
        step(buf0_ref, buf1_ref)

    @pl.when(i % 2 == 1)
    def _():
        step(buf1_ref, buf0_ref)


def _s5_weights(lam_re, lam_im, log_dt, b_re, b_im, c_re, c_im):
    a = lam_re.astype(F32) * jnp.exp(log_dt.astype(F32))[..., None]
    b = lam_im.astype(F32) * jnp.exp(log_dt.astype(F32))[..., None]
    lbr = jnp.exp(a) * jnp.cos(b)
    lbi = jnp.exp(a) * jnp.sin(b)
    nr = jnp.expm1(a) * jnp.cos(b) - 2.0 * jnp.sin(0.5 * b) ** 2
    d2 = lam_re * lam_re + lam_im * lam_im
    qr = (nr * lam_re + lbi * lam_im) / d2
    qi = (lbi * lam_re - nr * lam_im) / d2
    bbr = qr[..., None] * b_re - qi[..., None] * b_im
    bbi = qr[..., None] * b_im + qi[..., None] * b_re
    bb = jnp.stack([bbr, bbi], axis=1).reshape(2, 2, 4, 8, S5_STATE, S5_GROUP)
    bb = bb.transpose(2, 0, 5, 1, 3, 4).reshape(4, 2 * S5_GROUP, 2 * 512)
    cc = jnp.stack([c_re.astype(F32), -c_im.astype(F32)], axis=1).reshape(2, 2, 4, 8, S5_GROUP, S5_STATE)
    cc = cc.transpose(2, 0, 4, 1, 3, 5).reshape(4, 2 * S5_GROUP, 2 * 512)
    lre8 = jnp.repeat(lbr.reshape(2, 4, 512).transpose(1, 0, 2), 4, axis=1)
    lim8 = jnp.repeat(lbi.reshape(2, 4, 512).transpose(1, 0, 2), 4, axis=1)
    return jnp.stack([bb, cc]).astype(BF16), lre8, lim8


def _s5_expanders():
    rows = np.arange(2 * 128)
    d, g, h = rows // 128, (rows // S5_GROUP) % 8, rows % S5_GROUP
    k = np.arange(2 * S5_GROUP)
    rep = ((d[:, None] == k[None, :] // S5_GROUP) & (h[:, None] == k[None, :] % S5_GROUP)).astype(np.float32)
    cols = np.arange(2 * 512)
    same = (g[:, None] == (cols[None, :] // S5_STATE) % 8).astype(np.float32)
    return rep, same


def _s5(p, pf, pr, pft, prt, bc, rep, same, lre8, lim8):
    nrow = 8 * SCAN_BLK
    const2 = lambda i: (0, 0)
    const3 = lambda i: (0, 0, 0)
    proj = lambda i: jnp.minimum(i, N_SCAN - 1)
    read = lambda i: jnp.clip(i - 2, 0, N_SCAN - 1)
    return pl.pallas_call(
        _s5_kernel,
        grid=(N_SCAN + 2,),
        in_specs=[
            pl.BlockSpec((BATCH, SCAN_BLK, S5_WIDTH), lambda i: (0, _fwd_blk(proj(i)), 0)),
            pl.BlockSpec((BATCH, SCAN_BLK, S5_WIDTH), lambda i: (0, _rev_blk(proj(i)), 0)),
            pl.BlockSpec(pf.shape, const2), pl.BlockSpec(pr.shape, const2),
            pl.BlockSpec(pft.shape, const2), pl.BlockSpec(prt.shape, const2),
            pl.BlockSpec(bc.shape, lambda i: (0, 0, 0, 0)), pl.BlockSpec(rep.shape, const2),
            pl.BlockSpec(same.shape, const2),
            pl.BlockSpec(lre8.shape, const3), pl.BlockSpec(lim8.shape, const3),
        ],
        out_specs=[
            pl.BlockSpec((BATCH, SCAN_BLK, S5_WIDTH), lambda i: (0, _fwd_blk(read(i)), 0)),
            pl.BlockSpec((BATCH, SCAN_BLK, S5_WIDTH), lambda i: (0, _rev_blk(read(i)), 0)),
        ],
        out_shape=[jax.ShapeDtypeStruct((BATCH, LT, S5_WIDTH), BF16)] * 2,
        scratch_shapes=[pltpu.VMEM((nrow, 4096), F32), pltpu.VMEM((nrow, 4096), F32),
                        pltpu.VMEM((8, 4096), F32),
                        pltpu.VMEM((4, 256, 1024), BF16), pltpu.VMEM((4, 1024, 256), BF16)],
        compiler_params=_cparams(("arbitrary",)),
        name="s5_scan",
    )(p, p, pf, pr, pft, prt, bc, rep, same, lre8, lim8)


def _na_bias_table(rel_bias):
    w = np.arange(GRID_W)
    cs = np.clip(w - NA_COLS // 2, 0, GRID_W - NA_COLS)
    cp = np.arange(GRID_W)
    valid = (cp[None, :] >= cs[:, None]) & (cp[None, :] < cs[:, None] + NA_COLS)
    dc = cp[None, :] - w[:, None] + (NA_COLS - 1)
    n_dc = 2 * NA_COLS - 1
    onehot = ((dc[None] == np.arange(n_dc)[:, None, None]) & valid[None]).astype(np.float32)
    n_dr = 2 * NA_ROWS - 1
    oh2 = np.zeros((2, n_dc, GRID_W, 2, GRID_W), np.float32)
    oh2[0, :, :, 0, :] = onehot
    oh2[1, :, :, 1, :] = onehot
    rb = rel_bias.astype(F32)
    rb2 = jnp.concatenate([rb[:, :n_dr - 1], rb[:, 1:]], axis=2).reshape(NA_HEADS * (n_dr - 1), 2 * n_dc)
    band2 = jnp.dot(rb2, jnp.asarray(oh2.reshape(2 * n_dc, 2 * GRID_W * GRID_W)), precision=HIGHEST)
    neg2 = np.tile(np.where(valid, 0.0, NEG).astype(np.float32), (1, 2))
    return band2.reshape(NA_HEADS, n_dr - 1, GRID_W, 2 * GRID_W) * LOG2E + jnp.asarray(neg2)


def _fill_bias_tables(band_ref, bias_ref):
    for off in range(8):
        for i2 in range(NA_ROWS // 2):
            bias_ref[off, :, :, i2 * 128:(i2 + 1) * 128] = band_ref[:, off + 2 * i2].reshape(NA_HEADS // 2, 128, 128)
    bias_ref[8] = jnp.full(bias_ref.shape[1:], NEG, F32)


NA_STEP_ROWS = TOK_BLK // GRID_W


def _na_kernel(q_ref, k_ref, v_ref, band_ref, o_ref, bias_ref):
    s_idx = pl.program_id(1)
    is_ctx = s_idx == 0

    @pl.when(jnp.logical_and(pl.program_id(0) == 0, s_idx == 0))
    def _():
        _fill_bias_tables(band_ref, bias_ref)

    lo = lax.broadcasted_iota(jnp.int32, (GRID_W, 128), 1) < HEAD_DIM
    nwin = NA_ROWS * GRID_W
    for rr in range(NA_STEP_ROWS):
        r = jnp.maximum(s_idx - 1, 0) * NA_STEP_ROWS + rr
        start = jnp.clip(r - NA_ROWS // 2, 0, SEQ // GRID_W - NA_ROWS)
        koff = pl.multiple_of(CTX_LEN + start * GRID_W, GRID_W)
        off = jnp.where(is_ctx, 8, start - r + NA_ROWS - 1)
        rows = slice(rr * GRID_W, (rr + 1) * GRID_W)
        pairs = range(NA_HEADS // 2)
        cols = [slice(hp * 128, (hp + 1) * 128) for hp in pairs]
        s_loc, s_ctx, m, p_loc, p_ctx, l, o = [], [], [], [], [], [], []
        for hp in pairs:
            qp = q_ref[0, rows, cols[hp]]
            zero = jnp.zeros_like(qp)
            q2 = jnp.concatenate([jnp.where(lo, qp, zero), jnp.where(lo, zero, qp)], axis=0)
            s_loc.append(_dot_nt(q2, k_ref[0, pl.ds(koff, nwin), cols[hp]]) + bias_ref[off, hp])
            s_ctx.append(_dot_nt(q2, k_ref[0, 0:CTX_LEN, cols[hp]]))
        for hp in pairs:
            m.append(jnp.maximum(jnp.max(s_loc[hp], axis=-1, keepdims=True),
                                 jnp.max(s_ctx[hp], axis=-1, keepdims=True)))
        for hp in pairs:
            pl_, pc_ = jnp.exp2(s_loc[hp] - m[hp]), jnp.exp2(s_ctx[hp] - m[hp])
            l.append(jnp.sum(pl_, axis=-1, keepdims=True) + jnp.sum(pc_, axis=-1, keepdims=True))
            p_loc.append(pl_.astype(BF16))
            p_ctx.append(pc_.astype(BF16))
        for hp in pairs:
            o.append(_dot(p_loc[hp], v_ref[0, pl.ds(koff, nwin), cols[hp]]) +
                     _dot(p_ctx[hp], v_ref[0, 0:CTX_LEN, cols[hp]]))
        for hp in pairs:
            oh = o[hp] / l[hp]
            o_ref[0, rows, cols[hp]] = jnp.where(lo, oh[:GRID_W], oh[GRID_W:]).astype(BF16)


def _na(p, band2):
    return pl.pallas_call(
        _na_kernel,
        grid=(BATCH, LT // TOK_BLK),
        in_specs=[
            pl.BlockSpec((1, TOK_BLK, 512), lambda b, s: (b, s, 2)),
            pl.BlockSpec((1, LT, 512), lambda b, s: (b, 0, 3)),
            pl.BlockSpec((1, LT, 512), lambda b, s: (b, 0, 4)),
            pl.BlockSpec(band2.shape, lambda b, s: (0, 0, 0, 0)),
        ],
        out_specs=pl.BlockSpec((1, TOK_BLK, 512), lambda b, s: (b, s, 0)),
        out_shape=jax.ShapeDtypeStruct((BATCH, LT, 512), BF16),
        scratch_shapes=[pltpu.VMEM((9, NA_HEADS // 2, 128, NA_ROWS * GRID_W), F32)],
        compiler_params=_cparams(("arbitrary", "arbitrary")),
        name="na_attn",
    )(p, p, p, band2)


def _gelu_tanh(x):
    return 0.5 * x * (1.0 + jnp.tanh(math.sqrt(2.0 / math.pi) * (x + 0.044715 * (x * x * x))))


def _silu(x):
    h = 0.5 * x
    return h + h * jnp.tanh(h)


def _finish(h, out, gt, pg):
    r = lax.rsqrt(jnp.mean(out * out, axis=-1, keepdims=True) + EPS)
    return h + (out * r) * (gt * pg)


def _even_out_kernel(yf_ref, yr_ref, u_ref, ga_ref, gb_ref, at_ref, hc_ref, hl_ref, mod_ref, d_ref, wg_ref,
                     bg_ref, wo_ref, pg_ref, o_ref):
    tb = pl.program_id(1)
    chains = range(PAIR)
    y = [_gelu_tanh(d_ref[...] * u_ref[h].astype(F32) + yf_ref[h].astype(F32) + yr_ref[h].astype(F32))
         for h in chains]
    glu = [_dot(y[h].astype(BF16), wg_ref[...]) for h in chains]
    y_a = [(y[h] * (0.5 + 0.5 * jnp.tanh(glu[h] + bg_ref[...])) * _silu(ga_ref[h].astype(F32))).astype(BF16)
           for h in chains]
    y_b = [(at_ref[h].astype(F32) * _silu(gb_ref[h].astype(F32))).astype(BF16) for h in chains]
    out = [_dot(y_a[h], wo_ref[0:512, :]) + _dot(y_b[h], wo_ref[512:1024, :]) for h in chains]
    for h in chains:
        res = jnp.where(tb == 0, hc_ref[h], hl_ref[h])
        o_ref[h] = _finish(res, out[h], _mod_row(mod_ref, h, tb)[:, 2 * D_MODEL:], pg_ref[...])


def _even_out(yf, yr, p, attn, ctx, x, mod, d_skip, w_glu, b_glu, w_out, post_g):
    tok = lambda c: pl.BlockSpec((PAIR, TOK_BLK, 512), lambda b, t: (b, t, c))
    full = lambda a: pl.BlockSpec(a.shape, lambda b, t: (0,) * a.ndim)
    d2, bg2, pg2 = d_skip.reshape(1, 512), b_glu.reshape(1, 512), post_g.reshape(1, D_MODEL)
    return pl.pallas_call(
        _even_out_kernel,
        grid=(BATCH // PAIR, LT // TOK_BLK),
        in_specs=[tok(0), tok(0), tok(0), tok(1), tok(5), tok(0)] + _cat_specs(D_MODEL) +
                 [full(mod), full(d2), full(w_glu), full(bg2), full(w_out), full(pg2)],
        out_specs=pl.BlockSpec((PAIR, TOK_BLK, D_MODEL), lambda b, t: (b, t, 0)),
        out_shape=jax.ShapeDtypeStruct((BATCH, LT, D_MODEL), F32),
        compiler_params=_cparams(("arbitrary", "arbitrary")),
        name="even_out",
    )(yf, yr, p, p, p, attn, ctx, x, mod, d2, w_glu, bg2, w_out, pg2)


def _softplus(z):
    return jnp.maximum(z, 0.0) + jnp.log1p(jnp.exp(-jnp.abs(z)))


def _lru_kernel(xf_ref, xfp_ref, xfn_ref, xr_ref, xrp_ref, xrn_ref, pf_ref, pr_ref, ph_ref, pft_ref, prt_ref,
                coef_ref, cb_ref, wg_ref, bg_ref, lam_ref, hf_ref, hr_ref, a0_ref, b0_ref, a1_ref, b1_ref, st_ref):
    i = pl.program_id(0)

    @pl.when(i == 0)
    def _():
        for ref in (st_ref, a0_ref, b0_ref, a1_ref, b1_ref):
            ref[...] = jnp.zeros_like(ref)

    nrow = 8 * SCAN_BLK
    half = nrow // 2
    blk = jnp.minimum(i, N_SCAN - 1)

    def halo(prev_ref, next_ref, b):
        prev_ok = jnp.logical_and(b != 0, b != CTX_SCAN)
        next_ok = jnp.logical_and(b != CTX_SCAN - 1, b != N_SCAN - 1)
        pv = prev_ref[...].reshape(BATCH * 16, LRU_WIDTH)
        nx = next_ref[...].reshape(BATCH * 16, LRU_WIDTH)
        return [jnp.where(prev_ok, pv, jnp.zeros_like(pv)), jnp.where(next_ok, nx, jnp.zeros_like(nx))]

    def step(a_w, b_w, a_s, b_s):
        vals = {}

        def unperm():
            hs = b_w[...].astype(BF16)
            hf_ref[...] = _dot(pft_ref[...], hs).astype(BF16).reshape(BATCH, SCAN_BLK, LRU_WIDTH)
            hr_ref[...] = _dot(prt_ref[...], hs).astype(BF16).reshape(BATCH, SCAN_BLK, LRU_WIDTH)

        def perm():
            hal = jnp.concatenate(halo(xfp_ref, xfn_ref, _fwd_blk(blk)) + halo(xrp_ref, xrn_ref, _rev_blk(blk)),
                                  axis=0)
            xh = _dot(ph_ref[...], hal)
            xp = (_dot(pf_ref[...], xf_ref[...].reshape(BATCH * SCAN_BLK, LRU_WIDTH)) +
                  _dot(pr_ref[...], xr_ref[...].reshape(BATCH * SCAN_BLK, LRU_WIDTH)))
            vals['x_ext'] = jnp.concatenate([xh[:16], xp, xh[16:]], axis=0)

        def conv(h):
            xc = jnp.zeros((SCAN_BLK // 2, 8, LRU_WIDTH), F32) + cb_ref[...]
            for s in range(5):
                r0 = h * half + s * 8
                xc = xc + vals['x_ext'][r0:r0 + half].reshape(SCAN_BLK // 2, 8, LRU_WIDTH) * coef_ref[s]
            vals['xc%d' % h] = xc.reshape(half, LRU_WIDTH)

        def gates(h):
            is_f = (lax.broadcasted_iota(jnp.int32, (half, 1), 0) & 7) < 4
            xcb = vals['xc%d' % h].astype(BF16)
            zero = jnp.zeros_like(xcb[:, :256])
            pre_r, pre_i = [], []
            for c in range(2):
                xt = xcb[:, c * 256:(c + 1) * 256]
                lhs = jnp.concatenate([jnp.where(is_f, xt, zero), jnp.where(is_f, zero, xt)], axis=1)
                pre = _dot(lhs, wg_ref[c])
                pre_r.append(pre[:, :256])
                pre_i.append(pre[:, 256:])
            vals['pr%d' % h] = jnp.concatenate(pre_r, axis=1)
            vals['pi%d' % h] = jnp.concatenate(pre_i, axis=1)

        def elem(h):
            rows = slice(h * half, (h + 1) * half)
            shp = (SCAN_BLK // 2, 8, LRU_WIDTH)
            t_r = jnp.tanh(vals['pr%d' % h].reshape(shp) + bg_ref[:, :LRU_WIDTH])
            t_i = jnp.tanh(vals['pi%d' % h].reshape(shp) + bg_ref[:, LRU_WIDTH:])
            c = (-0.5 * LRU_C) * _softplus(-lam_ref[...])
            u = 1.0 + t_r
            a = jnp.exp2((c * LOG2E) * u)
            a_w[rows, :] = a.reshape(half, LRU_WIDTH)
            one_m_a2 = jnp.maximum(jnp.tanh((-c) * u) * (a * a + 1.0), 1e-37)
            mult = one_m_a2 * lax.rsqrt(one_m_a2)
            b = (mult * vals['xc%d' % h].reshape(shp)) * (0.5 + 0.5 * t_i)
            b_w[rows, :] = b.reshape(half, LRU_WIDTH)

        def scan(t0):
            h = st_ref[...]
            for t in range(t0, t0 + 16):
                rows = slice(t * 8, (t + 1) * 8)
                h = a_s[rows, :] * h + b_s[rows, :]
                b_s[rows, :] = h
            st_ref[...] = h

        for piece in (unperm, functools.partial(scan, 0), perm, functools.partial(conv, 0),
                      functools.partial(conv, 1), functools.partial(scan, 16), functools.partial(gates, 0),
                      functools.partial(gates, 1), functools.partial(scan, 32), functools.partial(elem, 0),
                      functools.partial(elem, 1), functools.partial(scan, 48)):
            piece()

    @pl.when(i % 2 == 0)
    def _():
        step(a0_ref, b0_ref, a1_ref, b1_ref)

    @pl.when(i % 2 == 1)
    def _():
        step(a1_ref, b1_ref, a0_ref, b0_ref)


def _lru_gate_weights(w_a, w_x):
    w = jnp.stack([w_a, w_x], axis=1).reshape(2, 2, 2, 4, LRU_BLOCK, LRU_BLOCK)
    same = jnp.asarray(np.eye(4, dtype=np.float32))
    w = w.transpose(2, 0, 3, 4, 1, 5)[:, :, :, :, :, None] * same[:, None, None, :, None]
    return (0.5 * w).reshape(2, 512, 512).astype(BF16)


def _lru(p, pf, pr, ph, pft, prt, coef, conv_b, wg, bg, lam):
    nrow = 8 * SCAN_BLK
    full = lambda a: pl.BlockSpec(a.shape, lambda i: (0,) * a.ndim)
    proj = lambda i: jnp.minimum(i, N_SCAN - 1)
    read = lambda i: jnp.clip(i - 2, 0, N_SCAN - 1)
    cur = lambda f: pl.BlockSpec((BATCH, SCAN_BLK, LRU_WIDTH), lambda i: (0, f(proj(i)), 0))
    prev = lambda f: pl.BlockSpec((BATCH, 16, LRU_WIDTH), lambda i: (0, jnp.maximum(f(proj(i)) * 4 - 1, 0), 0))
    nxt = lambda f: pl.BlockSpec((BATCH, 16, LRU_WIDTH),
                                 lambda i: (0, jnp.minimum(f(proj(i)) * 4 + 4, LT // 16 - 1), 0))
    cb2 = conv_b.reshape(1, LRU_WIDTH)
    return pl.pallas_call(
        _lru_kernel,
        grid=(N_SCAN + 2,),
        in_specs=[cur(_fwd_blk), prev(_fwd_blk), nxt(_fwd_blk), cur(_rev_blk), prev(_rev_blk), nxt(_rev_blk),
                  full(pf), full(pr), full(ph), full(pft), full(prt), full(coef), full(cb2), full(wg), full(bg),
                  full(lam)],
        out_specs=[
            pl.BlockSpec((BATCH, SCAN_BLK, LRU_WIDTH), lambda i: (0, _fwd_blk(read(i)), 0)),
            pl.BlockSpec((BATCH, SCAN_BLK, LRU_WIDTH), lambda i: (0, _rev_blk(read(i)), 0)),
        ],
        out_shape=[jax.ShapeDtypeStruct((BATCH, LT, LRU_WIDTH), BF16)] * 2,
        scratch_shapes=[pltpu.VMEM((nrow, LRU_WIDTH), F32)] * 4 + [pltpu.VMEM((8, LRU_WIDTH), F32)],
        compiler_params=_cparams(("arbitrary",)),
        name="lru_scan",
    )(p, p, p, p, p, p, pf, pr, ph, pft, prt, coef, cb2, wg, bg, lam)


def _norm_rope(xb, g, grot, cos, sin, ones_blk, swap):
    x = xb.astype(F32)
    xrot = _dot(xb, swap)
    ms = _dot((x * x).astype(BF16), ones_blk) * (1.0 / HEAD_DIM)
    rs = lax.rsqrt(ms + EPS)
    return rs * ((x * g) * cos + (xrot * grot) * sin)


def _gqa_kernel(q_ref, k_ref, v_ref, cq_ref, sq_ref, ck_ref, sk_ref, g_ref, ones_ref, swap_ref,
                o_ref, kn_ref, vlo_ref, vhi_ref):
    qb = pl.program_id(1)
    ones_blk = ones_ref[...]
    swap = swap_ref[...]

    @pl.when(qb == 0)
    def _():
        kn = _norm_rope(k_ref[0], g_ref[2:3, :], g_ref[3:4, :], ck_ref[...], sk_ref[...], ones_blk, swap)
        kn_ref[...] = kn.astype(BF16)
        v = v_ref[0]
        lo_k = lax.broadcasted_iota(jnp.int32, (LT, 128), 1) < HEAD_DIM
        one = jnp.ones_like(v)
        vlo_ref[...] = jnp.where(lo_k, v, one)
        vhi_ref[...] = jnp.where(lo_k, one, v)

    lo = lax.broadcasted_iota(jnp.int32, (Q_BLK, 128), 1) < HEAD_DIM
    kn = kn_ref[...]
    cos = cq_ref[...]
    sin = sq_ref[...]
    n_pb = GQA_HEADS // 2

    q_blocks = [q_ref[0, :, pb * 128:(pb + 1) * 128] for pb in range(n_pb)]
    x_rot = [_dot(qb_, swap) for qb_ in q_blocks]
    msq = [_dot((qb_.astype(F32) * qb_.astype(F32)).astype(BF16), ones_blk) for qb_ in q_blocks]
    qs_all = []
    for pb in range(n_pb):
        rs = lax.rsqrt(msq[pb] * (1.0 / HEAD_DIM) + EPS) * (HEAD_DIM ** -0.5 * LOG2E)
        qn = (rs * ((q_blocks[pb].astype(F32) * g_ref[0:1, :]) * cos + (x_rot[pb] * g_ref[1:2, :]) * sin))
        qn = qn.astype(BF16)
        zero = jnp.zeros_like(qn)
        qs_all.append(jnp.concatenate([jnp.where(lo, qn, zero), jnp.where(lo, zero, qn)], axis=0))

    def scores(pb):
        return _dot_nt(qs_all[pb], kn)

    s_next = scores(0)
    for pb in range(n_pb):
        cols = slice(pb * 128, (pb + 1) * 128)
        s = s_next
        if pb + 1 < n_pb:
            s_next = scores(pb + 1)
        m = jnp.max(s, axis=-1, keepdims=True)
        p = jnp.exp2(s - m).astype(BF16)
        o_lo = _dot(p[:Q_BLK], vlo_ref[...])
        o_hi = _dot(p[Q_BLK:], vhi_ref[...])
        num = jnp.where(lo, o_lo, o_hi)
        den = pltpu.roll(jnp.where(lo, o_hi, o_lo), HEAD_DIM, axis=1)
        o_ref[0, :, cols] = (num / den).astype(BF16)


def _gqa(p, cos128, sin128, gvec, ones_blk, swap):
    nqb = SEQ // Q_BLK
    cb = CTX_LEN // Q_BLK
    return pl.pallas_call(
        _gqa_kernel,
        grid=(BATCH, nqb),
        in_specs=[
            pl.BlockSpec((1, Q_BLK, 512), lambda b, q: (b, q + cb, 2)),
            pl.BlockSpec((1, LT, 128), lambda b, q: (b, 0, 16)),
            pl.BlockSpec((1, LT, 128), lambda b, q: (b, 0, 17)),
            pl.BlockSpec((Q_BLK, 128), lambda b, q: (q + cb, 0)),
            pl.BlockSpec((Q_BLK, 128), lambda b, q: (q + cb, 0)),
            pl.BlockSpec((LT, 128), lambda b, q: (0, 0)),
            pl.BlockSpec((LT, 128), lambda b, q: (0, 0)),
            pl.BlockSpec((8, 128), lambda b, q: (0, 0)),
            pl.BlockSpec((128, 128), lambda b, q: (0, 0)),
            pl.BlockSpec((128, 128), lambda b, q: (0, 0)),
        ],
        out_specs=pl.BlockSpec((1, Q_BLK, 512), lambda b, q: (b, q, 0)),
        out_shape=jax.ShapeDtypeStruct((BATCH, SEQ, 512), BF16),
        scratch_shapes=[pltpu.VMEM((LT, 128), BF16)] * 3,
        compiler_params=_cparams(("arbitrary", "arbitrary")),
        name="gqa_attn",
    )(p, p, p, cos128, sin128, cos128, sin128, gvec, ones_blk, swap)


def _odd_out_kernel(hf_ref, hr_ref, gc_ref, gd_ref, at_ref, h_ref, mod_ref, wo_ref, pg_ref, o_ref):
    chains = range(PAIR)
    y_c = [((hf_ref[h].astype(F32) + hr_ref[h].astype(F32)) * _silu(gc_ref[h].astype(F32))).astype(BF16)
           for h in chains]
    y_d = [(at_ref[h].astype(F32) * _silu(gd_ref[h].astype(F32))).astype(BF16) for h in chains]
    out = [_dot(y_c[h], wo_ref[0:512, :]) + _dot(y_d[h], wo_ref[512:1024, :]) for h in chains]
    for h in chains:
        gt = mod_ref[pl.ds(pl.program_id(0) * PAIR + h, 1), 2 * D_MODEL:]
        o_ref[h] = _finish(h_ref[h], out[h], gt, pg_ref[...])


def _odd_out(hf, hr, p, attn, h_cat, mod, w_out, post_g):
    cb = CTX_LEN // TOK_BLK
    cat = lambda c: pl.BlockSpec((PAIR, TOK_BLK, 512), lambda b, t: (b, t + cb, c))
    full = lambda a: pl.BlockSpec(a.shape, lambda b, t: (0,) * a.ndim)
    pg2 = post_g.reshape(1, D_MODEL)
    return pl.pallas_call(
        _odd_out_kernel,
        grid=(BATCH // PAIR, SEQ // TOK_BLK),
        in_specs=[cat(0), cat(0), cat(1), cat(3),
                  pl.BlockSpec((PAIR, TOK_BLK, 512), lambda b, t: (b, t, 0)),
                  pl.BlockSpec((PAIR, TOK_BLK, D_MODEL), lambda b, t: (b, t + cb, 0)),
                  full(mod), full(w_out), full(pg2)],
        out_specs=pl.BlockSpec((PAIR, TOK_BLK, D_MODEL), lambda b, t: (b, t, 0)),
        out_shape=jax.ShapeDtypeStruct((BATCH, SEQ, D_MODEL), F32),
        compiler_params=_cparams(("arbitrary", "arbitrary")),
        name="odd_out",
    )(hf, hr, p, p, attn, h_cat, mod, w_out, pg2)


def _pair_swap_matrix(n):
    r = np.zeros((n, n), np.float32)
    r[np.arange(1, n, 2), np.arange(0, n, 2)] = -1.0
    r[np.arange(0, n, 2), np.arange(1, n, 2)] = 1.0
    return r


def _interleave_kv_groups(w, axis):
    if axis == 0:
        return w.reshape(2, 4, HEAD_DIM, w.shape[1]).transpose(1, 0, 2, 3).reshape(w.shape)
    return w.reshape(w.shape[0], 2, 4, HEAD_DIM).transpose(0, 2, 1, 3).reshape(w.shape)


def _odd_w_in(w):
    w = w.astype(BF16)
    x, gc = w[:, 0:512], w[:, 512:1024]
    q = _interleave_kv_groups(w[:, 1024:1536], 1)
    k, v = w[:, 1536:1664], w[:, 1664:1792]
    gd = _interleave_kv_groups(w[:, 1792:2304], 1)
    return jnp.concatenate([x, gc, q, gd, k, v], axis=1)


def _rope_tables():
    t = np.arange(SEQ)
    row = (t // GRID_W).astype(np.float32)
    col = (t % GRID_W).astype(np.float32)
    half = HEAD_DIM // 2
    inv = (ROPE_THETA ** (-np.arange(0, half, 2, dtype=np.float32) / half)).astype(np.float32)
    ang = np.concatenate([row[:, None] * inv, col[:, None] * inv], axis=-1)
    cos = np.repeat(np.cos(ang), 2, axis=-1)
    sin = np.repeat(np.sin(ang), 2, axis=-1)
    cos = np.concatenate([np.ones((CTX_LEN, HEAD_DIM), np.float32), cos], axis=0)
    sin = np.concatenate([np.zeros((CTX_LEN, HEAD_DIM), np.float32), sin], axis=0)
    return np.tile(cos, (1, 2)).astype(np.float32), np.tile(sin, (1, 2)).astype(np.float32)


def _swap_pairs_vec(g):
    g2 = g.reshape(-1, 2)
    return jnp.stack([g2[:, 1], g2[:, 0]], axis=-1).reshape(g.shape)


def kernel(x, c, ctx, c_ctx, ada_w, ada_b, pre_g, post_g, ev_w_in, ev_w_out, s5_lam_re, s5_lam_im, s5_log_dt,
           s5_b_re, s5_b_im, s5_c_re, s5_c_im, s5_d, s5_w_glu, s5_b_glu, na_rel_bias, od_w_in, od_w_out,
           lru_conv_w, lru_conv_b, lru_lam, lru_w_a, lru_b_a, lru_w_x, lru_b_x, gqa_q_norm, gqa_k_norm):
    pf_np, pr_np = _scan_perms()
    pf, pr = jnp.asarray(pf_np, BF16), jnp.asarray(pr_np, BF16)
    pft, prt = jnp.asarray(pf_np.T, BF16), jnp.asarray(pr_np.T, BF16)
    ph = jnp.asarray(_halo_perm(), BF16)

    c8 = jnp.concatenate([c, c_ctx[None], jnp.zeros((3, D_MODEL), F32)], axis=0)
    mod = _adaln(c8, ada_w, ada_b)

    col_scale = np.ones((EVEN_IN,), np.float32)
    col_scale[1024:1536] = HEAD_DIM ** -0.5 * LOG2E
    p0 = _inproj((ctx, x), mod[0], pre_g[0], (ev_w_in[0] * jnp.asarray(col_scale)).astype(BF16))
    bc, lre8, lim8 = _s5_weights(s5_lam_re[0], s5_lam_im[0], s5_log_dt[0], s5_b_re[0], s5_b_im[0],
                                 s5_c_re[0], s5_c_im[0])
    rep_np, same_np = _s5_expanders()
    yf, yr = _s5(p0, pf, pr, pft, prt, bc, jnp.asarray(rep_np, BF16), jnp.asarray(same_np, BF16), lre8, lim8)
    attn0 = _na(p0, _na_bias_table(na_rel_bias[0]))
    h1 = _even_out(yf, yr, p0, attn0, ctx, x, mod[0], s5_d[0], (0.5 * s5_w_glu[0]).astype(BF16), 0.5 * s5_b_glu[0],
                   ev_w_out[0].astype(BF16), post_g[0])

    p1 = _inproj(h1, mod[1], pre_g[1], _odd_w_in(od_w_in[0]))
    wg = _lru_gate_weights(lru_w_a[0], lru_w_x[0])
    bg = jnp.repeat(0.5 * jnp.concatenate([lru_b_a[0], lru_b_x[0]], axis=1), 4, axis=0)
    lam8 = jnp.repeat(lru_lam[0], 4, axis=0)
    hf, hr = _lru(p1, pf, pr, ph, pft, prt, _conv_coef(lru_conv_w[0]), lru_conv_b[0], wg, bg, lam8)
    cos_np, sin_np = _rope_tables()
    gq, gk = gqa_q_norm[0], gqa_k_norm[0]
    gvec = jnp.stack([jnp.tile(gq, 2), jnp.tile(_swap_pairs_vec(gq), 2),
                      jnp.tile(gk, 2), jnp.tile(_swap_pairs_vec(gk), 2)] + [jnp.zeros((128,), F32)] * 4)
    ones_np = np.kron(np.eye(2, dtype=np.float32), np.ones((HEAD_DIM, HEAD_DIM), np.float32))
    attn1 = _gqa(p1, jnp.asarray(cos_np), jnp.asarray(sin_np), gvec, jnp.asarray(ones_np, BF16),
                 jnp.asarray(_pair_swap_matrix(128), BF16))
    w_out1 = od_w_out[0].astype(BF16)
    w_out1 = jnp.concatenate([w_out1[:512], _interleave_kv_groups(w_out1[512:], 0)], axis=0)
    return _odd_out(hf, hr, p1, attn1, h1, mod[1], w_out1, post_g[1])
```

```python
import functools
import math

import numpy as np
import jax
import jax.numpy as jnp
from jax import lax
from jax.experimental import pallas as pl
from jax.experimental.pallas import tpu as pltpu

F32 = jnp.float32
BF16 = jnp.bfloat16
HIGHEST = lax.Precision.HIGHEST

D_MODEL = 1024
BATCH = 4
SEQ = 4096
GRID_W = 64
CTX_LEN = 256
LT = CTX_LEN + SEQ
HEAD_DIM = 64
EPS = 1e-6
S5_WIDTH = 512
S5_GROUP = 16
S5_GROUPS = 32
S5_STATE = 64
NA_HEADS = 8
NA_ROWS = 8
NA_COLS = 16
LRU_WIDTH = 512
LRU_BLOCKS = 8
LRU_BLOCK = 64
LRU_C = 8.0
GQA_HEADS = 8
ROPE_THETA = 10000.0
EVEN_IN = 3072
ODD_IN = 2304

TOK_BLK = 256
SCAN_BLK = 64
N_SCAN = LT // SCAN_BLK
CTX_SCAN = CTX_LEN // SCAN_BLK
Q_BLK = 512
NEG = -1e30
LOG2E = math.log2(math.e)
VMEM_LIMIT = 56 * 1024 * 1024


def _cparams(sem):
    return pltpu.CompilerParams(dimension_semantics=sem, vmem_limit_bytes=VMEM_LIMIT)


def _dot(a, b):
    return jnp.dot(a, b, preferred_element_type=F32)


def _dot_nt(a, b):
    return lax.dot_general(a, b, (((1,), (1,)), ((), ())), preferred_element_type=F32)


def _adaln_kernel(c_ref, w_ref, b_ref, o_ref):
    c = c_ref[...]
    s = c * jax.nn.sigmoid(c)
    o_ref[0] = jnp.dot(s, w_ref[0], preferred_element_type=F32, precision=HIGHEST) + b_ref[0]


def _adaln(c8, ada_w, ada_b):
    depth = ada_w.shape[0]
    nb = 3 * D_MODEL // 1024
    return pl.pallas_call(
        _adaln_kernel,
        grid=(depth, nb),
        in_specs=[
            pl.BlockSpec((8, D_MODEL), lambda i, n: (0, 0)),
            pl.BlockSpec((1, D_MODEL, 1024), lambda i, n: (i, 0, n)),
            pl.BlockSpec((1, 1, 1024), lambda i, n: (i, 0, n)),
        ],
        out_specs=pl.BlockSpec((1, 8, 1024), lambda i, n: (i, 0, n)),
        out_shape=jax.ShapeDtypeStruct((depth, 8, 3 * D_MODEL), F32),
        compiler_params=_cparams(("arbitrary", "arbitrary")),
        name="adaln",
    )(c8, ada_w, ada_b.reshape(depth, 1, 3 * D_MODEL))


PAIR = 4


def _cat_specs(width):
    return [pl.BlockSpec((PAIR, TOK_BLK, width), lambda b, t: (b, 0, 0)),
            pl.BlockSpec((PAIR, TOK_BLK, width), lambda b, t: (b, jnp.maximum(t - 1, 0), 0))]


def _mod_row(mod_ref, h, tb):
    row = jnp.where(tb == 0, BATCH, pl.program_id(0) * PAIR + h)
    return mod_ref[pl.ds(row, 1), :]


def _inproj_kernel(*refs, two_src):
    tb = pl.program_id(1)
    if two_src:
        c_ref, x_ref, mod_ref, g_ref, w_ref, o_ref = refs
    else:
        x_ref, mod_ref, g_ref, w_ref, o_ref = refs
    ys = []
    for h in range(PAIR):
        x = jnp.where(tb == 0, c_ref[h], x_ref[h]) if two_src else x_ref[h]
        r = lax.rsqrt(jnp.mean(x * x, axis=-1, keepdims=True) + EPS)
        m = _mod_row(mod_ref, h, tb)
        y = (x * r) * g_ref[...]
        ys.append((y * (1.0 + m[:, D_MODEL:2 * D_MODEL]) + m[:, :D_MODEL]).astype(BF16))
    for h in range(PAIR):
        o_ref[h] = _dot(ys[h], w_ref[...]).astype(BF16)


def _inproj(src, mod, g, w_bf):
    n = w_bf.shape[1]
    two_src = isinstance(src, tuple)
    if two_src:
        src_specs = _cat_specs(D_MODEL)
    else:
        src_specs = [pl.BlockSpec((PAIR, TOK_BLK, D_MODEL), lambda b, t: (b, t, 0))]
        src = (src,)
    return pl.pallas_call(
        functools.partial(_inproj_kernel, two_src=two_src),
        grid=(BATCH // PAIR, LT // TOK_BLK),
        in_specs=src_specs + [
            pl.BlockSpec((8, 3 * D_MODEL), lambda b, t: (0, 0)),
            pl.BlockSpec((1, D_MODEL), lambda b, t: (0, 0)),
            pl.BlockSpec((D_MODEL, n), lambda b, t: (0, 0)),
        ],
        out_specs=pl.BlockSpec((PAIR, TOK_BLK, n), lambda b, t: (b, t, 0)),
        out_shape=jax.ShapeDtypeStruct((BATCH, LT, n), BF16),
        compiler_params=_cparams(("arbitrary", "arbitrary")),
        name="inproj",
    )(*src, mod, g.reshape(1, D_MODEL), w_bf)


def _fwd_blk(i):
    return i


def _rev_blk(i):
    return jnp.where(i < CTX_SCAN, CTX_SCAN - 1 - i, N_SCAN + CTX_SCAN - 1 - i)


def _scan_perms():
    t = SCAN_BLK
    pf = np.zeros((8 * t, BATCH * t), np.float32)
    pr = np.zeros((8 * t, BATCH * t), np.float32)
    for tt in range(t):
        for b in range(BATCH):
            pf[tt * 8 + b, b * t + tt] = 1.0
            pr[tt * 8 + 4 + b, b * t + (t - 1 - tt)] = 1.0
    return pf, pr


def _halo_perm():
    ph = np.zeros((32, 16 * 4 * BATCH), np.float32)
    for b in range(BATCH):
        ph[1 * 8 + b, 0 * 64 + b * 16 + 15] = 1.0
        ph[2 * 8 + b, 1 * 64 + b * 16 + 0] = 1.0
        ph[3 * 8 + b, 1 * 64 + b * 16 + 1] = 1.0
        ph[0 * 8 + 4 + b, 3 * 64 + b * 16 + 1] = 1.0
        ph[1 * 8 + 4 + b, 3 * 64 + b * 16 + 0] = 1.0
        ph[2 * 8 + 4 + b, 2 * 64 + b * 16 + 15] = 1.0
    return ph


def _conv_coef(conv_w):
    zero = jnp.zeros((1, LRU_WIDTH), conv_w.dtype)
    fwd = jnp.concatenate([zero, conv_w], axis=0)
    rev = jnp.concatenate([conv_w[::-1], zero], axis=0)
    return jnp.concatenate([jnp.broadcast_to(fwd[:, None], (5, 4, LRU_WIDTH)),
                            jnp.broadcast_to(rev[:, None], (5, 4, LRU_WIDTH))], axis=1)


def _s5_kernel(uf_ref, ur_ref, pf_ref, pr_ref, pft_ref, prt_ref, bc_ref, rep_ref, same_ref, lre_ref, lim_ref,
               yf_ref, yr_ref, buf0_ref, buf1_ref, st_ref, bcat_ref, ccat_ref):
    i = pl.program_id(0)

    @pl.when(i == 0)
    def _():
        st_ref[...] = jnp.zeros_like(st_ref)
        buf0_ref[...] = jnp.zeros_like(buf0_ref)
        buf1_ref[...] = jnp.zeros_like(buf1_ref)
        same = same_ref[...].astype(F32)
        for j in range(4):
            bcat_ref[j] = (_dot(rep_ref[...], bc_ref[0, j]) * same).astype(BF16)
            ccat_ref[j] = (_dot(rep_ref[...], bc_ref[1, j]) * same).T.astype(BF16)

    nrow = 8 * SCAN_BLK

    def step(buf_a, buf_b):
        vals = {}

        def perm():
            uf = uf_ref[...].reshape(BATCH * SCAN_BLK, S5_WIDTH)
            ur = ur_ref[...].reshape(BATCH * SCAN_BLK, S5_WIDTH)
            vals['u_f'] = _dot(pf_ref[...], uf).astype(BF16)
            vals['u_r'] = _dot(pr_ref[...], ur).astype(BF16)

        def readout(j):
            is_f = (lax.broadcasted_iota(jnp.int32, (nrow, 128), 0) & 7) < 4
            yj = _dot(buf_a[:, j * 1024:(j + 1) * 1024].astype(BF16), ccat_ref[j])
            vals['y%d' % j] = jnp.where(is_f, yj[:, :128], yj[:, 128:]).astype(BF16)

        def project(j):
            lhs = jnp.concatenate([vals['u_f'][:, j * 128:(j + 1) * 128], vals['u_r'][:, j * 128:(j + 1) * 128]],
                                  axis=1)
            buf_a[:, j * 1024:(j + 1) * 1024] = _dot(lhs, bcat_ref[j])

        def unperm(p_ref, o_ref):
            yp = jnp.concatenate([vals['y%d' % j] for j in range(4)], axis=1)
            o_ref[...] = _dot(p_ref[...], yp).astype(BF16).reshape(BATCH, SCAN_BLK, S5_WIDTH)

        mxu = [perm]
        for j in range(4):
            mxu += [functools.partial(readout, j), functools.partial(project, j)]
        mxu += [functools.partial(unperm, pft_ref, yf_ref), functools.partial(unperm, prt_ref, yr_ref)]

        def scan(j, t0):
            c_re = slice(j * 1024, j * 1024 + 512)
            c_im = slice(j * 1024 + 512, (j + 1) * 1024)
            lre = lre_ref[j]
            lim = lim_ref[j]
            hre = st_ref[:, c_re]
            him = st_ref[:, c_im]
            for t in range(t0, t0 + 16):
                rows = slice(t * 8, (t + 1) * 8)
                nre = lre * hre - lim * him + buf_b[rows, c_re]
                nim = lre * him + lim * hre + buf_b[rows, c_im]
                buf_b[rows, c_re] = nre
                buf_b[rows, c_im] = nim
                hre, him = nre, nim
            st_ref[:, c_re] = hre
            st_ref[:, c_im] = him

        vpu = [functools.partial(scan, j, t0) for j in range(4) for t0 in range(0, SCAN_BLK, 16)]

        per_piece = [1, 2, 1, 2, 1, 2, 1, 2, 1, 2, 1]
        for piece, n_scan in zip(mxu, per_piece):
            piece()
            for _ in range(n_scan):
                vpu.pop(0)()

    @pl.when(i % 2 == 0)
    def _():
        step(buf0_ref, buf1_ref)

    @pl.when(i % 2 == 1)
    def _():
        step(buf1_ref, buf0_ref)


def _s5_weights(lam_re, lam_im, log_dt, b_re, b_im, c_re, c_im):
    a = lam_re.astype(F32) * jnp.exp(log_dt.astype(F32))[..., None]
    b = lam_im.astype(F32) * jnp.exp(log_dt.astype(F32))[..., None]
    lbr = jnp.exp(a) * jnp.cos(b)
    lbi = jnp.exp(a) * jnp.sin(b)
    nr = jnp.expm1(a) * jnp.cos(b) - 2.0 * jnp.sin(0.5 * b) ** 2
    d2 = lam_re * lam_re + lam_im * lam_im
    qr = (nr * lam_re + lbi * lam_im) / d2
    qi = (lbi * lam_re - nr * lam_im) / d2
    bbr = qr[..., None] * b_re - qi[..., None] * b_im
    bbi = qr[..., None] * b_im + qi[..., None] * b_re
    bb = jnp.stack([bbr, bbi], axis=1).reshape(2, 2, 4, 8, S5_STATE, S5_GROUP)
    bb = bb.transpose(2, 0, 5, 1, 3, 4).reshape(4, 2 * S5_GROUP, 2 * 512)
    cc = jnp.stack([c_re.astype(F32), -c_im.astype(F32)], axis=1).reshape(2, 2, 4, 8, S5_GROUP, S5_STATE)
    cc = cc.transpose(2, 0, 4, 1, 3, 5).reshape(4, 2 * S5_GROUP, 2 * 512)
    lre8 = jnp.repeat(lbr.reshape(2, 4, 512).transpose(1, 0, 2), 4, axis=1)
    lim8 = jnp.repeat(lbi.reshape(2, 4, 512).transpose(1, 0, 2), 4, axis=1)
    return jnp.stack([bb, cc]).astype(BF16), lre8, lim8


def _s5_expanders():
    rows = np.arange(2 * 128)
    d, g, h = rows // 128, (rows // S5_GROUP) % 8, rows % S5_GROUP
    k = np.arange(2 * S5_GROUP)
    rep = ((d[:, None] == k[None, :] // S5_GROUP) & (h[:, None] == k[None, :] % S5_GROUP)).astype(np.float32)
    cols = np.arange(2 * 512)
    same = (g[:, None] == (cols[None, :] // S5_STATE) % 8).astype(np.float32)
    return rep, same


def _s5(p, pf, pr, pft, prt, bc, rep, same, lre8, lim8):
    nrow = 8 * SCAN_BLK
    const2 = lambda i: (0, 0)
    const3 = lambda i: (0, 0, 0)
    proj = lambda i: jnp.minimum(i, N_SCAN - 1)
    read = lambda i: jnp.clip(i - 2, 0, N_SCAN - 1)
    return pl.pallas_call(
        _s5_kernel,
        grid=(N_SCAN + 2,),
        in_specs=[
            pl.BlockSpec((BATCH, SCAN_BLK, S5_WIDTH), lambda i: (0, _fwd_blk(proj(i)), 0)),
            pl.BlockSpec((BATCH, SCAN_BLK, S5_WIDTH), lambda i: (0, _rev_blk(proj(i)), 0)),
            pl.BlockSpec(pf.shape, const2), pl.BlockSpec(pr.shape, const2),
            pl.BlockSpec(pft.shape, const2), pl.BlockSpec(prt.shape, const2),
            pl.BlockSpec(bc.shape, lambda i: (0, 0, 0, 0)), pl.BlockSpec(rep.shape, const2),
            pl.BlockSpec(same.shape, const2),
            pl.BlockSpec(lre8.shape, const3), pl.BlockSpec(lim8.shape, const3),
        ],
        out_specs=[
            pl.BlockSpec((BATCH, SCAN_BLK, S5_WIDTH), lambda i: (0, _fwd_blk(read(i)), 0)),
            pl.BlockSpec((BATCH, SCAN_BLK, S5_WIDTH), lambda i: (0, _rev_blk(read(i)), 0)),
        ],
        out_shape=[jax.ShapeDtypeStruct((BATCH, LT, S5_WIDTH), BF16)] * 2,
        scratch_shapes=[pltpu.VMEM((nrow, 4096), F32), pltpu.VMEM((nrow, 4096), F32),
                        pltpu.VMEM((8, 4096), F32),
                        pltpu.VMEM((4, 256, 1024), BF16), pltpu.VMEM((4, 1024, 256), BF16)],
        compiler_params=_cparams(("arbitrary",)),
        name="s5_scan",
    )(p, p, pf, pr, pft, prt, bc, rep, same, lre8, lim8)


def _na_bias_table(rel_bias):
    w = np.arange(GRID_W)
    cs = np.clip(w - NA_COLS // 2, 0, GRID_W - NA_COLS)
    cp = np.arange(GRID_W)
    valid = (cp[None, :] >= cs[:, None]) & (cp[None, :] < cs[:, None] + NA_COLS)
    dc = cp[None, :] - w[:, None] + (NA_COLS - 1)
    n_dc = 2 * NA_COLS - 1
    onehot = ((dc[None] == np.arange(n_dc)[:, None, None]) & valid[None]).astype(np.float32)
    n_dr = 2 * NA_ROWS - 1
    oh2 = np.zeros((2, n_dc, GRID_W, 2, GRID_W), np.float32)
    oh2[0, :, :, 0, :] = onehot
    oh2[1, :, :, 1, :] = onehot
    rb = rel_bias.astype(F32)
    rb2 = jnp.concatenate([rb[:, :n_dr - 1], rb[:, 1:]], axis=2).reshape(NA_HEADS * (n_dr - 1), 2 * n_dc)
    band2 = jnp.dot(rb2, jnp.asarray(oh2.reshape(2 * n_dc, 2 * GRID_W * GRID_W)), precision=HIGHEST)
    neg2 = np.tile(np.where(valid, 0.0, NEG).astype(np.float32), (1, 2))
    return band2.reshape(NA_HEADS, n_dr - 1, GRID_W, 2 * GRID_W) * LOG2E + jnp.asarray(neg2)


def _fill_bias_tables(band_ref, bias_ref):
    for off in range(8):
        for i2 in range(NA_ROWS // 2):
            bias_ref[off, :, :, i2 * 128:(i2 + 1) * 128] = band_ref[:, off + 2 * i2].reshape(NA_HEADS // 2, 128, 128)
    bias_ref[8] = jnp.full(bias_ref.shape[1:], NEG, F32)


NA_STEP_ROWS = TOK_BLK // GRID_W


def _na_kernel(q_ref, k_ref, v_ref, band_ref, o_ref, bias_ref):
    s_idx = pl.program_id(1)
    is_ctx = s_idx == 0

    @pl.when(jnp.logical_and(pl.program_id(0) == 0, s_idx == 0))
    def _():
        _fill_bias_tables(band_ref, bias_ref)

    lo = lax.broadcasted_iota(jnp.int32, (GRID_W, 128), 1) < HEAD_DIM
    nwin = NA_ROWS * GRID_W
    for rr in range(NA_STEP_ROWS):
        r = jnp.maximum(s_idx - 1, 0) * NA_STEP_ROWS + rr
        start = jnp.clip(r - NA_ROWS // 2, 0, SEQ // GRID_W - NA_ROWS)
        koff = pl.multiple_of(CTX_LEN + start * GRID_W, GRID_W)
        off = jnp.where(is_ctx, 8, start - r + NA_ROWS - 1)
        rows = slice(rr * GRID_W, (rr + 1) * GRID_W)
        pairs = range(NA_HEADS // 2)
        cols = [slice(hp * 128, (hp + 1) * 128) for hp in pairs]
        s_loc, s_ctx, m, p_loc, p_ctx, l, o = [], [], [], [], [], [], []
        for hp in pairs:
            qp = q_ref[0, rows, cols[hp]]
            zero = jnp.zeros_like(qp)
            q2 = jnp.concatenate([jnp.where(lo, qp, zero), jnp.where(lo, zero, qp)], axis=0)
            s_loc.append(_dot_nt(q2, k_ref[0, pl.ds(koff, nwin), cols[hp]]) + bias_ref[off, hp])
            s_ctx.append(_dot_nt(q2, k_ref[0, 0:CTX_LEN, cols[hp]]))
        for hp in pairs:
            m.append(jnp.maximum(jnp.max(s_loc[hp], axis=-1, keepdims=True),
                                 jnp.max(s_ctx[hp], axis=-1, keepdims=True)))
        for hp in pairs:
            pl_, pc_ = jnp.exp2(s_loc[hp] - m[hp]), jnp.exp2(s_ctx[hp] - m[hp])
            l.append(jnp.sum(pl_, axis=-1, keepdims=True) + jnp.sum(pc_, axis=-1, keepdims=True))
            p_loc.append(pl_.astype(BF16))
            p_ctx.append(pc_.astype(BF16))
        for hp in pairs:
            o.append(_dot(p_loc[hp], v_ref[0, pl.ds(koff, nwin), cols[hp]]) +
                     _dot(p_ctx[hp], v_ref[0, 0:CTX_LEN, cols[hp]]))
        for hp in pairs:
            oh = o[hp] / l[hp]
            o_ref[0, rows, cols[hp]] = jnp.where(lo, oh[:GRID_W], oh[GRID_W:]).astype(BF16)


def _na(p, band2):
    return pl.pallas_call(
        _na_kernel,
        grid=(BATCH, LT // TOK_BLK),
        in_specs=[
            pl.BlockSpec((1, TOK_BLK, 512), lambda b, s: (b, s, 2)),
            pl.BlockSpec((1, LT, 512), lambda b, s: (b, 0, 3)),
            pl.BlockSpec((1, LT, 512), lambda b, s: (b, 0, 4)),
            pl.BlockSpec(band2.shape, lambda b, s: (0, 0, 0, 0)),
        ],
        out_specs=pl.BlockSpec((1, TOK_BLK, 512), lambda b, s: (b, s, 0)),
        out_shape=jax.ShapeDtypeStruct((BATCH, LT, 512), BF16),
        scratch_shapes=[pltpu.VMEM((9, NA_HEADS // 2, 128, NA_ROWS * GRID_W), F32)],
        compiler_params=_cparams(("arbitrary", "arbitrary")),
        name="na_attn",
    )(p, p, p, band2)


def _gelu_tanh(x):
    return 0.5 * x * (1.0 + jnp.tanh(math.sqrt(2.0 / math.pi) * (x + 0.044715 * (x * x * x))))


def _silu(x):
    h = 0.5 * x
    return h + h * jnp.tanh(h)


def _finish(h, out, gt, pg):
    r = lax.rsqrt(jnp.mean(out * out, axis=-1, keepdims=True) + EPS)
    return h + (out * r) * (gt * pg)


def _even_out_kernel(yf_ref, yr_ref, u_ref, ga_ref, gb_ref, at_ref, hc_ref, hl_ref, mod_ref, d_ref, wg_ref,
                     bg_ref, wo_ref, pg_ref, o_ref):
    tb = pl.program_id(1)
    chains = range(PAIR)
    y = [_gelu_tanh(d_ref[...] * u_ref[h].astype(F32) + yf_ref[h].astype(F32) + yr_ref[h].astype(F32))
         for h in chains]
    glu = [_dot(y[h].astype(BF16), wg_ref[...]) for h in chains]
    y_a = [(y[h] * (0.5 + 0.5 * jnp.tanh(glu[h] + bg_ref[...])) * _silu(ga_ref[h].astype(F32))).astype(BF16)
           for h in chains]
    y_b = [(at_ref[h].astype(F32) * _silu(gb_ref[h].astype(F32))).astype(BF16) for h in chains]
    out = [_dot(y_a[h], wo_ref[0:512, :]) + _dot(y_b[h], wo_ref[512:1024, :]) for h in chains]
    for h in chains:
        res = jnp.where(tb == 0, hc_ref[h], hl_ref[h])
        o_ref[h] = _finish(res, out[h], _mod_row(mod_ref, h, tb)[:, 2 * D_MODEL:], pg_ref[...])


def _even_out(yf, yr, p, attn, ctx, x, mod, d_skip, w_glu, b_glu, w_out, post_g):
    tok = lambda c: pl.BlockSpec((PAIR, TOK_BLK, 512), lambda b, t: (b, t, c))
    full = lambda a: pl.BlockSpec(a.shape, lambda b, t: (0,) * a.ndim)
    d2, bg2, pg2 = d_skip.reshape(1, 512), b_glu.reshape(1, 512), post_g.reshape(1, D_MODEL)
    return pl.pallas_call(
        _even_out_kernel,
        grid=(BATCH // PAIR, LT // TOK_BLK),
        in_specs=[tok(0), tok(0), tok(0), tok(1), tok(5), tok(0)] + _cat_specs(D_MODEL) +
                 [full(mod), full(d2), full(w_glu), full(bg2), full(w_out), full(pg2)],
        out_specs=pl.BlockSpec((PAIR, TOK_BLK, D_MODEL), lambda b, t: (b, t, 0)),
        out_shape=jax.ShapeDtypeStruct((BATCH, LT, D_MODEL), F32),
        compiler_params=_cparams(("arbitrary", "arbitrary")),
        name="even_out",
    )(yf, yr, p, p, p, attn, ctx, x, mod, d2, w_glu, bg2, w_out, pg2)


def _softplus(z):
    return jnp.maximum(z, 0.0) + jnp.log1p(jnp.exp(-jnp.abs(z)))


def _lru_kernel(xf_ref, xfp_ref, xfn_ref, xr_ref, xrp_ref, xrn_ref, pf_ref, pr_ref, ph_ref, pft_ref, prt_ref,
                coef_ref, cb_ref, wg_ref, bg_ref, lam_ref, hf_ref, hr_ref, a0_ref, b0_ref, a1_ref, b1_ref, st_ref):
    i = pl.program_id(0)

    @pl.when(i == 0)
    def _():
        for ref in (st_ref, a0_ref, b0_ref, a1_ref, b1_ref):
            ref[...] = jnp.zeros_like(ref)

    nrow = 8 * SCAN_BLK
    half = nrow // 2
    blk = jnp.minimum(i, N_SCAN - 1)

    def halo(prev_ref, next_ref, b):
        prev_ok = jnp.logical_and(b != 0, b != CTX_SCAN)
        next_ok = jnp.logical_and(b != CTX_SCAN - 1, b != N_SCAN - 1)
        pv = prev_ref[...].reshape(BATCH * 16, LRU_WIDTH)
        nx = next_ref[...].reshape(BATCH * 16, LRU_WIDTH)
        return [jnp.where(prev_ok, pv, jnp.zeros_like(pv)), jnp.where(next_ok, nx, jnp.zeros_like(nx))]

    def step(a_w, b_w, a_s, b_s):
        vals = {}

        def unperm():
            hs = b_w[...].astype(BF16)
            hf_ref[...] = _dot(pft_ref[...], hs).astype(BF16).reshape(BATCH, SCAN_BLK, LRU_WIDTH)
            hr_ref[...] = _dot(prt_ref[...], hs).astype(BF16).reshape(BATCH, SCAN_BLK, LRU_WIDTH)

        def perm():
            hal = jnp.concatenate(halo(xfp_ref, xfn_ref, _fwd_blk(blk)) + halo(xrp_ref, xrn_ref, _rev_blk(blk)),
                                  axis=0)
            xh = _dot(ph_ref[...], hal)
            xp = (_dot(pf_ref[...], xf_ref[...].reshape(BATCH * SCAN_BLK, LRU_WIDTH)) +
                  _dot(pr_ref[...], xr_ref[...].reshape(BATCH * SCAN_BLK, LRU_WIDTH)))
            vals['x_ext'] = jnp.concatenate([xh[:16], xp, xh[16:]], axis=0)

        def conv(h):
            xc = jnp.zeros((SCAN_BLK // 2, 8, LRU_WIDTH), F32) + cb_ref[...]
            for s in range(5):
                r0 = h * half + s * 8
                xc = xc + vals['x_ext'][r0:r0 + half].reshape(SCAN_BLK // 2, 8, LRU_WIDTH) * coef_ref[s]
            vals['xc%d' % h] = xc.reshape(half, LRU_WIDTH)

        def gates(h):
            is_f = (lax.broadcasted_iota(jnp.int32, (half, 1), 0) & 7) < 4
            xcb = vals['xc%d' % h].astype(BF16)
            zero = jnp.zeros_like(xcb[:, :256])
            pre_r, pre_i = [], []
            for c in range(2):
                xt = xcb[:, c * 256:(c + 1) * 256]
                lhs = jnp.concatenate([jnp.where(is_f, xt, zero), jnp.where(is_f, zero, xt)], axis=1)
                pre = _dot(lhs, wg_ref[c])
                pre_r.append(pre[:, :256])
                pre_i.append(pre[:, 256:])
            vals['pr%d' % h] = jnp.concatenate(pre_r, axis=1)
            vals['pi%d' % h] = jnp.concatenate(pre_i, axis=1)

        def elem(h):
            rows = slice(h * half, (h + 1) * half)
            shp = (SCAN_BLK // 2, 8, LRU_WIDTH)
            t_r = jnp.tanh(vals['pr%d' % h].reshape(shp) + bg_ref[:, :LRU_WIDTH])
            t_i = jnp.tanh(vals['pi%d' % h].reshape(shp) + bg_ref[:, LRU_WIDTH:])
            c = (-0.5 * LRU_C) * _softplus(-lam_ref[...])
            u = 1.0 + t_r
            a = jnp.exp2((c * LOG2E) * u)
            a_w[rows, :] = a.reshape(half, LRU_WIDTH)
            one_m_a2 = jnp.maximum(jnp.tanh((-c) * u) * (a * a + 1.0), 1e-37)
            mult = one_m_a2 * lax.rsqrt(one_m_a2)
            b = (mult * vals['xc%d' % h].reshape(shp)) * (0.5 + 0.5 * t_i)
            b_w[rows, :] = b.reshape(half, LRU_WIDTH)

        def scan(t0):
            h = st_ref[...]
            for t in range(t0, t0 + 16):
                rows = slice(t * 8, (t + 1) * 8)
                h = a_s[rows, :] * h + b_s[rows, :]
                b_s[rows, :] = h
            st_ref[...] = h

        for piece in (unperm, functools.partial(scan, 0), perm, functools.partial(conv, 0),
                      functools.partial(conv, 1), functools.partial(scan, 16), functools.partial(gates, 0),
                      functools.partial(gates, 1), functools.partial(scan, 32), functools.partial(elem, 0),
                      functools.partial(elem, 1), functools.partial(scan, 48)):
            piece()

    @pl.when(i % 2 == 0)
    def _():
        step(a0_ref, b0_ref, a1_ref, b1_ref)

    @pl.when(i % 2 == 1)
    def _():
        step(a1_ref, b1_ref, a0_ref, b0_ref)


def _lru_gate_weights(w_a, w_x):
    w = jnp.stack([w_a, w_x], axis=1).reshape(2, 2, 2, 4, LRU_BLOCK, LRU_BLOCK)
    same = jnp.asarray(np.eye(4, dtype=np.float32))
    w = w.transpose(2, 0, 3, 4, 1, 5)[:, :, :, :, :, None] * same[:, None, None, :, None]
    return (0.5 * w).reshape(2, 512, 512).astype(BF16)


def _lru(p, pf, pr, ph, pft, prt, coef, conv_b, wg, bg, lam):
    nrow = 8 * SCAN_BLK
    full = lambda a: pl.BlockSpec(a.shape, lambda i: (0,) * a.ndim)
    proj = lambda i: jnp.minimum(i, N_SCAN - 1)
    read = lambda i: jnp.clip(i - 2, 0, N_SCAN - 1)
    cur = lambda f: pl.BlockSpec((BATCH, SCAN_BLK, LRU_WIDTH), lambda i: (0, f(proj(i)), 0))
    prev = lambda f: pl.BlockSpec((BATCH, 16, LRU_WIDTH), lambda i: (0, jnp.maximum(f(proj(i)) * 4 - 1, 0), 0))
    nxt = lambda f: pl.BlockSpec((BATCH, 16, LRU_WIDTH),
                                 lambda i: (0, jnp.minimum(f(proj(i)) * 4 + 4, LT // 16 - 1), 0))
    cb2 = conv_b.reshape(1, LRU_WIDTH)
    return pl.pallas_call(
        _lru_kernel,
        grid=(N_SCAN + 2,),
        in_specs=[cur(_fwd_blk), prev(_fwd_blk), nxt(_fwd_blk), cur(_rev_blk), prev(_rev_blk), nxt(_rev_blk),
                  full(pf), full(pr), full(ph), full(pft), full(prt), full(coef), full(cb2), full(wg), full(bg),
                  full(lam)],
        out_specs=[
            pl.BlockSpec((BATCH, SCAN_BLK, LRU_WIDTH), lambda i: (0, _fwd_blk(read(i)), 0)),
            pl.BlockSpec((BATCH, SCAN_BLK, LRU_WIDTH), lambda i: (0, _rev_blk(read(i)), 0)),
        ],
        out_shape=[jax.ShapeDtypeStruct((BATCH, LT, LRU_WIDTH), BF16)] * 2,
        scratch_shapes=[pltpu.VMEM((nrow, LRU_WIDTH), F32)] * 4 + [pltpu.VMEM((8, LRU_WIDTH), F32)],
        compiler_params=_cparams(("arbitrary",)),
        name="lru_scan",
    )(p, p, p, p, p, p, pf, pr, ph, pft, prt, coef, cb2, wg, bg, lam)


def _norm_rope(xb, g, grot, cos, sin, ones_blk, swap):
    x = xb.astype(F32)
    xrot = _dot(xb, swap)
    ms = _dot((x * x).astype(BF16), ones_blk) * (1.0 / HEAD_DIM)
    rs = lax.rsqrt(ms + EPS)
    return rs * ((x * g) * cos + (xrot * grot) * sin)


def _gqa_kernel(q_ref, k_ref, v_ref, cq_ref, sq_ref, ck_ref, sk_ref, g_ref, ones_ref, swap_ref,
                o_ref, kn_ref, vlo_ref, vhi_ref):
    qb = pl.program_id(1)
    ones_blk = ones_ref[...]
    swap = swap_ref[...]

    @pl.when(qb == 0)
    def _():
        kn = _norm_rope(k_ref[0], g_ref[2:3, :], g_ref[3:4, :], ck_ref[...], sk_ref[...], ones_blk, swap)
        kn_ref[...] = kn.astype(BF16)
        v = v_ref[0]
        lo_k = lax.broadcasted_iota(jnp.int32, (LT, 128), 1) < HEAD_DIM
        one = jnp.ones_like(v)
        vlo_ref[...] = jnp.where(lo_k, v, one)
        vhi_ref[...] = jnp.where(lo_k, one, v)

    lo = lax.broadcasted_iota(jnp.int32, (Q_BLK, 128), 1) < HEAD_DIM
    kn = kn_ref[...]
    cos = cq_ref[...]
    sin = sq_ref[...]
    n_pb = GQA_HEADS // 2

    q_blocks = [q_ref[:, pb * 128:(pb + 1) * 128] for pb in range(n_pb)]
    x_rot = [_dot(qb_, swap) for qb_ in q_blocks]
    msq = [_dot((qb_.astype(F32) * qb_.astype(F32)).astype(BF16), ones_blk) for qb_ in q_blocks]
    qs_all = []
    for pb in range(n_pb):
        rs = lax.rsqrt(msq[pb] * (1.0 / HEAD_DIM) + EPS) * (HEAD_DIM ** -0.5 * LOG2E)
        qn = (rs * ((q_blocks[pb].astype(F32) * g_ref[0:1, :]) * cos + (x_rot[pb] * g_ref[1:2, :]) * sin))
        qn = qn.astype(BF16)
        zero = jnp.zeros_like(qn)
        qs_all.append(jnp.concatenate([jnp.where(lo, qn, zero), jnp.where(lo, zero, qn)], axis=0))

    def scores(pb):
        return _dot_nt(qs_all[pb], kn)

    s_next = scores(0)
    for pb in range(n_pb):
        cols = slice(pb * 128, (pb + 1) * 128)
        s = s_next
        if pb + 1 < n_pb:
            s_next = scores(pb + 1)
        m = jnp.max(s, axis=-1, keepdims=True)
        p = jnp.exp2(s - m).astype(BF16)
        o_lo = _dot(p[:Q_BLK], vlo_ref[...])
        o_hi = _dot(p[Q_BLK:], vhi_ref[...])
        num = jnp.where(lo, o_lo, o_hi)
        den = pltpu.roll(jnp.where(lo, o_hi, o_lo), HEAD_DIM, axis=1)
        o_ref[0, :, cols] = (num / den).astype(BF16)


def _gqa(p, cos128, sin128, gvec, ones_blk, swap):
    nqb = SEQ // Q_BLK
    cos_lat, sin_lat = cos128[CTX_LEN:], sin128[CTX_LEN:]
    return pl.pallas_call(
        _gqa_kernel,
        grid=(BATCH, nqb),
        in_specs=[
            pl.BlockSpec((pl.Squeezed(), pl.Element(Q_BLK), pl.Element(512)),
                         lambda b, q: (b, pl.multiple_of(CTX_LEN + q * Q_BLK, CTX_LEN), 1024)),
            pl.BlockSpec((1, LT, 128), lambda b, q: (b, 0, 16)),
            pl.BlockSpec((1, LT, 128), lambda b, q: (b, 0, 17)),
            pl.BlockSpec((Q_BLK, 128), lambda b, q: (q, 0)),
            pl.BlockSpec((Q_BLK, 128), lambda b, q: (q, 0)),
            pl.BlockSpec((LT, 128), lambda b, q: (0, 0)),
            pl.BlockSpec((LT, 128), lambda b, q: (0, 0)),
            pl.BlockSpec((8, 128), lambda b, q: (0, 0)),
            pl.BlockSpec((128, 128), lambda b, q: (0, 0)),
            pl.BlockSpec((128, 128), lambda b, q: (0, 0)),
        ],
        out_specs=pl.BlockSpec((1, Q_BLK, 512), lambda b, q: (b, q, 0)),
        out_shape=jax.ShapeDtypeStruct((BATCH, SEQ, 512), BF16),
        scratch_shapes=[pltpu.VMEM((LT, 128), BF16)] * 3,
        compiler_params=_cparams(("arbitrary", "arbitrary")),
        name="gqa_attn",
    )(p, p, p, cos_lat, sin_lat, cos128, sin128, gvec, ones_blk, swap)


def _odd_out_kernel(hf_ref, hr_ref, gc_ref, gd_ref, at_ref, h_ref, mod_ref, wo_ref, pg_ref, o_ref):
    chains = range(PAIR)
    y_c = [((hf_ref[h].astype(F32) + hr_ref[h].astype(F32)) * _silu(gc_ref[h].astype(F32))).astype(BF16)
           for h in chains]
    y_d = [(at_ref[h].astype(F32) * _silu(gd_ref[h].astype(F32))).astype(BF16) for h in chains]
    out = [_dot(y_c[h], wo_ref[0:512, :]) + _dot(y_d[h], wo_ref[512:1024, :]) for h in chains]
    for h in chains:
        gt = mod_ref[pl.ds(pl.program_id(0) * PAIR + h, 1), 2 * D_MODEL:]
        o_ref[h] = _finish(h_ref[h], out[h], gt, pg_ref[...])


def _odd_out(hf, hr, p, attn, h_cat, mod, w_out, post_g):
    cb = CTX_LEN // TOK_BLK
    cat = lambda c: pl.BlockSpec((PAIR, TOK_BLK, 512), lambda b, t: (b, t + cb, c))
    full = lambda a: pl.BlockSpec(a.shape, lambda b, t: (0,) * a.ndim)
    pg2 = post_g.reshape(1, D_MODEL)
    return pl.pallas_call(
        _odd_out_kernel,
        grid=(BATCH // PAIR, SEQ // TOK_BLK),
        in_specs=[cat(0), cat(0), cat(1), cat(3),
                  pl.BlockSpec((PAIR, TOK_BLK, 512), lambda b, t: (b, t, 0)),
                  pl.BlockSpec((PAIR, TOK_BLK, D_MODEL), lambda b, t: (b, t + cb, 0)),
                  full(mod), full(w_out), full(pg2)],
        out_specs=pl.BlockSpec((PAIR, TOK_BLK, D_MODEL), lambda b, t: (b, t, 0)),
        out_shape=jax.ShapeDtypeStruct((BATCH, SEQ, D_MODEL), F32),
        compiler_params=_cparams(("arbitrary", "arbitrary")),
        name="odd_out",
    )(hf, hr, p, p, attn, h_cat, mod, w_out, pg2)


def _pair_swap_matrix(n):
    r = np.zeros((n, n), np.float32)
    r[np.arange(1, n, 2), np.arange(0, n, 2)] = -1.0
    r[np.arange(0, n, 2), np.arange(1, n, 2)] = 1.0
    return r


def _interleave_kv_groups(w, axis):
    if axis == 0:
        return w.reshape(2, 4, HEAD_DIM, w.shape[1]).transpose(1, 0, 2, 3).reshape(w.shape)
    return w.reshape(w.shape[0], 2, 4, HEAD_DIM).transpose(0, 2, 1, 3).reshape(w.shape)


def _odd_w_in(w):
    w = w.astype(BF16)
    x, gc = w[:, 0:512], w[:, 512:1024]
    q = _interleave_kv_groups(w[:, 1024:1536], 1)
    k, v = w[:, 1536:1664], w[:, 1664:1792]
    gd = _interleave_kv_groups(w[:, 1792:2304], 1)
    return jnp.concatenate([x, gc, q, gd, k, v], axis=1)


def _rope_tables():
    t = np.arange(SEQ)
    row = (t // GRID_W).astype(np.float32)
    col = (t % GRID_W).astype(np.float32)
    half = HEAD_DIM // 2
    inv = (ROPE_THETA ** (-np.arange(0, half, 2, dtype=np.float32) / half)).astype(np.float32)
    ang = np.concatenate([row[:, None] * inv, col[:, None] * inv], axis=-1)
    cos = np.repeat(np.cos(ang), 2, axis=-1)
    sin = np.repeat(np.sin(ang), 2, axis=-1)
    cos = np.concatenate([np.ones((CTX_LEN, HEAD_DIM), np.float32), cos], axis=0)
    sin = np.concatenate([np.zeros((CTX_LEN, HEAD_DIM), np.float32), sin], axis=0)
    return np.tile(cos, (1, 2)).astype(np.float32), np.tile(sin, (1, 2)).astype(np.float32)


def _swap_pairs_vec(g):
    g2 = g.reshape(-1, 2)
    return jnp.stack([g2[:, 1], g2[:, 0]], axis=-1).reshape(g.shape)


def kernel(x, c, ctx, c_ctx, ada_w, ada_b, pre_g, post_g, ev_w_in, ev_w_out, s5_lam_re, s5_lam_im, s5_log_dt,
           s5_b_re, s5_b_im, s5_c_re, s5_c_im, s5_d, s5_w_glu, s5_b_glu, na_rel_bias, od_w_in, od_w_out,
           lru_conv_w, lru_conv_b, lru_lam, lru_w_a, lru_b_a, lru_w_x, lru_b_x, gqa_q_norm, gqa_k_norm):
    pf_np, pr_np = _scan_perms()
    pf, pr = jnp.asarray(pf_np, BF16), jnp.asarray(pr_np, BF16)
    pft, prt = jnp.asarray(pf_np.T, BF16), jnp.asarray(pr_np.T, BF16)
    ph = jnp.asarray(_halo_perm(), BF16)

    c8 = jnp.concatenate([c, c_ctx[None], jnp.zeros((3, D_MODEL), F32)], axis=0)
    mod = _adaln(c8, ada_w, ada_b)

    col_scale = np.ones((EVEN_IN,), np.float32)
    col_scale[1024:1536] = HEAD_DIM ** -0.5 * LOG2E
    p0 = _inproj((ctx, x), mod[0], pre_g[0], (ev_w_in[0] * jnp.asarray(col_scale)).astype(BF16))
    bc, lre8, lim8 = _s5_weights(s5_lam_re[0], s5_lam_im[0], s5_log_dt[0], s5_b_re[0], s5_b_im[0],
                                 s5_c_re[0], s5_c_im[0])
    rep_np, same_np = _s5_expanders()
    yf, yr = _s5(p0, pf, pr, pft, prt, bc, jnp.asarray(rep_np, BF16), jnp.asarray(same_np, BF16), lre8, lim8)
    attn0 = _na(p0, _na_bias_table(na_rel_bias[0]))
    h1 = _even_out(yf, yr, p0, attn0, ctx, x, mod[0], s5_d[0], (0.5 * s5_w_glu[0]).astype(BF16), 0.5 * s5_b_glu[0],
                   ev_w_out[0].astype(BF16), post_g[0])

    p1 = _inproj(h1, mod[1], pre_g[1], _odd_w_in(od_w_in[0]))
    wg = _lru_gate_weights(lru_w_a[0], lru_w_x[0])
    bg = jnp.repeat(0.5 * jnp.concatenate([lru_b_a[0], lru_b_x[0]], axis=1), 4, axis=0)
    lam8 = jnp.repeat(lru_lam[0], 4, axis=0)
    hf, hr = _lru(p1, pf, pr, ph, pft, prt, _conv_coef(lru_conv_w[0]), lru_conv_b[0], wg, bg, lam8)
    cos_np, sin_np = _rope_tables()
    gq, gk = gqa_q_norm[0], gqa_k_norm[0]
    gvec = jnp.stack([jnp.tile(gq, 2), jnp.tile(_swap_pairs_vec(gq), 2),
                      jnp.tile(gk, 2), jnp.tile(_swap_pairs_vec(gk), 2)] + [jnp.zeros((128,), F32)] * 4)
    ones_np = np.kron(np.eye(2, dtype=np.float32), np.ones((HEAD_DIM, HEAD_DIM), np.float32))
    attn1 = _gqa(p1, jnp.asarray(cos_np), jnp.asarray(sin_np), gvec, jnp.asarray(ones_np, BF16),
                 jnp.asarray(_pair_swap_matrix(128), BF16))
    w_out1 = od_w_out[0].astype(BF16)
    w_out1 = jnp.concatenate([w_out1[:512], _interleave_kv_groups(w_out1[512:], 0)], axis=0)
    return _odd_out(hf, hr, p1, attn1, h1, mod[1], w_out1, post_g[1])
```

```python
import functools
import math

import numpy as np
import jax
import jax.numpy as jnp
from jax import lax
from jax.experimental import pallas as pl
from jax.experimental.pallas import tpu as pltpu

F32 = jnp.float32
BF16 = jnp.bfloat16
HIGHEST = lax.Precision.HIGHEST

D_MODEL = 1024
BATCH = 4
SEQ = 4096
GRID_W = 64
CTX_LEN = 256
LT = CTX_LEN + SEQ
HEAD_DIM = 64
EPS = 1e-6
S5_WIDTH = 512
S5_GROUP = 16
S5_GROUPS = 32
S5_STATE = 64
NA_HEADS = 8
NA_ROWS = 8
NA_COLS = 16
LRU_WIDTH = 512
LRU_BLOCKS = 8
LRU_BLOCK = 64
LRU_C = 8.0
GQA_HEADS = 8
ROPE_THETA = 10000.0
EVEN_IN = 3072
ODD_IN = 2304

TOK_BLK = 256
SCAN_BLK = 64
N_SCAN = LT // SCAN_BLK
CTX_SCAN = CTX_LEN // SCAN_BLK
Q_BLK = 512
NEG = -1e30
LOG2E = math.log2(math.e)
VMEM_LIMIT = 56 * 1024 * 1024


def _cparams(sem):
    return pltpu.CompilerParams(dimension_semantics=sem, vmem_limit_bytes=VMEM_LIMIT)


def _dot(a, b):
    return jnp.dot(a, b, preferred_element_type=F32)


def _dot_nt(a, b):
    return lax.dot_general(a, b, (((1,), (1,)), ((), ())), preferred_element_type=F32)


def _adaln_kernel(c_ref, w_ref, b_ref, o_ref):
    c = c_ref[...]
    s = c * jax.nn.sigmoid(c)
    o_ref[0] = jnp.dot(s, w_ref[0], preferred_element_type=F32, precision=HIGHEST) + b_ref[0]


def _adaln(c8, ada_w, ada_b):
    depth = ada_w.shape[0]
    nb = 3 * D_MODEL // 1024
    return pl.pallas_call(
        _adaln_kernel,
        grid=(depth, nb),
        in_specs=[
            pl.BlockSpec((8, D_MODEL), lambda i, n: (0, 0)),
            pl.BlockSpec((1, D_MODEL, 1024), lambda i, n: (i, 0, n)),
            pl.BlockSpec((1, 1, 1024), lambda i, n: (i, 0, n)),
        ],
        out_specs=pl.BlockSpec((1, 8, 1024), lambda i, n: (i, 0, n)),
        out_shape=jax.ShapeDtypeStruct((depth, 8, 3 * D_MODEL), F32),
        compiler_params=_cparams(("arbitrary", "arbitrary")),
        name="adaln",
    )(c8, ada_w, ada_b.reshape(depth, 1, 3 * D_MODEL))


PAIR = 4


def _cat_specs(width):
    return [pl.BlockSpec((PAIR, TOK_BLK, width), lambda b, t: (b, 0, 0)),
            pl.BlockSpec((PAIR, TOK_BLK, width), lambda b, t: (b, jnp.maximum(t - 1, 0), 0))]


def _mod_row(mod_ref, h, tb):
    row = jnp.where(tb == 0, BATCH, pl.program_id(0) * PAIR + h)
    return mod_ref[pl.ds(row, 1), :]


def _inproj_kernel(*refs, two_src):
    tb = pl.program_id(1)
    if two_src:
        c_ref, x_ref, mod_ref, g_ref, w_ref, o_ref = refs
    else:
        x_ref, mod_ref, g_ref, w_ref, o_ref = refs
    ys = []
    for h in range(PAIR):
        x = jnp.where(tb == 0, c_ref[h], x_ref[h]) if two_src else x_ref[h]
        r = lax.rsqrt(jnp.mean(x * x, axis=-1, keepdims=True) + EPS)
        m = _mod_row(mod_ref, h, tb)
        y = (x * r) * g_ref[...]
        ys.append((y * (1.0 + m[:, D_MODEL:2 * D_MODEL]) + m[:, :D_MODEL]).astype(BF16))
    for h in range(PAIR):
        o_ref[h] = _dot(ys[h], w_ref[...]).astype(BF16)


def _inproj(src, mod, g, w_bf):
    n = w_bf.shape[1]
    two_src = isinstance(src, tuple)
    if two_src:
        src_specs = _cat_specs(D_MODEL)
    else:
        src_specs = [pl.BlockSpec((PAIR, TOK_BLK, D_MODEL), lambda b, t: (b, t, 0))]
        src = (src,)
    return pl.pallas_call(
        functools.partial(_inproj_kernel, two_src=two_src),
        grid=(BATCH // PAIR, LT // TOK_BLK),
        in_specs=src_specs + [
            pl.BlockSpec((8, 3 * D_MODEL), lambda b, t: (0, 0)),
            pl.BlockSpec((1, D_MODEL), lambda b, t: (0, 0)),
            pl.BlockSpec((D_MODEL, n), lambda b, t: (0, 0)),
        ],
        out_specs=pl.BlockSpec((PAIR, TOK_BLK, n), lambda b, t: (b, t, 0)),
        out_shape=jax.ShapeDtypeStruct((BATCH, LT, n), BF16),
        compiler_params=_cparams(("arbitrary", "arbitrary")),
        name="inproj",
    )(*src, mod, g.reshape(1, D_MODEL), w_bf)


def _fwd_blk(i):
    return i


def _rev_blk(i):
    return jnp.where(i < CTX_SCAN, CTX_SCAN - 1 - i, N_SCAN + CTX_SCAN - 1 - i)


def _scan_perms():
    t = SCAN_BLK
    pf = np.zeros((8 * t, BATCH * t), np.float32)
    pr = np.zeros((8 * t, BATCH * t), np.float32)
    for tt in range(t):
        for b in range(BATCH):
            pf[tt * 8 + b, b * t + tt] = 1.0
            pr[tt * 8 + 4 + b, b * t + (t - 1 - tt)] = 1.0
    return pf, pr


def _halo_perm():
    ph = np.zeros((32, 16 * 4 * BATCH), np.float32)
    for b in range(BATCH):
        ph[1 * 8 + b, 0 * 64 + b * 16 + 15] = 1.0
        ph[2 * 8 + b, 1 * 64 + b * 16 + 0] = 1.0
        ph[3 * 8 + b, 1 * 64 + b * 16 + 1] = 1.0
        ph[0 * 8 + 4 + b, 3 * 64 + b * 16 + 1] = 1.0
        ph[1 * 8 + 4 + b, 3 * 64 + b * 16 + 0] = 1.0
        ph[2 * 8 + 4 + b, 2 * 64 + b * 16 + 15] = 1.0
    return ph


def _conv_coef(conv_w):
    zero = jnp.zeros((1, LRU_WIDTH), conv_w.dtype)
    fwd = jnp.concatenate([zero, conv_w], axis=0)
    rev = jnp.concatenate([conv_w[::-1], zero], axis=0)
    return jnp.concatenate([jnp.broadcast_to(fwd[:, None], (5, 4, LRU_WIDTH)),
                            jnp.broadcast_to(rev[:, None], (5, 4, LRU_WIDTH))], axis=1)


def _s5_kernel(uf_ref, ur_ref, pf_ref, pr_ref, pft_ref, prt_ref, bc_ref, rep_ref, same_ref, lre_ref, lim_ref,
               yf_ref, yr_ref, buf0_ref, buf1_ref, st_ref, bcat_ref, ccat_ref):
    i = pl.program_id(0)

    @pl.when(i == 0)
    def _():
        st_ref[...] = jnp.zeros_like(st_ref)
        buf0_ref[...] = jnp.zeros_like(buf0_ref)
        buf1_ref[...] = jnp.zeros_like(buf1_ref)
        same = same_ref[...].astype(F32)
        for j in range(4):
            bcat_ref[j] = (_dot(rep_ref[...], bc_ref[0, j]) * same).astype(BF16)
            ccat_ref[j] = (_dot(rep_ref[...], bc_ref[1, j]) * same).T.astype(BF16)

    nrow = 8 * SCAN_BLK

    def step(buf_a, buf_b):
        vals = {}

        def perm():
            uf = uf_ref[...].reshape(BATCH * SCAN_BLK, S5_WIDTH)
            ur = ur_ref[...].reshape(BATCH * SCAN_BLK, S5_WIDTH)
            vals['u_f'] = _dot(pf_ref[...], uf).astype(BF16)
            vals['u_r'] = _dot(pr_ref[...], ur).astype(BF16)

        def readout(j):
            is_f = (lax.broadcasted_iota(jnp.int32, (nrow, 128), 0) & 7) < 4
            yj = _dot(buf_a[:, j * 1024:(j + 1) * 1024].astype(BF16), ccat_ref[j])
            vals['y%d' % j] = jnp.where(is_f, yj[:, :128], yj[:, 128:]).astype(BF16)

        def project(j):
            lhs = jnp.concatenate([vals['u_f'][:, j * 128:(j + 1) * 128], vals['u_r'][:, j * 128:(j + 1) * 128]],
                                  axis=1)
            buf_a[:, j * 1024:(j + 1) * 1024] = _dot(lhs, bcat_ref[j])

        def unperm(p_ref, o_ref):
            yp = jnp.concatenate([vals['y%d' % j] for j in range(4)], axis=1)
            o_ref[...] = _dot(p_ref[...], yp).astype(BF16).reshape(BATCH, SCAN_BLK, S5_WIDTH)

        mxu = [perm]
        for j in range(4):
            mxu += [functools.partial(readout, j), functools.partial(project, j)]
        mxu += [functools.partial(unperm, pft_ref, yf_ref), functools.partial(unperm, prt_ref, yr_ref)]

        def scan(j, t0):
            c_re = slice(j * 1024, j * 1024 + 512)
            c_im = slice(j * 1024 + 512, (j + 1) * 1024)
            lre = lre_ref[j]
            lim = lim_ref[j]
            hre = st_ref[:, c_re]
            him = st_ref[:, c_im]
            for t in range(t0, t0 + 16):
                rows = slice(t * 8, (t + 1) * 8)
                nre = lre * hre - lim * him + buf_b[rows, c_re]
                nim = lre * him + lim * hre + buf_b[rows, c_im]
                buf_b[rows, c_re] = nre
                buf_b[rows, c_im] = nim
                hre, him = nre, nim
            st_ref[:, c_re] = hre
            st_ref[:, c_im] = him

        vpu = [functools.partial(scan, j, t0) for j in range(4) for t0 in range(0, SCAN_BLK, 16)]

        per_piece = [1, 2, 1, 2, 1, 2, 1, 2, 1, 2, 1]
        for piece, n_scan in zip(mxu, per_piece):
            piece()
            for _ in range(n_scan):
                vpu.pop(0)()

    @pl.when(i % 2 == 0)
    def _():
        step(buf0_ref, buf1_ref)

    @pl.when(i % 2 == 1)
    def _():
        step(buf1_ref, buf0_ref)


def _s5_weights(lam_re, lam_im, log_dt, b_re, b_im, c_re, c_im):
    a = lam_re.astype(F32) * jnp.exp(log_dt.astype(F32))[..., None]
    b = lam_im.astype(F32) * jnp.exp(log_dt.astype(F32))[..., None]
    lbr = jnp.exp(a) * jnp.cos(b)
    lbi = jnp.exp(a) * jnp.sin(b)
    nr = jnp.expm1(a) * jnp.cos(b) - 2.0 * jnp.sin(0.5 * b) ** 2
    d2 = lam_re * lam_re + lam_im * lam_im
    qr = (nr * lam_re + lbi * lam_im) / d2
    qi = (lbi * lam_re - nr * lam_im) / d2
    bbr = qr[..., None] * b_re - qi[..., None] * b_im
    bbi = qr[..., None] * b_im + qi[..., None] * b_re
    bb = jnp.stack([bbr, bbi], axis=1).reshape(2, 2, 4, 8, S5_STATE, S5_GROUP)
    bb = bb.transpose(2, 0, 5, 1, 3, 4).reshape(4, 2 * S5_GROUP, 2 * 512)
    cc = jnp.stack([c_re.astype(F32), -c_im.astype(F32)], axis=1).reshape(2, 2, 4, 8, S5_GROUP, S5_STATE)
    cc = cc.transpose(2, 0, 4, 1, 3, 5).reshape(4, 2 * S5_GROUP, 2 * 512)
    lre8 = jnp.repeat(lbr.reshape(2, 4, 512).transpose(1, 0, 2), 4, axis=1)
    lim8 = jnp.repeat(lbi.reshape(2, 4, 512).transpose(1, 0, 2), 4, axis=1)
    return jnp.stack([bb, cc]).astype(BF16), lre8, lim8


def _s5_expanders():
    rows = np.arange(2 * 128)
    d, g, h = rows // 128, (rows // S5_GROUP) % 8, rows % S5_GROUP
    k = np.arange(2 * S5_GROUP)
    rep = ((d[:, None] == k[None, :] // S5_GROUP) & (h[:, None] == k[None, :] % S5_GROUP)).astype(np.float32)
    cols = np.arange(2 * 512)
    same = (g[:, None] == (cols[None, :] // S5_STATE) % 8).astype(np.float32)
    return rep, same


def _s5(p, pf, pr, pft, prt, bc, rep, same, lre8, lim8):
    nrow = 8 * SCAN_BLK
    const2 = lambda i: (0, 0)
    const3 = lambda i: (0, 0, 0)
    proj = lambda i: jnp.minimum(i, N_SCAN - 1)
    read = lambda i: jnp.clip(i - 2, 0, N_SCAN - 1)
    return pl.pallas_call(
        _s5_kernel,
        grid=(N_SCAN + 2,),
        in_specs=[
            pl.BlockSpec((BATCH, SCAN_BLK, S5_WIDTH), lambda i: (0, _fwd_blk(proj(i)), 0)),
            pl.BlockSpec((BATCH, SCAN_BLK, S5_WIDTH), lambda i: (0, _rev_blk(proj(i)), 0)),
            pl.BlockSpec(pf.shape, const2), pl.BlockSpec(pr.shape, const2),
            pl.BlockSpec(pft.shape, const2), pl.BlockSpec(prt.shape, const2),
            pl.BlockSpec(bc.shape, lambda i: (0, 0, 0, 0)), pl.BlockSpec(rep.shape, const2),
            pl.BlockSpec(same.shape, const2),
            pl.BlockSpec(lre8.shape, const3), pl.BlockSpec(lim8.shape, const3),
        ],
        out_specs=[
            pl.BlockSpec((BATCH, SCAN_BLK, S5_WIDTH), lambda i: (0, _fwd_blk(read(i)), 0)),
            pl.BlockSpec((BATCH, SCAN_BLK, S5_WIDTH), lambda i: (0, _rev_blk(read(i)), 0)),
        ],
        out_shape=[jax.ShapeDtypeStruct((BATCH, LT, S5_WIDTH), BF16)] * 2,
        scratch_shapes=[pltpu.VMEM((nrow, 4096), F32), pltpu.VMEM((nrow, 4096), F32),
                        pltpu.VMEM((8, 4096), F32),
                        pltpu.VMEM((4, 256, 1024), BF16), pltpu.VMEM((4, 1024, 256), BF16)],
        compiler_params=_cparams(("arbitrary",)),
        name="s5_scan",
    )(p, p, pf, pr, pft, prt, bc, rep, same, lre8, lim8)


def _na_bias_table(rel_bias):
    w = np.arange(GRID_W)
    cs = np.clip(w - NA_COLS // 2, 0, GRID_W - NA_COLS)
    cp = np.arange(GRID_W)
    valid = (cp[None, :] >= cs[:, None]) & (cp[None, :] < cs[:, None] + NA_COLS)
    dc = cp[None, :] - w[:, None] + (NA_COLS - 1)
    n_dc = 2 * NA_COLS - 1
    onehot = ((dc[None] == np.arange(n_dc)[:, None, None]) & valid[None]).astype(np.float32)
    n_dr = 2 * NA_ROWS - 1
    oh2 = np.zeros((2, n_dc, GRID_W, 2, GRID_W), np.float32)
    oh2[0, :, :, 0, :] = onehot
    oh2[1, :, :, 1, :] = onehot
    rb = rel_bias.astype(F32)
    rb2 = jnp.concatenate([rb[:, :n_dr - 1], rb[:, 1:]], axis=2).reshape(NA_HEADS * (n_dr - 1), 2 * n_dc)
    band2 = jnp.dot(rb2, jnp.asarray(oh2.reshape(2 * n_dc, 2 * GRID_W * GRID_W)), precision=HIGHEST)
    neg2 = np.tile(np.where(valid, 0.0, NEG).astype(np.float32), (1, 2))
    return band2.reshape(NA_HEADS, n_dr - 1, GRID_W, 2 * GRID_W) * LOG2E + jnp.asarray(neg2)


def _fill_bias_tables(band_ref, bias_ref):
    for off in range(8):
        for i2 in range(NA_ROWS // 2):
            bias_ref[off, :, :, i2 * 128:(i2 + 1) * 128] = band_ref[:, off + 2 * i2].reshape(NA_HEADS // 2, 128, 128)
    bias_ref[8] = jnp.full(bias_ref.shape[1:], NEG, F32)


NA_STEP_ROWS = TOK_BLK // GRID_W


def _na_kernel(q_ref, k_ref, v_ref, band_ref, o_ref, bias_ref):
    s_idx = pl.program_id(1)
    is_ctx = s_idx == 0

    @pl.when(jnp.logical_and(pl.program_id(0) == 0, s_idx == 0))
    def _():
        _fill_bias_tables(band_ref, bias_ref)

    lo = lax.broadcasted_iota(jnp.int32, (GRID_W, 128), 1) < HEAD_DIM
    nwin = NA_ROWS * GRID_W
    for rr in range(NA_STEP_ROWS):
        r = jnp.maximum(s_idx - 1, 0) * NA_STEP_ROWS + rr
        start = jnp.clip(r - NA_ROWS // 2, 0, SEQ // GRID_W - NA_ROWS)
        koff = pl.multiple_of(CTX_LEN + start * GRID_W, GRID_W)
        off = jnp.where(is_ctx, 8, start - r + NA_ROWS - 1)
        rows = slice(rr * GRID_W, (rr + 1) * GRID_W)
        pairs = range(NA_HEADS // 2)
        cols = [slice(hp * 128, (hp + 1) * 128) for hp in pairs]
        s_loc, s_ctx, m, p_loc, p_ctx, l, o = [], [], [], [], [], [], []
        for hp in pairs:
            qp = q_ref[0, rows, cols[hp]]
            zero = jnp.zeros_like(qp)
            q2 = jnp.concatenate([jnp.where(lo, qp, zero), jnp.where(lo, zero, qp)], axis=0)
            s_loc.append(_dot_nt(q2, k_ref[0, pl.ds(koff, nwin), cols[hp]]) + bias_ref[off, hp])
            s_ctx.append(_dot_nt(q2, k_ref[0, 0:CTX_LEN, cols[hp]]))
        for hp in pairs:
            m.append(jnp.maximum(jnp.max(s_loc[hp], axis=-1, keepdims=True),
                                 jnp.max(s_ctx[hp], axis=-1, keepdims=True)))
        for hp in pairs:
            pl_, pc_ = jnp.exp2(s_loc[hp] - m[hp]), jnp.exp2(s_ctx[hp] - m[hp])
            l.append(jnp.sum(pl_, axis=-1, keepdims=True) + jnp.sum(pc_, axis=-1, keepdims=True))
            p_loc.append(pl_.astype(BF16))
            p_ctx.append(pc_.astype(BF16))
        for hp in pairs:
            o.append(_dot(p_loc[hp], v_ref[0, pl.ds(koff, nwin), cols[hp]]) +
                     _dot(p_ctx[hp], v_ref[0, 0:CTX_LEN, cols[hp]]))
        for hp in pairs:
            oh = o[hp] / l[hp]
            o_ref[0, rows, cols[hp]] = jnp.where(lo, oh[:GRID_W], oh[GRID_W:]).astype(BF16)


def _na(p, band2):
    return pl.pallas_call(
        _na_kernel,
        grid=(BATCH, LT // TOK_BLK),
        in_specs=[
            pl.BlockSpec((1, TOK_BLK, 512), lambda b, s: (b, s, 2)),
            pl.BlockSpec((1, LT, 512), lambda b, s: (b, 0, 3)),
            pl.BlockSpec((1, LT, 512), lambda b, s: (b, 0, 4)),
            pl.BlockSpec(band2.shape, lambda b, s: (0, 0, 0, 0)),
        ],
        out_specs=pl.BlockSpec((1, TOK_BLK, 512), lambda b, s: (b, s, 0)),
        out_shape=jax.ShapeDtypeStruct((BATCH, LT, 512), BF16),
        scratch_shapes=[pltpu.VMEM((9, NA_HEADS // 2, 128, NA_ROWS * GRID_W), F32)],
        compiler_params=_cparams(("arbitrary", "arbitrary")),
        name="na_attn",
    )(p, p, p, band2)


def _gelu_tanh(x):
    return 0.5 * x * (1.0 + jnp.tanh(math.sqrt(2.0 / math.pi) * (x + 0.044715 * (x * x * x))))


def _silu(x):
    h = 0.5 * x
    return h + h * jnp.tanh(h)


def _finish(h, out, gt, pg):
    r = lax.rsqrt(jnp.mean(out * out, axis=-1, keepdims=True) + EPS)
    return h + (out * r) * (gt * pg)


def _even_out_kernel(yf_ref, yr_ref, u_ref, ga_ref, gb_ref, at_ref, hc_ref, hl_ref, mod_ref, d_ref, wg_ref,
                     bg_ref, wo_ref, pg_ref, o_ref):
    tb = pl.program_id(1)
    chains = range(PAIR)
    y = [_gelu_tanh(d_ref[...] * u_ref[h].astype(F32) + yf_ref[h].astype(F32) + yr_ref[h].astype(F32))
         for h in chains]
    glu = [_dot(y[h].astype(BF16), wg_ref[...]) for h in chains]
    y_a = [(y[h] * (0.5 + 0.5 * jnp.tanh(glu[h] + bg_ref[...])) * _silu(ga_ref[h].astype(F32))).astype(BF16)
           for h in chains]
    y_b = [(at_ref[h].astype(F32) * _silu(gb_ref[h].astype(F32))).astype(BF16) for h in chains]
    out = [_dot(y_a[h], wo_ref[0:512, :]) + _dot(y_b[h], wo_ref[512:1024, :]) for h in chains]
    for h in chains:
        res = jnp.where(tb == 0, hc_ref[h], hl_ref[h])
        o_ref[h] = _finish(res, out[h], _mod_row(mod_ref, h, tb)[:, 2 * D_MODEL:], pg_ref[...])


def _even_out(yf, yr, p, attn, ctx, x, mod, d_skip, w_glu, b_glu, w_out, post_g):
    tok = lambda c: pl.BlockSpec((PAIR, TOK_BLK, 512), lambda b, t: (b, t, c))
    full = lambda a: pl.BlockSpec(a.shape, lambda b, t: (0,) * a.ndim)
    d2, bg2, pg2 = d_skip.reshape(1, 512), b_glu.reshape(1, 512), post_g.reshape(1, D_MODEL)
    return pl.pallas_call(
        _even_out_kernel,
        grid=(BATCH // PAIR, LT // TOK_BLK),
        in_specs=[tok(0), tok(0), tok(0), tok(1), tok(5), tok(0)] + _cat_specs(D_MODEL) +
                 [full(mod), full(d2), full(w_glu), full(bg2), full(w_out), full(pg2)],
        out_specs=pl.BlockSpec((PAIR, TOK_BLK, D_MODEL), lambda b, t: (b, t, 0)),
        out_shape=jax.ShapeDtypeStruct((BATCH, LT, D_MODEL), F32),
        compiler_params=_cparams(("arbitrary", "arbitrary")),
        name="even_out",
    )(yf, yr, p, p, p, attn, ctx, x, mod, d2, w_glu, bg2, w_out, pg2)


def _softplus(z):
    return jnp.maximum(z, 0.0) + jnp.log1p(jnp.exp(-jnp.abs(z)))


def _lru_kernel(xf_ref, xfp_ref, xfn_ref, xr_ref, xrp_ref, xrn_ref, pf_ref, pr_ref, ph_ref, pft_ref, prt_ref,
                coef_ref, cb_ref, wg_ref, bg_ref, lam_ref, hf_ref, hr_ref, a0_ref, b0_ref, a1_ref, b1_ref,
                x0_ref, x1_ref, st_ref):
    i = pl.program_id(0)

    @pl.when(i == 0)
    def _():
        for ref in (st_ref, a0_ref, b0_ref, a1_ref, b1_ref, x0_ref, x1_ref):
            ref[...] = jnp.zeros_like(ref)

    nrow = 8 * SCAN_BLK
    half = nrow // 2
    blk = jnp.minimum(i, N_SCAN - 1)
    live = jnp.where(i >= 1, 1.0, 0.0).astype(F32)

    def halo(prev_ref, next_ref, b):
        prev_ok = jnp.logical_and(b != 0, b != CTX_SCAN)
        next_ok = jnp.logical_and(b != CTX_SCAN - 1, b != N_SCAN - 1)
        pv = prev_ref[...].reshape(BATCH * 16, LRU_WIDTH)
        nx = next_ref[...].reshape(BATCH * 16, LRU_WIDTH)
        return [jnp.where(prev_ok, pv, jnp.zeros_like(pv)), jnp.where(next_ok, nx, jnp.zeros_like(nx))]

    def step(x_w, x_r, a_w, b_w, a_s, b_s):
        vals = {}

        def unperm():
            hs = b_w[...].astype(BF16)
            hf_ref[...] = _dot(pft_ref[...], hs).astype(BF16).reshape(BATCH, SCAN_BLK, LRU_WIDTH)
            hr_ref[...] = _dot(prt_ref[...], hs).astype(BF16).reshape(BATCH, SCAN_BLK, LRU_WIDTH)

        def perm():
            hal = jnp.concatenate(halo(xfp_ref, xfn_ref, _fwd_blk(blk)) + halo(xrp_ref, xrn_ref, _rev_blk(blk)),
                                  axis=0)
            xh = _dot(ph_ref[...], hal)
            xp = (_dot(pf_ref[...], xf_ref[...].reshape(BATCH * SCAN_BLK, LRU_WIDTH)) +
                  _dot(pr_ref[...], xr_ref[...].reshape(BATCH * SCAN_BLK, LRU_WIDTH)))
            x_w[0:16, :] = xh[:16]
            x_w[16:16 + nrow, :] = xp
            x_w[16 + nrow:, :] = xh[16:]

        def conv(h):
            xc = jnp.zeros((SCAN_BLK // 2, 8, LRU_WIDTH), F32) + cb_ref[...]
            for s in range(5):
                r0 = h * half + s * 8
                xc = xc + x_r[r0:r0 + half, :].reshape(SCAN_BLK // 2, 8, LRU_WIDTH) * coef_ref[s]
            vals['xc%d' % h] = xc.reshape(half, LRU_WIDTH) * live

        def gates(h):
            is_f = (lax.broadcasted_iota(jnp.int32, (half, 1), 0) & 7) < 4
            xcb = vals['xc%d' % h].astype(BF16)
            zero = jnp.zeros_like(xcb[:, :256])
            pre_r, pre_i = [], []
            for c in range(2):
                xt = xcb[:, c * 256:(c + 1) * 256]
                lhs = jnp.concatenate([jnp.where(is_f, xt, zero), jnp.where(is_f, zero, xt)], axis=1)
                pre = _dot(lhs, wg_ref[c])
                pre_r.append(pre[:, :256])
                pre_i.append(pre[:, 256:])
            vals['pr%d' % h] = jnp.concatenate(pre_r, axis=1)
            vals['pi%d' % h] = jnp.concatenate(pre_i, axis=1)

        def elem(h):
            rows = slice(h * half, (h + 1) * half)
            shp = (SCAN_BLK // 2, 8, LRU_WIDTH)
            t_r = jnp.tanh(vals['pr%d' % h].reshape(shp) + bg_ref[:, :LRU_WIDTH])
            t_i = jnp.tanh(vals['pi%d' % h].reshape(shp) + bg_ref[:, LRU_WIDTH:])
            c = (-0.5 * LRU_C) * _softplus(-lam_ref[...])
            u = 1.0 + t_r
            a = jnp.exp2((c * LOG2E) * u)
            a_w[rows, :] = a.reshape(half, LRU_WIDTH)
            one_m_a2 = jnp.maximum(jnp.tanh((-c) * u) * (a * a + 1.0), 1e-37)
            mult = one_m_a2 * lax.rsqrt(one_m_a2)
            b = (mult * vals['xc%d' % h].reshape(shp)) * (0.5 + 0.5 * t_i)
            b_w[rows, :] = b.reshape(half, LRU_WIDTH)

        def scan(t0):
            h = st_ref[...]
            for t in range(t0, t0 + 16):
                rows = slice(t * 8, (t + 1) * 8)
                h = a_s[rows, :] * h + b_s[rows, :]
                b_s[rows, :] = h
            st_ref[...] = h

        for piece in (functools.partial(conv, 0), unperm, functools.partial(conv, 1), functools.partial(scan, 0),
                      functools.partial(gates, 0), perm, functools.partial(gates, 1), functools.partial(scan, 16),
                      functools.partial(elem, 0), functools.partial(scan, 32), functools.partial(elem, 1),
                      functools.partial(scan, 48)):
            piece()

    @pl.when(i % 2 == 0)
    def _():
        step(x0_ref, x1_ref, a1_ref, b1_ref, a0_ref, b0_ref)

    @pl.when(i % 2 == 1)
    def _():
        step(x1_ref, x0_ref, a0_ref, b0_ref, a1_ref, b1_ref)


def _lru_gate_weights(w_a, w_x):
    w = jnp.stack([w_a, w_x], axis=1).reshape(2, 2, 2, 4, LRU_BLOCK, LRU_BLOCK)
    same = jnp.asarray(np.eye(4, dtype=np.float32))
    w = w.transpose(2, 0, 3, 4, 1, 5)[:, :, :, :, :, None] * same[:, None, None, :, None]
    return (0.5 * w).reshape(2, 512, 512).astype(BF16)


def _lru(p, pf, pr, ph, pft, prt, coef, conv_b, wg, bg, lam):
    nrow = 8 * SCAN_BLK
    full = lambda a: pl.BlockSpec(a.shape, lambda i: (0,) * a.ndim)
    proj = lambda i: jnp.minimum(i, N_SCAN - 1)
    read = lambda i: jnp.clip(i - 3, 0, N_SCAN - 1)
    cur = lambda f: pl.BlockSpec((BATCH, SCAN_BLK, LRU_WIDTH), lambda i: (0, f(proj(i)), 0))
    prev = lambda f: pl.BlockSpec((BATCH, 16, LRU_WIDTH), lambda i: (0, jnp.maximum(f(proj(i)) * 4 - 1, 0), 0))
    nxt = lambda f: pl.BlockSpec((BATCH, 16, LRU_WIDTH),
                                 lambda i: (0, jnp.minimum(f(proj(i)) * 4 + 4, LT // 16 - 1), 0))
    cb2 = conv_b.reshape(1, LRU_WIDTH)
    return pl.pallas_call(
        _lru_kernel,
        grid=(N_SCAN + 3,),
        in_specs=[cur(_fwd_blk), prev(_fwd_blk), nxt(_fwd_blk), cur(_rev_blk), prev(_rev_blk), nxt(_rev_blk),
                  full(pf), full(pr), full(ph), full(pft), full(prt), full(coef), full(cb2), full(wg), full(bg),
                  full(lam)],
        out_specs=[
            pl.BlockSpec((BATCH, SCAN_BLK, LRU_WIDTH), lambda i: (0, _fwd_blk(read(i)), 0)),
            pl.BlockSpec((BATCH, SCAN_BLK, LRU_WIDTH), lambda i: (0, _rev_blk(read(i)), 0)),
        ],
        out_shape=[jax.ShapeDtypeStruct((BATCH, LT, LRU_WIDTH), BF16)] * 2,
        scratch_shapes=([pltpu.VMEM((nrow, LRU_WIDTH), F32)] * 4 + [pltpu.VMEM((nrow + 32, LRU_WIDTH), F32)] * 2 +
                        [pltpu.VMEM((8, LRU_WIDTH), F32)]),
        compiler_params=_cparams(("arbitrary",)),
        name="lru_scan",
    )(p, p, p, p, p, p, pf, pr, ph, pft, prt, coef, cb2, wg, bg, lam)


def _norm_rope(xb, g, grot, cos, sin, ones_blk, swap):
    x = xb.astype(F32)
    xrot = _dot(xb, swap)
    ms = _dot((x * x).astype(BF16), ones_blk) * (1.0 / HEAD_DIM)
    rs = lax.rsqrt(ms + EPS)
    return rs * ((x * g) * cos + (xrot * grot) * sin)


def _gqa_kernel(q_ref, k_ref, v_ref, cq_ref, sq_ref, ck_ref, sk_ref, g_ref, ones_ref, swap_ref,
                o_ref, kn_ref, vlo_ref, vhi_ref):
    qb = pl.program_id(1)
    ones_blk = ones_ref[...]
    swap = swap_ref[...]

    @pl.when(qb == 0)
    def _():
        kn = _norm_rope(k_ref[0], g_ref[2:3, :], g_ref[3:4, :], ck_ref[...], sk_ref[...], ones_blk, swap)
        kn_ref[...] = kn.astype(BF16)
        v = v_ref[0]
        lo_k = lax.broadcasted_iota(jnp.int32, (LT, 128), 1) < HEAD_DIM
        one = jnp.ones_like(v)
        vlo_ref[...] = jnp.where(lo_k, v, one)
        vhi_ref[...] = jnp.where(lo_k, one, v)

    lo = lax.broadcasted_iota(jnp.int32, (Q_BLK, 128), 1) < HEAD_DIM
    kn = kn_ref[...]
    cos = cq_ref[...]
    sin = sq_ref[...]
    n_pb = GQA_HEADS // 2

    q_blocks = [q_ref[:, pb * 128:(pb + 1) * 128] for pb in range(n_pb)]
    x_rot = [_dot(qb_, swap) for qb_ in q_blocks]
    msq = [_dot((qb_.astype(F32) * qb_.astype(F32)).astype(BF16), ones_blk) for qb_ in q_blocks]
    qs_all = []
    for pb in range(n_pb):
        rs = lax.rsqrt(msq[pb] * (1.0 / HEAD_DIM) + EPS) * (HEAD_DIM ** -0.5 * LOG2E)
        qn = (rs * ((q_blocks[pb].astype(F32) * g_ref[0:1, :]) * cos + (x_rot[pb] * g_ref[1:2, :]) * sin))
        qn = qn.astype(BF16)
        zero = jnp.zeros_like(qn)
        qs_all.append(jnp.concatenate([jnp.where(lo, qn, zero), jnp.where(lo, zero, qn)], axis=0))

    def scores(pb):
        return _dot_nt(qs_all[pb], kn)

    s_next = scores(0)
    for pb in range(n_pb):
        cols = slice(pb * 128, (pb + 1) * 128)
        s = s_next
        if pb + 1 < n_pb:
            s_next = scores(pb + 1)
        m = jnp.max(s, axis=-1, keepdims=True)
        p = jnp.exp2(s - m).astype(BF16)
        o_lo = _dot(p[:Q_BLK], vlo_ref[...])
        o_hi = _dot(p[Q_BLK:], vhi_ref[...])
        num = jnp.where(lo, o_lo, o_hi)
        den = pltpu.roll(jnp.where(lo, o_hi, o_lo), HEAD_DIM, axis=1)
        o_ref[0, :, cols] = (num / den).astype(BF16)


def _gqa(p, cos128, sin128, gvec, ones_blk, swap):
    nqb = SEQ // Q_BLK
    cos_lat, sin_lat = cos128[CTX_LEN:], sin128[CTX_LEN:]
    return pl.pallas_call(
        _gqa_kernel,
        grid=(BATCH, nqb),
        in_specs=[
            pl.BlockSpec((pl.Squeezed(), pl.Element(Q_BLK), pl.Element(512)),
                         lambda b, q: (b, pl.multiple_of(CTX_LEN + q * Q_BLK, CTX_LEN), 1024)),
            pl.BlockSpec((1, LT, 128), lambda b, q: (b, 0, 16)),
            pl.BlockSpec((1, LT, 128), lambda b, q: (b, 0, 17)),
            pl.BlockSpec((Q_BLK, 128), lambda b, q: (q, 0)),
            pl.BlockSpec((Q_BLK, 128), lambda b, q: (q, 0)),
            pl.BlockSpec((LT, 128), lambda b, q: (0, 0)),
            pl.BlockSpec((LT, 128), lambda b, q: (0, 0)),
            pl.BlockSpec((8, 128), lambda b, q: (0, 0)),
            pl.BlockSpec((128, 128), lambda b, q: (0, 0)),
            pl.BlockSpec((128, 128), lambda b, q: (0, 0)),
        ],
        out_specs=pl.BlockSpec((1, Q_BLK, 512), lambda b, q: (b, q, 0)),
        out_shape=jax.ShapeDtypeStruct((BATCH, SEQ, 512), BF16),
        scratch_shapes=[pltpu.VMEM((LT, 128), BF16)] * 3,
        compiler_params=_cparams(("arbitrary", "arbitrary")),
        name="gqa_attn",
    )(p, p, p, cos_lat, sin_lat, cos128, sin128, gvec, ones_blk, swap)


def _odd_out_kernel(hf_ref, hr_ref, gc_ref, gd_ref, at_ref, h_ref, mod_ref, wo_ref, pg_ref, o_ref):
    chains = range(PAIR)
    y_c = [((hf_ref[h].astype(F32) + hr_ref[h].astype(F32)) * _silu(gc_ref[h].astype(F32))).astype(BF16)
           for h in chains]
    y_d = [(at_ref[h].astype(F32) * _silu(gd_ref[h].astype(F32))).astype(BF16) for h in chains]
    out = [_dot(y_c[h], wo_ref[0:512, :]) + _dot(y_d[h], wo_ref[512:1024, :]) for h in chains]
    for h in chains:
        gt = mod_ref[pl.ds(pl.program_id(0) * PAIR + h, 1), 2 * D_MODEL:]
        o_ref[h] = _finish(h_ref[h], out[h], gt, pg_ref[...])


def _odd_out(hf, hr, p, attn, h_cat, mod, w_out, post_g):
    cb = CTX_LEN // TOK_BLK
    cat = lambda c: pl.BlockSpec((PAIR, TOK_BLK, 512), lambda b, t: (b, t + cb, c))
    full = lambda a: pl.BlockSpec(a.shape, lambda b, t: (0,) * a.ndim)
    pg2 = post_g.reshape(1, D_MODEL)
    return pl.pallas_call(
        _odd_out_kernel,
        grid=(BATCH // PAIR, SEQ // TOK_BLK),
        in_specs=[cat(0), cat(0), cat(1), cat(3),
                  pl.BlockSpec((PAIR, TOK_BLK, 512), lambda b, t: (b, t, 0)),
                  pl.BlockSpec((PAIR, TOK_BLK, D_MODEL), lambda b, t: (b, t + cb, 0)),
                  full(mod), full(w_out), full(pg2)],
        out_specs=pl.BlockSpec((PAIR, TOK_BLK, D_MODEL), lambda b, t: (b, t, 0)),
        out_shape=jax.ShapeDtypeStruct((BATCH, SEQ, D_MODEL), F32),
        compiler_params=_cparams(("arbitrary", "arbitrary")),
        name="odd_out",
    )(hf, hr, p, p, attn, h_cat, mod, w_out, pg2)


def _pair_swap_matrix(n):
    r = np.zeros((n, n), np.float32)
    r[np.arange(1, n, 2), np.arange(0, n, 2)] = -1.0
    r[np.arange(0, n, 2), np.arange(1, n, 2)] = 1.0
    return r


def _interleave_kv_groups(w, axis):
    if axis == 0:
        return w.reshape(2, 4, HEAD_DIM, w.shape[1]).transpose(1, 0, 2, 3).reshape(w.shape)
    return w.reshape(w.shape[0], 2, 4, HEAD_DIM).transpose(0, 2, 1, 3).reshape(w.shape)


def _odd_w_in(w):
    w = w.astype(BF16)
    x, gc = w[:, 0:512], w[:, 512:1024]
    q = _interleave_kv_groups(w[:, 1024:1536], 1)
    k, v = w[:, 1536:1664], w[:, 1664:1792]
    gd = _interleave_kv_groups(w[:, 1792:2304], 1)
    return jnp.concatenate([x, gc, q, gd, k, v], axis=1)


def _rope_tables():
    t = np.arange(SEQ)
    row = (t // GRID_W).astype(np.float32)
    col = (t % GRID_W).astype(np.float32)
    half = HEAD_DIM // 2
    inv = (ROPE_THETA ** (-np.arange(0, half, 2, dtype=np.float32) / half)).astype(np.float32)
    ang = np.concatenate([row[:, None] * inv, col[:, None] * inv], axis=-1)
    cos = np.repeat(np.cos(ang), 2, axis=-1)
    sin = np.repeat(np.sin(ang), 2, axis=-1)
    cos = np.concatenate([np.ones((CTX_LEN, HEAD_DIM), np.float32), cos], axis=0)
    sin = np.concatenate([np.zeros((CTX_LEN, HEAD_DIM), np.float32), sin], axis=0)
    return np.tile(cos, (1, 2)).astype(np.float32), np.tile(sin, (1, 2)).astype(np.float32)


def _swap_pairs_vec(g):
    g2 = g.reshape(-1, 2)
    return jnp.stack([g2[:, 1], g2[:, 0]], axis=-1).reshape(g.shape)


def kernel(x, c, ctx, c_ctx, ada_w, ada_b, pre_g, post_g, ev_w_in, ev_w_out, s5_lam_re, s5_lam_im, s5_log_dt,
           s5_b_re, s5_b_im, s5_c_re, s5_c_im, s5_d, s5_w_glu, s5_b_glu, na_rel_bias, od_w_in, od_w_out,
           lru_conv_w, lru_conv_b, lru_lam, lru_w_a, lru_b_a, lru_w_x, lru_b_x, gqa_q_norm, gqa_k_norm):
    pf_np, pr_np = _scan_perms()
    pf, pr = jnp.asarray(pf_np, BF16), jnp.asarray(pr_np, BF16)
    pft, prt = jnp.asarray(pf_np.T, BF16), jnp.asarray(pr_np.T, BF16)
    ph = jnp.asarray(_halo_perm(), BF16)

    c8 = jnp.concatenate([c, c_ctx[None], jnp.zeros((3, D_MODEL), F32)], axis=0)
    mod = _adaln(c8, ada_w, ada_b)

    col_scale = np.ones((EVEN_IN,), np.float32)
    col_scale[1024:1536] = HEAD_DIM ** -0.5 * LOG2E
    p0 = _inproj((ctx, x), mod[0], pre_g[0], (ev_w_in[0] * jnp.asarray(col_scale)).astype(BF16))
    bc, lre8, lim8 = _s5_weights(s5_lam_re[0], s5_lam_im[0], s5_log_dt[0], s5_b_re[0], s5_b_im[0],
                                 s5_c_re[0], s5_c_im[0])
    rep_np, same_np = _s5_expanders()
    yf, yr = _s5(p0, pf, pr, pft, prt, bc, jnp.asarray(rep_np, BF16), jnp.asarray(same_np, BF16), lre8, lim8)
    attn0 = _na(p0, _na_bias_table(na_rel_bias[0]))
    h1 = _even_out(yf, yr, p0, attn0, ctx, x, mod[0], s5_d[0], (0.5 * s5_w_glu[0]).astype(BF16), 0.5 * s5_b_glu[0],
                   ev_w_out[0].astype(BF16), post_g[0])

    p1 = _inproj(h1, mod[1], pre_g[1], _odd_w_in(od_w_in[0]))
    wg = _lru_gate_weights(lru_w_a[0], lru_w_x[0])
    bg = jnp.repeat(0.5 * jnp.concatenate([lru_b_a[0], lru_b_x[0]], axis=1), 4, axis=0)
    lam8 = jnp.repeat(lru_lam[0], 4, axis=0)
    hf, hr = _lru(p1, pf, pr, ph, pft, prt, _conv_coef(lru_conv_w[0]), lru_conv_b[0], wg, bg, lam8)
    cos_np, sin_np = _rope_tables()
    gq, gk = gqa_q_norm[0], gqa_k_norm[0]
    gvec = jnp.stack([jnp.tile(gq, 2), jnp.tile(_swap_pairs_vec(gq), 2),
                      jnp.tile(gk, 2), jnp.tile(_swap_pairs_vec(gk), 2)] + [jnp.zeros((128,), F32)] * 4)
    ones_np = np.kron(np.eye(2, dtype=np.float32), np.ones((HEAD_DIM, HEAD_DIM), np.float32))
    attn1 = _gqa(p1, jnp.asarray(cos_np), jnp.asarray(sin_np), gvec, jnp.asarray(ones_np, BF16),
                 jnp.asarray(_pair_swap_matrix(128), BF16))
    w_out1 = od_w_out[0].astype(BF16)
    w_out1 = jnp.concatenate([w_out1[:512], _interleave_kv_groups(w_out1[512:], 0)], axis=0)
    return _odd_out(hf, hr, p1, attn1, h1, mod[1], w_out1, post_g[1])
```

```python
import functools
import math

import numpy as np
import jax
import jax.numpy as jnp
from jax import lax
from jax.experimental import pallas as pl
from jax.experimental.pallas import tpu as pltpu

F32 = jnp.float32
BF16 = jnp.bfloat16
HIGHEST = lax.Precision.HIGHEST

D_MODEL = 1024
BATCH = 4
SEQ = 4096
GRID_W = 64
CTX_LEN = 256
LT = CTX_LEN + SEQ
HEAD_DIM = 64
EPS = 1e-6
S5_WIDTH = 512
S5_GROUP = 16
S5_GROUPS = 32
S5_STATE = 64
NA_HEADS = 8
NA_ROWS = 8
NA_COLS = 16
LRU_WIDTH = 512
LRU_BLOCKS = 8
LRU_BLOCK = 64
LRU_C = 8.0
GQA_HEADS = 8
ROPE_THETA = 10000.0
EVEN_IN = 3072
ODD_IN = 2304

TOK_BLK = 256
SCAN_BLK = 64
N_SCAN = LT // SCAN_BLK
CTX_SCAN = CTX_LEN // SCAN_BLK
Q_BLK = 512
NEG = -1e30
LOG2E = math.log2(math.e)
VMEM_LIMIT = 56 * 1024 * 1024


def _cparams(sem):
    return pltpu.CompilerParams(dimension_semantics=sem, vmem_limit_bytes=VMEM_LIMIT)


def _dot(a, b):
    return jnp.dot(a, b, preferred_element_type=F32)


def _dot_nt(a, b):
    return lax.dot_general(a, b, (((1,), (1,)), ((), ())), preferred_element_type=F32)


def _adaln_kernel(c_ref, w_ref, b_ref, o_ref):
    c = c_ref[...]
    s = c * jax.nn.sigmoid(c)
    o_ref[0] = jnp.dot(s, w_ref[0], preferred_element_type=F32, precision=HIGHEST) + b_ref[0]


def _adaln(c8, ada_w, ada_b):
    depth = ada_w.shape[0]
    nb = 3 * D_MODEL // 1024
    return pl.pallas_call(
        _adaln_kernel,
        grid=(depth, nb),
        in_specs=[
            pl.BlockSpec((8, D_MODEL), lambda i, n: (0, 0)),
            pl.BlockSpec((1, D_MODEL, 1024), lambda i, n: (i, 0, n)),
            pl.BlockSpec((1, 1, 1024), lambda i, n: (i, 0, n)),
        ],
        out_specs=pl.BlockSpec((1, 8, 1024), lambda i, n: (i, 0, n)),
        out_shape=jax.ShapeDtypeStruct((depth, 8, 3 * D_MODEL), F32),
        compiler_params=_cparams(("arbitrary", "arbitrary")),
        name="adaln",
    )(c8, ada_w, ada_b.reshape(depth, 1, 3 * D_MODEL))


PAIR = 4


def _cat_specs(width):
    return [pl.BlockSpec((PAIR, TOK_BLK, width), lambda b, t: (b, 0, 0)),
            pl.BlockSpec((PAIR, TOK_BLK, width), lambda b, t: (b, jnp.maximum(t - 1, 0), 0))]


def _mod_row(mod_ref, h, tb):
    row = jnp.where(tb == 0, BATCH, pl.program_id(0) * PAIR + h)
    return mod_ref[pl.ds(row, 1), :]


def _inproj_kernel(*refs, two_src):
    tb = pl.program_id(1)
    if two_src:
        c_ref, x_ref, mod_ref, g_ref, w_ref, o_ref = refs
    else:
        x_ref, mod_ref, g_ref, w_ref, o_ref = refs
    ys = []
    for h in range(PAIR):
        x = jnp.where(tb == 0, c_ref[h], x_ref[h]) if two_src else x_ref[h]
        r = lax.rsqrt(jnp.mean(x * x, axis=-1, keepdims=True) + EPS)
        m = _mod_row(mod_ref, h, tb)
        y = (x * r) * g_ref[...]
        ys.append((y * (1.0 + m[:, D_MODEL:2 * D_MODEL]) + m[:, :D_MODEL]).astype(BF16))
    for h in range(PAIR):
        o_ref[h] = _dot(ys[h], w_ref[...]).astype(BF16)


def _inproj(src, mod, g, w_bf):
    n = w_bf.shape[1]
    two_src = isinstance(src, tuple)
    if two_src:
        src_specs = _cat_specs(D_MODEL)
    else:
        src_specs = [pl.BlockSpec((PAIR, TOK_BLK, D_MODEL), lambda b, t: (b, t, 0))]
        src = (src,)
    return pl.pallas_call(
        functools.partial(_inproj_kernel, two_src=two_src),
        grid=(BATCH // PAIR, LT // TOK_BLK),
        in_specs=src_specs + [
            pl.BlockSpec((8, 3 * D_MODEL), lambda b, t: (0, 0)),
            pl.BlockSpec((1, D_MODEL), lambda b, t: (0, 0)),
            pl.BlockSpec((D_MODEL, n), lambda b, t: (0, 0)),
        ],
        out_specs=pl.BlockSpec((PAIR, TOK_BLK, n), lambda b, t: (b, t, 0)),
        out_shape=jax.ShapeDtypeStruct((BATCH, LT, n), BF16),
        compiler_params=_cparams(("arbitrary", "arbitrary")),
        name="inproj",
    )(*src, mod, g.reshape(1, D_MODEL), w_bf)


def _fwd_blk(i):
    return i


def _rev_blk(i):
    return jnp.where(i < CTX_SCAN, CTX_SCAN - 1 - i, N_SCAN + CTX_SCAN - 1 - i)


def _scan_perms():
    t = SCAN_BLK
    pf = np.zeros((8 * t, BATCH * t), np.float32)
    pr = np.zeros((8 * t, BATCH * t), np.float32)
    for tt in range(t):
        for b in range(BATCH):
            pf[tt * 8 + b, b * t + tt] = 1.0
            pr[tt * 8 + 4 + b, b * t + (t - 1 - tt)] = 1.0
    return pf, pr


def _halo_perm():
    ph = np.zeros((32, 16 * 4 * BATCH), np.float32)
    for b in range(BATCH):
        ph[1 * 8 + b, 0 * 64 + b * 16 + 15] = 1.0
        ph[2 * 8 + b, 1 * 64 + b * 16 + 0] = 1.0
        ph[3 * 8 + b, 1 * 64 + b * 16 + 1] = 1.0
        ph[0 * 8 + 4 + b, 3 * 64 + b * 16 + 1] = 1.0
        ph[1 * 8 + 4 + b, 3 * 64 + b * 16 + 0] = 1.0
        ph[2 * 8 + 4 + b, 2 * 64 + b * 16 + 15] = 1.0
    return ph


def _conv_coef(conv_w):
    zero = jnp.zeros((1, LRU_WIDTH), conv_w.dtype)
    fwd = jnp.concatenate([zero, conv_w], axis=0)
    rev = jnp.concatenate([conv_w[::-1], zero], axis=0)
    return jnp.concatenate([jnp.broadcast_to(fwd[:, None], (5, 4, LRU_WIDTH)),
                            jnp.broadcast_to(rev[:, None], (5, 4, LRU_WIDTH))], axis=1)


def _s5_kernel(uf_ref, ur_ref, pf_ref, pr_ref, pft_ref, prt_ref, bc_ref, rep_ref, same_ref, lre_ref, lim_ref,
               yf_ref, yr_ref, buf0_ref, buf1_ref, st_ref, bcat_ref, ccat_ref):
    i = pl.program_id(0)

    @pl.when(i == 0)
    def _():
        st_ref[...] = jnp.zeros_like(st_ref)
        buf0_ref[...] = jnp.zeros_like(buf0_ref)
        buf1_ref[...] = jnp.zeros_like(buf1_ref)
        same = same_ref[...].astype(F32)
        for j in range(4):
            bcat_ref[j] = (_dot(rep_ref[...], bc_ref[0, j]) * same).astype(BF16)
            ccat_ref[j] = (_dot(rep_ref[...], bc_ref[1, j]) * same).T.astype(BF16)

    nrow = 8 * SCAN_BLK

    def step(buf_a, buf_b):
        vals = {}

        def perm():
            uf = uf_ref[...].reshape(BATCH * SCAN_BLK, S5_WIDTH)
            ur = ur_ref[...].reshape(BATCH * SCAN_BLK, S5_WIDTH)
            vals['u_f'] = _dot(pf_ref[...], uf).astype(BF16)
            vals['u_r'] = _dot(pr_ref[...], ur).astype(BF16)

        def readout(j):
            is_f = (lax.broadcasted_iota(jnp.int32, (nrow, 128), 0) & 7) < 4
            yj = _dot(buf_a[:, j * 1024:(j + 1) * 1024].astype(BF16), ccat_ref[j])
            vals['y%d' % j] = jnp.where(is_f, yj[:, :128], yj[:, 128:]).astype(BF16)

        def project(j):
            lhs = jnp.concatenate([vals['u_f'][:, j * 128:(j + 1) * 128], vals['u_r'][:, j * 128:(j + 1) * 128]],
                                  axis=1)
            buf_a[:, j * 1024:(j + 1) * 1024] = _dot(lhs, bcat_ref[j])

        def unperm(p_ref, o_ref):
            yp = jnp.concatenate([vals['y%d' % j] for j in range(4)], axis=1)
            o_ref[...] = _dot(p_ref[...], yp).astype(BF16).reshape(BATCH, SCAN_BLK, S5_WIDTH)

        mxu = [perm]
        for j in range(4):
            mxu += [functools.partial(readout, j), functools.partial(project, j)]
        mxu += [functools.partial(unperm, pft_ref, yf_ref), functools.partial(unperm, prt_ref, yr_ref)]

        def scan(j, t0):
            c_re = slice(j * 1024, j * 1024 + 512)
            c_im = slice(j * 1024 + 512, (j + 1) * 1024)
            lre = lre_ref[j]
            lim = lim_ref[j]
            hre = st_ref[:, c_re]
            him = st_ref[:, c_im]
            for t in range(t0, t0 + 8):
                rows = slice(t * 8, (t + 1) * 8)
                nre = lre * hre - lim * him + buf_b[rows, c_re]
                nim = lre * him + lim * hre + buf_b[rows, c_im]
                buf_b[rows, c_re] = nre
                buf_b[rows, c_im] = nim
                hre, him = nre, nim
            st_ref[:, c_re] = hre
            st_ref[:, c_im] = him

        vpu = [functools.partial(scan, j, t0) for j in range(4) for t0 in range(0, SCAN_BLK, 8)]

        per_piece = [2, 3, 3, 3, 3, 3, 3, 3, 3, 3, 3]
        for piece, n_scan in zip(mxu, per_piece):
            piece()
            for _ in range(n_scan):
                vpu.pop(0)()

    @pl.when(i % 2 == 0)
    def _():
        step(buf0_ref, buf1_ref)

    @pl.when(i % 2 == 1)
    def _():
        step(buf1_ref, buf0_ref)


def _s5_weights(lam_re, lam_im, log_dt, b_re, b_im, c_re, c_im):
    a = lam_re.astype(F32) * jnp.exp(log_dt.astype(F32))[..., None]
    b = lam_im.astype(F32) * jnp.exp(log_dt.astype(F32))[..., None]
    lbr = jnp.exp(a) * jnp.cos(b)
    lbi = jnp.exp(a) * jnp.sin(b)
    nr = jnp.expm1(a) * jnp.cos(b) - 2.0 * jnp.sin(0.5 * b) ** 2
    d2 = lam_re * lam_re + lam_im * lam_im
    qr = (nr * lam_re + lbi * lam_im) / d2
    qi = (lbi * lam_re - nr * lam_im) / d2
    bbr = qr[..., None] * b_re - qi[..., None] * b_im
    bbi = qr[..., None] * b_im + qi[..., None] * b_re
    bb = jnp.stack([bbr, bbi], axis=1).reshape(2, 2, 4, 8, S5_STATE, S5_GROUP)
    bb = bb.transpose(2, 0, 5, 1, 3, 4).reshape(4, 2 * S5_GROUP, 2 * 512)
    cc = jnp.stack([c_re.astype(F32), -c_im.astype(F32)], axis=1).reshape(2, 2, 4, 8, S5_GROUP, S5_STATE)
    cc = cc.transpose(2, 0, 4, 1, 3, 5).reshape(4, 2 * S5_GROUP, 2 * 512)
    lre8 = jnp.repeat(lbr.reshape(2, 4, 512).transpose(1, 0, 2), 4, axis=1)
    lim8 = jnp.repeat(lbi.reshape(2, 4, 512).transpose(1, 0, 2), 4, axis=1)
    return jnp.stack([bb, cc]).astype(BF16), lre8, lim8


def _s5_expanders():
    rows = np.arange(2 * 128)
    d, g, h = rows // 128, (rows // S5_GROUP) % 8, rows % S5_GROUP
    k = np.arange(2 * S5_GROUP)
    rep = ((d[:, None] == k[None, :] // S5_GROUP) & (h[:, None] == k[None, :] % S5_GROUP)).astype(np.float32)
    cols = np.arange(2 * 512)
    same = (g[:, None] == (cols[None, :] // S5_STATE) % 8).astype(np.float32)
    return rep, same


def _s5(p, pf, pr, pft, prt, bc, rep, same, lre8, lim8):
    nrow = 8 * SCAN_BLK
    const2 = lambda i: (0, 0)
    const3 = lambda i: (0, 0, 0)
    proj = lambda i: jnp.minimum(i, N_SCAN - 1)
    read = lambda i: jnp.clip(i - 2, 0, N_SCAN - 1)
    return pl.pallas_call(
        _s5_kernel,
        grid=(N_SCAN + 2,),
        in_specs=[
            pl.BlockSpec((BATCH, SCAN_BLK, S5_WIDTH), lambda i: (0, _fwd_blk(proj(i)), 0)),
            pl.BlockSpec((BATCH, SCAN_BLK, S5_WIDTH), lambda i: (0, _rev_blk(proj(i)), 0)),
            pl.BlockSpec(pf.shape, const2), pl.BlockSpec(pr.shape, const2),
            pl.BlockSpec(pft.shape, const2), pl.BlockSpec(prt.shape, const2),
            pl.BlockSpec(bc.shape, lambda i: (0, 0, 0, 0)), pl.BlockSpec(rep.shape, const2),
            pl.BlockSpec(same.shape, const2),
            pl.BlockSpec(lre8.shape, const3), pl.BlockSpec(lim8.shape, const3),
        ],
        out_specs=[
            pl.BlockSpec((BATCH, SCAN_BLK, S5_WIDTH), lambda i: (0, _fwd_blk(read(i)), 0)),
            pl.BlockSpec((BATCH, SCAN_BLK, S5_WIDTH), lambda i: (0, _rev_blk(read(i)), 0)),
        ],
        out_shape=[jax.ShapeDtypeStruct((BATCH, LT, S5_WIDTH), BF16)] * 2,
        scratch_shapes=[pltpu.VMEM((nrow, 4096), F32), pltpu.VMEM((nrow, 4096), F32),
                        pltpu.VMEM((8, 4096), F32),
                        pltpu.VMEM((4, 256, 1024), BF16), pltpu.VMEM((4, 1024, 256), BF16)],
        compiler_params=_cparams(("arbitrary",)),
        name="s5_scan",
    )(p, p, pf, pr, pft, prt, bc, rep, same, lre8, lim8)


def _na_bias_table(rel_bias):
    w = np.arange(GRID_W)
    cs = np.clip(w - NA_COLS // 2, 0, GRID_W - NA_COLS)
    cp = np.arange(GRID_W)
    valid = (cp[None, :] >= cs[:, None]) & (cp[None, :] < cs[:, None] + NA_COLS)
    dc = cp[None, :] - w[:, None] + (NA_COLS - 1)
    n_dc = 2 * NA_COLS - 1
    onehot = ((dc[None] == np.arange(n_dc)[:, None, None]) & valid[None]).astype(np.float32)
    n_dr = 2 * NA_ROWS - 1
    oh2 = np.zeros((2, n_dc, GRID_W, 2, GRID_W), np.float32)
    oh2[0, :, :, 0, :] = onehot
    oh2[1, :, :, 1, :] = onehot
    rb = rel_bias.astype(F32)
    rb2 = jnp.concatenate([rb[:, :n_dr - 1], rb[:, 1:]], axis=2).reshape(NA_HEADS * (n_dr - 1), 2 * n_dc)
    band2 = jnp.dot(rb2, jnp.asarray(oh2.reshape(2 * n_dc, 2 * GRID_W * GRID_W)), precision=HIGHEST)
    neg2 = np.tile(np.where(valid, 0.0, NEG).astype(np.float32), (1, 2))
    return band2.reshape(NA_HEADS, n_dr - 1, GRID_W, 2 * GRID_W) * LOG2E + jnp.asarray(neg2)


def _fill_bias_tables(band_ref, bias_ref):
    for off in range(8):
        for i2 in range(NA_ROWS // 2):
            bias_ref[off, :, :, i2 * 128:(i2 + 1) * 128] = band_ref[:, off + 2 * i2].reshape(NA_HEADS // 2, 128, 128)
    bias_ref[8] = jnp.full(bias_ref.shape[1:], NEG, F32)


NA_STEP_ROWS = TOK_BLK // GRID_W


def _na_kernel(q_ref, k_ref, v_ref, band_ref, o_ref, bias_ref):
    s_idx = pl.program_id(1)
    is_ctx = s_idx == 0

    @pl.when(jnp.logical_and(pl.program_id(0) == 0, s_idx == 0))
    def _():
        _fill_bias_tables(band_ref, bias_ref)

    lo = lax.broadcasted_iota(jnp.int32, (GRID_W, 128), 1) < HEAD_DIM
    nwin = NA_ROWS * GRID_W
    for rr in range(NA_STEP_ROWS):
        r = jnp.maximum(s_idx - 1, 0) * NA_STEP_ROWS + rr
        start = jnp.clip(r - NA_ROWS // 2, 0, SEQ // GRID_W - NA_ROWS)
        koff = pl.multiple_of(CTX_LEN + start * GRID_W, GRID_W)
        off = jnp.where(is_ctx, 8, start - r + NA_ROWS - 1)
        rows = slice(rr * GRID_W, (rr + 1) * GRID_W)
        pairs = range(NA_HEADS // 2)
        cols = [slice(hp * 128, (hp + 1) * 128) for hp in pairs]
        s_loc, s_ctx, m, p_loc, p_ctx, l, o = [], [], [], [], [], [], []
        for hp in pairs:
            qp = q_ref[0, rows, cols[hp]]
            zero = jnp.zeros_like(qp)
            q2 = jnp.concatenate([jnp.where(lo, qp, zero), jnp.where(lo, zero, qp)], axis=0)
            s_loc.append(_dot_nt(q2, k_ref[0, pl.ds(koff, nwin), cols[hp]]) + bias_ref[off, hp])
            s_ctx.append(_dot_nt(q2, k_ref[0, 0:CTX_LEN, cols[hp]]))
        for hp in pairs:
            m.append(jnp.maximum(jnp.max(s_loc[hp], axis=-1, keepdims=True),
                                 jnp.max(s_ctx[hp], axis=-1, keepdims=True)))
        for hp in pairs:
            pl_, pc_ = jnp.exp2(s_loc[hp] - m[hp]), jnp.exp2(s_ctx[hp] - m[hp])
            l.append(jnp.sum(pl_, axis=-1, keepdims=True) + jnp.sum(pc_, axis=-1, keepdims=True))
            p_loc.append(pl_.astype(BF16))
            p_ctx.append(pc_.astype(BF16))
        for hp in pairs:
            o.append(_dot(p_loc[hp], v_ref[0, pl.ds(koff, nwin), cols[hp]]) +
                     _dot(p_ctx[hp], v_ref[0, 0:CTX_LEN, cols[hp]]))
        for hp in pairs:
            oh = o[hp] / l[hp]
            o_ref[0, rows, cols[hp]] = jnp.where(lo, oh[:GRID_W], oh[GRID_W:]).astype(BF16)


def _na(p, band2):
    return pl.pallas_call(
        _na_kernel,
        grid=(BATCH, LT // TOK_BLK),
        in_specs=[
            pl.BlockSpec((1, TOK_BLK, 512), lambda b, s: (b, s, 2)),
            pl.BlockSpec((1, LT, 512), lambda b, s: (b, 0, 3)),
            pl.BlockSpec((1, LT, 512), lambda b, s: (b, 0, 4)),
            pl.BlockSpec(band2.shape, lambda b, s: (0, 0, 0, 0)),
        ],
        out_specs=pl.BlockSpec((1, TOK_BLK, 512), lambda b, s: (b, s, 0)),
        out_shape=jax.ShapeDtypeStruct((BATCH, LT, 512), BF16),
        scratch_shapes=[pltpu.VMEM((9, NA_HEADS // 2, 128, NA_ROWS * GRID_W), F32)],
        compiler_params=_cparams(("arbitrary", "arbitrary")),
        name="na_attn",
    )(p, p, p, band2)


def _gelu_tanh(x):
    return 0.5 * x * (1.0 + jnp.tanh(math.sqrt(2.0 / math.pi) * (x + 0.044715 * (x * x * x))))


def _silu(x):
    h = 0.5 * x
    return h + h * jnp.tanh(h)


def _finish(h, out, gt, pg):
    r = lax.rsqrt(jnp.mean(out * out, axis=-1, keepdims=True) + EPS)
    return h + (out * r) * (gt * pg)


def _even_out_kernel(yf_ref, yr_ref, u_ref, ga_ref, gb_ref, at_ref, hc_ref, hl_ref, mod_ref, d_ref, wg_ref,
                     bg_ref, wo_ref, pg_ref, o_ref):
    tb = pl.program_id(1)
    chains = range(PAIR)
    y = [_gelu_tanh(d_ref[...] * u_ref[h].astype(F32) + yf_ref[h].astype(F32) + yr_ref[h].astype(F32))
         for h in chains]
    glu = [_dot(y[h].astype(BF16), wg_ref[...]) for h in chains]
    y_a = [(y[h] * (0.5 + 0.5 * jnp.tanh(glu[h] + bg_ref[...])) * _silu(ga_ref[h].astype(F32))).astype(BF16)
           for h in chains]
    y_b = [(at_ref[h].astype(F32) * _silu(gb_ref[h].astype(F32))).astype(BF16) for h in chains]
    out = [_dot(y_a[h], wo_ref[0:512, :]) + _dot(y_b[h], wo_ref[512:1024, :]) for h in chains]
    for h in chains:
        res = jnp.where(tb == 0, hc_ref[h], hl_ref[h])
        o_ref[h] = _finish(res, out[h], _mod_row(mod_ref, h, tb)[:, 2 * D_MODEL:], pg_ref[...])


def _even_out(yf, yr, p, attn, ctx, x, mod, d_skip, w_glu, b_glu, w_out, post_g):
    tok = lambda c: pl.BlockSpec((PAIR, TOK_BLK, 512), lambda b, t: (b, t, c))
    full = lambda a: pl.BlockSpec(a.shape, lambda b, t: (0,) * a.ndim)
    d2, bg2, pg2 = d_skip.reshape(1, 512), b_glu.reshape(1, 512), post_g.reshape(1, D_MODEL)
    return pl.pallas_call(
        _even_out_kernel,
        grid=(BATCH // PAIR, LT // TOK_BLK),
        in_specs=[tok(0), tok(0), tok(0), tok(1), tok(5), tok(0)] + _cat_specs(D_MODEL) +
                 [full(mod), full(d2), full(w_glu), full(bg2), full(w_out), full(pg2)],
        out_specs=pl.BlockSpec((PAIR, TOK_BLK, D_MODEL), lambda b, t: (b, t, 0)),
        out_shape=jax.ShapeDtypeStruct((BATCH, LT, D_MODEL), F32),
        compiler_params=_cparams(("arbitrary", "arbitrary")),
        name="even_out",
    )(yf, yr, p, p, p, attn, ctx, x, mod, d2, w_glu, bg2, w_out, pg2)


def _softplus(z):
    return jnp.maximum(z, 0.0) + jnp.log1p(jnp.exp(-jnp.abs(z)))


def _lru_kernel(xf_ref, xfp_ref, xfn_ref, xr_ref, xrp_ref, xrn_ref, pf_ref, pr_ref, ph_ref, pft_ref, prt_ref,
                coef_ref, cb_ref, wg_ref, bg_ref, lam_ref, hf_ref, hr_ref, a0_ref, b0_ref, a1_ref, b1_ref,
                x0_ref, x1_ref, st_ref):
    i = pl.program_id(0)

    @pl.when(i == 0)
    def _():
        for ref in (st_ref, a0_ref, b0_ref, a1_ref, b1_ref, x0_ref, x1_ref):
            ref[...] = jnp.zeros_like(ref)

    nrow = 8 * SCAN_BLK
    half = nrow // 2
    blk = jnp.minimum(i, N_SCAN - 1)
    live = jnp.where(i >= 1, 1.0, 0.0).astype(F32)

    def halo(prev_ref, next_ref, b):
        prev_ok = jnp.logical_and(b != 0, b != CTX_SCAN)
        next_ok = jnp.logical_and(b != CTX_SCAN - 1, b != N_SCAN - 1)
        pv = prev_ref[...].reshape(BATCH * 16, LRU_WIDTH)
        nx = next_ref[...].reshape(BATCH * 16, LRU_WIDTH)
        return [jnp.where(prev_ok, pv, jnp.zeros_like(pv)), jnp.where(next_ok, nx, jnp.zeros_like(nx))]

    def step(x_w, x_r, a_w, b_w, a_s, b_s):
        vals = {}

        def unperm():
            hs = b_w[...].astype(BF16)
            hf_ref[...] = _dot(pft_ref[...], hs).astype(BF16).reshape(BATCH, SCAN_BLK, LRU_WIDTH)
            hr_ref[...] = _dot(prt_ref[...], hs).astype(BF16).reshape(BATCH, SCAN_BLK, LRU_WIDTH)

        def perm():
            hal = jnp.concatenate(halo(xfp_ref, xfn_ref, _fwd_blk(blk)) + halo(xrp_ref, xrn_ref, _rev_blk(blk)),
                                  axis=0)
            xh = _dot(ph_ref[...], hal)
            xp = (_dot(pf_ref[...], xf_ref[...].reshape(BATCH * SCAN_BLK, LRU_WIDTH)) +
                  _dot(pr_ref[...], xr_ref[...].reshape(BATCH * SCAN_BLK, LRU_WIDTH)))
            x_w[0:16, :] = xh[:16]
            x_w[16:16 + nrow, :] = xp
            x_w[16 + nrow:, :] = xh[16:]

        def conv(h):
            xc = jnp.zeros((SCAN_BLK // 2, 8, LRU_WIDTH), F32) + cb_ref[...]
            for s in range(5):
                r0 = h * half + s * 8
                xc = xc + x_r[r0:r0 + half, :].reshape(SCAN_BLK // 2, 8, LRU_WIDTH) * coef_ref[s]
            vals['xc%d' % h] = xc.reshape(half, LRU_WIDTH) * live

        def gates(h):
            is_f = (lax.broadcasted_iota(jnp.int32, (half, 1), 0) & 7) < 4
            xcb = vals['xc%d' % h].astype(BF16)
            zero = jnp.zeros_like(xcb[:, :256])
            pre_r, pre_i = [], []
            for c in range(2):
                xt = xcb[:, c * 256:(c + 1) * 256]
                lhs = jnp.concatenate([jnp.where(is_f, xt, zero), jnp.where(is_f, zero, xt)], axis=1)
                pre = _dot(lhs, wg_ref[c])
                pre_r.append(pre[:, :256])
                pre_i.append(pre[:, 256:])
            vals['pr%d' % h] = jnp.concatenate(pre_r, axis=1)
            vals['pi%d' % h] = jnp.concatenate(pre_i, axis=1)

        def elem(h):
            rows = slice(h * half, (h + 1) * half)
            shp = (SCAN_BLK // 2, 8, LRU_WIDTH)
            t_r = jnp.tanh(vals['pr%d' % h].reshape(shp) + bg_ref[:, :LRU_WIDTH])
            t_i = jnp.tanh(vals['pi%d' % h].reshape(shp) + bg_ref[:, LRU_WIDTH:])
            c = (-0.5 * LRU_C) * _softplus(-lam_ref[...])
            u = 1.0 + t_r
            a = jnp.exp2((c * LOG2E) * u)
            a_w[rows, :] = a.reshape(half, LRU_WIDTH)
            one_m_a2 = jnp.maximum(jnp.tanh((-c) * u) * (a * a + 1.0), 1e-37)
            mult = one_m_a2 * lax.rsqrt(one_m_a2)
            b = (mult * vals['xc%d' % h].reshape(shp)) * (0.5 + 0.5 * t_i)
            b_w[rows, :] = b.reshape(half, LRU_WIDTH)

        def scan(t0):
            h = st_ref[...]
            for t in range(t0, t0 + 16):
                rows = slice(t * 8, (t + 1) * 8)
                h = a_s[rows, :] * h + b_s[rows, :]
                b_s[rows, :] = h
            st_ref[...] = h

        for piece in (functools.partial(conv, 0), unperm, functools.partial(conv, 1), functools.partial(scan, 0),
                      functools.partial(gates, 0), perm, functools.partial(gates, 1), functools.partial(scan, 16),
                      functools.partial(elem, 0), functools.partial(scan, 32), functools.partial(elem, 1),
                      functools.partial(scan, 48)):
            piece()

    @pl.when(i % 2 == 0)
    def _():
        step(x0_ref, x1_ref, a1_ref, b1_ref, a0_ref, b0_ref)

    @pl.when(i % 2 == 1)
    def _():
        step(x1_ref, x0_ref, a0_ref, b0_ref, a1_ref, b1_ref)


def _lru_gate_weights(w_a, w_x):
    w = jnp.stack([w_a, w_x], axis=1).reshape(2, 2, 2, 4, LRU_BLOCK, LRU_BLOCK)
    same = jnp.asarray(np.eye(4, dtype=np.float32))
    w = w.transpose(2, 0, 3, 4, 1, 5)[:, :, :, :, :, None] * same[:, None, None, :, None]
    return (0.5 * w).reshape(2, 512, 512).astype(BF16)


def _lru(p, pf, pr, ph, pft, prt, coef, conv_b, wg, bg, lam):
    nrow = 8 * SCAN_BLK
    full = lambda a: pl.BlockSpec(a.shape, lambda i: (0,) * a.ndim)
    proj = lambda i: jnp.minimum(i, N_SCAN - 1)
    read = lambda i: jnp.clip(i - 3, 0, N_SCAN - 1)
    cur = lambda f: pl.BlockSpec((BATCH, SCAN_BLK, LRU_WIDTH), lambda i: (0, f(proj(i)), 0))
    prev = lambda f: pl.BlockSpec((BATCH, 16, LRU_WIDTH), lambda i: (0, jnp.maximum(f(proj(i)) * 4 - 1, 0), 0))
    nxt = lambda f: pl.BlockSpec((BATCH, 16, LRU_WIDTH),
                                 lambda i: (0, jnp.minimum(f(proj(i)) * 4 + 4, LT // 16 - 1), 0))
    cb2 = conv_b.reshape(1, LRU_WIDTH)
    return pl.pallas_call(
        _lru_kernel,
        grid=(N_SCAN + 3,),
        in_specs=[cur(_fwd_blk), prev(_fwd_blk), nxt(_fwd_blk), cur(_rev_blk), prev(_rev_blk), nxt(_rev_blk),
                  full(pf), full(pr), full(ph), full(pft), full(prt), full(coef), full(cb2), full(wg), full(bg),
                  full(lam)],
        out_specs=[
            pl.BlockSpec((BATCH, SCAN_BLK, LRU_WIDTH), lambda i: (0, _fwd_blk(read(i)), 0)),
            pl.BlockSpec((BATCH, SCAN_BLK, LRU_WIDTH), lambda i: (0, _rev_blk(read(i)), 0)),
        ],
        out_shape=[jax.ShapeDtypeStruct((BATCH, LT, LRU_WIDTH), BF16)] * 2,
        scratch_shapes=([pltpu.VMEM((nrow, LRU_WIDTH), F32)] * 4 + [pltpu.VMEM((nrow + 32, LRU_WIDTH), F32)] * 2 +
                        [pltpu.VMEM((8, LRU_WIDTH), F32)]),
        compiler_params=_cparams(("arbitrary",)),
        name="lru_scan",
    )(p, p, p, p, p, p, pf, pr, ph, pft, prt, coef, cb2, wg, bg, lam)


def _norm_rope(xb, g, grot, cos, sin, ones_blk, swap):
    x = xb.astype(F32)
    xrot = _dot(xb, swap)
    ms = _dot((x * x).astype(BF16), ones_blk) * (1.0 / HEAD_DIM)
    rs = lax.rsqrt(ms + EPS)
    return rs * ((x * g) * cos + (xrot * grot) * sin)


def _gqa_kernel(q_ref, k_ref, v_ref, cq_ref, sq_ref, ck_ref, sk_ref, g_ref, ones_ref, swap_ref,
                o_ref, kn_ref, vlo_ref, vhi_ref):
    qb = pl.program_id(1)
    ones_blk = ones_ref[...]
    swap = swap_ref[...]

    @pl.when(qb == 0)
    def _():
        kn = _norm_rope(k_ref[0], g_ref[2:3, :], g_ref[3:4, :], ck_ref[...], sk_ref[...], ones_blk, swap)
        kn_ref[...] = kn.astype(BF16)
        v = v_ref[0]
        lo_k = lax.broadcasted_iota(jnp.int32, (LT, 128), 1) < HEAD_DIM
        one = jnp.ones_like(v)
        vlo_ref[...] = jnp.where(lo_k, v, one)
        vhi_ref[...] = jnp.where(lo_k, one, v)

    lo = lax.broadcasted_iota(jnp.int32, (Q_BLK, 128), 1) < HEAD_DIM
    kn = kn_ref[...]
    cos = cq_ref[...]
    sin = sq_ref[...]
    n_pb = GQA_HEADS // 2

    q_blocks = [q_ref[:, pb * 128:(pb + 1) * 128] for pb in range(n_pb)]
    x_rot = [_dot(qb_, swap) for qb_ in q_blocks]
    msq = [_dot((qb_.astype(F32) * qb_.astype(F32)).astype(BF16), ones_blk) for qb_ in q_blocks]
    qs_all = []
    for pb in range(n_pb):
        rs = lax.rsqrt(msq[pb] * (1.0 / HEAD_DIM) + EPS) * (HEAD_DIM ** -0.5 * LOG2E)
        qn = (rs * ((q_blocks[pb].astype(F32) * g_ref[0:1, :]) * cos + (x_rot[pb] * g_ref[1:2, :]) * sin))
        qn = qn.astype(BF16)
        zero = jnp.zeros_like(qn)
        qs_all.append(jnp.concatenate([jnp.where(lo, qn, zero), jnp.where(lo, zero, qn)], axis=0))

    def scores(pb):
        return _dot_nt(qs_all[pb], kn)

    for pb in range(n_pb):
        cols = slice(pb * 128, (pb + 1) * 128)
        s = scores(pb)
        m = jnp.max(s, axis=-1, keepdims=True)
        p = jnp.exp2(s - m).astype(BF16)
        o_lo = _dot(p[:Q_BLK], vlo_ref[...])
        o_hi = _dot(p[Q_BLK:], vhi_ref[...])
        num = jnp.where(lo, o_lo, o_hi)
        den = pltpu.roll(jnp.where(lo, o_hi, o_lo), HEAD_DIM, axis=1)
        o_ref[0, :, cols] = (num / den).astype(BF16)


def _gqa(p, cos128, sin128, gvec, ones_blk, swap):
    nqb = SEQ // Q_BLK
    cos_lat, sin_lat = cos128[CTX_LEN:], sin128[CTX_LEN:]
    return pl.pallas_call(
        _gqa_kernel,
        grid=(BATCH, nqb),
        in_specs=[
            pl.BlockSpec((pl.Squeezed(), pl.Element(Q_BLK), pl.Element(512)),
                         lambda b, q: (b, pl.multiple_of(CTX_LEN + q * Q_BLK, CTX_LEN), 1024)),
            pl.BlockSpec((1, LT, 128), lambda b, q: (b, 0, 16)),
            pl.BlockSpec((1, LT, 128), lambda b, q: (b, 0, 17)),
            pl.BlockSpec((Q_BLK, 128), lambda b, q: (q, 0)),
            pl.BlockSpec((Q_BLK, 128), lambda b, q: (q, 0)),
            pl.BlockSpec((LT, 128), lambda b, q: (0, 0)),
            pl.BlockSpec((LT, 128), lambda b, q: (0, 0)),
            pl.BlockSpec((8, 128), lambda b, q: (0, 0)),
            pl.BlockSpec((128, 128), lambda b, q: (0, 0)),
            pl.BlockSpec((128, 128), lambda b, q: (0, 0)),
        ],
        out_specs=pl.BlockSpec((1, Q_BLK, 512), lambda b, q: (b, q, 0)),
        out_shape=jax.ShapeDtypeStruct((BATCH, SEQ, 512), BF16),
        scratch_shapes=[pltpu.VMEM((LT, 128), BF16)] * 3,
        compiler_params=_cparams(("arbitrary", "arbitrary")),
        name="gqa_attn",
    )(p, p, p, cos_lat, sin_lat, cos128, sin128, gvec, ones_blk, swap)


def _odd_out_kernel(hf_ref, hr_ref, gc_ref, gd_ref, at_ref, h_ref, mod_ref, wo_ref, pg_ref, o_ref):
    chains = range(PAIR)
    y_c = [((hf_ref[h].astype(F32) + hr_ref[h].astype(F32)) * _silu(gc_ref[h].astype(F32))).astype(BF16)
           for h in chains]
    y_d = [(at_ref[h].astype(F32) * _silu(gd_ref[h].astype(F32))).astype(BF16) for h in chains]
    out = [_dot(y_c[h], wo_ref[0:512, :]) + _dot(y_d[h], wo_ref[512:1024, :]) for h in chains]
    for h in chains:
        gt = mod_ref[pl.ds(pl.program_id(0) * PAIR + h, 1), 2 * D_MODEL:]
        o_ref[h] = _finish(h_ref[h], out[h], gt, pg_ref[...])


def _odd_out(hf, hr, p, attn, h_cat, mod, w_out, post_g):
    cb = CTX_LEN // TOK_BLK
    cat = lambda c: pl.BlockSpec((PAIR, TOK_BLK, 512), lambda b, t: (b, t + cb, c))
    full = lambda a: pl.BlockSpec(a.shape, lambda b, t: (0,) * a.ndim)
    pg2 = post_g.reshape(1, D_MODEL)
    return pl.pallas_call(
        _odd_out_kernel,
        grid=(BATCH // PAIR, SEQ // TOK_BLK),
        in_specs=[cat(0), cat(0), cat(1), cat(3),
                  pl.BlockSpec((PAIR, TOK_BLK, 512), lambda b, t: (b, t, 0)),
                  pl.BlockSpec((PAIR, TOK_BLK, D_MODEL), lambda b, t: (b, t + cb, 0)),
                  full(mod), full(w_out), full(pg2)],
        out_specs=pl.BlockSpec((PAIR, TOK_BLK, D_MODEL), lambda b, t: (b, t, 0)),
        out_shape=jax.ShapeDtypeStruct((BATCH, SEQ, D_MODEL), F32),
        compiler_params=_cparams(("arbitrary", "arbitrary")),
        name="odd_out",
    )(hf, hr, p, p, attn, h_cat, mod, w_out, pg2)


def _pair_swap_matrix(n):
    r = np.zeros((n, n), np.float32)
    r[np.arange(1, n, 2), np.arange(0, n, 2)] = -1.0
    r[np.arange(0, n, 2), np.arange(1, n, 2)] = 1.0
    return r


def _interleave_kv_groups(w, axis):
    if axis == 0:
        return w.reshape(2, 4, HEAD_DIM, w.shape[1]).transpose(1, 0, 2, 3).reshape(w.shape)
    return w.reshape(w.shape[0], 2, 4, HEAD_DIM).transpose(0, 2, 1, 3).reshape(w.shape)


def _odd_w_in(w):
    w = w.astype(BF16)
    x, gc = w[:, 0:512], w[:, 512:1024]
    q = _interleave_kv_groups(w[:, 1024:1536], 1)
    k, v = w[:, 1536:1664], w[:, 1664:1792]
    gd = _interleave_kv_groups(w[:, 1792:2304], 1)
    return jnp.concatenate([x, gc, q, gd, k, v], axis=1)


def _rope_tables():
    t = np.arange(SEQ)
    row = (t // GRID_W).astype(np.float32)
    col = (t % GRID_W).astype(np.float32)
    half = HEAD_DIM // 2
    inv = (ROPE_THETA ** (-np.arange(0, half, 2, dtype=np.float32) / half)).astype(np.float32)
    ang = np.concatenate([row[:, None] * inv, col[:, None] * inv], axis=-1)
    cos = np.repeat(np.cos(ang), 2, axis=-1)
    sin = np.repeat(np.sin(ang), 2, axis=-1)
    cos = np.concatenate([np.ones((CTX_LEN, HEAD_DIM), np.float32), cos], axis=0)
    sin = np.concatenate([np.zeros((CTX_LEN, HEAD_DIM), np.float32), sin], axis=0)
    return np.tile(cos, (1, 2)).astype(np.float32), np.tile(sin, (1, 2)).astype(np.float32)


def _swap_pairs_vec(g):
    g2 = g.reshape(-1, 2)
    return jnp.stack([g2[:, 1], g2[:, 0]], axis=-1).reshape(g.shape)


def kernel(x, c, ctx, c_ctx, ada_w, ada_b, pre_g, post_g, ev_w_in, ev_w_out, s5_lam_re, s5_lam_im, s5_log_dt,
           s5_b_re, s5_b_im, s5_c_re, s5_c_im, s5_d, s5_w_glu, s5_b_glu, na_rel_bias, od_w_in, od_w_out,
           lru_conv_w, lru_conv_b, lru_lam, lru_w_a, lru_b_a, lru_w_x, lru_b_x, gqa_q_norm, gqa_k_norm):
    pf_np, pr_np = _scan_perms()
    pf, pr = jnp.asarray(pf_np, BF16), jnp.asarray(pr_np, BF16)
    pft, prt = jnp.asarray(pf_np.T, BF16), jnp.asarray(pr_np.T, BF16)
    ph = jnp.asarray(_halo_perm(), BF16)

    c8 = jnp.concatenate([c, c_ctx[None], jnp.zeros((3, D_MODEL), F32)], axis=0)
    mod = _adaln(c8, ada_w, ada_b)

    col_scale = np.ones((EVEN_IN,), np.float32)
    col_scale[1024:1536] = HEAD_DIM ** -0.5 * LOG2E
    p0 = _inproj((ctx, x), mod[0], pre_g[0], (ev_w_in[0] * jnp.asarray(col_scale)).astype(BF16))
    bc, lre8, lim8 = _s5_weights(s5_lam_re[0], s5_lam_im[0], s5_log_dt[0], s5_b_re[0], s5_b_im[0],
                                 s5_c_re[0], s5_c_im[0])
    rep_np, same_np = _s5_expanders()
    yf, yr = _s5(p0, pf, pr, pft, prt, bc, jnp.asarray(rep_np, BF16), jnp.asarray(same_np, BF16), lre8, lim8)
    attn0 = _na(p0, _na_bias_table(na_rel_bias[0]))
    h1 = _even_out(yf, yr, p0, attn0, ctx, x, mod[0], s5_d[0], (0.5 * s5_w_glu[0]).astype(BF16), 0.5 * s5_b_glu[0],
                   ev_w_out[0].astype(BF16), post_g[0])

    p1 = _inproj(h1, mod[1], pre_g[1], _odd_w_in(od_w_in[0]))
    wg = _lru_gate_weights(lru_w_a[0], lru_w_x[0])
    bg = jnp.repeat(0.5 * jnp.concatenate([lru_b_a[0], lru_b_x[0]], axis=1), 4, axis=0)
    lam8 = jnp.repeat(lru_lam[0], 4, axis=0)
    hf, hr = _lru(p1, pf, pr, ph, pft, prt, _conv_coef(lru_conv_w[0]), lru_conv_b[0], wg, bg, lam8)
    cos_np, sin_np = _rope_tables()
    gq, gk = gqa_q_norm[0], gqa_k_norm[0]
    gvec = jnp.stack([jnp.tile(gq, 2), jnp.tile(_swap_pairs_vec(gq), 2),
                      jnp.tile(gk, 2), jnp.tile(_swap_pairs_vec(gk), 2)] + [jnp.zeros((128,), F32)] * 4)
    ones_np = np.kron(np.eye(2, dtype=np.float32), np.ones((HEAD_DIM, HEAD_DIM), np.float32))
    attn1 = _gqa(p1, jnp.asarray(cos_np), jnp.asarray(sin_np), gvec, jnp.asarray(ones_np, BF16),
                 jnp.asarray(_pair_swap_matrix(128), BF16))
    w_out1 = od_w_out[0].astype(BF16)
    w_out1 = jnp.concatenate([w_out1[:512], _interleave_kv_groups(w_out1[512:], 0)], axis=0)
    return _odd_out(hf, hr, p1, attn1, h1, mod[1], w_out1, post_g[1])
```

```python
import functools
import math

import numpy as np
import jax
import jax.numpy as jnp
from jax import lax
from jax.experimental import pallas as pl
from jax.experimental.pallas import tpu as pltpu

F32 = jnp.float32
BF16 = jnp.bfloat16
HIGHEST = lax.Precision.HIGHEST

D_MODEL = 1024
BATCH = 4
SEQ = 4096
GRID_W = 64
CTX_LEN = 256
LT = CTX_LEN + SEQ
HEAD_DIM = 64
EPS = 1e-6
S5_WIDTH = 512
S5_GROUP = 16
S5_GROUPS = 32
S5_STATE = 64
NA_HEADS = 8
NA_ROWS = 8
NA_COLS = 16
LRU_WIDTH = 512
LRU_BLOCKS = 8
LRU_BLOCK = 64
LRU_C = 8.0
GQA_HEADS = 8
ROPE_THETA = 10000.0
EVEN_IN = 3072
ODD_IN = 2304

TOK_BLK = 256
SCAN_BLK = 64
N_SCAN = LT // SCAN_BLK
CTX_SCAN = CTX_LEN // SCAN_BLK
Q_BLK = 512
NEG = -1e30
LOG2E = math.log2(math.e)
VMEM_LIMIT = 56 * 1024 * 1024


def _cparams(sem):
    return pltpu.CompilerParams(dimension_semantics=sem, vmem_limit_bytes=VMEM_LIMIT)


def _dot(a, b):
    return jnp.dot(a, b, preferred_element_type=F32)


def _dot_nt(a, b):
    return lax.dot_general(a, b, (((1,), (1,)), ((), ())), preferred_element_type=F32)


def _adaln_kernel(c_ref, w_ref, b_ref, o_ref):
    c = c_ref[...]
    s = c * jax.nn.sigmoid(c)
    o_ref[0] = jnp.dot(s, w_ref[0], preferred_element_type=F32, precision=HIGHEST) + b_ref[0]


def _adaln(c8, ada_w, ada_b):
    depth = ada_w.shape[0]
    nb = 3 * D_MODEL // 1024
    return pl.pallas_call(
        _adaln_kernel,
        grid=(depth, nb),
        in_specs=[
            pl.BlockSpec((8, D_MODEL), lambda i, n: (0, 0)),
            pl.BlockSpec((1, D_MODEL, 1024), lambda i, n: (i, 0, n)),
            pl.BlockSpec((1, 1, 1024), lambda i, n: (i, 0, n)),
        ],
        out_specs=pl.BlockSpec((1, 8, 1024), lambda i, n: (i, 0, n)),
        out_shape=jax.ShapeDtypeStruct((depth, 8, 3 * D_MODEL), F32),
        compiler_params=_cparams(("arbitrary", "arbitrary")),
        name="adaln",
    )(c8, ada_w, ada_b.reshape(depth, 1, 3 * D_MODEL))


PAIR = 4


def _cat_specs(width):
    return [pl.BlockSpec((PAIR, TOK_BLK, width), lambda b, t: (b, 0, 0)),
            pl.BlockSpec((PAIR, TOK_BLK, width), lambda b, t: (b, jnp.maximum(t - 1, 0), 0))]


def _mod_row(mod_ref, h, tb):
    row = jnp.where(tb == 0, BATCH, pl.program_id(0) * PAIR + h)
    return mod_ref[pl.ds(row, 1), :]


def _inproj_kernel(*refs, two_src):
    tb = pl.program_id(1)
    if two_src:
        c_ref, x_ref, mod_ref, g_ref, w_ref, o_ref = refs
    else:
        x_ref, mod_ref, g_ref, w_ref, o_ref = refs
    ys = []
    for h in range(PAIR):
        x = jnp.where(tb == 0, c_ref[h], x_ref[h]) if two_src else x_ref[h]
        r = lax.rsqrt(jnp.mean(x * x, axis=-1, keepdims=True) + EPS)
        m = _mod_row(mod_ref, h, tb)
        y = (x * r) * g_ref[...]
        ys.append((y * (1.0 + m[:, D_MODEL:2 * D_MODEL]) + m[:, :D_MODEL]).astype(BF16))
    for h in range(PAIR):
        o_ref[h] = _dot(ys[h], w_ref[...]).astype(BF16)


def _inproj(src, mod, g, w_bf):
    n = w_bf.shape[1]
    two_src = isinstance(src, tuple)
    if two_src:
        src_specs = _cat_specs(D_MODEL)
    else:
        src_specs = [pl.BlockSpec((PAIR, TOK_BLK, D_MODEL), lambda b, t: (b, t, 0))]
        src = (src,)
    return pl.pallas_call(
        functools.partial(_inproj_kernel, two_src=two_src),
        grid=(BATCH // PAIR, LT // TOK_BLK),
        in_specs=src_specs + [
            pl.BlockSpec((8, 3 * D_MODEL), lambda b, t: (0, 0)),
            pl.BlockSpec((1, D_MODEL), lambda b, t: (0, 0)),
            pl.BlockSpec((D_MODEL, n), lambda b, t: (0, 0)),
        ],
        out_specs=pl.BlockSpec((PAIR, TOK_BLK, n), lambda b, t: (b, t, 0)),
        out_shape=jax.ShapeDtypeStruct((BATCH, LT, n), BF16),
        compiler_params=_cparams(("arbitrary", "arbitrary")),
        name="inproj",
    )(*src, mod, g.reshape(1, D_MODEL), w_bf)


def _fwd_blk(i):
    return i


def _rev_blk(i):
    return jnp.where(i < CTX_SCAN, CTX_SCAN - 1 - i, N_SCAN + CTX_SCAN - 1 - i)


def _scan_perms():
    t = SCAN_BLK
    pf = np.zeros((8 * t, BATCH * t), np.float32)
    pr = np.zeros((8 * t, BATCH * t), np.float32)
    for tt in range(t):
        for b in range(BATCH):
            pf[tt * 8 + b, b * t + tt] = 1.0
            pr[tt * 8 + 4 + b, b * t + (t - 1 - tt)] = 1.0
    return pf, pr


def _halo_perm():
    ph = np.zeros((32, 16 * 4 * BATCH), np.float32)
    for b in range(BATCH):
        ph[1 * 8 + b, 0 * 64 + b * 16 + 15] = 1.0
        ph[2 * 8 + b, 1 * 64 + b * 16 + 0] = 1.0
        ph[3 * 8 + b, 1 * 64 + b * 16 + 1] = 1.0
        ph[0 * 8 + 4 + b, 3 * 64 + b * 16 + 1] = 1.0
        ph[1 * 8 + 4 + b, 3 * 64 + b * 16 + 0] = 1.0
        ph[2 * 8 + 4 + b, 2 * 64 + b * 16 + 15] = 1.0
    return ph


def _conv_coef(conv_w):
    zero = jnp.zeros((1, LRU_WIDTH), conv_w.dtype)
    fwd = jnp.concatenate([zero, conv_w], axis=0)
    rev = jnp.concatenate([conv_w[::-1], zero], axis=0)
    return jnp.concatenate([jnp.broadcast_to(fwd[:, None], (5, 4, LRU_WIDTH)),
                            jnp.broadcast_to(rev[:, None], (5, 4, LRU_WIDTH))], axis=1)


def _s5_kernel(uf_ref, ur_ref, pf_ref, pr_ref, pft_ref, prt_ref, bc_ref, rep_ref, same_ref, lre_ref, lim_ref,
               yf_ref, yr_ref, buf0_ref, buf1_ref, st_ref, bcat_ref, ccat_ref):
    i = pl.program_id(0)

    @pl.when(i == 0)
    def _():
        st_ref[...] = jnp.zeros_like(st_ref)
        buf0_ref[...] = jnp.zeros_like(buf0_ref)
        buf1_ref[...] = jnp.zeros_like(buf1_ref)
        same = same_ref[...].astype(F32)
        for j in range(4):
            bcat_ref[j] = (_dot(rep_ref[...], bc_ref[0, j]) * same).astype(BF16)
            ccat_ref[j] = (_dot(rep_ref[...], bc_ref[1, j]) * same).T.astype(BF16)

    nrow = 8 * SCAN_BLK

    def step(buf_a, buf_b):
        vals = {}

        def perm():
            uf = uf_ref[...].reshape(BATCH * SCAN_BLK, S5_WIDTH)
            ur = ur_ref[...].reshape(BATCH * SCAN_BLK, S5_WIDTH)
            vals['u_f'] = _dot(pf_ref[...], uf).astype(BF16)
            vals['u_r'] = _dot(pr_ref[...], ur).astype(BF16)

        def readout(j):
            is_f = (lax.broadcasted_iota(jnp.int32, (nrow, 128), 0) & 7) < 4
            yj = _dot(buf_a[:, j * 1024:(j + 1) * 1024].astype(BF16), ccat_ref[j])
            vals['y%d' % j] = jnp.where(is_f, yj[:, :128], yj[:, 128:]).astype(BF16)

        def project(j):
            lhs = jnp.concatenate([vals['u_f'][:, j * 128:(j + 1) * 128], vals['u_r'][:, j * 128:(j + 1) * 128]],
                                  axis=1)
            buf_a[:, j * 1024:(j + 1) * 1024] = _dot(lhs, bcat_ref[j])

        def unperm(p_ref, o_ref):
            yp = jnp.concatenate([vals['y%d' % j] for j in range(4)], axis=1)
            o_ref[...] = _dot(p_ref[...], yp).astype(BF16).reshape(BATCH, SCAN_BLK, S5_WIDTH)

        mxu = [perm]
        for j in range(4):
            mxu += [functools.partial(readout, j), functools.partial(project, j)]
        mxu += [functools.partial(unperm, pft_ref, yf_ref), functools.partial(unperm, prt_ref, yr_ref)]

        def scan(j, t0):
            c_re = slice(j * 1024, j * 1024 + 512)
            c_im = slice(j * 1024 + 512, (j + 1) * 1024)
            lre = lre_ref[j]
            lim = lim_ref[j]
            hre = st_ref[:, c_re]
            him = st_ref[:, c_im]
            for t in range(t0, t0 + 8):
                rows = slice(t * 8, (t + 1) * 8)
                nre = lre * hre - lim * him + buf_b[rows, c_re]
                nim = lre * him + lim * hre + buf_b[rows, c_im]
                buf_b[rows, c_re] = nre
                buf_b[rows, c_im] = nim
                hre, him = nre, nim
            st_ref[:, c_re] = hre
            st_ref[:, c_im] = him

        vpu = [functools.partial(scan, j, t0) for j in range(4) for t0 in range(0, SCAN_BLK, 8)]

        per_piece = [2, 3, 3, 3, 3, 3, 3, 3, 3, 3, 3]
        for piece, n_scan in zip(mxu, per_piece):
            piece()
            for _ in range(n_scan):
                vpu.pop(0)()

    @pl.when(i % 2 == 0)
    def _():
        step(buf0_ref, buf1_ref)

    @pl.when(i % 2 == 1)
    def _():
        step(buf1_ref, buf0_ref)


def _s5_weights(lam_re, lam_im, log_dt, b_re, b_im, c_re, c_im):
    a = lam_re.astype(F32) * jnp.exp(log_dt.astype(F32))[..., None]
    b = lam_im.astype(F32) * jnp.exp(log_dt.astype(F32))[..., None]
    lbr = jnp.exp(a) * jnp.cos(b)
    lbi = jnp.exp(a) * jnp.sin(b)
    nr = jnp.expm1(a) * jnp.cos(b) - 2.0 * jnp.sin(0.5 * b) ** 2
    d2 = lam_re * lam_re + lam_im * lam_im
    qr = (nr * lam_re + lbi * lam_im) / d2
    qi = (lbi * lam_re - nr * lam_im) / d2
    bbr = qr[..., None] * b_re - qi[..., None] * b_im
    bbi = qr[..., None] * b_im + qi[..., None] * b_re
    bb = jnp.stack([bbr, bbi], axis=1).reshape(2, 2, 4, 8, S5_STATE, S5_GROUP)
    bb = bb.transpose(2, 0, 5, 1, 3, 4).reshape(4, 2 * S5_GROUP, 2 * 512)
    cc = jnp.stack([c_re.astype(F32), -c_im.astype(F32)], axis=1).reshape(2, 2, 4, 8, S5_GROUP, S5_STATE)
    cc = cc.transpose(2, 0, 4, 1, 3, 5).reshape(4, 2 * S5_GROUP, 2 * 512)
    lre8 = jnp.repeat(lbr.reshape(2, 4, 512).transpose(1, 0, 2), 4, axis=1)
    lim8 = jnp.repeat(lbi.reshape(2, 4, 512).transpose(1, 0, 2), 4, axis=1)
    return jnp.stack([bb, cc]).astype(BF16), lre8, lim8


def _s5_expanders():
    rows = np.arange(2 * 128)
    d, g, h = rows // 128, (rows // S5_GROUP) % 8, rows % S5_GROUP
    k = np.arange(2 * S5_GROUP)
    rep = ((d[:, None] == k[None, :] // S5_GROUP) & (h[:, None] == k[None, :] % S5_GROUP)).astype(np.float32)
    cols = np.arange(2 * 512)
    same = (g[:, None] == (cols[None, :] // S5_STATE) % 8).astype(np.float32)
    return rep, same


def _s5(p, pf, pr, pft, prt, bc, rep, same, lre8, lim8):
    nrow = 8 * SCAN_BLK
    const2 = lambda i: (0, 0)
    const3 = lambda i: (0, 0, 0)
    proj = lambda i: jnp.minimum(i, N_SCAN - 1)
    read = lambda i: jnp.clip(i - 2, 0, N_SCAN - 1)
    return pl.pallas_call(
        _s5_kernel,
        grid=(N_SCAN + 2,),
        in_specs=[
            pl.BlockSpec((BATCH, SCAN_BLK, S5_WIDTH), lambda i: (0, _fwd_blk(proj(i)), 0)),
            pl.BlockSpec((BATCH, SCAN_BLK, S5_WIDTH), lambda i: (0, _rev_blk(proj(i)), 0)),
            pl.BlockSpec(pf.shape, const2), pl.BlockSpec(pr.shape, const2),
            pl.BlockSpec(pft.shape, const2), pl.BlockSpec(prt.shape, const2),
            pl.BlockSpec(bc.shape, lambda i: (0, 0, 0, 0)), pl.BlockSpec(rep.shape, const2),
            pl.BlockSpec(same.shape, const2),
            pl.BlockSpec(lre8.shape, const3), pl.BlockSpec(lim8.shape, const3),
        ],
        out_specs=[
            pl.BlockSpec((BATCH, SCAN_BLK, S5_WIDTH), lambda i: (0, _fwd_blk(read(i)), 0)),
            pl.BlockSpec((BATCH, SCAN_BLK, S5_WIDTH), lambda i: (0, _rev_blk(read(i)), 0)),
        ],
        out_shape=[jax.ShapeDtypeStruct((BATCH, LT, S5_WIDTH), BF16)] * 2,
        scratch_shapes=[pltpu.VMEM((nrow, 4096), F32), pltpu.VMEM((nrow, 4096), F32),
                        pltpu.VMEM((8, 4096), F32),
                        pltpu.VMEM((4, 256, 1024), BF16), pltpu.VMEM((4, 1024, 256), BF16)],
        compiler_params=_cparams(("arbitrary",)),
        name="s5_scan",
    )(p, p, pf, pr, pft, prt, bc, rep, same, lre8, lim8)


def _na_bias_table(rel_bias):
    w = np.arange(GRID_W)
    cs = np.clip(w - NA_COLS // 2, 0, GRID_W - NA_COLS)
    cp = np.arange(GRID_W)
    valid = (cp[None, :] >= cs[:, None]) & (cp[None, :] < cs[:, None] + NA_COLS)
    dc = cp[None, :] - w[:, None] + (NA_COLS - 1)
    n_dc = 2 * NA_COLS - 1
    onehot = ((dc[None] == np.arange(n_dc)[:, None, None]) & valid[None]).astype(np.float32)
    n_dr = 2 * NA_ROWS - 1
    oh2 = np.zeros((2, n_dc, GRID_W, 2, GRID_W), np.float32)
    oh2[0, :, :, 0, :] = onehot
    oh2[1, :, :, 1, :] = onehot
    rb = rel_bias.astype(F32)
    rb2 = jnp.concatenate([rb[:, :n_dr - 1], rb[:, 1:]], axis=2).reshape(NA_HEADS * (n_dr - 1), 2 * n_dc)
    band2 = jnp.dot(rb2, jnp.asarray(oh2.reshape(2 * n_dc, 2 * GRID_W * GRID_W)), precision=HIGHEST)
    neg2 = np.tile(np.where(valid, 0.0, NEG).astype(np.float32), (1, 2))
    return band2.reshape(NA_HEADS, n_dr - 1, GRID_W, 2 * GRID_W) * LOG2E + jnp.asarray(neg2)


def _fill_bias_tables(band_ref, bias_ref):
    for off in range(8):
        for i2 in range(NA_ROWS // 2):
            bias_ref[off, :, :, i2 * 128:(i2 + 1) * 128] = band_ref[:, off + 2 * i2].reshape(NA_HEADS // 2, 128, 128)
    bias_ref[8] = jnp.full(bias_ref.shape[1:], NEG, F32)


NA_STEP_ROWS = TOK_BLK // GRID_W


def _na_kernel(q_ref, k_ref, v_ref, band_ref, o_ref, bias_ref):
    s_idx = pl.program_id(1)
    is_ctx = s_idx == 0

    @pl.when(jnp.logical_and(pl.program_id(0) == 0, s_idx == 0))
    def _():
        _fill_bias_tables(band_ref, bias_ref)

    lo = lax.broadcasted_iota(jnp.int32, (GRID_W, 128), 1) < HEAD_DIM
    nwin = NA_ROWS * GRID_W
    for rr in range(NA_STEP_ROWS):
        r = jnp.maximum(s_idx - 1, 0) * NA_STEP_ROWS + rr
        start = jnp.clip(r - NA_ROWS // 2, 0, SEQ // GRID_W - NA_ROWS)
        koff = pl.multiple_of(CTX_LEN + start * GRID_W, GRID_W)
        off = jnp.where(is_ctx, 8, start - r + NA_ROWS - 1)
        rows = slice(rr * GRID_W, (rr + 1) * GRID_W)
        pairs = range(NA_HEADS // 2)
        cols = [slice(hp * 128, (hp + 1) * 128) for hp in pairs]
        s_loc, s_ctx, m, p_loc, p_ctx, l, o = [], [], [], [], [], [], []
        for hp in pairs:
            qp = q_ref[0, rows, cols[hp]]
            zero = jnp.zeros_like(qp)
            q2 = jnp.concatenate([jnp.where(lo, qp, zero), jnp.where(lo, zero, qp)], axis=0)
            s_loc.append(_dot_nt(q2, k_ref[0, pl.ds(koff, nwin), cols[hp]]) + bias_ref[off, hp])
            s_ctx.append(_dot_nt(q2, k_ref[0, 0:CTX_LEN, cols[hp]]))
        for hp in pairs:
            m.append(jnp.maximum(jnp.max(s_loc[hp], axis=-1, keepdims=True),
                                 jnp.max(s_ctx[hp], axis=-1, keepdims=True)))
        for hp in pairs:
            pl_, pc_ = jnp.exp2(s_loc[hp] - m[hp]), jnp.exp2(s_ctx[hp] - m[hp])
            l.append(jnp.sum(pl_, axis=-1, keepdims=True) + jnp.sum(pc_, axis=-1, keepdims=True))
            p_loc.append(pl_.astype(BF16))
            p_ctx.append(pc_.astype(BF16))
        for hp in pairs:
            o.append(_dot(p_loc[hp], v_ref[0, pl.ds(koff, nwin), cols[hp]]) +
                     _dot(p_ctx[hp], v_ref[0, 0:CTX_LEN, cols[hp]]))
        for hp in pairs:
            oh = o[hp] / l[hp]
            o_ref[0, rows, cols[hp]] = jnp.where(lo, oh[:GRID_W], oh[GRID_W:]).astype(BF16)


def _na(p, band2):
    return pl.pallas_call(
        _na_kernel,
        grid=(BATCH, LT // TOK_BLK),
        in_specs=[
            pl.BlockSpec((1, TOK_BLK, 512), lambda b, s: (b, s, 2)),
            pl.BlockSpec((1, LT, 512), lambda b, s: (b, 0, 3)),
            pl.BlockSpec((1, LT, 512), lambda b, s: (b, 0, 4)),
            pl.BlockSpec(band2.shape, lambda b, s: (0, 0, 0, 0)),
        ],
        out_specs=pl.BlockSpec((1, TOK_BLK, 512), lambda b, s: (b, s, 0)),
        out_shape=jax.ShapeDtypeStruct((BATCH, LT, 512), BF16),
        scratch_shapes=[pltpu.VMEM((9, NA_HEADS // 2, 128, NA_ROWS * GRID_W), F32)],
        compiler_params=_cparams(("arbitrary", "arbitrary")),
        name="na_attn",
    )(p, p, p, band2)


def _gelu_tanh(x):
    return 0.5 * x * (1.0 + jnp.tanh(math.sqrt(2.0 / math.pi) * (x + 0.044715 * (x * x * x))))


def _silu(x):
    h = 0.5 * x
    return h + h * jnp.tanh(h)


def _finish(h, out, gt, pg):
    r = lax.rsqrt(jnp.mean(out * out, axis=-1, keepdims=True) + EPS)
    return h + (out * r) * (gt * pg)


def _even_out_kernel(yf_ref, yr_ref, u_ref, ga_ref, gb_ref, at_ref, hc_ref, hl_ref, mod_ref, d_ref, wg_ref,
                     bg_ref, wo_ref, pg_ref, o_ref):
    tb = pl.program_id(1)
    chains = range(PAIR)
    y = [_gelu_tanh(d_ref[...] * u_ref[h].astype(F32) + yf_ref[h].astype(F32) + yr_ref[h].astype(F32))
         for h in chains]
    glu = [_dot(y[h].astype(BF16), wg_ref[...]) for h in chains]
    y_a = [(y[h] * (0.5 + 0.5 * jnp.tanh(glu[h] + bg_ref[...])) * _silu(ga_ref[h].astype(F32))).astype(BF16)
           for h in chains]
    y_b = [(at_ref[h].astype(F32) * _silu(gb_ref[h].astype(F32))).astype(BF16) for h in chains]
    out = [_dot(y_a[h], wo_ref[0:512, :]) + _dot(y_b[h], wo_ref[512:1024, :]) for h in chains]
    for h in chains:
        res = jnp.where(tb == 0, hc_ref[h], hl_ref[h])
        o_ref[h] = _finish(res, out[h], _mod_row(mod_ref, h, tb)[:, 2 * D_MODEL:], pg_ref[...])


def _even_out(yf, yr, p, attn, ctx, x, mod, d_skip, w_glu, b_glu, w_out, post_g):
    tok = lambda c: pl.BlockSpec((PAIR, TOK_BLK, 512), lambda b, t: (b, t, c))
    full = lambda a: pl.BlockSpec(a.shape, lambda b, t: (0,) * a.ndim)
    d2, bg2, pg2 = d_skip.reshape(1, 512), b_glu.reshape(1, 512), post_g.reshape(1, D_MODEL)
    return pl.pallas_call(
        _even_out_kernel,
        grid=(BATCH // PAIR, LT // TOK_BLK),
        in_specs=[tok(0), tok(0), tok(0), tok(1), tok(5), tok(0)] + _cat_specs(D_MODEL) +
                 [full(mod), full(d2), full(w_glu), full(bg2), full(w_out), full(pg2)],
        out_specs=pl.BlockSpec((PAIR, TOK_BLK, D_MODEL), lambda b, t: (b, t, 0)),
        out_shape=jax.ShapeDtypeStruct((BATCH, LT, D_MODEL), F32),
        compiler_params=_cparams(("arbitrary", "arbitrary")),
        name="even_out",
    )(yf, yr, p, p, p, attn, ctx, x, mod, d2, w_glu, bg2, w_out, pg2)


def _softplus(z):
    return jnp.maximum(z, 0.0) + jnp.log1p(jnp.exp(-jnp.abs(z)))


def _lru_kernel(xf_ref, xfp_ref, xfn_ref, xr_ref, xrp_ref, xrn_ref, pf_ref, pr_ref, ph_ref, pft_ref, prt_ref,
                coef_ref, cb_ref, wg_ref, bg_ref, lam_ref, hf_ref, hr_ref, a0_ref, b0_ref, a1_ref, b1_ref,
                x0_ref, x1_ref, st_ref):
    i = pl.program_id(0)

    @pl.when(i == 0)
    def _():
        for ref in (st_ref, a0_ref, b0_ref, a1_ref, b1_ref, x0_ref, x1_ref):
            ref[...] = jnp.zeros_like(ref)

    nrow = 8 * SCAN_BLK
    half = nrow // 2
    blk = jnp.minimum(i, N_SCAN - 1)
    live = jnp.where(i >= 1, 1.0, 0.0).astype(F32)

    def halo(prev_ref, next_ref, b):
        prev_ok = jnp.logical_and(b != 0, b != CTX_SCAN)
        next_ok = jnp.logical_and(b != CTX_SCAN - 1, b != N_SCAN - 1)
        pv = prev_ref[...].reshape(BATCH * 16, LRU_WIDTH)
        nx = next_ref[...].reshape(BATCH * 16, LRU_WIDTH)
        return [jnp.where(prev_ok, pv, jnp.zeros_like(pv)), jnp.where(next_ok, nx, jnp.zeros_like(nx))]

    def step(x_w, x_r, a_w, b_w, a_s, b_s):
        vals = {}

        def unperm():
            hs = b_w[...].astype(BF16)
            hf_ref[...] = _dot(pft_ref[...], hs).astype(BF16).reshape(BATCH, SCAN_BLK, LRU_WIDTH)
            hr_ref[...] = _dot(prt_ref[...], hs).astype(BF16).reshape(BATCH, SCAN_BLK, LRU_WIDTH)

        def perm():
            hal = jnp.concatenate(halo(xfp_ref, xfn_ref, _fwd_blk(blk)) + halo(xrp_ref, xrn_ref, _rev_blk(blk)),
                                  axis=0)
            xh = _dot(ph_ref[...], hal)
            xp = (_dot(pf_ref[...], xf_ref[...].reshape(BATCH * SCAN_BLK, LRU_WIDTH)) +
                  _dot(pr_ref[...], xr_ref[...].reshape(BATCH * SCAN_BLK, LRU_WIDTH)))
            x_w[0:16, :] = xh[:16]
            x_w[16:16 + nrow, :] = xp
            x_w[16 + nrow:, :] = xh[16:]

        def conv(h):
            xc = jnp.zeros((SCAN_BLK // 2, 8, LRU_WIDTH), F32) + cb_ref[...]
            for s in range(5):
                r0 = h * half + s * 8
                xc = xc + x_r[r0:r0 + half, :].reshape(SCAN_BLK // 2, 8, LRU_WIDTH) * coef_ref[s]
            vals['xc%d' % h] = xc.reshape(half, LRU_WIDTH) * live

        def gates(h):
            is_f = (lax.broadcasted_iota(jnp.int32, (half, 1), 0) & 7) < 4
            xcb = vals['xc%d' % h].astype(BF16)
            zero = jnp.zeros_like(xcb[:, :256])
            pre_r, pre_i = [], []
            for c in range(2):
                xt = xcb[:, c * 256:(c + 1) * 256]
                lhs = jnp.concatenate([jnp.where(is_f, xt, zero), jnp.where(is_f, zero, xt)], axis=1)
                pre = _dot(lhs, wg_ref[c])
                pre_r.append(pre[:, :256])
                pre_i.append(pre[:, 256:])
            vals['pr%d' % h] = jnp.concatenate(pre_r, axis=1)
            vals['pi%d' % h] = jnp.concatenate(pre_i, axis=1)

        def elem(h):
            rows = slice(h * half, (h + 1) * half)
            shp = (SCAN_BLK // 2, 8, LRU_WIDTH)
            t_r = jnp.tanh(vals['pr%d' % h].reshape(shp) + bg_ref[:, :LRU_WIDTH])
            t_i = jnp.tanh(vals['pi%d' % h].reshape(shp) + bg_ref[:, LRU_WIDTH:])
            c = (-0.5 * LRU_C) * _softplus(-lam_ref[...])
            u = 1.0 + t_r
            a = jnp.exp2((c * LOG2E) * u)
            a_w[rows, :] = a.reshape(half, LRU_WIDTH)
            one_m_a2 = jnp.maximum(jnp.tanh((-c) * u) * (a * a + 1.0), 1e-37)
            mult = one_m_a2 * lax.rsqrt(one_m_a2)
            b = (mult * vals['xc%d' % h].reshape(shp)) * (0.5 + 0.5 * t_i)
            b_w[rows, :] = b.reshape(half, LRU_WIDTH)

        def scan(t0):
            h = st_ref[...]
            for t in range(t0, t0 + 16):
                rows = slice(t * 8, (t + 1) * 8)
                h = a_s[rows, :] * h + b_s[rows, :]
                b_s[rows, :] = h
            st_ref[...] = h

        for piece in (functools.partial(conv, 0), unperm, functools.partial(conv, 1), functools.partial(scan, 0),
                      functools.partial(gates, 0), perm, functools.partial(gates, 1), functools.partial(scan, 16),
                      functools.partial(elem, 0), functools.partial(scan, 32), functools.partial(elem, 1),
                      functools.partial(scan, 48)):
            piece()

    @pl.when(i % 2 == 0)
    def _():
        step(x0_ref, x1_ref, a1_ref, b1_ref, a0_ref, b0_ref)

    @pl.when(i % 2 == 1)
    def _():
        step(x1_ref, x0_ref, a0_ref, b0_ref, a1_ref, b1_ref)


def _lru_gate_weights(w_a, w_x):
    w = jnp.stack([w_a, w_x], axis=1).reshape(2, 2, 2, 4, LRU_BLOCK, LRU_BLOCK)
    same = jnp.asarray(np.eye(4, dtype=np.float32))
    w = w.transpose(2, 0, 3, 4, 1, 5)[:, :, :, :, :, None] * same[:, None, None, :, None]
    return (0.5 * w).reshape(2, 512, 512).astype(BF16)


def _lru(p, pf, pr, ph, pft, prt, coef, conv_b, wg, bg, lam):
    nrow = 8 * SCAN_BLK
    full = lambda a: pl.BlockSpec(a.shape, lambda i: (0,) * a.ndim)
    proj = lambda i: jnp.minimum(i, N_SCAN - 1)
    read = lambda i: jnp.clip(i - 3, 0, N_SCAN - 1)
    cur = lambda f: pl.BlockSpec((BATCH, SCAN_BLK, LRU_WIDTH), lambda i: (0, f(proj(i)), 0))
    prev = lambda f: pl.BlockSpec((BATCH, 16, LRU_WIDTH), lambda i: (0, jnp.maximum(f(proj(i)) * 4 - 1, 0), 0))
    nxt = lambda f: pl.BlockSpec((BATCH, 16, LRU_WIDTH),
                                 lambda i: (0, jnp.minimum(f(proj(i)) * 4 + 4, LT // 16 - 1), 0))
    cb2 = conv_b.reshape(1, LRU_WIDTH)
    return pl.pallas_call(
        _lru_kernel,
        grid=(N_SCAN + 3,),
        in_specs=[cur(_fwd_blk), prev(_fwd_blk), nxt(_fwd_blk), cur(_rev_blk), prev(_rev_blk), nxt(_rev_blk),
                  full(pf), full(pr), full(ph), full(pft), full(prt), full(coef), full(cb2), full(wg), full(bg),
                  full(lam)],
        out_specs=[
            pl.BlockSpec((BATCH, SCAN_BLK, LRU_WIDTH), lambda i: (0, _fwd_blk(read(i)), 0)),
            pl.BlockSpec((BATCH, SCAN_BLK, LRU_WIDTH), lambda i: (0, _rev_blk(read(i)), 0)),
        ],
        out_shape=[jax.ShapeDtypeStruct((BATCH, LT, LRU_WIDTH), BF16)] * 2,
        scratch_shapes=([pltpu.VMEM((nrow, LRU_WIDTH), F32)] * 4 + [pltpu.VMEM((nrow + 32, LRU_WIDTH), F32)] * 2 +
                        [pltpu.VMEM((8, LRU_WIDTH), F32)]),
        compiler_params=_cparams(("arbitrary",)),
        name="lru_scan",
    )(p, p, p, p, p, p, pf, pr, ph, pft, prt, coef, cb2, wg, bg, lam)


def _norm_rope(xb, g, grot, cos, sin, ones_blk, swap):
    x = xb.astype(F32)
    xrot = _dot(xb, swap)
    ms = _dot((x * x).astype(BF16), ones_blk) * (1.0 / HEAD_DIM)
    rs = lax.rsqrt(ms + EPS)
    return rs * ((x * g) * cos + (xrot * grot) * sin)


def _gqa_kernel(q_ref, k_ref, v_ref, cq_ref, sq_ref, ck_ref, sk_ref, g_ref, ones_ref, swap_ref,
                o_ref, kn_ref, vlo_ref, vhi_ref):
    qb = pl.program_id(1)
    ones_blk = ones_ref[...]
    swap = swap_ref[...]

    @pl.when(qb == 0)
    def _():
        kn = _norm_rope(k_ref[0], g_ref[2:3, :], g_ref[3:4, :], ck_ref[...], sk_ref[...], ones_blk, swap)
        kn_ref[...] = kn.astype(BF16)
        v = v_ref[0]
        lo_k = lax.broadcasted_iota(jnp.int32, (LT, 128), 1) < HEAD_DIM
        one = jnp.ones_like(v)
        vlo_ref[...] = jnp.where(lo_k, v, one)
        vhi_ref[...] = jnp.where(lo_k, one, v)

    lo = lax.broadcasted_iota(jnp.int32, (Q_BLK, 128), 1) < HEAD_DIM
    kn = kn_ref[...]
    cos = cq_ref[...]
    sin = sq_ref[...]
    n_pb = GQA_HEADS // 2

    q_blocks = [q_ref[:, pb * 128:(pb + 1) * 128] for pb in range(n_pb)]
    x_rot = [_dot(qb_, swap) for qb_ in q_blocks]
    msq = [_dot((qb_.astype(F32) * qb_.astype(F32)).astype(BF16), ones_blk) for qb_ in q_blocks]
    qs_all = []
    for pb in range(n_pb):
        rs = lax.rsqrt(msq[pb] * (1.0 / HEAD_DIM) + EPS) * (HEAD_DIM ** -0.5 * LOG2E)
        qn = (rs * ((q_blocks[pb].astype(F32) * g_ref[0:1, :]) * cos + (x_rot[pb] * g_ref[1:2, :]) * sin))
        qn = qn.astype(BF16)
        zero = jnp.zeros_like(qn)
        qs_all.append(jnp.concatenate([jnp.where(lo, qn, zero), jnp.where(lo, zero, qn)], axis=0))

    def scores(pb):
        return _dot_nt(qs_all[pb], kn)

    for pb in range(n_pb):
        cols = slice(pb * 128, (pb + 1) * 128)
        s = scores(pb)
        halves = []
        for rows, vx_ref in ((slice(0, Q_BLK), vlo_ref), (slice(Q_BLK, 2 * Q_BLK), vhi_ref)):
            sh = s[rows]
            mh = jnp.max(sh, axis=-1, keepdims=True)
            halves.append(_dot(jnp.exp2(sh - mh).astype(BF16), vx_ref[...]))
        o_lo, o_hi = halves
        num = jnp.where(lo, o_lo, o_hi)
        den = pltpu.roll(jnp.where(lo, o_hi, o_lo), HEAD_DIM, axis=1)
        o_ref[0, :, cols] = (num / den).astype(BF16)


def _gqa(p, cos128, sin128, gvec, ones_blk, swap):
    nqb = SEQ // Q_BLK
    cos_lat, sin_lat = cos128[CTX_LEN:], sin128[CTX_LEN:]
    return pl.pallas_call(
        _gqa_kernel,
        grid=(BATCH, nqb),
        in_specs=[
            pl.BlockSpec((pl.Squeezed(), pl.Element(Q_BLK), pl.Element(512)),
                         lambda b, q: (b, pl.multiple_of(CTX_LEN + q * Q_BLK, CTX_LEN), 1024)),
            pl.BlockSpec((1, LT, 128), lambda b, q: (b, 0, 16)),
            pl.BlockSpec((1, LT, 128), lambda b, q: (b, 0, 17)),
            pl.BlockSpec((Q_BLK, 128), lambda b, q: (q, 0)),
            pl.BlockSpec((Q_BLK, 128), lambda b, q: (q, 0)),
            pl.BlockSpec((LT, 128), lambda b, q: (0, 0)),
            pl.BlockSpec((LT, 128), lambda b, q: (0, 0)),
            pl.BlockSpec((8, 128), lambda b, q: (0, 0)),
            pl.BlockSpec((128, 128), lambda b, q: (0, 0)),
            pl.BlockSpec((128, 128), lambda b, q: (0, 0)),
        ],
        out_specs=pl.BlockSpec((1, Q_BLK, 512), lambda b, q: (b, q, 0)),
        out_shape=jax.ShapeDtypeStruct((BATCH, SEQ, 512), BF16),
        scratch_shapes=[pltpu.VMEM((LT, 128), BF16)] * 3,
        compiler_params=_cparams(("arbitrary", "arbitrary")),
        name="gqa_attn",
    )(p, p, p, cos_lat, sin_lat, cos128, sin128, gvec, ones_blk, swap)


def _odd_out_kernel(hf_ref, hr_ref, gc_ref, gd_ref, at_ref, h_ref, mod_ref, wo_ref, pg_ref, o_ref):
    chains = range(PAIR)
    y_c = [((hf_ref[h].astype(F32) + hr_ref[h].astype(F32)) * _silu(gc_ref[h].astype(F32))).astype(BF16)
           for h in chains]
    y_d = [(at_ref[h].astype(F32) * _silu(gd_ref[h].astype(F32))).astype(BF16) for h in chains]
    out = [_dot(y_c[h], wo_ref[0:512, :]) + _dot(y_d[h], wo_ref[512:1024, :]) for h in chains]
    for h in chains:
        gt = mod_ref[pl.ds(pl.program_id(0) * PAIR + h, 1), 2 * D_MODEL:]
        o_ref[h] = _finish(h_ref[h], out[h], gt, pg_ref[...])


def _odd_out(hf, hr, p, attn, h_cat, mod, w_out, post_g):
    cb = CTX_LEN // TOK_BLK
    cat = lambda c: pl.BlockSpec((PAIR, TOK_BLK, 512), lambda b, t: (b, t + cb, c))
    full = lambda a: pl.BlockSpec(a.shape, lambda b, t: (0,) * a.ndim)
    pg2 = post_g.reshape(1, D_MODEL)
    return pl.pallas_call(
        _odd_out_kernel,
        grid=(BATCH // PAIR, SEQ // TOK_BLK),
        in_specs=[cat(0), cat(0), cat(1), cat(3),
                  pl.BlockSpec((PAIR, TOK_BLK, 512), lambda b, t: (b, t, 0)),
                  pl.BlockSpec((PAIR, TOK_BLK, D_MODEL), lambda b, t: (b, t + cb, 0)),
                  full(mod), full(w_out), full(pg2)],
        out_specs=pl.BlockSpec((PAIR, TOK_BLK, D_MODEL), lambda b, t: (b, t, 0)),
        out_shape=jax.ShapeDtypeStruct((BATCH, SEQ, D_MODEL), F32),
        compiler_params=_cparams(("arbitrary", "arbitrary")),
        name="odd_out",
    )(hf, hr, p, p, attn, h_cat, mod, w_out, pg2)


def _pair_swap_matrix(n):
    r = np.zeros((n, n), np.float32)
    r[np.arange(1, n, 2), np.arange(0, n, 2)] = -1.0
    r[np.arange(0, n, 2), np.arange(1, n, 2)] = 1.0
    return r


def _interleave_kv_groups(w, axis):
    if axis == 0:
        return w.reshape(2, 4, HEAD_DIM, w.shape[1]).transpose(1, 0, 2, 3).reshape(w.shape)
    return w.reshape(w.shape[0], 2, 4, HEAD_DIM).transpose(0, 2, 1, 3).reshape(w.shape)


def _odd_w_in(w):
    w = w.astype(BF16)
    x, gc = w[:, 0:512], w[:, 512:1024]
    q = _interleave_kv_groups(w[:, 1024:1536], 1)
    k, v = w[:, 1536:1664], w[:, 1664:1792]
    gd = _interleave_kv_groups(w[:, 1792:2304], 1)
    return jnp.concatenate([x, gc, q, gd, k, v], axis=1)


def _rope_tables():
    t = np.arange(SEQ)
    row = (t // GRID_W).astype(np.float32)
    col = (t % GRID_W).astype(np.float32)
    half = HEAD_DIM // 2
    inv = (ROPE_THETA ** (-np.arange(0, half, 2, dtype=np.float32) / half)).astype(np.float32)
    ang = np.concatenate([row[:, None] * inv, col[:, None] * inv], axis=-1)
    cos = np.repeat(np.cos(ang), 2, axis=-1)
    sin = np.repeat(np.sin(ang), 2, axis=-1)
    cos = np.concatenate([np.ones((CTX_LEN, HEAD_DIM), np.float32), cos], axis=0)
    sin = np.concatenate([np.zeros((CTX_LEN, HEAD_DIM), np.float32), sin], axis=0)
    return np.tile(cos, (1, 2)).astype(np.float32), np.tile(sin, (1, 2)).astype(np.float32)


def _swap_pairs_vec(g):
    g2 = g.reshape(-1, 2)
    return jnp.stack([g2[:, 1], g2[:, 0]], axis=-1).reshape(g.shape)


def kernel(x, c, ctx, c_ctx, ada_w, ada_b, pre_g, post_g, ev_w_in, ev_w_out, s5_lam_re, s5_lam_im, s5_log_dt,
           s5_b_re, s5_b_im, s5_c_re, s5_c_im, s5_d, s5_w_glu, s5_b_glu, na_rel_bias, od_w_in, od_w_out,
           lru_conv_w, lru_conv_b, lru_lam, lru_w_a, lru_b_a, lru_w_x, lru_b_x, gqa_q_norm, gqa_k_norm):
    pf_np, pr_np = _scan_perms()
    pf, pr = jnp.asarray(pf_np, BF16), jnp.asarray(pr_np, BF16)
    pft, prt = jnp.asarray(pf_np.T, BF16), jnp.asarray(pr_np.T, BF16)
    ph = jnp.asarray(_halo_perm(), BF16)

    c8 = jnp.concatenate([c, c_ctx[None], jnp.zeros((3, D_MODEL), F32)], axis=0)
    mod = _adaln(c8, ada_w, ada_b)

    col_scale = np.ones((EVEN_IN,), np.float32)
    col_scale[1024:1536] = HEAD_DIM ** -0.5 * LOG2E
    p0 = _inproj((ctx, x), mod[0], pre_g[0], (ev_w_in[0] * jnp.asarray(col_scale)).astype(BF16))
    bc, lre8, lim8 = _s5_weights(s5_lam_re[0], s5_lam_im[0], s5_log_dt[0], s5_b_re[0], s5_b_im[0],
                                 s5_c_re[0], s5_c_im[0])
    rep_np, same_np = _s5_expanders()
    yf, yr = _s5(p0, pf, pr, pft, prt, bc, jnp.asarray(rep_np, BF16), jnp.asarray(same_np, BF16), lre8, lim8)
    attn0 = _na(p0, _na_bias_table(na_rel_bias[0]))
    h1 = _even_out(yf, yr, p0, attn0, ctx, x, mod[0], s5_d[0], (0.5 * s5_w_glu[0]).astype(BF16), 0.5 * s5_b_glu[0],
                   ev_w_out[0].astype(BF16), post_g[0])

    p1 = _inproj(h1, mod[1], pre_g[1], _odd_w_in(od_w_in[0]))
    wg = _lru_gate_weights(lru_w_a[0], lru_w_x[0])
    bg = jnp.repeat(0.5 * jnp.concatenate([lru_b_a[0], lru_b_x[0]], axis=1), 4, axis=0)
    lam8 = jnp.repeat(lru_lam[0], 4, axis=0)
    hf, hr = _lru(p1, pf, pr, ph, pft, prt, _conv_coef(lru_conv_w[0]), lru_conv_b[0], wg, bg, lam8)
    cos_np, sin_np = _rope_tables()
    gq, gk = gqa_q_norm[0], gqa_k_norm[0]
    gvec = jnp.stack([jnp.tile(gq, 2), jnp.tile(_swap_pairs_vec(gq), 2),
                      jnp.tile(gk, 2), jnp.tile(_swap_pairs_vec(gk), 2)] + [jnp.zeros((128,), F32)] * 4)
    ones_np = np.kron(np.eye(2, dtype=np.float32), np.ones((HEAD_DIM, HEAD_DIM), np.float32))
    attn1 = _gqa(p1, jnp.asarray(cos_np), jnp.asarray(sin_np), gvec, jnp.asarray(ones_np, BF16),
                 jnp.asarray(_pair_swap_matrix(128), BF16))
    w_out1 = od_w_out[0].astype(BF16)
    w_out1 = jnp.concatenate([w_out1[:512], _interleave_kv_groups(w_out1[512:], 0)], axis=0)
    return _odd_out(hf, hr, p1, attn1, h1, mod[1], w_out1, post_g[1])
```
